```python
import jax, jax.numpy as jnp
from jax import lax
import numpy as np

D_MODEL = 1024
BATCH = 8
SEQ = 4096
DEPTH = 2

EPS = 1e-6
NEG = -1e30
N_BRANCH = 3
CONV_DIM = 512
CONV_WIDTH = 31
NSA_HEADS = 8
NSA_KV_HEADS = 2
NSA_GROUP = NSA_HEADS // NSA_KV_HEADS
HEAD_DIM = 64
CMP_BLOCK = 32
CMP_STRIDE = 16
CMP_HIDDEN = 256
SLC_BLOCK = 64
SLC_TOPK = 16
WINDOW = 512
NSA_QBLK = 64
FORCE_BONUS = 1e4
ROPE_THETA = 10000.0
GLA_HEADS = 4
GLA_DK = 64
GLA_DV = 128
GLA_GATE_RANK = 16
GLA_GATE_TEMP = 16.0
GLA_CHUNK = 64
PEER_HEADS = 8
PEER_NKEYS = 128
PEER_EXPERTS = PEER_NKEYS * PEER_NKEYS
PEER_DKEY = 256
PEER_TOPK = 16
PEER_TBLK = 128
NSA_Q = NSA_HEADS * HEAD_DIM
NSA_KV = 3 * 2 * NSA_KV_HEADS * HEAD_DIM
NSA_G = 3 * NSA_HEADS
GLA_QK = GLA_HEADS * GLA_DK
GLA_V = GLA_HEADS * GLA_DV
SPLITS = [2 * CONV_DIM, NSA_Q, NSA_KV, NSA_G, GLA_QK, GLA_QK, GLA_V, GLA_V, GLA_GATE_RANK, N_BRANCH * D_MODEL]
N_IN = sum(SPLITS)

kernel_name = 'hybrid_conv_nsa_gla_peer_block'


def rmsnorm(x, g):
    xf = x.astype(jnp.float32)
    y = xf * lax.rsqrt(jnp.mean(jnp.square(xf), -1, keepdims=True) + EPS) * g
    return y.astype(x.dtype)


def layernorm(x, g, b):
    xf = x.astype(jnp.float32)
    mu = jnp.mean(xf, -1, keepdims=True)
    var = jnp.mean(jnp.square(xf - mu), -1, keepdims=True)
    return ((xf - mu) * lax.rsqrt(var + EPS) * g + b).astype(x.dtype)


def masked_softmax(s, mask):
    return jax.nn.softmax(jnp.where(mask, s, NEG), axis=-1) * mask


def rope(x, pos):
    half = x.shape[-1] // 2
    freq = ROPE_THETA ** (-jnp.arange(half, dtype=jnp.float32) / half)
    ang = pos.astype(jnp.float32)[..., None] * freq
    cos = jnp.cos(ang)[:, :, None, :]
    sin = jnp.sin(ang)[:, :, None, :]
    x1 = x[..., :half].astype(jnp.float32)
    x2 = x[..., half:].astype(jnp.float32)
    return jnp.concatenate([x1 * cos - x2 * sin, x2 * cos + x1 * sin], -1).astype(x.dtype)


def adaln(c, w, b):
    mod = jax.nn.silu(c) @ w + b
    shift, scale, gate = jnp.split(mod[:, None, :], 3, axis=-1)
    return shift, scale, gate


def conformer_conv(a, dw_w, dw_b, ln_g, ln_b, w_up):
    h = a[..., :CONV_DIM] * jax.nn.sigmoid(a[..., CONV_DIM:])
    h = lax.conv_general_dilated(h, dw_w[:, None, :], window_strides=(1,), padding=[(CONV_WIDTH - 1, 0)],
                                 dimension_numbers=('NWC', 'WIO', 'NWC'), feature_group_count=CONV_DIM) + dw_b
    h = jax.nn.silu(layernorm(h, ln_g, ln_b))
    return h @ w_up


def compress(x, cidx, pos_emb, w1, w2):
    B, _, hk, dh = x.shape
    blocks = x[:, cidx] + pos_emb[None, None, :, None, :]
    flat = blocks.transpose(0, 1, 3, 2, 4).reshape(B, cidx.shape[0], hk, CMP_BLOCK * dh)
    return jax.nn.gelu(flat @ w1) @ w2


def nsa_attention(q_raw, kv_raw, gate_raw, positions, cmp_pos, cmp_w1, cmp_w2):
    B, S, _ = q_raw.shape
    T = NSA_QBLK
    q = rope(q_raw.reshape(B, S, NSA_HEADS, HEAD_DIM), positions)
    kv = kv_raw.reshape(B, S, 3, 2, NSA_KV_HEADS, HEAD_DIM)
    gates = jax.nn.sigmoid(gate_raw.reshape(B, S, NSA_KV_HEADS, NSA_GROUP, 3))
    scale = HEAD_DIM ** -0.5
    ncmp = (S - CMP_BLOCK) // CMP_STRIDE + 1
    cidx = np.arange(ncmp)[:, None] * CMP_STRIDE + np.arange(CMP_BLOCK)[None, :]
    cmp_end = cidx[:, -1]
    k_cmp = rope(compress(kv[:, :, 0, 0], cidx, cmp_pos[0], cmp_w1[0], cmp_w2[0]), positions[:, cmp_end])
    v_cmp = compress(kv[:, :, 0, 1], cidx, cmp_pos[1], cmp_w1[1], cmp_w2[1])
    nslc = S // SLC_BLOCK
    n_sel = min(SLC_TOPK, nslc)
    cs = np.arange(ncmp) * CMP_STRIDE
    ss = np.arange(nslc) * SLC_BLOCK
    ov = np.minimum(cs[:, None] + CMP_BLOCK, ss[None, :] + SLC_BLOCK) - np.maximum(cs[:, None], ss[None, :])
    cmp_to_slc = jnp.asarray(np.clip(ov, 0, None) / CMP_BLOCK, dtype=jnp.float32)
    k_blk = rope(kv[:, :, 1, 0], positions).reshape(B, nslc, SLC_BLOCK, NSA_KV_HEADS, HEAD_DIM).transpose(0, 3, 1, 2, 4)
    v_blk = kv[:, :, 1, 1].reshape(B, nslc, SLC_BLOCK, NSA_KV_HEADS, HEAD_DIM).transpose(0, 3, 1, 2, 4)
    bi = jnp.arange(B)[:, None, None, None]
    hi = jnp.arange(NSA_KV_HEADS)[None, :, None, None]
    pad = ((0, 0), (WINDOW, 0), (0, 0), (0, 0))
    k_win = jnp.pad(rope(kv[:, :, 2, 0], positions), pad)
    v_win = jnp.pad(kv[:, :, 2, 1], pad)

    def block(qb):
        t0 = qb * T
        tpos = t0 + jnp.arange(T)
        qg = lax.dynamic_slice_in_dim(q, t0, T, 1).reshape(B, T, NSA_KV_HEADS, NSA_GROUP, HEAD_DIM)
        g = lax.dynamic_slice_in_dim(gates, t0, T, 1)
        s_c = jnp.einsum('btkgd,bnkd->bkgtn', qg, k_cmp).astype(jnp.float32) * scale
        p_c = masked_softmax(s_c, cmp_end[None, :] <= tpos[:, None])
        o_c = jnp.einsum('bkgtn,bnkd->btkgd', p_c.astype(v_cmp.dtype), v_cmp)
        imp = jnp.einsum('bkgtn,nj->bktj', p_c, cmp_to_slc)
        jb = jnp.arange(nslc)[None, :]
        cur = (tpos // SLC_BLOCK)[:, None]
        forced = (jb == 0) | (jb == cur) | (jb == cur - 1)
        score = jnp.where(jb <= cur, imp + jnp.where(forced, FORCE_BONUS, 0.0), NEG)
        top_s, top_i = lax.top_k(score, n_sel)
        kg = k_blk[bi, hi, top_i]
        vg = v_blk[bi, hi, top_i]
        key_pos = top_i[..., None] * SLC_BLOCK + jnp.arange(SLC_BLOCK)
        mask_s = (top_s > 0.5 * NEG)[..., None] & (key_pos <= tpos[None, None, :, None, None])
        s_s = jnp.einsum('btkgd,bktnld->bkgtnl', qg, kg).astype(jnp.float32) * scale
        nl = n_sel * SLC_BLOCK
        p_s = masked_softmax(s_s.reshape(B, NSA_KV_HEADS, NSA_GROUP, T, nl),
                             mask_s.reshape(B, NSA_KV_HEADS, 1, T, nl)).reshape(s_s.shape)
        o_s = jnp.einsum('bkgtnl,bktnld->btkgd', p_s.astype(vg.dtype), vg)
        kw = lax.dynamic_slice_in_dim(k_win, t0, WINDOW + T, 1)
        vw = lax.dynamic_slice_in_dim(v_win, t0, WINDOW + T, 1)
        kpos = t0 - WINDOW + jnp.arange(WINDOW + T)
        diff = tpos[:, None] - kpos[None, :]
        mask_w = (kpos[None, :] >= 0) & (diff >= 0) & (diff < WINDOW)
        s_w = jnp.einsum('btkgd,bskd->bkgts', qg, kw).astype(jnp.float32) * scale
        p_w = masked_softmax(s_w, mask_w)
        o_w = jnp.einsum('bkgts,bskd->btkgd', p_w.astype(vw.dtype), vw)
        o = g[..., 0:1] * o_c + g[..., 1:2] * o_s + g[..., 2:3] * o_w
        return o.reshape(B, T, NSA_HEADS * HEAD_DIM).astype(q.dtype)

    out = lax.map(block, jnp.arange(S // T))
    return out.transpose(1, 0, 2, 3).reshape(B, S, NSA_HEADS * HEAD_DIM)


def gla_attention(q_raw, k_raw, v_raw, r_raw, glr, gate_w, gate_b, norm_g):
    B, S, _ = q_raw.shape
    C = GLA_CHUNK
    nC = S // C

    def chunks(t, d):
        return t.astype(jnp.float32).reshape(B, nC, C, GLA_HEADS, d).transpose(1, 0, 3, 2, 4)

    log_a = jax.nn.log_sigmoid((glr @ gate_w + gate_b).astype(jnp.float32)) / GLA_GATE_TEMP
    qc = chunks(q_raw, GLA_DK) * (GLA_DK ** -0.5)
    kc = chunks(k_raw, GLA_DK)
    vc = chunks(v_raw, GLA_DV)
    ac = chunks(log_a, GLA_DK)
    causal = jnp.tril(jnp.ones((C, C), dtype=bool))[:, :, None]

    def step(state, inp):
        q_, k_, v_, a_ = inp
        bcum = jnp.cumsum(a_, axis=2)
        o_inter = jnp.einsum('bhcd,bhde->bhce', q_ * jnp.exp(bcum), state)
        diff = bcum[:, :, :, None, :] - bcum[:, :, None, :, :]
        decay = jnp.exp(jnp.where(causal, diff, NEG))
        att = jnp.einsum('bhid,bhjd,bhijd->bhij', q_, k_, decay)
        o_intra = jnp.einsum('bhij,bhje->bhie', att, v_)
        blast = bcum[:, :, -1:, :]
        state = state * jnp.exp(blast)[:, :, 0, :, None] + jnp.einsum('bhjd,bhje->bhde', k_ * jnp.exp(blast - bcum), v_)
        return state, o_inter + o_intra

    state0 = jnp.zeros((B, GLA_HEADS, GLA_DK, GLA_DV), jnp.float32)
    _, o = lax.scan(step, state0, (qc, kc, vc, ac))
    o = o.transpose(1, 0, 3, 2, 4).reshape(B, S, GLA_HEADS, GLA_DV)
    o = rmsnorm(o, norm_g).reshape(B, S, GLA_V).astype(r_raw.dtype)
    return o * jax.nn.silu(r_raw)


def token_mixer(u, positions, w_in, dw_w, dw_b, ln_g, ln_b, w_conv_up, cmp_pos, cmp_w1, cmp_w2, w_nsa_up,
                gla_gate_w, gla_gate_b, gla_norm_g, w_gla_up, w_out):
    B, S, D = u.shape
    z = u @ w_in
    a_conv, nq, nkv, ng, gq, gk, gv, gr, glr, mg = jnp.split(z, np.cumsum(SPLITS)[:-1].tolist(), axis=-1)
    y_conv = conformer_conv(a_conv, dw_w, dw_b, ln_g, ln_b, w_conv_up)
    y_nsa = nsa_attention(nq, nkv, ng, positions, cmp_pos, cmp_w1, cmp_w2) @ w_nsa_up
    y_gla = gla_attention(gq, gk, gv, gr, glr, gla_gate_w, gla_gate_b, gla_norm_g) @ w_gla_up
    gm = jax.nn.sigmoid(mg).reshape(B, S, N_BRANCH, D)
    y = gm[:, :, 0] * y_conv + gm[:, :, 1] * y_nsa + gm[:, :, 2] * y_gla
    return y @ w_out


def peer(h, wq, keys, u, v):
    B, S, D = h.shape
    T = PEER_TBLK
    K = PEER_TOPK
    hb = h.reshape(B * S // T, T, D)

    def blk(xb):
        qh = (xb @ wq).reshape(T, PEER_HEADS, 2, PEER_DKEY // 2)
        s = jnp.einsum('thpd,hpnd->thpn', qh, keys).astype(jnp.float32)
        s1, i1 = lax.top_k(s[:, :, 0], K)
        s2, i2 = lax.top_k(s[:, :, 1], K)
        cand = (s1[..., :, None] + s2[..., None, :]).reshape(T, PEER_HEADS, K * K)
        cidx = (i1[..., :, None] * PEER_NKEYS + i2[..., None, :]).reshape(T, PEER_HEADS, K * K)
        top, pos = lax.top_k(cand, K)
        eidx = jnp.take_along_axis(cidx, pos, axis=-1)
        g = jax.nn.softmax(top, axis=-1)
        ue = u[eidx]
        ve = v[eidx]
        act = jax.nn.gelu(jnp.einsum('td,thkd->thk', xb, ue).astype(jnp.float32))
        return jnp.einsum('thk,thkd->td', (g * act).astype(ve.dtype), ve)

    return lax.map(blk, hb).reshape(B, S, D)


def setup_inputs(seed: int = 0) -> dict:
    key = jax.random.key(seed)
    ks = iter(jax.random.split(key, 32))

    def nrm(shape, scale):
        return jax.random.normal(next(ks), shape, jnp.float32) * scale

    L = DEPTH
    return {
        'x': nrm((BATCH, SEQ, D_MODEL), 1.0),
        'c': nrm((BATCH, D_MODEL), 1.0),
        'positions': jnp.broadcast_to(jnp.arange(SEQ, dtype=jnp.int32)[None, :], (BATCH, SEQ)),
        'ada_w': nrm((L, 2, D_MODEL, 3 * D_MODEL), D_MODEL ** -0.5),
        'ada_b': nrm((L, 2, 3 * D_MODEL), 0.02),
        'norm_g': 1.0 + nrm((L, 2, D_MODEL), 0.02),
        'w_in': nrm((L, D_MODEL, N_IN), D_MODEL ** -0.5),
        'conv_dw_w': nrm((L, CONV_WIDTH, CONV_DIM), CONV_WIDTH ** -0.5),
        'conv_dw_b': nrm((L, CONV_DIM), 0.02),
        'conv_ln_g': 1.0 + nrm((L, CONV_DIM), 0.02),
        'conv_ln_b': nrm((L, CONV_DIM), 0.02),
        'w_conv_up': nrm((L, CONV_DIM, D_MODEL), CONV_DIM ** -0.5),
        'cmp_pos': nrm((L, 2, CMP_BLOCK, HEAD_DIM), 0.02),
        'cmp_w1': nrm((L, 2, CMP_BLOCK * HEAD_DIM, CMP_HIDDEN), (CMP_BLOCK * HEAD_DIM) ** -0.5),
        'cmp_w2': nrm((L, 2, CMP_HIDDEN, HEAD_DIM), CMP_HIDDEN ** -0.5),
        'w_nsa_up': nrm((L, NSA_Q, D_MODEL), NSA_Q ** -0.5),
        'gla_gate_w': nrm((L, GLA_GATE_RANK, GLA_QK), GLA_GATE_RANK ** -0.5),
        'gla_gate_b': nrm((L, GLA_QK), 0.02),
        'gla_norm_g': 1.0 + nrm((L, GLA_DV), 0.02),
        'w_gla_up': nrm((L, GLA_V, D_MODEL), GLA_V ** -0.5),
        'w_out': nrm((L, D_MODEL, D_MODEL), D_MODEL ** -0.5),
        'peer_wq': nrm((L, D_MODEL, PEER_HEADS * PEER_DKEY), D_MODEL ** -0.5),
        'peer_keys': nrm((L, PEER_HEADS, 2, PEER_NKEYS, PEER_DKEY // 2), (PEER_DKEY // 2) ** -0.5),
        'peer_u': nrm((L, PEER_EXPERTS, D_MODEL), D_MODEL ** -0.5),
        'peer_v': nrm((L, PEER_EXPERTS, D_MODEL), PEER_HEADS ** -0.5),
        'final_g': 1.0 + nrm((D_MODEL,), 0.02),
    }


def reference(x, c, positions, ada_w, ada_b, norm_g, w_in, conv_dw_w, conv_dw_b, conv_ln_g, conv_ln_b, w_conv_up,
              cmp_pos, cmp_w1, cmp_w2, w_nsa_up, gla_gate_w, gla_gate_b, gla_norm_g, w_gla_up, w_out,
              peer_wq, peer_keys, peer_u, peer_v, final_g):
    for l in range(DEPTH):
        shift, scale, gate = adaln(c, ada_w[l, 0], ada_b[l, 0])
        u = rmsnorm(x, norm_g[l, 0]) * (1.0 + scale) + shift
        y = token_mixer(u, positions, w_in[l], conv_dw_w[l], conv_dw_b[l], conv_ln_g[l], conv_ln_b[l], w_conv_up[l],
                        cmp_pos[l], cmp_w1[l], cmp_w2[l], w_nsa_up[l], gla_gate_w[l], gla_gate_b[l], gla_norm_g[l],
                        w_gla_up[l], w_out[l])
        x = x + gate * y
        shift, scale, gate = adaln(c, ada_w[l, 1], ada_b[l, 1])
        u = rmsnorm(x, norm_g[l, 1]) * (1.0 + scale) + shift
        x = x + gate * peer(u, peer_wq[l], peer_keys[l], peer_u[l], peer_v[l])
    return rmsnorm(x, final_g)
```

```python
import functools

import jax
import jax.numpy as jnp
import numpy as np
from jax import lax
from jax.experimental import pallas as pl
from jax.experimental.pallas import tpu as pltpu

D_MODEL = 1024
DEPTH = 2
EPS = 1e-6
NEG = -1e30
N_BRANCH = 3
CONV_DIM = 512
CONV_WIDTH = 31
NSA_HEADS = 8
NSA_KV_HEADS = 2
NSA_GROUP = NSA_HEADS // NSA_KV_HEADS
HEAD_DIM = 64
CMP_BLOCK = 32
CMP_STRIDE = 16
CMP_HIDDEN = 256
SLC_BLOCK = 64
SLC_TOPK = 16
WINDOW = 512
NSA_QBLK = 64
FORCE_BONUS = 1e4
ROPE_THETA = 10000.0
GLA_HEADS = 4
GLA_DK = 64
GLA_DV = 128
GLA_GATE_RANK = 16
GLA_GATE_TEMP = 16.0
GLA_CHUNK = 64
PEER_HEADS = 8
PEER_NKEYS = 128
PEER_DKEY = 256
PEER_TOPK = 16
PEER_TBLK = 128
NSA_Q = NSA_HEADS * HEAD_DIM
NSA_KV = 3 * 2 * NSA_KV_HEADS * HEAD_DIM
NSA_G = 3 * NSA_HEADS
GLA_QK = GLA_HEADS * GLA_DK
GLA_V = GLA_HEADS * GLA_DV
SPLITS = [2 * CONV_DIM, NSA_Q, NSA_KV, NSA_G, GLA_QK, GLA_QK, GLA_V, GLA_V, GLA_GATE_RANK, N_BRANCH * D_MODEL]

LANE = 128


def _mm_kernel(a_ref, b_ref, o_ref):
    o_ref[...] = jnp.dot(a_ref[...].astype(jnp.bfloat16), b_ref[...].astype(jnp.bfloat16),
                         preferred_element_type=jnp.float32)


def _pick(n, cands):
    for c in cands:
        if n % c == 0:
            return c
    return n


def pmm(a, b):
    m, k = a.shape
    n = b.shape[1]
    n_pad = -n % LANE
    if n_pad:
        b = jnp.pad(b, ((0, 0), (0, n_pad)))
    m_pad = -m % 8
    if m_pad:
        a = jnp.pad(a, ((0, m_pad), (0, 0)))
    mp, np_ = m + m_pad, n + n_pad
    tm = _pick(mp, (512, 256, 128, 64, 32, 16, 8))
    tn = _pick(np_, (512, 640, 384, 256, 128))
    out = pl.pallas_call(
        _mm_kernel,
        grid=(mp // tm, np_ // tn),
        in_specs=[pl.BlockSpec((tm, k), lambda i, j: (i, 0)),
                  pl.BlockSpec((k, tn), lambda i, j: (0, j))],
        out_specs=pl.BlockSpec((tm, tn), lambda i, j: (i, j)),
        out_shape=jax.ShapeDtypeStruct((mp, np_), jnp.float32),
        compiler_params=pltpu.CompilerParams(dimension_semantics=("parallel", "parallel")),
    )(a, b)
    return out[:m, :n]


def pmm3(a, b):
    lead = a.shape[:-1]
    return pmm(a.reshape(-1, a.shape[-1]), b).reshape(*lead, b.shape[1])


def rmsnorm(x, g):
    xf = x.astype(jnp.float32)
    y = xf * lax.rsqrt(jnp.mean(jnp.square(xf), -1, keepdims=True) + EPS) * g
    return y.astype(x.dtype)


def layernorm(x, g, b):
    xf = x.astype(jnp.float32)
    mu = jnp.mean(xf, -1, keepdims=True)
    var = jnp.mean(jnp.square(xf - mu), -1, keepdims=True)
    return ((xf - mu) * lax.rsqrt(var + EPS) * g + b).astype(x.dtype)


def masked_softmax(s, mask):
    return jax.nn.softmax(jnp.where(mask, s, NEG), axis=-1) * mask


def rope(x, pos):
    half = x.shape[-1] // 2
    freq = ROPE_THETA ** (-jnp.arange(half, dtype=jnp.float32) / half)
    ang = pos.astype(jnp.float32)[..., None] * freq
    cos = jnp.cos(ang)[:, :, None, :]
    sin = jnp.sin(ang)[:, :, None, :]
    x1 = x[..., :half].astype(jnp.float32)
    x2 = x[..., half:].astype(jnp.float32)
    return jnp.concatenate([x1 * cos - x2 * sin, x2 * cos + x1 * sin], -1).astype(x.dtype)


def adaln(c, w, b):
    mod = pmm(jax.nn.silu(c), w) + b
    shift, scale, gate = jnp.split(mod[:, None, :], 3, axis=-1)
    return shift, scale, gate


def conformer_conv(a, dw_w, dw_b, ln_g, ln_b, w_up):
    h = a[..., :CONV_DIM] * jax.nn.sigmoid(a[..., CONV_DIM:])
    h = lax.conv_general_dilated(h, dw_w[:, None, :], window_strides=(1,), padding=[(CONV_WIDTH - 1, 0)],
                                 dimension_numbers=('NWC', 'WIO', 'NWC'), feature_group_count=CONV_DIM) + dw_b
    h = jax.nn.silu(layernorm(h, ln_g, ln_b))
    return pmm3(h, w_up)


def compress(x, cidx, pos_emb, w1, w2):
    B, _, hk, dh = x.shape
    blocks = x[:, cidx] + pos_emb[None, None, :, None, :]
    flat = blocks.transpose(0, 1, 3, 2, 4).reshape(B, cidx.shape[0], hk, CMP_BLOCK * dh)
    return pmm3(jax.nn.gelu(pmm3(flat, w1)), w2)


def nsa_attention(q_raw, kv_raw, gate_raw, positions, cmp_pos, cmp_w1, cmp_w2):
    B, S, _ = q_raw.shape
    T = NSA_QBLK
    q = rope(q_raw.reshape(B, S, NSA_HEADS, HEAD_DIM), positions)
    kv = kv_raw.reshape(B, S, 3, 2, NSA_KV_HEADS, HEAD_DIM)
    gates = jax.nn.sigmoid(gate_raw.reshape(B, S, NSA_KV_HEADS, NSA_GROUP, 3))
    scale = HEAD_DIM ** -0.5
    ncmp = (S - CMP_BLOCK) // CMP_STRIDE + 1
    cidx = np.arange(ncmp)[:, None] * CMP_STRIDE + np.arange(CMP_BLOCK)[None, :]
    cmp_end = cidx[:, -1]
    k_cmp = rope(compress(kv[:, :, 0, 0], cidx, cmp_pos[0], cmp_w1[0], cmp_w2[0]), positions[:, cmp_end])
    v_cmp = compress(kv[:, :, 0, 1], cidx, cmp_pos[1], cmp_w1[1], cmp_w2[1])
    nslc = S // SLC_BLOCK
    n_sel = min(SLC_TOPK, nslc)
    cs = np.arange(ncmp) * CMP_STRIDE
    ss = np.arange(nslc) * SLC_BLOCK
    ov = np.minimum(cs[:, None] + CMP_BLOCK, ss[None, :] + SLC_BLOCK) - np.maximum(cs[:, None], ss[None, :])
    cmp_to_slc = jnp.asarray(np.clip(ov, 0, None) / CMP_BLOCK, dtype=jnp.float32)
    k_blk = rope(kv[:, :, 1, 0], positions).reshape(B, nslc, SLC_BLOCK, NSA_KV_HEADS, HEAD_DIM).transpose(0, 3, 1, 2, 4)
    v_blk = kv[:, :, 1, 1].reshape(B, nslc, SLC_BLOCK, NSA_KV_HEADS, HEAD_DIM).transpose(0, 3, 1, 2, 4)
    bi = jnp.arange(B)[:, None, None, None]
    hi = jnp.arange(NSA_KV_HEADS)[None, :, None, None]
    pad = ((0, 0), (WINDOW, 0), (0, 0), (0, 0))
    k_win = jnp.pad(rope(kv[:, :, 2, 0], positions), pad)
    v_win = jnp.pad(kv[:, :, 2, 1], pad)

    def block(qb):
        t0 = qb * T
        tpos = t0 + jnp.arange(T)
        qg = lax.dynamic_slice_in_dim(q, t0, T, 1).reshape(B, T, NSA_KV_HEADS, NSA_GROUP, HEAD_DIM)
        g = lax.dynamic_slice_in_dim(gates, t0, T, 1)
        s_c = jnp.einsum('btkgd,bnkd->bkgtn', qg, k_cmp).astype(jnp.float32) * scale
        p_c = masked_softmax(s_c, cmp_end[None, :] <= tpos[:, None])
        o_c = jnp.einsum('bkgtn,bnkd->btkgd', p_c.astype(v_cmp.dtype), v_cmp)
        imp = jnp.einsum('bkgtn,nj->bktj', p_c, cmp_to_slc)
        jb = jnp.arange(nslc)[None, :]
        cur = (tpos // SLC_BLOCK)[:, None]
        forced = (jb == 0) | (jb == cur) | (jb == cur - 1)
        score = jnp.where(jb <= cur, imp + jnp.where(forced, FORCE_BONUS, 0.0), NEG)
        top_s, top_i = lax.top_k(score, n_sel)
        kg = k_blk[bi, hi, top_i]
        vg = v_blk[bi, hi, top_i]
        key_pos = top_i[..., None] * SLC_BLOCK + jnp.arange(SLC_BLOCK)
        mask_s = (top_s > 0.5 * NEG)[..., None] & (key_pos <= tpos[None, None, :, None, None])
        s_s = jnp.einsum('btkgd,bktnld->bkgtnl', qg, kg).astype(jnp.float32) * scale
        nl = n_sel * SLC_BLOCK
        p_s = masked_softmax(s_s.reshape(B, NSA_KV_HEADS, NSA_GROUP, T, nl),
                             mask_s.reshape(B, NSA_KV_HEADS, 1, T, nl)).reshape(s_s.shape)
        o_s = jnp.einsum('bkgtnl,bktnld->btkgd', p_s.astype(vg.dtype), vg)
        kw = lax.dynamic_slice_in_dim(k_win, t0, WINDOW + T, 1)
        vw = lax.dynamic_slice_in_dim(v_win, t0, WINDOW + T, 1)
        kpos = t0 - WINDOW + jnp.arange(WINDOW + T)
        diff = tpos[:, None] - kpos[None, :]
        mask_w = (kpos[None, :] >= 0) & (diff >= 0) & (diff < WINDOW)
        s_w = jnp.einsum('btkgd,bskd->bkgts', qg, kw).astype(jnp.float32) * scale
        p_w = masked_softmax(s_w, mask_w)
        o_w = jnp.einsum('bkgts,bskd->btkgd', p_w.astype(vw.dtype), vw)
        o = g[..., 0:1] * o_c + g[..., 1:2] * o_s + g[..., 2:3] * o_w
        return o.reshape(B, T, NSA_HEADS * HEAD_DIM).astype(q.dtype)

    out = lax.map(block, jnp.arange(S // T))
    return out.transpose(1, 0, 2, 3).reshape(B, S, NSA_HEADS * HEAD_DIM)


def gla_attention(q_raw, k_raw, v_raw, r_raw, glr, gate_w, gate_b, norm_g):
    B, S, _ = q_raw.shape
    C = GLA_CHUNK
    nC = S // C

    def chunks(t, d):
        return t.astype(jnp.float32).reshape(B, nC, C, GLA_HEADS, d).transpose(1, 0, 3, 2, 4)

    log_a = jax.nn.log_sigmoid((glr @ gate_w + gate_b).astype(jnp.float32)) / GLA_GATE_TEMP
    qc = chunks(q_raw, GLA_DK) * (GLA_DK ** -0.5)
    kc = chunks(k_raw, GLA_DK)
    vc = chunks(v_raw, GLA_DV)
    ac = chunks(log_a, GLA_DK)
    causal = jnp.tril(jnp.ones((C, C), dtype=bool))[:, :, None]

    def step(state, inp):
        q_, k_, v_, a_ = inp
        bcum = jnp.cumsum(a_, axis=2)
        o_inter = jnp.einsum('bhcd,bhde->bhce', q_ * jnp.exp(bcum), state)
        diff = bcum[:, :, :, None, :] - bcum[:, :, None, :, :]
        decay = jnp.exp(jnp.where(causal, diff, NEG))
        att = jnp.einsum('bhid,bhjd,bhijd->bhij', q_, k_, decay)
        o_intra = jnp.einsum('bhij,bhje->bhie', att, v_)
        blast = bcum[:, :, -1:, :]
        state = state * jnp.exp(blast)[:, :, 0, :, None] + jnp.einsum('bhjd,bhje->bhde', k_ * jnp.exp(blast - bcum), v_)
        return state, o_inter + o_intra

    state0 = jnp.zeros((B, GLA_HEADS, GLA_DK, GLA_DV), jnp.float32)
    _, o = lax.scan(step, state0, (qc, kc, vc, ac))
    o = o.transpose(1, 0, 3, 2, 4).reshape(B, S, GLA_HEADS, GLA_DV)
    o = rmsnorm(o, norm_g).reshape(B, S, GLA_V).astype(r_raw.dtype)
    return o * jax.nn.silu(r_raw)


def token_mixer(u, positions, w_in, dw_w, dw_b, ln_g, ln_b, w_conv_up, cmp_pos, cmp_w1, cmp_w2, w_nsa_up,
                gla_gate_w, gla_gate_b, gla_norm_g, w_gla_up, w_out):
    B, S, D = u.shape
    z = pmm3(u, w_in)
    a_conv, nq, nkv, ng, gq, gk, gv, gr, glr, mg = jnp.split(z, np.cumsum(SPLITS)[:-1].tolist(), axis=-1)
    y_conv = conformer_conv(a_conv, dw_w, dw_b, ln_g, ln_b, w_conv_up)
    y_nsa = pmm3(nsa_attention(nq, nkv, ng, positions, cmp_pos, cmp_w1, cmp_w2), w_nsa_up)
    y_gla = pmm3(gla_attention(gq, gk, gv, gr, glr, gla_gate_w, gla_gate_b, gla_norm_g), w_gla_up)
    gm = jax.nn.sigmoid(mg).reshape(B, S, N_BRANCH, D)
    y = gm[:, :, 0] * y_conv + gm[:, :, 1] * y_nsa + gm[:, :, 2] * y_gla
    return pmm3(y, w_out)


def peer(h, wq, keys, u, v):
    B, S, D = h.shape
    T = PEER_TBLK
    K = PEER_TOPK
    q_all = pmm3(h, wq).reshape(B * S // T, T, PEER_HEADS, 2, PEER_DKEY // 2)
    hb = h.reshape(B * S // T, T, D)

    def blk(inp):
        xb, qh = inp
        s = jnp.einsum('thpd,hpnd->thpn', qh, keys).astype(jnp.float32)
        s1, i1 = lax.top_k(s[:, :, 0], K)
        s2, i2 = lax.top_k(s[:, :, 1], K)
        cand = (s1[..., :, None] + s2[..., None, :]).reshape(T, PEER_HEADS, K * K)
        cidx = (i1[..., :, None] * PEER_NKEYS + i2[..., None, :]).reshape(T, PEER_HEADS, K * K)
        top, pos = lax.top_k(cand, K)
        eidx = jnp.take_along_axis(cidx, pos, axis=-1)
        g = jax.nn.softmax(top, axis=-1)
        ue = u[eidx]
        ve = v[eidx]
        act = jax.nn.gelu(jnp.einsum('td,thkd->thk', xb, ue).astype(jnp.float32))
        return jnp.einsum('thk,thkd->td', (g * act).astype(ve.dtype), ve)

    return lax.map(blk, (hb, q_all)).reshape(B, S, D)


def kernel(x, c, positions, ada_w, ada_b, norm_g, w_in, conv_dw_w, conv_dw_b, conv_ln_g, conv_ln_b, w_conv_up,
           cmp_pos, cmp_w1, cmp_w2, w_nsa_up, gla_gate_w, gla_gate_b, gla_norm_g, w_gla_up, w_out,
           peer_wq, peer_keys, peer_u, peer_v, final_g):
    for l in range(DEPTH):
        shift, scale, gate = adaln(c, ada_w[l, 0], ada_b[l, 0])
        u = rmsnorm(x, norm_g[l, 0]) * (1.0 + scale) + shift
        y = token_mixer(u, positions, w_in[l], conv_dw_w[l], conv_dw_b[l], conv_ln_g[l], conv_ln_b[l], w_conv_up[l],
                        cmp_pos[l], cmp_w1[l], cmp_w2[l], w_nsa_up[l], gla_gate_w[l], gla_gate_b[l], gla_norm_g[l],
                        w_gla_up[l], w_out[l])
        x = x + gate * y
        shift, scale, gate = adaln(c, ada_w[l, 1], ada_b[l, 1])
        u = rmsnorm(x, norm_g[l, 1]) * (1.0 + scale) + shift
        x = x + gate * peer(u, peer_wq[l], peer_keys[l], peer_u[l], peer_v[l])
    return rmsnorm(x, final_g)
```

```python
import functools

import jax
import jax.numpy as jnp
import numpy as np
from jax import lax
from jax.experimental import pallas as pl
from jax.experimental.pallas import tpu as pltpu

D_MODEL = 1024
DEPTH = 2
EPS = 1e-6
NEG = -1e30
N_BRANCH = 3
CONV_DIM = 512
CONV_WIDTH = 31
NSA_HEADS = 8
NSA_KV_HEADS = 2
NSA_GROUP = NSA_HEADS // NSA_KV_HEADS
HEAD_DIM = 64
CMP_BLOCK = 32
CMP_STRIDE = 16
CMP_HIDDEN = 256
SLC_BLOCK = 64
SLC_TOPK = 16
WINDOW = 512
FORCE_BONUS = 1e4
ROPE_THETA = 10000.0
GLA_HEADS = 4
GLA_DK = 64
GLA_DV = 128
GLA_GATE_RANK = 16
GLA_GATE_TEMP = 16.0
GLA_CHUNK = 64
PEER_HEADS = 8
PEER_NKEYS = 128
PEER_DKEY = 256
PEER_TOPK = 16
NSA_Q = NSA_HEADS * HEAD_DIM
NSA_KV = 3 * 2 * NSA_KV_HEADS * HEAD_DIM
NSA_G = 3 * NSA_HEADS
GLA_QK = GLA_HEADS * GLA_DK
GLA_V = GLA_HEADS * GLA_DV
SPLITS = [2 * CONV_DIM, NSA_Q, NSA_KV, NSA_G, GLA_QK, GLA_QK, GLA_V, GLA_V, GLA_GATE_RANK, N_BRANCH * D_MODEL]

LANE = 128
TQ = 128
KC = 512
GW = NSA_GROUP * HEAD_DIM
VMEM_LIMIT = 48 * 1024 * 1024


def _mm_kernel(a_ref, b_ref, o_ref):
    o_ref[...] = jnp.dot(a_ref[...].astype(jnp.bfloat16), b_ref[...].astype(jnp.bfloat16),
                         preferred_element_type=jnp.float32)


def _pick(n, cands):
    for c in cands:
        if n % c == 0:
            return c
    return n


def pmm(a, b):
    m, k = a.shape
    n = b.shape[1]
    n_pad = -n % LANE
    if n_pad:
        b = jnp.pad(b, ((0, 0), (0, n_pad)))
    m_pad = -m % 8
    if m_pad:
        a = jnp.pad(a, ((0, m_pad), (0, 0)))
    mp, np_ = m + m_pad, n + n_pad
    tm = _pick(mp, (512, 256, 128, 64, 32, 16, 8))
    tn = _pick(np_, (512, 640, 384, 256, 128))
    out = pl.pallas_call(
        _mm_kernel,
        grid=(mp // tm, np_ // tn),
        in_specs=[pl.BlockSpec((tm, k), lambda i, j: (i, 0)),
                  pl.BlockSpec((k, tn), lambda i, j: (0, j))],
        out_specs=pl.BlockSpec((tm, tn), lambda i, j: (i, j)),
        out_shape=jax.ShapeDtypeStruct((mp, np_), jnp.float32),
        compiler_params=pltpu.CompilerParams(dimension_semantics=("parallel", "parallel")),
        name="pmm",
    )(a, b)
    return out[:m, :n]


def pmm3(a, b):
    lead = a.shape[:-1]
    return pmm(a.reshape(-1, a.shape[-1]), b).reshape(*lead, b.shape[1])


def rmsnorm(x, g):
    xf = x.astype(jnp.float32)
    y = xf * lax.rsqrt(jnp.mean(jnp.square(xf), -1, keepdims=True) + EPS) * g
    return y.astype(x.dtype)


def layernorm(x, g, b):
    xf = x.astype(jnp.float32)
    mu = jnp.mean(xf, -1, keepdims=True)
    var = jnp.mean(jnp.square(xf - mu), -1, keepdims=True)
    return ((xf - mu) * lax.rsqrt(var + EPS) * g + b).astype(x.dtype)


def rope(x, pos):
    half = x.shape[-1] // 2
    freq = ROPE_THETA ** (-jnp.arange(half, dtype=jnp.float32) / half)
    ang = pos.astype(jnp.float32)[..., None] * freq
    cos = jnp.cos(ang)[:, :, None, :]
    sin = jnp.sin(ang)[:, :, None, :]
    x1 = x[..., :half].astype(jnp.float32)
    x2 = x[..., half:].astype(jnp.float32)
    return jnp.concatenate([x1 * cos - x2 * sin, x2 * cos + x1 * sin], -1).astype(x.dtype)


def adaln(c, w, b):
    mod = pmm(jax.nn.silu(c), w) + b
    shift, scale, gate = jnp.split(mod[:, None, :], 3, axis=-1)
    return shift, scale, gate


def conformer_conv(a, dw_w, dw_b, ln_g, ln_b, w_up):
    h = a[..., :CONV_DIM] * jax.nn.sigmoid(a[..., CONV_DIM:])
    h = lax.conv_general_dilated(h, dw_w[:, None, :], window_strides=(1,), padding=[(CONV_WIDTH - 1, 0)],
                                 dimension_numbers=('NWC', 'WIO', 'NWC'), feature_group_count=CONV_DIM) + dw_b
    h = jax.nn.silu(layernorm(h, ln_g, ln_b))
    return pmm3(h, w_up)


def compress(x, cidx, pos_emb, w1, w2):
    B, _, hk, dh = x.shape
    blocks = x[:, cidx] + pos_emb[None, None, :, None, :]
    flat = blocks.transpose(0, 1, 3, 2, 4).reshape(B, cidx.shape[0], hk, CMP_BLOCK * dh)
    return pmm3(jax.nn.gelu(pmm3(flat, w1)), w2)


def _msoftmax(s, mask):
    s = jnp.where(mask, s, NEG)
    m = jnp.max(s, axis=1, keepdims=True)
    e = jnp.where(mask, jnp.exp(s - m), 0.0)
    l = jnp.sum(e, axis=1, keepdims=True)
    return e / jnp.where(l > 0.0, l, 1.0)


def _nsa_kernel(q_ref, g_ref, kct_ref, vc_ref, kst_ref, vs_ref, kwt_ref, vw_ref, c2st_ref, o_ref, *, n_sel):
    qb = pl.program_id(2)
    t0 = pl.multiple_of(qb * TQ, TQ)
    q = q_ref[...]
    lane_g = lax.broadcasted_iota(jnp.int32, (TQ, GW), 1) // HEAD_DIM
    qm = [jnp.where(lane_g == h, q, jnp.zeros_like(q)) for h in range(NSA_GROUP)]
    tpos = t0 + lax.broadcasted_iota(jnp.int32, (TQ, 1), 0)

    kct = kct_ref[0, 0]
    vc = vc_ref[0, 0]
    ncp = kct.shape[1]
    ncol = lax.broadcasted_iota(jnp.int32, (TQ, ncp), 1)
    maskc = (ncol * CMP_STRIDE + (CMP_BLOCK - 1)) <= tpos
    pc_sum = jnp.zeros((TQ, ncp), jnp.float32)
    oc = []
    for h in range(NSA_GROUP):
        p = _msoftmax(jnp.dot(qm[h], kct, preferred_element_type=jnp.float32), maskc)
        pc_sum = pc_sum + p
        oc.append(jnp.dot(p.astype(jnp.bfloat16), vc, preferred_element_type=jnp.float32))

    c2st = c2st_ref[...]
    jp = c2st.shape[0]
    hi = pc_sum.astype(jnp.bfloat16)
    lo = (pc_sum - hi.astype(jnp.float32)).astype(jnp.bfloat16)
    dn = (((1,), (1,)), ((), ()))
    imp_t = (lax.dot_general(c2st, hi, dn, preferred_element_type=jnp.float32)
             + lax.dot_general(c2st, lo, dn, preferred_element_type=jnp.float32))
    jrow = lax.broadcasted_iota(jnp.int32, (jp, TQ), 0)
    cur = (t0 + lax.broadcasted_iota(jnp.int32, (jp, TQ), 1)) // SLC_BLOCK
    forced = (jrow == 0) | (jrow == cur) | (jrow == cur - 1)
    score = jnp.where(jrow <= cur, imp_t + jnp.where(forced, FORCE_BONUS, 0.0), NEG)
    rank = jnp.zeros((jp, TQ), jnp.int32)
    nblk = kst_ref.shape[3] // SLC_BLOCK
    for i in range(nblk):
        si = score[i:i + 1, :]
        beats = (si > score) | ((si == score) & (jrow > i))
        rank = rank + beats.astype(jnp.int32)
    sel_t = ((rank < n_sel) & (score > 0.5 * NEG)).astype(jnp.float32)
    selb = sel_t.T.astype(jnp.bfloat16)

    nchunks = (t0 + TQ + KC - 1) // KC

    def chunk(c, carry):
        ms, ls, accs = carry
        k0 = pl.multiple_of(c * KC, KC)
        kt = kst_ref[0, 0, :, pl.ds(k0, KC)]
        v = vs_ref[0, 0, pl.ds(k0, KC), :]
        kblk = (k0 + lax.broadcasted_iota(jnp.int32, (jp, KC), 1)) // SLC_BLOCK
        expand = (kblk == lax.broadcasted_iota(jnp.int32, (jp, KC), 0)).astype(jnp.bfloat16)
        member = jnp.dot(selb, expand, preferred_element_type=jnp.float32) > 0.5
        kpos = k0 + lax.broadcasted_iota(jnp.int32, (TQ, KC), 1)
        msk = member & (kpos <= tpos)
        nm, nl, na = [], [], []
        for h in range(NSA_GROUP):
            s = jnp.where(msk, jnp.dot(qm[h], kt, preferred_element_type=jnp.float32), NEG)
            m_new = jnp.maximum(ms[h], jnp.max(s, axis=1, keepdims=True))
            alpha = jnp.exp(ms[h] - m_new)
            p = jnp.where(msk, jnp.exp(s - m_new), 0.0)
            nl.append(alpha * ls[h] + jnp.sum(p, axis=1, keepdims=True))
            na.append(alpha * accs[h] + jnp.dot(p.astype(jnp.bfloat16), v, preferred_element_type=jnp.float32))
            nm.append(m_new)
        return tuple(nm), tuple(nl), tuple(na)

    init = (tuple(jnp.full((TQ, 1), NEG, jnp.float32) for _ in range(NSA_GROUP)),
            tuple(jnp.zeros((TQ, 1), jnp.float32) for _ in range(NSA_GROUP)),
            tuple(jnp.zeros((TQ, GW), jnp.float32) for _ in range(NSA_GROUP)))
    _, ls, accs = lax.fori_loop(0, nchunks, chunk, init)

    wk = WINDOW + TQ
    kwt = kwt_ref[0, 0, :, pl.ds(t0, wk)]
    vw = vw_ref[0, 0, pl.ds(t0, wk), :]
    kposw = t0 - WINDOW + lax.broadcasted_iota(jnp.int32, (TQ, wk), 1)
    diff = tpos - kposw
    maskw = (kposw >= 0) & (diff >= 0) & (diff < WINDOW)

    sig = jax.nn.sigmoid(g_ref[0])
    out = jnp.zeros((TQ, GW), jnp.float32)
    for h in range(NSA_GROUP):
        pw = _msoftmax(jnp.dot(qm[h], kwt, preferred_element_type=jnp.float32), maskw)
        ow = jnp.dot(pw.astype(jnp.bfloat16), vw, preferred_element_type=jnp.float32)
        os_ = accs[h] / jnp.where(ls[h] > 0.0, ls[h], 1.0)
        o_h = (sig[:, 3 * h:3 * h + 1] * oc[h] + sig[:, 3 * h + 1:3 * h + 2] * os_
               + sig[:, 3 * h + 2:3 * h + 3] * ow)
        out = out + jnp.where(lane_g == h, o_h, 0.0)
    o_ref[...] = out


def _tile_t(k):
    return jnp.tile(k, (1, 1, 1, NSA_GROUP)).transpose(0, 2, 3, 1).astype(jnp.bfloat16)


def _tile_v(v):
    return jnp.tile(v, (1, 1, 1, NSA_GROUP)).transpose(0, 2, 1, 3).astype(jnp.bfloat16)


def nsa_core(q, gate_raw, k_cmp, v_cmp, k_slc, v_slc, k_win, v_win):
    B, S = q.shape[:2]
    n = B * S
    ncmp = k_cmp.shape[1]
    nslc = S // SLC_BLOCK
    n_sel = min(SLC_TOPK, nslc)
    ncp = -(-max(ncmp, 1) // LANE) * LANE
    jp = LANE
    assert nslc <= jp and S % KC == 0 and S % TQ == 0
    qs = (q.reshape(n, NSA_KV_HEADS * GW) * (HEAD_DIM ** -0.5)).astype(jnp.bfloat16)
    g = jnp.pad(gate_raw.reshape(n, NSA_KV_HEADS, NSA_GROUP * 3).transpose(1, 0, 2),
                ((0, 0), (0, 0), (0, LANE - NSA_GROUP * 3)))
    padc = ((0, 0), (0, ncp - ncmp), (0, 0), (0, 0))
    kct = _tile_t(jnp.pad(k_cmp, padc))
    vc = _tile_v(jnp.pad(v_cmp, padc))
    kst = _tile_t(k_slc)
    vs = _tile_v(v_slc)
    padw = ((0, 0), (WINDOW, 0), (0, 0), (0, 0))
    kwt = _tile_t(jnp.pad(k_win, padw))
    vw = _tile_v(jnp.pad(v_win, padw))
    cs = np.arange(ncmp) * CMP_STRIDE
    ss = np.arange(nslc) * SLC_BLOCK
    ov = np.minimum(cs[:, None] + CMP_BLOCK, ss[None, :] + SLC_BLOCK) - np.maximum(cs[:, None], ss[None, :])
    c2s = np.zeros((ncp, jp), np.float32)
    c2s[:ncmp, :nslc] = np.clip(ov, 0, None) / CMP_BLOCK
    c2st = jnp.asarray(c2s.T, dtype=jnp.bfloat16)
    nqb = S // TQ

    def kv_spec(shp):
        return pl.BlockSpec((1, 1) + shp, lambda b, k, i: (b, k, 0, 0))

    return pl.pallas_call(
        functools.partial(_nsa_kernel, n_sel=n_sel),
        grid=(B, NSA_KV_HEADS, nqb),
        in_specs=[pl.BlockSpec((TQ, GW), lambda b, k, i: (b * nqb + i, k)),
                  pl.BlockSpec((1, TQ, LANE), lambda b, k, i: (k, b * nqb + i, 0)),
                  kv_spec((GW, ncp)), kv_spec((ncp, GW)),
                  kv_spec((GW, S)), kv_spec((S, GW)),
                  kv_spec((GW, S + WINDOW)), kv_spec((S + WINDOW, GW)),
                  pl.BlockSpec((jp, ncp), lambda b, k, i: (0, 0))],
        out_specs=pl.BlockSpec((TQ, GW), lambda b, k, i: (b * nqb + i, k)),
        out_shape=jax.ShapeDtypeStruct((n, NSA_KV_HEADS * GW), jnp.float32),
        compiler_params=pltpu.CompilerParams(dimension_semantics=("parallel", "parallel", "arbitrary"),
                                             vmem_limit_bytes=VMEM_LIMIT),
        name="nsa_core",
    )(qs, g, kct, vc, kst, vs, kwt, vw, c2st)


def nsa_attention(q_raw, kv_raw, gate_raw, positions, cmp_pos, cmp_w1, cmp_w2):
    B, S, _ = q_raw.shape
    q = rope(q_raw.reshape(B, S, NSA_HEADS, HEAD_DIM), positions)
    kv = kv_raw.reshape(B, S, 3, 2, NSA_KV_HEADS, HEAD_DIM)
    ncmp = (S - CMP_BLOCK) // CMP_STRIDE + 1
    cidx = np.arange(ncmp)[:, None] * CMP_STRIDE + np.arange(CMP_BLOCK)[None, :]
    cmp_end = cidx[:, -1]
    k_cmp = rope(compress(kv[:, :, 0, 0], cidx, cmp_pos[0], cmp_w1[0], cmp_w2[0]), positions[:, cmp_end])
    v_cmp = compress(kv[:, :, 0, 1], cidx, cmp_pos[1], cmp_w1[1], cmp_w2[1])
    o = nsa_core(q, gate_raw, k_cmp, v_cmp, rope(kv[:, :, 1, 0], positions), kv[:, :, 1, 1],
                 rope(kv[:, :, 2, 0], positions), kv[:, :, 2, 1])
    return o.reshape(B, S, NSA_HEADS * HEAD_DIM)


def gla_attention(q_raw, k_raw, v_raw, r_raw, glr, gate_w, gate_b, norm_g):
    B, S, _ = q_raw.shape
    C = GLA_CHUNK
    nC = S // C

    def chunks(t, d):
        return t.astype(jnp.float32).reshape(B, nC, C, GLA_HEADS, d).transpose(1, 0, 3, 2, 4)

    log_a = jax.nn.log_sigmoid((glr @ gate_w + gate_b).astype(jnp.float32)) / GLA_GATE_TEMP
    qc = chunks(q_raw, GLA_DK) * (GLA_DK ** -0.5)
    kc = chunks(k_raw, GLA_DK)
    vc = chunks(v_raw, GLA_DV)
    ac = chunks(log_a, GLA_DK)
    causal = jnp.tril(jnp.ones((C, C), dtype=bool))[:, :, None]

    def step(state, inp):
        q_, k_, v_, a_ = inp
        bcum = jnp.cumsum(a_, axis=2)
        o_inter = jnp.einsum('bhcd,bhde->bhce', q_ * jnp.exp(bcum), state)
        diff = bcum[:, :, :, None, :] - bcum[:, :, None, :, :]
        decay = jnp.exp(jnp.where(causal, diff, NEG))
        att = jnp.einsum('bhid,bhjd,bhijd->bhij', q_, k_, decay)
        o_intra = jnp.einsum('bhij,bhje->bhie', att, v_)
        blast = bcum[:, :, -1:, :]
        state = state * jnp.exp(blast)[:, :, 0, :, None] + jnp.einsum('bhjd,bhje->bhde', k_ * jnp.exp(blast - bcum), v_)
        return state, o_inter + o_intra

    state0 = jnp.zeros((B, GLA_HEADS, GLA_DK, GLA_DV), jnp.float32)
    _, o = lax.scan(step, state0, (qc, kc, vc, ac))
    o = o.transpose(1, 0, 3, 2, 4).reshape(B, S, GLA_HEADS, GLA_DV)
    o = rmsnorm(o, norm_g).reshape(B, S, GLA_V).astype(r_raw.dtype)
    return o * jax.nn.silu(r_raw)


def token_mixer(u, positions, w_in, dw_w, dw_b, ln_g, ln_b, w_conv_up, cmp_pos, cmp_w1, cmp_w2, w_nsa_up,
                gla_gate_w, gla_gate_b, gla_norm_g, w_gla_up, w_out):
    B, S, D = u.shape
    z = pmm3(u, w_in)
    a_conv, nq, nkv, ng, gq, gk, gv, gr, glr, mg = jnp.split(z, np.cumsum(SPLITS)[:-1].tolist(), axis=-1)
    y_conv = conformer_conv(a_conv, dw_w, dw_b, ln_g, ln_b, w_conv_up)
    y_nsa = pmm3(nsa_attention(nq, nkv, ng, positions, cmp_pos, cmp_w1, cmp_w2), w_nsa_up)
    y_gla = pmm3(gla_attention(gq, gk, gv, gr, glr, gla_gate_w, gla_gate_b, gla_norm_g), w_gla_up)
    gm = jax.nn.sigmoid(mg).reshape(B, S, N_BRANCH, D)
    y = gm[:, :, 0] * y_conv + gm[:, :, 1] * y_nsa + gm[:, :, 2] * y_gla
    return pmm3(y, w_out)


_FULL_BLOCKS = 8


def _extract_top(s, k):
    n, t = s.shape
    row = lax.broadcasted_iota(jnp.int32, (n, t), 0)
    rank = jnp.full((n, t), k, jnp.int32)
    vals = []
    for r in range(k):
        m = jnp.max(s, axis=0, keepdims=True)
        first = jnp.min(jnp.where(s == m, row, n), axis=0, keepdims=True)
        hit = row == first
        rank = jnp.where(hit, r, rank)
        s = jnp.where(hit, float('-inf'), s)
        vals.append(m)
    return jnp.concatenate(vals, axis=0), rank


def _peer_gate_kernel(q_ref, keys_ref, e1_ref, a_ref, r2_ref, e2_ref):
    K = PEER_TOPK
    q = q_ref[...].astype(jnp.bfloat16)
    k1 = keys_ref[0, 0].astype(jnp.bfloat16)
    k2 = keys_ref[0, 1].astype(jnp.bfloat16)
    dn = (((1,), (1,)), ((), ()))
    s1 = lax.dot_general(k1, q[:, :PEER_DKEY // 2], dn, preferred_element_type=jnp.float32)
    s2 = lax.dot_general(k2, q[:, PEER_DKEY // 2:], dn, preferred_element_type=jnp.float32)
    t = s1.shape[1]
    v1, rank1 = _extract_top(s1, K)
    v2, rank2 = _extract_top(s2, K)
    blocks, poss = [], []
    row16 = lax.broadcasted_iota(jnp.int32, (K, t), 0)
    for r1 in range(_FULL_BLOCKS):
        c = v1[r1:r1 + 1, :] + v2
        c = jnp.where(row16 < K // (r1 + 1), c, float('-inf'))
        blocks.append(c)
        poss.append(row16 + r1 * K)
    row8 = lax.broadcasted_iota(jnp.int32, (K - _FULL_BLOCKS, t), 0)
    blocks.append(v1[_FULL_BLOCKS:, :] + v2[0:1, :])
    poss.append((row8 + _FULL_BLOCKS) * K)
    sels = [jnp.zeros(b.shape, jnp.bool_) for b in blocks]
    big = K * K
    for _ in range(K):
        m = blocks[0].max(axis=0, keepdims=True)
        for b in blocks[1:]:
            m = jnp.maximum(m, b.max(axis=0, keepdims=True))
        first = None
        for b, p in zip(blocks, poss):
            f = jnp.min(jnp.where(b == m, p, big), axis=0, keepdims=True)
            first = f if first is None else jnp.minimum(first, f)
        for i in range(len(blocks)):
            hit = poss[i] == first
            sels[i] = sels[i] | hit
            blocks[i] = jnp.where(hit, float('-inf'), blocks[i])
    top = v1[0:1, :] + v2[0:1, :]
    cnt_rows = []
    z = jnp.zeros((1, t), jnp.float32)
    for r1 in range(_FULL_BLOCKS):
        sel = sels[r1]
        cnt_rows.append(jnp.sum(sel.astype(jnp.float32), axis=0, keepdims=True))
        c = v1[r1:r1 + 1, :] + v2
        z = z + jnp.sum(jnp.where(sel, jnp.exp(c - top), 0.0), axis=0, keepdims=True)
    sel = sels[_FULL_BLOCKS]
    cnt_tail = sel.astype(jnp.float32)
    c = v1[_FULL_BLOCKS:, :] + v2[0:1, :]
    z = z + jnp.sum(jnp.where(sel, jnp.exp(c - top), 0.0), axis=0, keepdims=True)
    cnt = jnp.concatenate(cnt_rows + [cnt_tail], axis=0)
    a = jnp.zeros(s1.shape, jnp.float32)
    for r in range(K):
        a = jnp.where(rank1 == r, cnt[r:r + 1, :], a)
    e1_ref[0] = jnp.exp(s1 - v1[0:1, :])
    a_ref[0] = a
    r2_ref[0] = rank2.astype(jnp.float32).astype(jnp.bfloat16)
    e2_ref[0] = (jnp.exp(s2 - v2[0:1, :]) / z).astype(jnp.bfloat16)


def peer_gates(qh, keys, tt=256):
    n = qh.shape[0]
    H = PEER_HEADS
    shp32 = jax.ShapeDtypeStruct((H, PEER_NKEYS, n), jnp.float32)
    shp16 = jax.ShapeDtypeStruct((H, PEER_NKEYS, n), jnp.bfloat16)
    ospec = pl.BlockSpec((1, PEER_NKEYS, tt), lambda i, h: (h, 0, i))
    return pl.pallas_call(
        _peer_gate_kernel,
        grid=(n // tt, H),
        in_specs=[pl.BlockSpec((tt, PEER_DKEY), lambda i, h: (i, h)),
                  pl.BlockSpec((1, 2, PEER_NKEYS, PEER_DKEY // 2), lambda i, h: (h, 0, 0, 0))],
        out_specs=[ospec, ospec, ospec, ospec],
        out_shape=[shp32, shp32, shp16, shp16],
        compiler_params=pltpu.CompilerParams(dimension_semantics=("parallel", "parallel")),
        name="peer_gates",
    )(qh, keys)


def _gelu(x):
    return 0.5 * x * (1.0 + jnp.tanh(0.7978845608028654 * (x + 0.044715 * x * x * x)))


def _peer_dense_kernel(xt_ref, u_ref, vt_ref, e1_ref, a_ref, r2_ref, e2_ref, o_ref, acc_ref, w_ref, *, ec):
    j = pl.program_id(1)
    nsub = ec // PEER_NKEYS

    @pl.when(j == 0)
    def _():
        acc_ref[...] = jnp.zeros_like(acc_ref)

    def sub(i, carry):
        a_idx = j * nsub + i
        r0 = pl.multiple_of(i * PEER_NKEYS, PEER_NKEYS)
        st = jnp.dot(u_ref[pl.ds(r0, PEER_NKEYS), :], xt_ref[...], preferred_element_type=jnp.float32)
        act = _gelu(st).astype(jnp.bfloat16)
        g = None
        for h in range(PEER_HEADS):
            cnt = a_ref[h, pl.ds(a_idx, 1), :].astype(jnp.bfloat16)
            e1 = e1_ref[h, pl.ds(a_idx, 1), :].astype(jnp.bfloat16)
            term = jnp.where(r2_ref[h] < cnt, e2_ref[h], jnp.bfloat16(0)) * e1
            g = term if g is None else g + term
        w_ref[pl.ds(r0, PEER_NKEYS), :] = (act * g).astype(jnp.bfloat16)
        return carry

    lax.fori_loop(0, nsub, sub, 0)
    acc_ref[...] += jnp.dot(vt_ref[...], w_ref[...], preferred_element_type=jnp.float32)

    @pl.when(j == pl.num_programs(1) - 1)
    def _():
        o_ref[...] = acc_ref[...]


def peer_dense(xt, u16, vt16, e1, a, r2, e2, tt=512, ec=1024):
    d, n = xt.shape
    ne = u16.shape[0]
    H = PEER_HEADS
    gspec = pl.BlockSpec((H, PEER_NKEYS, tt), lambda i, j: (0, 0, i))
    return pl.pallas_call(
        functools.partial(_peer_dense_kernel, ec=ec),
        grid=(n // tt, ne // ec),
        in_specs=[pl.BlockSpec((d, tt), lambda i, j: (0, i)),
                  pl.BlockSpec((ec, d), lambda i, j: (j, 0)),
                  pl.BlockSpec((d, ec), lambda i, j: (0, j)),
                  gspec, gspec, gspec, gspec],
        out_specs=pl.BlockSpec((d, tt), lambda i, j: (0, i)),
        out_shape=jax.ShapeDtypeStruct((d, n), jnp.float32),
        scratch_shapes=[pltpu.VMEM((d, tt), jnp.float32), pltpu.VMEM((ec, tt), jnp.bfloat16)],
        compiler_params=pltpu.CompilerParams(dimension_semantics=("parallel", "arbitrary"),
                                             vmem_limit_bytes=VMEM_LIMIT),
        name="peer_dense",
    )(xt, u16, vt16, e1, a, r2, e2)


def peer(h, wq, keys, u, v):
    B, S, D = h.shape
    h2 = h.reshape(B * S, D)
    e1, a, r2, e2 = peer_gates(pmm(h2, wq), keys)
    xt = h2.astype(jnp.bfloat16).T
    out_t = peer_dense(xt, u.astype(jnp.bfloat16), v.astype(jnp.bfloat16).T, e1, a, r2, e2)
    return out_t.T.reshape(B, S, D)


def kernel(x, c, positions, ada_w, ada_b, norm_g, w_in, conv_dw_w, conv_dw_b, conv_ln_g, conv_ln_b, w_conv_up,
           cmp_pos, cmp_w1, cmp_w2, w_nsa_up, gla_gate_w, gla_gate_b, gla_norm_g, w_gla_up, w_out,
           peer_wq, peer_keys, peer_u, peer_v, final_g):
    for l in range(DEPTH):
        shift, scale, gate = adaln(c, ada_w[l, 0], ada_b[l, 0])
        u = rmsnorm(x, norm_g[l, 0]) * (1.0 + scale) + shift
        y = token_mixer(u, positions, w_in[l], conv_dw_w[l], conv_dw_b[l], conv_ln_g[l], conv_ln_b[l], w_conv_up[l],
                        cmp_pos[l], cmp_w1[l], cmp_w2[l], w_nsa_up[l], gla_gate_w[l], gla_gate_b[l], gla_norm_g[l],
                        w_gla_up[l], w_out[l])
        x = x + gate * y
        shift, scale, gate = adaln(c, ada_w[l, 1], ada_b[l, 1])
        u = rmsnorm(x, norm_g[l, 1]) * (1.0 + scale) + shift
        x = x + gate * peer(u, peer_wq[l], peer_keys[l], peer_u[l], peer_v[l])
    return rmsnorm(x, final_g)
```

```python
import functools

import jax
import jax.numpy as jnp
import numpy as np
from jax import lax
from jax.experimental import pallas as pl
from jax.experimental.pallas import tpu as pltpu

D_MODEL = 1024
DEPTH = 2
EPS = 1e-6
NEG = -1e30
N_BRANCH = 3
CONV_DIM = 512
CONV_WIDTH = 31
NSA_HEADS = 8
NSA_KV_HEADS = 2
NSA_GROUP = NSA_HEADS // NSA_KV_HEADS
HEAD_DIM = 64
CMP_BLOCK = 32
CMP_STRIDE = 16
CMP_HIDDEN = 256
SLC_BLOCK = 64
SLC_TOPK = 16
WINDOW = 512
FORCE_BONUS = 1e4
ROPE_THETA = 10000.0
GLA_HEADS = 4
GLA_DK = 64
GLA_DV = 128
GLA_GATE_RANK = 16
GLA_GATE_TEMP = 16.0
GLA_CHUNK = 64
PEER_HEADS = 8
PEER_NKEYS = 128
PEER_DKEY = 256
PEER_TOPK = 16
NSA_Q = NSA_HEADS * HEAD_DIM
NSA_KV = 3 * 2 * NSA_KV_HEADS * HEAD_DIM
NSA_G = 3 * NSA_HEADS
GLA_QK = GLA_HEADS * GLA_DK
GLA_V = GLA_HEADS * GLA_DV
SPLITS = [2 * CONV_DIM, NSA_Q, NSA_KV, NSA_G, GLA_QK, GLA_QK, GLA_V, GLA_V, GLA_GATE_RANK, N_BRANCH * D_MODEL]

LANE = 128
TQ = 128
KC = 512
GW = NSA_GROUP * HEAD_DIM
VMEM_LIMIT = 48 * 1024 * 1024
GLA_LEVELS = (32, 16, 8, 4, 2, 1)
GLA_TB = 512
CONV_TS = 512
CONV_HALO = 32
ROW_TILE = 512
MIX_TILE = 256

Z_MG = 0
Z_CONV = Z_MG + N_BRANCH * D_MODEL
Z_Q = Z_CONV + 2 * CONV_DIM
Z_QSW = Z_Q + NSA_Q
Z_GV = Z_QSW + NSA_Q
Z_GR = Z_GV + GLA_V
Z_GQ = Z_GR + GLA_V
Z_GK = Z_GQ + GLA_QK
Z_CMP = Z_GK + GLA_QK
Z_SLC = Z_CMP + 256
Z_WIN = Z_SLC + 384
Z_NG = Z_WIN + 384
Z_GLR = Z_NG + NSA_KV_HEADS * LANE
NZ = Z_GLR + LANE
assert Z_SLC % 384 == 0 and Z_WIN % 384 == 0 and Z_CONV % (2 * CONV_DIM) == 0 and Z_GQ % GLA_QK == 0


def _mm_kernel(a_ref, b_ref, o_ref):
    o_ref[...] = jnp.dot(a_ref[...].astype(jnp.bfloat16), b_ref[...].astype(jnp.bfloat16),
                         preferred_element_type=jnp.float32)


def _pick(n, cands):
    for c in cands:
        if n % c == 0:
            return c
    return n


def pmm(a, b):
    m, k = a.shape
    n = b.shape[1]
    n_pad = -n % LANE
    if n_pad:
        b = jnp.pad(b, ((0, 0), (0, n_pad)))
    m_pad = -m % 8
    if m_pad:
        a = jnp.pad(a, ((0, m_pad), (0, 0)))
    mp, np_ = m + m_pad, n + n_pad
    tm = _pick(mp, (512, 256, 128, 64, 32, 16, 8))
    tn = _pick(np_, (512, 640, 384, 256, 128))
    out = pl.pallas_call(
        _mm_kernel,
        grid=(mp // tm, np_ // tn),
        in_specs=[pl.BlockSpec((tm, k), lambda i, j: (i, 0)),
                  pl.BlockSpec((k, tn), lambda i, j: (0, j))],
        out_specs=pl.BlockSpec((tm, tn), lambda i, j: (i, j)),
        out_shape=jax.ShapeDtypeStruct((mp, np_), jnp.float32),
        compiler_params=pltpu.CompilerParams(dimension_semantics=("parallel", "parallel")),
        name="pmm",
    )(a, b)
    return out[:m, :n]


def _in_proj_kernel(x_ref, g_ref, sc_ref, sh_ref, w_ref, z_ref, u_ref):
    @pl.when(pl.program_id(1) == 0)
    def _():
        x = x_ref[...]
        y = x * lax.rsqrt(jnp.mean(x * x, axis=1, keepdims=True) + EPS) * g_ref[...]
        u_ref[...] = (y * sc_ref[0] + sh_ref[0]).astype(jnp.bfloat16)

    z_ref[...] = jnp.dot(u_ref[...], w_ref[...], preferred_element_type=jnp.float32)


def in_proj(x2, g, scale1p, shift, w16, S):
    n, d = x2.shape
    nz = w16.shape[1]
    tm = ROW_TILE
    tn = _pick(nz, (1152, 1024, 896, 512, 384, 256, 128))
    per_b = S // tm
    mod = lambda: pl.BlockSpec((1, 1, d), lambda i, j: (i // per_b, 0, 0))
    return pl.pallas_call(
        _in_proj_kernel,
        grid=(n // tm, nz // tn),
        in_specs=[pl.BlockSpec((tm, d), lambda i, j: (i, 0)),
                  pl.BlockSpec((1, d), lambda i, j: (0, 0)), mod(), mod(),
                  pl.BlockSpec((d, tn), lambda i, j: (0, j))],
        out_specs=pl.BlockSpec((tm, tn), lambda i, j: (i, j)),
        out_shape=jax.ShapeDtypeStruct((n, nz), jnp.float32),
        scratch_shapes=[pltpu.VMEM((tm, d), jnp.bfloat16)],
        compiler_params=pltpu.CompilerParams(dimension_semantics=("parallel", "arbitrary"),
                                             vmem_limit_bytes=VMEM_LIMIT),
        name="in_proj",
    )(x2, g.reshape(1, d), scale1p, shift, w16)


def _half_swap(n_heads):
    idx = np.arange(n_heads * HEAD_DIM).reshape(n_heads, 2, HEAD_DIM // 2)
    return idx[:, ::-1, :].reshape(-1)


def pack_w_in(w_in):
    o = np.cumsum([0] + SPLITS)
    a_conv, nq, nkv, ng, gq, gk, gv, gr, glr, mg = [w_in[:, o[i]:o[i + 1]] for i in range(len(SPLITS))]
    kvw = 2 * NSA_KV_HEADS * HEAD_DIM
    kw = NSA_KV_HEADS * HEAD_DIM
    d = w_in.shape[0]
    sw = _half_swap(NSA_KV_HEADS)
    cols = [mg, a_conv, nq, nq[:, _half_swap(NSA_HEADS)], gv, gr, gq, gk, nkv[:, 0:kvw]]
    for br in (1, 2):
        k = nkv[:, br * kvw:br * kvw + kw]
        cols += [k, k[:, sw], nkv[:, br * kvw + kw:(br + 1) * kvw]]
    gpk = NSA_GROUP * 3
    for kvh in range(NSA_KV_HEADS):
        cols += [ng[:, kvh * gpk:(kvh + 1) * gpk], jnp.zeros((d, LANE - gpk), w_in.dtype)]
    cols += [glr, jnp.zeros((d, LANE - GLA_GATE_RANK), w_in.dtype)]
    w = jnp.concatenate(cols, axis=1)
    assert w.shape[1] == NZ
    return w.astype(jnp.bfloat16)


def rope_tables(positions):
    half = HEAD_DIM // 2
    freq = ROPE_THETA ** (-jnp.arange(half, dtype=jnp.float32) / half)
    ang = positions.astype(jnp.float32)[..., None] * freq
    cos, sin = jnp.cos(ang), jnp.sin(ang)
    c = jnp.tile(jnp.concatenate([cos, cos], -1), (1, 1, LANE // HEAD_DIM))
    s = jnp.tile(jnp.concatenate([-sin, sin], -1), (1, 1, LANE // HEAD_DIM))
    return jnp.concatenate([c, s], -1).reshape(-1, 2 * LANE)


def _nsa_prep_kernel(q_ref, qs_ref, slc_ref, win_ref, cs_ref, sel_ref, qo_ref, kst_ref, vs_ref, kwt_ref, vw_ref):
    cos = cs_ref[:, :LANE]
    sin = cs_ref[:, LANE:]
    scale = HEAD_DIM ** -0.5
    for t in range(NSA_Q // LANE):
        sl = slice(t * LANE, (t + 1) * LANE)
        qo_ref[:, sl] = ((q_ref[:, sl] * cos + qs_ref[:, sl] * sin) * scale).astype(jnp.bfloat16)
    dn = (((1,), (1,)), ((), ()))
    for src, kt_ref, v_ref in ((slc_ref, kst_ref, vs_ref), (win_ref, kwt_ref, vw_ref)):
        k = (src[:, 0:LANE] * cos + src[:, LANE:2 * LANE] * sin).astype(jnp.bfloat16)
        v = src[:, 2 * LANE:3 * LANE].astype(jnp.bfloat16)
        for kvh in range(NSA_KV_HEADS):
            sel = sel_ref[kvh]
            kt_ref[0, kvh] = lax.dot_general(sel, k, dn, preferred_element_type=jnp.float32).astype(jnp.bfloat16)
            v_ref[0, kvh] = lax.dot_general(v, sel, dn, preferred_element_type=jnp.float32).astype(jnp.bfloat16)


def _head_repeat_sel():
    sel = np.zeros((NSA_KV_HEADS, GW, LANE), np.float32)
    for kvh in range(NSA_KV_HEADS):
        for r in range(GW):
            sel[kvh, r, kvh * HEAD_DIM + r % HEAD_DIM] = 1.0
    return jnp.asarray(sel, dtype=jnp.bfloat16)


def nsa_prep(z, cs, B, S):
    n = B * S
    tp = ROW_TILE
    nb = S // tp
    zb = lambda col, w: pl.BlockSpec((tp, w), lambda b, i: (b * nb + i, col // w))
    kt = lambda: pl.BlockSpec((1, NSA_KV_HEADS, GW, tp), lambda b, i: (b, 0, 0, i))
    vv = lambda: pl.BlockSpec((1, NSA_KV_HEADS, tp, GW), lambda b, i: (b, 0, i, 0))
    kt_shape = jax.ShapeDtypeStruct((B, NSA_KV_HEADS, GW, S), jnp.bfloat16)
    v_shape = jax.ShapeDtypeStruct((B, NSA_KV_HEADS, S, GW), jnp.bfloat16)
    return pl.pallas_call(
        _nsa_prep_kernel,
        grid=(B, nb),
        in_specs=[zb(Z_Q, NSA_Q), zb(Z_QSW, NSA_Q), zb(Z_SLC, 384), zb(Z_WIN, 384),
                  pl.BlockSpec((tp, 2 * LANE), lambda b, i: (b * nb + i, 0)),
                  pl.BlockSpec((NSA_KV_HEADS, GW, LANE), lambda b, i: (0, 0, 0))],
        out_specs=[pl.BlockSpec((tp, NSA_Q), lambda b, i: (b * nb + i, 0)), kt(), vv(), kt(), vv()],
        out_shape=[jax.ShapeDtypeStruct((n, NSA_Q), jnp.bfloat16), kt_shape, v_shape, kt_shape, v_shape],
        compiler_params=pltpu.CompilerParams(dimension_semantics=("parallel", "parallel")),
        name="nsa_prep",
    )(z, z, z, z, cs, _head_repeat_sel())


def _gelu(x):
    return 0.5 * x * (1.0 + jnp.tanh(0.7978845608028654 * (x + 0.044715 * x * x * x)))


def _compress_kernel(gk_ref, gv_ref, pe_ref, w1_ref, w2_ref, cs_ref, kct_ref, vc_ref):
    nrow = gk_ref.shape[2]
    outs = []
    for kv, g_ref in enumerate((gk_ref, gv_ref)):
        g = g_ref[0, 0]
        lo = jnp.dot((g + pe_ref[kv, 0]).astype(jnp.bfloat16), w1_ref[kv, 0], preferred_element_type=jnp.float32)
        hi = jnp.dot((g + pe_ref[kv, 1]).astype(jnp.bfloat16), w1_ref[kv, 1], preferred_element_type=jnp.float32)
        hid = _gelu(lo + pltpu.roll(hi, nrow - 1, 0)).astype(jnp.bfloat16)
        outs.append(jnp.dot(hid, w2_ref[kv], preferred_element_type=jnp.float32))
    k = outs[0][:, :GW] * cs_ref[0, :, :GW] + outs[0][:, GW:] * cs_ref[0, :, GW:]
    kct_ref[0, 0] = k.T.astype(jnp.bfloat16)
    vc_ref[0, 0] = outs[1][:, :GW].astype(jnp.bfloat16)


def nsa_compress(z, positions, cmp_pos, cmp_w1, cmp_w2, B, S):
    ng = S // CMP_STRIDE
    grp = CMP_STRIDE * HEAD_DIM
    c = z[:, Z_CMP:Z_CMP + 256].reshape(B, ng, CMP_STRIDE, 2, NSA_KV_HEADS, HEAD_DIM)
    g = c.transpose(3, 0, 4, 1, 2, 5).reshape(2, B, NSA_KV_HEADS, ng, grp)
    pe = cmp_pos.reshape(2, 2, 1, grp)
    w1 = cmp_w1.reshape(2, 2, grp, CMP_HIDDEN).astype(jnp.bfloat16)
    rep = jnp.tile(cmp_w2, (1, 1, NSA_GROUP))
    sw = np.tile(_half_swap(1), NSA_GROUP) + np.repeat(np.arange(NSA_GROUP) * HEAD_DIM, HEAD_DIM)
    w2 = jnp.concatenate([rep, rep[:, :, sw]], axis=-1).astype(jnp.bfloat16)
    end = jnp.minimum(jnp.arange(ng) * CMP_STRIDE + CMP_BLOCK - 1, S - 1)
    half = HEAD_DIM // 2
    freq = ROPE_THETA ** (-jnp.arange(half, dtype=jnp.float32) / half)
    ang = positions[:, end].astype(jnp.float32)[..., None] * freq
    cos, sin = jnp.cos(ang), jnp.sin(ang)
    cs = jnp.concatenate([jnp.tile(jnp.concatenate([cos, cos], -1), (1, 1, NSA_GROUP)),
                          jnp.tile(jnp.concatenate([-sin, sin], -1), (1, 1, NSA_GROUP))], -1)
    gspec = lambda: pl.BlockSpec((1, 1, ng, grp), lambda b, k: (b, k, 0, 0))
    return pl.pallas_call(
        _compress_kernel,
        grid=(B, NSA_KV_HEADS),
        in_specs=[gspec(), gspec(),
                  pl.BlockSpec((2, 2, 1, grp), lambda b, k: (0, 0, 0, 0)),
                  pl.BlockSpec((2, 2, grp, CMP_HIDDEN), lambda b, k: (0, 0, 0, 0)),
                  pl.BlockSpec((2, CMP_HIDDEN, 2 * GW), lambda b, k: (0, 0, 0)),
                  pl.BlockSpec((1, ng, 2 * GW), lambda b, k: (b, 0, 0))],
        out_specs=[pl.BlockSpec((1, 1, GW, ng), lambda b, k: (b, k, 0, 0)),
                   pl.BlockSpec((1, 1, ng, GW), lambda b, k: (b, k, 0, 0))],
        out_shape=[jax.ShapeDtypeStruct((B, NSA_KV_HEADS, GW, ng), jnp.bfloat16),
                   jax.ShapeDtypeStruct((B, NSA_KV_HEADS, ng, GW), jnp.bfloat16)],
        compiler_params=pltpu.CompilerParams(dimension_semantics=("parallel", "parallel")),
        name="nsa_compress",
    )(g[0], g[1], pe, w1, w2, cs)


def _mix_kernel(hc_ref, on_ref, og_ref, mg_ref, x_ref, gate_ref, wc_ref, wn_ref, wg_ref, wo_ref,
                g2_ref, sc_ref, sh_ref, wq_ref, x1_ref, ut_ref, qh_ref):
    d = D_MODEL
    f32 = jnp.float32
    y = (jax.nn.sigmoid(mg_ref[:, 0:d]) * jnp.dot(hc_ref[...], wc_ref[...], preferred_element_type=f32)
         + jax.nn.sigmoid(mg_ref[:, d:2 * d]) * jnp.dot(on_ref[...], wn_ref[...], preferred_element_type=f32)
         + jax.nn.sigmoid(mg_ref[:, 2 * d:3 * d]) * jnp.dot(og_ref[...], wg_ref[...], preferred_element_type=f32))
    x1 = x_ref[...] + gate_ref[0] * jnp.dot(y.astype(jnp.bfloat16), wo_ref[...], preferred_element_type=f32)
    x1_ref[...] = x1
    u = x1 * lax.rsqrt(jnp.mean(x1 * x1, axis=1, keepdims=True) + EPS) * g2_ref[...] * sc_ref[0] + sh_ref[0]
    ut_ref[...] = u.T.astype(jnp.bfloat16)
    qh_ref[...] = jnp.dot(u.astype(jnp.bfloat16), wq_ref[...], preferred_element_type=f32)


def mix_out(hc, on, og, z, x2, gate1, wc, wn, wg, wo, g2, scale2p, shift2, wq, S):
    n, d = x2.shape
    tm = MIX_TILE
    per_b = S // tm
    row = lambda w: pl.BlockSpec((tm, w), lambda i: (i, 0))
    mod = lambda: pl.BlockSpec((1, 1, d), lambda i: (i // per_b, 0, 0))
    full = lambda a: pl.BlockSpec(a.shape, lambda i: (0,) * a.ndim)
    nq = wq.shape[1]
    return pl.pallas_call(
        _mix_kernel,
        grid=(n // tm,),
        in_specs=[row(CONV_DIM), row(NSA_Q), row(GLA_V),
                  pl.BlockSpec((tm, N_BRANCH * d), lambda i: (i, Z_MG // (N_BRANCH * d))),
                  row(d), mod(), full(wc), full(wn), full(wg), full(wo),
                  pl.BlockSpec((1, d), lambda i: (0, 0)), mod(), mod(), full(wq)],
        out_specs=[row(d), pl.BlockSpec((d, tm), lambda i: (0, i)), row(nq)],
        out_shape=[jax.ShapeDtypeStruct((n, d), jnp.float32), jax.ShapeDtypeStruct((d, n), jnp.bfloat16),
                   jax.ShapeDtypeStruct((n, nq), jnp.float32)],
        compiler_params=pltpu.CompilerParams(dimension_semantics=("parallel",), vmem_limit_bytes=VMEM_LIMIT),
        name="mix_out",
    )(hc, on, og, z, x2, gate1, wc, wn, wg, wo, g2.reshape(1, d), scale2p, shift2, wq)


def _conv_kernel(a_ref, ah_ref, w_ref, b_ref, g_ref, beta_ref, o_ref, hbuf):
    i = pl.program_id(1)
    ts = a_ref.shape[0]
    a = a_ref[...]
    hbuf[pl.ds(CONV_HALO, ts), :] = a[:, :CONV_DIM] * jax.nn.sigmoid(a[:, CONV_DIM:])
    ah = ah_ref[...]
    halo = ah[:, :CONV_DIM] * jax.nn.sigmoid(ah[:, CONV_DIM:])
    hbuf[pl.ds(0, CONV_HALO), :] = jnp.where(i > 0, halo, 0.0)
    acc = jnp.zeros((ts, CONV_DIM), jnp.float32) + b_ref[...]
    off = CONV_HALO - (CONV_WIDTH - 1)
    for k in range(CONV_WIDTH):
        acc = acc + w_ref[k:k + 1, :] * hbuf[pl.ds(off + k, ts), :]
    mu = jnp.mean(acc, axis=1, keepdims=True)
    d = acc - mu
    var = jnp.mean(d * d, axis=1, keepdims=True)
    y = d * lax.rsqrt(var + EPS) * g_ref[...] + beta_ref[...]
    o_ref[...] = (y * jax.nn.sigmoid(y)).astype(o_ref.dtype)


def conv_pallas(z, dw_w, dw_b, ln_g, ln_b, B, S):
    n = B * S
    ts = CONV_TS
    ns = S // ts
    hb = ts // CONV_HALO
    cb = Z_CONV // (2 * CONV_DIM)
    wpad = jnp.pad(dw_w, ((0, 32 - CONV_WIDTH), (0, 0)))
    vec = lambda: pl.BlockSpec((1, CONV_DIM), lambda b, i: (0, 0))
    return pl.pallas_call(
        _conv_kernel,
        grid=(B, ns),
        in_specs=[pl.BlockSpec((ts, 2 * CONV_DIM), lambda b, i: (b * ns + i, cb)),
                  pl.BlockSpec((CONV_HALO, 2 * CONV_DIM), lambda b, i: (jnp.maximum((b * ns + i) * hb - 1, 0), cb)),
                  pl.BlockSpec((32, CONV_DIM), lambda b, i: (0, 0)), vec(), vec(), vec()],
        out_specs=pl.BlockSpec((ts, CONV_DIM), lambda b, i: (b * ns + i, 0)),
        out_shape=jax.ShapeDtypeStruct((n, CONV_DIM), jnp.bfloat16),
        scratch_shapes=[pltpu.VMEM((ts + CONV_HALO, CONV_DIM), jnp.float32)],
        compiler_params=pltpu.CompilerParams(dimension_semantics=("parallel", "parallel")),
        name="conformer_conv",
    )(z, z, wpad, dw_b.reshape(1, -1), ln_g.reshape(1, -1), ln_b.reshape(1, -1))


def _msoftmax(s, mask):
    s = jnp.where(mask, s, NEG)
    m = jnp.max(s, axis=1, keepdims=True)
    e = jnp.where(mask, jnp.exp(s - m), 0.0)
    l = jnp.sum(e, axis=1, keepdims=True)
    return e / jnp.where(l > 0.0, l, 1.0)


def _nsa_kernel(q_ref, g_ref, kct_ref, vc_ref, kst_ref, vs_ref, kwt_ref, vw_ref, c2st_ref, o_ref, *, n_sel):
    qb = pl.program_id(2)
    t0 = pl.multiple_of(qb * TQ, TQ)
    q = q_ref[...]
    lane_g = lax.broadcasted_iota(jnp.int32, (TQ, GW), 1) // HEAD_DIM
    qm = [jnp.where(lane_g == h, q, jnp.zeros_like(q)) for h in range(NSA_GROUP)]
    tpos = t0 + lax.broadcasted_iota(jnp.int32, (TQ, 1), 0)

    kct = kct_ref[0, 0]
    vc = vc_ref[0, 0]
    ncp = kct.shape[1]
    ncol = lax.broadcasted_iota(jnp.int32, (TQ, ncp), 1)
    maskc = (ncol * CMP_STRIDE + (CMP_BLOCK - 1)) <= tpos
    pc_sum = jnp.zeros((TQ, ncp), jnp.float32)
    oc = []
    for h in range(NSA_GROUP):
        p = _msoftmax(jnp.dot(qm[h], kct, preferred_element_type=jnp.float32), maskc)
        pc_sum = pc_sum + p
        oc.append(jnp.dot(p.astype(jnp.bfloat16), vc, preferred_element_type=jnp.float32))

    c2st = c2st_ref[...]
    jp = c2st.shape[0]
    hi = pc_sum.astype(jnp.bfloat16)
    lo = (pc_sum - hi.astype(jnp.float32)).astype(jnp.bfloat16)
    dn = (((1,), (1,)), ((), ()))
    imp_t = (lax.dot_general(c2st, hi, dn, preferred_element_type=jnp.float32)
             + lax.dot_general(c2st, lo, dn, preferred_element_type=jnp.float32))
    jrow = lax.broadcasted_iota(jnp.int32, (jp, TQ), 0)
    cur = (t0 + lax.broadcasted_iota(jnp.int32, (jp, TQ), 1)) // SLC_BLOCK
    forced = (jrow == 0) | (jrow == cur) | (jrow == cur - 1)
    score = jnp.where(jrow <= cur, imp_t + jnp.where(forced, FORCE_BONUS, 0.0), NEG)
    rank = jnp.zeros((jp, TQ), jnp.int32)
    nblk = kst_ref.shape[3] // SLC_BLOCK
    for i in range(nblk):
        si = score[i:i + 1, :]
        beats = (si > score) | ((si == score) & (jrow > i))
        rank = rank + beats.astype(jnp.int32)
    sel_t = ((rank < n_sel) & (score > 0.5 * NEG)).astype(jnp.float32)
    selb = sel_t.T.astype(jnp.bfloat16)

    nchunks = (t0 + TQ + KC - 1) // KC

    def chunk(c, carry):
        ms, ls, accs = carry
        k0 = pl.multiple_of(c * KC, KC)
        kt = kst_ref[0, 0, :, pl.ds(k0, KC)]
        v = vs_ref[0, 0, pl.ds(k0, KC), :]
        kblk = (k0 + lax.broadcasted_iota(jnp.int32, (jp, KC), 1)) // SLC_BLOCK
        expand = (kblk == lax.broadcasted_iota(jnp.int32, (jp, KC), 0)).astype(jnp.bfloat16)
        member = jnp.dot(selb, expand, preferred_element_type=jnp.float32) > 0.5
        kpos = k0 + lax.broadcasted_iota(jnp.int32, (TQ, KC), 1)
        msk = member & (kpos <= tpos)
        nm, nl, na = [], [], []
        for h in range(NSA_GROUP):
            s = jnp.where(msk, jnp.dot(qm[h], kt, preferred_element_type=jnp.float32), NEG)
            m_new = jnp.maximum(ms[h], jnp.max(s, axis=1, keepdims=True))
            alpha = jnp.exp(ms[h] - m_new)
            p = jnp.where(msk, jnp.exp(s - m_new), 0.0)
            nl.append(alpha * ls[h] + jnp.sum(p, axis=1, keepdims=True))
            na.append(alpha * accs[h] + jnp.dot(p.astype(jnp.bfloat16), v, preferred_element_type=jnp.float32))
            nm.append(m_new)
        return tuple(nm), tuple(nl), tuple(na)

    init = (tuple(jnp.full((TQ, 1), NEG, jnp.float32) for _ in range(NSA_GROUP)),
            tuple(jnp.zeros((TQ, 1), jnp.float32) for _ in range(NSA_GROUP)),
            tuple(jnp.zeros((TQ, GW), jnp.float32) for _ in range(NSA_GROUP)))
    _, ls, accs = lax.fori_loop(0, nchunks, chunk, init)

    starts = [t0 - WINDOW + TQ * c for c in range(WINDOW // TQ + 1)]
    reads = [pl.multiple_of(jnp.maximum(s, 0), TQ) for s in starts]
    kwt = jnp.concatenate([kwt_ref[0, 0, :, pl.ds(r, TQ)] for r in reads], axis=1)
    vw = jnp.concatenate([vw_ref[0, 0, pl.ds(r, TQ), :] for r in reads], axis=0)
    wk = WINDOW + TQ
    kposw = t0 - WINDOW + lax.broadcasted_iota(jnp.int32, (TQ, wk), 1)
    diff = tpos - kposw
    maskw = (kposw >= 0) & (diff >= 0) & (diff < WINDOW)

    sig = jax.nn.sigmoid(g_ref[...])
    out = jnp.zeros((TQ, GW), jnp.float32)
    for h in range(NSA_GROUP):
        pw = _msoftmax(jnp.dot(qm[h], kwt, preferred_element_type=jnp.float32), maskw)
        ow = jnp.dot(pw.astype(jnp.bfloat16), vw, preferred_element_type=jnp.float32)
        os_ = accs[h] / jnp.where(ls[h] > 0.0, ls[h], 1.0)
        o_h = (sig[:, 3 * h:3 * h + 1] * oc[h] + sig[:, 3 * h + 1:3 * h + 2] * os_
               + sig[:, 3 * h + 2:3 * h + 3] * ow)
        out = out + jnp.where(lane_g == h, o_h, 0.0)
    o_ref[...] = out.astype(o_ref.dtype)


def nsa_core(q, z, kct, vc, kst, vs, kwt, vw, B, S):
    n = B * S
    ncp = kct.shape[3]
    ncmp = (S - CMP_BLOCK) // CMP_STRIDE + 1
    nslc = S // SLC_BLOCK
    n_sel = min(SLC_TOPK, nslc)
    jp = LANE
    assert nslc <= jp and S % KC == 0 and S % TQ == 0 and WINDOW % TQ == 0
    cs = np.arange(ncmp) * CMP_STRIDE
    ss = np.arange(nslc) * SLC_BLOCK
    ov = np.minimum(cs[:, None] + CMP_BLOCK, ss[None, :] + SLC_BLOCK) - np.maximum(cs[:, None], ss[None, :])
    c2s = np.zeros((ncp, jp), np.float32)
    c2s[:ncmp, :nslc] = np.clip(ov, 0, None) / CMP_BLOCK
    c2st = jnp.asarray(c2s.T, dtype=jnp.bfloat16)
    nqb = S // TQ

    def kv_spec(shp):
        return pl.BlockSpec((1, 1) + shp, lambda b, k, i: (b, k, 0, 0))

    return pl.pallas_call(
        functools.partial(_nsa_kernel, n_sel=n_sel),
        grid=(B, NSA_KV_HEADS, nqb),
        in_specs=[pl.BlockSpec((TQ, GW), lambda b, k, i: (b * nqb + i, k)),
                  pl.BlockSpec((TQ, LANE), lambda b, k, i: (b * nqb + i, Z_NG // LANE + k)),
                  kv_spec((GW, ncp)), kv_spec((ncp, GW)),
                  kv_spec((GW, S)), kv_spec((S, GW)),
                  kv_spec((GW, S)), kv_spec((S, GW)),
                  pl.BlockSpec((jp, ncp), lambda b, k, i: (0, 0))],
        out_specs=pl.BlockSpec((TQ, GW), lambda b, k, i: (b * nqb + i, k)),
        out_shape=jax.ShapeDtypeStruct((n, NSA_KV_HEADS * GW), jnp.bfloat16),
        compiler_params=pltpu.CompilerParams(dimension_semantics=("parallel", "parallel", "arbitrary"),
                                             vmem_limit_bytes=VMEM_LIMIT),
        name="nsa_core",
    )(q, z, kct, vc, kst, vs, kwt, vw, c2st)


def _gla_consts():
    C = GLA_CHUNK
    i = np.arange(C)
    mats = [np.tril(np.ones((C, C), np.float32))]
    for s in GLA_LEVELS:
        r = (i // (2 * s)) * 2 * s + s - 1
        upper = (i // s) % 2 == 1
        m = np.arange(C)[None, :]
        mq = ((m > r[:, None]) & (m <= i[:, None]) & upper[:, None]).astype(np.float32)
        mk = ((m > i[:, None]) & (m <= r[:, None]) & (~upper)[:, None]).astype(np.float32)
        mats += [mq, mk]
    return np.concatenate(mats, axis=0)


def _split3(x):
    h = x.astype(jnp.bfloat16)
    r = x - h.astype(jnp.float32)
    m = r.astype(jnp.bfloat16)
    l = (r - m.astype(jnp.float32)).astype(jnp.bfloat16)
    return h, m, l


def _gla_kernel(q_ref, k_ref, v_ref, r_ref, glr_ref, gw_ref, gb_ref, ng_ref, msel_ref, o_ref, state_ref):
    C = GLA_CHUNK

    @pl.when(pl.program_id(1) == 0)
    def _():
        state_ref[...] = jnp.zeros_like(state_ref)

    ii = lax.broadcasted_iota(jnp.int32, (C, C), 0)
    jj = lax.broadcasted_iota(jnp.int32, (C, C), 1)
    lane_h = lax.broadcasted_iota(jnp.int32, (C, GLA_QK), 1) // GLA_DK
    bd = (lax.broadcasted_iota(jnp.int32, (GLA_QK, GLA_V), 0) // GLA_DK
          == lax.broadcasted_iota(jnp.int32, (GLA_QK, GLA_V), 1) // GLA_DV)
    msel = msel_ref[...]
    gw = gw_ref[...].astype(jnp.bfloat16)
    dn = (((1,), (1,)), ((), ()))

    def chunk(c, carry):
        r0 = pl.multiple_of(c * C, C)
        q = q_ref[pl.ds(r0, C), :] * (GLA_DK ** -0.5)
        k = k_ref[pl.ds(r0, C), :]
        v = v_ref[pl.ds(r0, C), :].astype(jnp.bfloat16)
        z = jnp.dot(glr_ref[pl.ds(r0, C), :].astype(jnp.bfloat16), gw, preferred_element_type=jnp.float32) + gb_ref[...]
        a = jax.nn.log_sigmoid(z) / GLA_GATE_TEMP
        ah, am, al = _split3(a)
        ex = (jnp.dot(msel, ah, preferred_element_type=jnp.float32)
              + jnp.dot(msel, am, preferred_element_type=jnp.float32)
              + jnp.dot(msel, al, preferred_element_type=jnp.float32))
        bcum = ex[0:C]
        state = state_ref[...]
        o = jnp.dot((q * jnp.exp(bcum)).astype(jnp.bfloat16), state.astype(jnp.bfloat16),
                    preferred_element_type=jnp.float32)
        kb = k.astype(jnp.bfloat16)
        for h in range(GLA_HEADS):
            hm = lane_h == h
            att = jnp.where(ii == jj, lax.dot_general(jnp.where(hm, q, 0.0).astype(jnp.bfloat16), kb, dn,
                                                      preferred_element_type=jnp.float32), 0.0)
            for li, s in enumerate(GLA_LEVELS):
                eq = jnp.exp(ex[(1 + 2 * li) * C:(2 + 2 * li) * C])
                ek = jnp.exp(ex[(2 + 2 * li) * C:(3 + 2 * li) * C])
                qs = jnp.where(hm, q * eq, 0.0).astype(jnp.bfloat16)
                ks = (k * ek).astype(jnp.bfloat16)
                blk = (ii // (2 * s) == jj // (2 * s)) & ((ii // s) % 2 == 1) & ((jj // s) % 2 == 0)
                att = att + jnp.where(blk, lax.dot_general(qs, ks, dn, preferred_element_type=jnp.float32), 0.0)
            oh = jnp.dot(att.astype(jnp.bfloat16), v, preferred_element_type=jnp.float32)
            o = o + jnp.where(lax.broadcasted_iota(jnp.int32, (C, GLA_V), 1) // GLA_DV == h, oh, 0.0)
        blast = bcum[C - 1:C, :]
        kd_t = (k * jnp.exp(blast - bcum)).T.astype(jnp.bfloat16)
        decay_col = jnp.exp(jnp.sum(a.T, axis=1, keepdims=True))
        upd = jnp.dot(kd_t, v, preferred_element_type=jnp.float32)
        state_ref[...] = state * decay_col + jnp.where(bd, upd, 0.0)
        outs = []
        for h in range(GLA_HEADS):
            oh = o[:, h * GLA_DV:(h + 1) * GLA_DV]
            outs.append(oh * lax.rsqrt(jnp.mean(oh * oh, axis=1, keepdims=True) + EPS) * ng_ref[...])
        on = jnp.concatenate(outs, axis=1)
        r = r_ref[pl.ds(r0, C), :]
        o_ref[pl.ds(r0, C), :] = (on * (r * jax.nn.sigmoid(r))).astype(o_ref.dtype)
        return carry

    lax.fori_loop(0, q_ref.shape[0] // C, chunk, 0)


def gla_pallas(z, gate_w, gate_b, norm_g, B, S):
    n = B * S
    tb = GLA_TB
    nb = S // tb
    gw = jnp.pad(gate_w, ((0, LANE - gate_w.shape[0]), (0, 0)))
    msel = jnp.asarray(_gla_consts(), dtype=jnp.bfloat16)
    zb = lambda col, w: pl.BlockSpec((tb, w), lambda b, i: (b * nb + i, col // w))
    full = lambda shp: pl.BlockSpec(shp, lambda b, i: (0,) * len(shp))
    return pl.pallas_call(
        _gla_kernel,
        grid=(B, nb),
        in_specs=[zb(Z_GQ, GLA_QK), zb(Z_GK, GLA_QK), zb(Z_GV, GLA_V), zb(Z_GR, GLA_V), zb(Z_GLR, LANE),
                  full((LANE, GLA_QK)), full((1, GLA_QK)), full((1, GLA_DV)), full(msel.shape)],
        out_specs=pl.BlockSpec((tb, GLA_V), lambda b, i: (b * nb + i, 0)),
        out_shape=jax.ShapeDtypeStruct((n, GLA_V), jnp.bfloat16),
        scratch_shapes=[pltpu.VMEM((GLA_QK, GLA_V), jnp.float32)],
        compiler_params=pltpu.CompilerParams(dimension_semantics=("parallel", "arbitrary")),
        name="gla",
    )(z, z, z, z, z, gw, gate_b.reshape(1, -1), norm_g.reshape(1, -1), msel)


_FULL_BLOCKS = 8


def _extract_top(s, k):
    n, t = s.shape
    row = lax.broadcasted_iota(jnp.int32, (n, t), 0)
    rank = jnp.full((n, t), k, jnp.int32)
    vals = []
    for r in range(k):
        m = jnp.max(s, axis=0, keepdims=True)
        first = jnp.min(jnp.where(s == m, row, n), axis=0, keepdims=True)
        hit = row == first
        rank = jnp.where(hit, r, rank)
        s = jnp.where(hit, float('-inf'), s)
        vals.append(m)
    return jnp.concatenate(vals, axis=0), rank


def _peer_gate_kernel(q_ref, keys_ref, e1_ref, a_ref, r2_ref, e2_ref):
    K = PEER_TOPK
    q = q_ref[...].astype(jnp.bfloat16)
    k1 = keys_ref[0, 0].astype(jnp.bfloat16)
    k2 = keys_ref[0, 1].astype(jnp.bfloat16)
    dn = (((1,), (1,)), ((), ()))
    s1 = lax.dot_general(k1, q[:, :PEER_DKEY // 2], dn, preferred_element_type=jnp.float32)
    s2 = lax.dot_general(k2, q[:, PEER_DKEY // 2:], dn, preferred_element_type=jnp.float32)
    t = s1.shape[1]
    v1, rank1 = _extract_top(s1, K)
    v2, rank2 = _extract_top(s2, K)
    blocks, poss = [], []
    row16 = lax.broadcasted_iota(jnp.int32, (K, t), 0)
    for r1 in range(_FULL_BLOCKS):
        c = v1[r1:r1 + 1, :] + v2
        c = jnp.where(row16 < K // (r1 + 1), c, float('-inf'))
        blocks.append(c)
        poss.append(row16 + r1 * K)
    row8 = lax.broadcasted_iota(jnp.int32, (K - _FULL_BLOCKS, t), 0)
    blocks.append(v1[_FULL_BLOCKS:, :] + v2[0:1, :])
    poss.append((row8 + _FULL_BLOCKS) * K)
    sels = [jnp.zeros(b.shape, jnp.bool_) for b in blocks]
    big = K * K
    for _ in range(K):
        m = blocks[0].max(axis=0, keepdims=True)
        for b in blocks[1:]:
            m = jnp.maximum(m, b.max(axis=0, keepdims=True))
        first = None
        for b, p in zip(blocks, poss):
            f = jnp.min(jnp.where(b == m, p, big), axis=0, keepdims=True)
            first = f if first is None else jnp.minimum(first, f)
        for i in range(len(blocks)):
            hit = poss[i] == first
            sels[i] = sels[i] | hit
            blocks[i] = jnp.where(hit, float('-inf'), blocks[i])
    top = v1[0:1, :] + v2[0:1, :]
    cnt_rows = []
    z = jnp.zeros((1, t), jnp.float32)
    for r1 in range(_FULL_BLOCKS):
        sel = sels[r1]
        cnt_rows.append(jnp.sum(sel.astype(jnp.float32), axis=0, keepdims=True))
        c = v1[r1:r1 + 1, :] + v2
        z = z + jnp.sum(jnp.where(sel, jnp.exp(c - top), 0.0), axis=0, keepdims=True)
    sel = sels[_FULL_BLOCKS]
    cnt_tail = sel.astype(jnp.float32)
    c = v1[_FULL_BLOCKS:, :] + v2[0:1, :]
    z = z + jnp.sum(jnp.where(sel, jnp.exp(c - top), 0.0), axis=0, keepdims=True)
    cnt = jnp.concatenate(cnt_rows + [cnt_tail], axis=0)
    a = jnp.zeros(s1.shape, jnp.float32)
    for r in range(K):
        a = jnp.where(rank1 == r, cnt[r:r + 1, :], a)
    e1_ref[0] = jnp.exp(s1 - v1[0:1, :])
    a_ref[0] = a
    r2_ref[0] = rank2.astype(jnp.float32).astype(jnp.bfloat16)
    e2_ref[0] = (jnp.exp(s2 - v2[0:1, :]) / z).astype(jnp.bfloat16)


def peer_gates(qh, keys, tt=256):
    n = qh.shape[0]
    H = PEER_HEADS
    shp32 = jax.ShapeDtypeStruct((H, PEER_NKEYS, n), jnp.float32)
    shp16 = jax.ShapeDtypeStruct((H, PEER_NKEYS, n), jnp.bfloat16)
    ospec = pl.BlockSpec((1, PEER_NKEYS, tt), lambda i, h: (h, 0, i))
    return pl.pallas_call(
        _peer_gate_kernel,
        grid=(n // tt, H),
        in_specs=[pl.BlockSpec((tt, PEER_DKEY), lambda i, h: (i, h)),
                  pl.BlockSpec((1, 2, PEER_NKEYS, PEER_DKEY // 2), lambda i, h: (h, 0, 0, 0))],
        out_specs=[ospec, ospec, ospec, ospec],
        out_shape=[shp32, shp32, shp16, shp16],
        compiler_params=pltpu.CompilerParams(dimension_semantics=("parallel", "parallel")),
        name="peer_gates",
    )(qh, keys)


def _peer_dense_kernel(xt_ref, u_ref, vt_ref, e1_ref, a_ref, r2_ref, e2_ref, x1_ref, gate_ref, fg_ref,
                       o_ref, acc_ref, *, ec, final):
    j = pl.program_id(1)
    nsub = ec // PEER_NKEYS

    @pl.when(j == 0)
    def _():
        acc_ref[...] = jnp.zeros_like(acc_ref)

    def gated(i):
        a_idx = j * nsub + i
        st = jnp.dot(u_ref[i * PEER_NKEYS:(i + 1) * PEER_NKEYS, :], xt_ref[...], preferred_element_type=jnp.float32)
        act = _gelu(st).astype(jnp.bfloat16)
        g = None
        for h in range(PEER_HEADS):
            cnt = a_ref[h, pl.ds(a_idx, 1), :].astype(jnp.bfloat16)
            e1 = e1_ref[h, pl.ds(a_idx, 1), :].astype(jnp.bfloat16)
            term = jnp.where(r2_ref[h] < cnt, e2_ref[h], jnp.bfloat16(0)) * e1
            g = term if g is None else g + term
        return act * g

    for p in range(nsub // 2):
        w = jnp.concatenate([gated(2 * p), gated(2 * p + 1)], axis=0)
        acc_ref[...] += jnp.dot(vt_ref[:, 2 * p * PEER_NKEYS:(2 * p + 2) * PEER_NKEYS], w,
                                preferred_element_type=jnp.float32)

    @pl.when(j == pl.num_programs(1) - 1)
    def _():
        x2 = x1_ref[...] + gate_ref[0] * acc_ref[...].T
        if final:
            x2 = x2 * lax.rsqrt(jnp.mean(x2 * x2, axis=1, keepdims=True) + EPS) * fg_ref[...]
        o_ref[...] = x2


def peer_dense(xt, u16, vt16, e1, a, r2, e2, x1, gate2, final_g, S, final, tt=512, ec=1024):
    d, n = xt.shape
    ne = u16.shape[0]
    H = PEER_HEADS
    per_b = S // tt
    gspec = lambda: pl.BlockSpec((H, PEER_NKEYS, tt), lambda i, j: (0, 0, i))
    return pl.pallas_call(
        functools.partial(_peer_dense_kernel, ec=ec, final=final),
        grid=(n // tt, ne // ec),
        in_specs=[pl.BlockSpec((d, tt), lambda i, j: (0, i)),
                  pl.BlockSpec((ec, d), lambda i, j: (j, 0)),
                  pl.BlockSpec((d, ec), lambda i, j: (0, j)),
                  gspec(), gspec(), gspec(), gspec(),
                  pl.BlockSpec((tt, d), lambda i, j: (i, 0)),
                  pl.BlockSpec((1, 1, d), lambda i, j: (i // per_b, 0, 0)),
                  pl.BlockSpec((1, d), lambda i, j: (0, 0))],
        out_specs=pl.BlockSpec((tt, d), lambda i, j: (i, 0)),
        out_shape=jax.ShapeDtypeStruct((n, d), jnp.float32),
        scratch_shapes=[pltpu.VMEM((d, tt), jnp.float32)],
        compiler_params=pltpu.CompilerParams(dimension_semantics=("parallel", "arbitrary"),
                                             vmem_limit_bytes=VMEM_LIMIT),
        name="peer_dense",
    )(xt, u16, vt16, e1, a, r2, e2, x1, gate2, final_g.reshape(1, d))


def adaln(c, w, b):
    mod = pmm(jax.nn.silu(c), w) + b
    shift, scale, gate = jnp.split(mod[:, None, :], 3, axis=-1)
    return shift, 1.0 + scale, gate


def kernel(x, c, positions, ada_w, ada_b, norm_g, w_in, conv_dw_w, conv_dw_b, conv_ln_g, conv_ln_b, w_conv_up,
           cmp_pos, cmp_w1, cmp_w2, w_nsa_up, gla_gate_w, gla_gate_b, gla_norm_g, w_gla_up, w_out,
           peer_wq, peer_keys, peer_u, peer_v, final_g):
    B, S, D = x.shape
    bf = jnp.bfloat16
    x2 = x.reshape(B * S, D)
    cs = rope_tables(positions)
    for l in range(DEPTH):
        shift, scale1p, gate = adaln(c, ada_w[l, 0], ada_b[l, 0])
        z = in_proj(x2, norm_g[l, 0], scale1p, shift, pack_w_in(w_in[l]), S)
        hc = conv_pallas(z, conv_dw_w[l], conv_dw_b[l], conv_ln_g[l], conv_ln_b[l], B, S)
        q, kst, vs, kwt, vw = nsa_prep(z, cs, B, S)
        kct, vc = nsa_compress(z, positions, cmp_pos[l], cmp_w1[l], cmp_w2[l], B, S)
        on = nsa_core(q, z, kct, vc, kst, vs, kwt, vw, B, S)
        og = gla_pallas(z, gla_gate_w[l], gla_gate_b[l], gla_norm_g[l], B, S)
        shift2, scale2p, gate2 = adaln(c, ada_w[l, 1], ada_b[l, 1])
        x1, ut, qh = mix_out(hc, on, og, z, x2, gate, w_conv_up[l].astype(bf), w_nsa_up[l].astype(bf),
                             w_gla_up[l].astype(bf), w_out[l].astype(bf), norm_g[l, 1], scale2p, shift2,
                             peer_wq[l].astype(bf), S)
        e1, a, r2, e2 = peer_gates(qh, peer_keys[l])
        x2 = peer_dense(ut, peer_u[l].astype(bf), peer_v[l].astype(bf).T, e1, a, r2, e2, x1, gate2, final_g,
                        S, final=(l == DEPTH - 1))
    return x2.reshape(B, S, D)
```

```python
import functools

import jax
import jax.numpy as jnp
import numpy as np
from jax import lax
from jax.experimental import pallas as pl
from jax.experimental.pallas import tpu as pltpu

D_MODEL = 1024
DEPTH = 2
EPS = 1e-6
NEG = -1e30
N_BRANCH = 3
CONV_DIM = 512
CONV_WIDTH = 31
NSA_HEADS = 8
NSA_KV_HEADS = 2
NSA_GROUP = NSA_HEADS // NSA_KV_HEADS
HEAD_DIM = 64
CMP_BLOCK = 32
CMP_STRIDE = 16
CMP_HIDDEN = 256
SLC_BLOCK = 64
SLC_TOPK = 16
WINDOW = 512
FORCE_BONUS = 1e4
ROPE_THETA = 10000.0
GLA_HEADS = 4
GLA_DK = 64
GLA_DV = 128
GLA_GATE_RANK = 16
GLA_GATE_TEMP = 16.0
GLA_CHUNK = 64
PEER_HEADS = 8
PEER_NKEYS = 128
PEER_DKEY = 256
PEER_TOPK = 16
NSA_Q = NSA_HEADS * HEAD_DIM
NSA_KV = 3 * 2 * NSA_KV_HEADS * HEAD_DIM
NSA_G = 3 * NSA_HEADS
GLA_QK = GLA_HEADS * GLA_DK
GLA_V = GLA_HEADS * GLA_DV
SPLITS = [2 * CONV_DIM, NSA_Q, NSA_KV, NSA_G, GLA_QK, GLA_QK, GLA_V, GLA_V, GLA_GATE_RANK, N_BRANCH * D_MODEL]

LANE = 128
TQ = 128
KC = 512
GW = NSA_GROUP * HEAD_DIM
VMEM_LIMIT = 48 * 1024 * 1024
GLA_LEVELS = (32, 16, 8, 4, 2, 1)
GLA_TB = 512
CONV_TS = 512
CONV_HALO = 32
ROW_TILE = 512
MIX_TILE = 256
PEER_GROUPS_PER_DOT = 2

Z_MG = 0
Z_CONV = Z_MG + N_BRANCH * D_MODEL
Z_Q = Z_CONV + 2 * CONV_DIM
Z_QSW = Z_Q + NSA_Q
Z_GV = Z_QSW + NSA_Q
Z_GR = Z_GV + GLA_V
Z_GQ = Z_GR + GLA_V
Z_GK = Z_GQ + GLA_QK
Z_CMP = Z_GK + GLA_QK
Z_SLC = Z_CMP + 256
Z_WIN = Z_SLC + 384
Z_NG = Z_WIN + 384
Z_GLR = Z_NG + NSA_KV_HEADS * LANE
NZ = Z_GLR + LANE
assert Z_SLC % 384 == 0 and Z_WIN % 384 == 0 and Z_CONV % (2 * CONV_DIM) == 0 and Z_GQ % GLA_QK == 0


def _mm_kernel(a_ref, b_ref, o_ref):
    o_ref[...] = jnp.dot(a_ref[...].astype(jnp.bfloat16), b_ref[...].astype(jnp.bfloat16),
                         preferred_element_type=jnp.float32)


def _pick(n, cands):
    for c in cands:
        if n % c == 0:
            return c
    return n


def pmm(a, b):
    m, k = a.shape
    n = b.shape[1]
    n_pad = -n % LANE
    if n_pad:
        b = jnp.pad(b, ((0, 0), (0, n_pad)))
    m_pad = -m % 8
    if m_pad:
        a = jnp.pad(a, ((0, m_pad), (0, 0)))
    mp, np_ = m + m_pad, n + n_pad
    tm = _pick(mp, (512, 256, 128, 64, 32, 16, 8))
    tn = _pick(np_, (512, 640, 384, 256, 128))
    out = pl.pallas_call(
        _mm_kernel,
        grid=(mp // tm, np_ // tn),
        in_specs=[pl.BlockSpec((tm, k), lambda i, j: (i, 0)),
                  pl.BlockSpec((k, tn), lambda i, j: (0, j))],
        out_specs=pl.BlockSpec((tm, tn), lambda i, j: (i, j)),
        out_shape=jax.ShapeDtypeStruct((mp, np_), jnp.float32),
        compiler_params=pltpu.CompilerParams(dimension_semantics=("parallel", "parallel")),
        name="pmm",
    )(a, b)
    return out[:m, :n]


def _in_proj_kernel(x_ref, g_ref, sc_ref, sh_ref, w_ref, z_ref, u_ref):
    @pl.when(pl.program_id(1) == 0)
    def _():
        x = x_ref[...]
        y = x * lax.rsqrt(jnp.mean(x * x, axis=1, keepdims=True) + EPS) * g_ref[...]
        u_ref[...] = (y * sc_ref[0] + sh_ref[0]).astype(jnp.bfloat16)

    z_ref[...] = jnp.dot(u_ref[...], w_ref[...], preferred_element_type=jnp.float32)


def in_proj(x2, g, scale1p, shift, w16, S):
    n, d = x2.shape
    nz = w16.shape[1]
    tm = ROW_TILE
    tn = _pick(nz, (1152, 1024, 896, 512, 384, 256, 128))
    per_b = S // tm
    mod = lambda: pl.BlockSpec((1, 1, d), lambda i, j: (i // per_b, 0, 0))
    return pl.pallas_call(
        _in_proj_kernel,
        grid=(n // tm, nz // tn),
        in_specs=[pl.BlockSpec((tm, d), lambda i, j: (i, 0)),
                  pl.BlockSpec((1, d), lambda i, j: (0, 0)), mod(), mod(),
                  pl.BlockSpec((d, tn), lambda i, j: (0, j))],
        out_specs=pl.BlockSpec((tm, tn), lambda i, j: (i, j)),
        out_shape=jax.ShapeDtypeStruct((n, nz), jnp.float32),
        scratch_shapes=[pltpu.VMEM((tm, d), jnp.bfloat16)],
        compiler_params=pltpu.CompilerParams(dimension_semantics=("parallel", "arbitrary"),
                                             vmem_limit_bytes=VMEM_LIMIT),
        name="in_proj",
    )(x2, g.reshape(1, d), scale1p, shift, w16)


def _half_swap(n_heads):
    idx = np.arange(n_heads * HEAD_DIM).reshape(n_heads, 2, HEAD_DIM // 2)
    return idx[:, ::-1, :].reshape(-1)


def pack_w_in(w_in):
    o = np.cumsum([0] + SPLITS)
    a_conv, nq, nkv, ng, gq, gk, gv, gr, glr, mg = [w_in[:, o[i]:o[i + 1]] for i in range(len(SPLITS))]
    kvw = 2 * NSA_KV_HEADS * HEAD_DIM
    kw = NSA_KV_HEADS * HEAD_DIM
    d = w_in.shape[0]
    sw = _half_swap(NSA_KV_HEADS)
    cols = [mg, a_conv, nq, nq[:, _half_swap(NSA_HEADS)], gv, gr, gq, gk, nkv[:, 0:kvw]]
    for br in (1, 2):
        k = nkv[:, br * kvw:br * kvw + kw]
        cols += [k, k[:, sw], nkv[:, br * kvw + kw:(br + 1) * kvw]]
    gpk = NSA_GROUP * 3
    for kvh in range(NSA_KV_HEADS):
        cols += [ng[:, kvh * gpk:(kvh + 1) * gpk], jnp.zeros((d, LANE - gpk), w_in.dtype)]
    cols += [glr, jnp.zeros((d, LANE - GLA_GATE_RANK), w_in.dtype)]
    w = jnp.concatenate(cols, axis=1)
    assert w.shape[1] == NZ
    return w.astype(jnp.bfloat16)


def rope_tables(positions):
    half = HEAD_DIM // 2
    freq = ROPE_THETA ** (-jnp.arange(half, dtype=jnp.float32) / half)
    ang = positions.astype(jnp.float32)[..., None] * freq
    cos, sin = jnp.cos(ang), jnp.sin(ang)
    c = jnp.tile(jnp.concatenate([cos, cos], -1), (1, 1, LANE // HEAD_DIM))
    s = jnp.tile(jnp.concatenate([-sin, sin], -1), (1, 1, LANE // HEAD_DIM))
    return jnp.concatenate([c, s], -1).reshape(-1, 2 * LANE)


def _nsa_prep_kernel(q_ref, qs_ref, slc_ref, win_ref, cs_ref, sel_ref, qo_ref, kst_ref, vs_ref, kwt_ref, vw_ref):
    cos = cs_ref[:, :LANE]
    sin = cs_ref[:, LANE:]
    scale = HEAD_DIM ** -0.5
    for t in range(NSA_Q // LANE):
        sl = slice(t * LANE, (t + 1) * LANE)
        qo_ref[:, sl] = ((q_ref[:, sl] * cos + qs_ref[:, sl] * sin) * scale).astype(jnp.bfloat16)
    dn = (((1,), (1,)), ((), ()))
    for src, kt_ref, v_ref in ((slc_ref, kst_ref, vs_ref), (win_ref, kwt_ref, vw_ref)):
        k = (src[:, 0:LANE] * cos + src[:, LANE:2 * LANE] * sin).astype(jnp.bfloat16)
        v = src[:, 2 * LANE:3 * LANE].astype(jnp.bfloat16)
        for kvh in range(NSA_KV_HEADS):
            sel = sel_ref[kvh]
            kt_ref[0, kvh] = lax.dot_general(sel, k, dn, preferred_element_type=jnp.float32).astype(jnp.bfloat16)
            v_ref[0, kvh] = lax.dot_general(v, sel, dn, preferred_element_type=jnp.float32).astype(jnp.bfloat16)


def _head_repeat_sel():
    sel = np.zeros((NSA_KV_HEADS, GW, LANE), np.float32)
    for kvh in range(NSA_KV_HEADS):
        for r in range(GW):
            sel[kvh, r, kvh * HEAD_DIM + r % HEAD_DIM] = 1.0
    return jnp.asarray(sel, dtype=jnp.bfloat16)


def nsa_prep(z, cs, B, S):
    n = B * S
    tp = ROW_TILE
    nb = S // tp
    zb = lambda col, w: pl.BlockSpec((tp, w), lambda b, i: (b * nb + i, col // w))
    kt = lambda: pl.BlockSpec((1, NSA_KV_HEADS, GW, tp), lambda b, i: (b, 0, 0, i))
    vv = lambda: pl.BlockSpec((1, NSA_KV_HEADS, tp, GW), lambda b, i: (b, 0, i, 0))
    kt_shape = jax.ShapeDtypeStruct((B, NSA_KV_HEADS, GW, S), jnp.bfloat16)
    v_shape = jax.ShapeDtypeStruct((B, NSA_KV_HEADS, S, GW), jnp.bfloat16)
    return pl.pallas_call(
        _nsa_prep_kernel,
        grid=(B, nb),
        in_specs=[zb(Z_Q, NSA_Q), zb(Z_QSW, NSA_Q), zb(Z_SLC, 384), zb(Z_WIN, 384),
                  pl.BlockSpec((tp, 2 * LANE), lambda b, i: (b * nb + i, 0)),
                  pl.BlockSpec((NSA_KV_HEADS, GW, LANE), lambda b, i: (0, 0, 0))],
        out_specs=[pl.BlockSpec((tp, NSA_Q), lambda b, i: (b * nb + i, 0)), kt(), vv(), kt(), vv()],
        out_shape=[jax.ShapeDtypeStruct((n, NSA_Q), jnp.bfloat16), kt_shape, v_shape, kt_shape, v_shape],
        compiler_params=pltpu.CompilerParams(dimension_semantics=("parallel", "parallel")),
        name="nsa_prep",
    )(z, z, z, z, cs, _head_repeat_sel())


def _gelu(x):
    return 0.5 * x * (1.0 + jnp.tanh(0.7978845608028654 * (x + 0.044715 * x * x * x)))


def _compress_kernel(gk_ref, gv_ref, pe_ref, w1_ref, w2_ref, cs_ref, kct_ref, vc_ref):
    nrow = gk_ref.shape[2]
    outs = []
    for kv, g_ref in enumerate((gk_ref, gv_ref)):
        g = g_ref[0, 0]
        lo = jnp.dot((g + pe_ref[kv, 0]).astype(jnp.bfloat16), w1_ref[kv, 0], preferred_element_type=jnp.float32)
        hi = jnp.dot((g + pe_ref[kv, 1]).astype(jnp.bfloat16), w1_ref[kv, 1], preferred_element_type=jnp.float32)
        hid = _gelu(lo + pltpu.roll(hi, nrow - 1, 0)).astype(jnp.bfloat16)
        outs.append(jnp.dot(hid, w2_ref[kv], preferred_element_type=jnp.float32))
    k = outs[0][:, :GW] * cs_ref[0, :, :GW] + outs[0][:, GW:] * cs_ref[0, :, GW:]
    kct_ref[0, 0] = k.T.astype(jnp.bfloat16)
    vc_ref[0, 0] = outs[1][:, :GW].astype(jnp.bfloat16)


def nsa_compress(z, positions, cmp_pos, cmp_w1, cmp_w2, B, S):
    ng = S // CMP_STRIDE
    grp = CMP_STRIDE * HEAD_DIM
    c = z[:, Z_CMP:Z_CMP + 256].reshape(B, ng, CMP_STRIDE, 2, NSA_KV_HEADS, HEAD_DIM)
    g = c.transpose(3, 0, 4, 1, 2, 5).reshape(2, B, NSA_KV_HEADS, ng, grp)
    pe = cmp_pos.reshape(2, 2, 1, grp)
    w1 = cmp_w1.reshape(2, 2, grp, CMP_HIDDEN).astype(jnp.bfloat16)
    rep = jnp.tile(cmp_w2, (1, 1, NSA_GROUP))
    sw = np.tile(_half_swap(1), NSA_GROUP) + np.repeat(np.arange(NSA_GROUP) * HEAD_DIM, HEAD_DIM)
    w2 = jnp.concatenate([rep, rep[:, :, sw]], axis=-1).astype(jnp.bfloat16)
    end = jnp.minimum(jnp.arange(ng) * CMP_STRIDE + CMP_BLOCK - 1, S - 1)
    half = HEAD_DIM // 2
    freq = ROPE_THETA ** (-jnp.arange(half, dtype=jnp.float32) / half)
    ang = positions[:, end].astype(jnp.float32)[..., None] * freq
    cos, sin = jnp.cos(ang), jnp.sin(ang)
    cs = jnp.concatenate([jnp.tile(jnp.concatenate([cos, cos], -1), (1, 1, NSA_GROUP)),
                          jnp.tile(jnp.concatenate([-sin, sin], -1), (1, 1, NSA_GROUP))], -1)
    gspec = lambda: pl.BlockSpec((1, 1, ng, grp), lambda b, k: (b, k, 0, 0))
    return pl.pallas_call(
        _compress_kernel,
        grid=(B, NSA_KV_HEADS),
        in_specs=[gspec(), gspec(),
                  pl.BlockSpec((2, 2, 1, grp), lambda b, k: (0, 0, 0, 0)),
                  pl.BlockSpec((2, 2, grp, CMP_HIDDEN), lambda b, k: (0, 0, 0, 0)),
                  pl.BlockSpec((2, CMP_HIDDEN, 2 * GW), lambda b, k: (0, 0, 0)),
                  pl.BlockSpec((1, ng, 2 * GW), lambda b, k: (b, 0, 0))],
        out_specs=[pl.BlockSpec((1, 1, GW, ng), lambda b, k: (b, k, 0, 0)),
                   pl.BlockSpec((1, 1, ng, GW), lambda b, k: (b, k, 0, 0))],
        out_shape=[jax.ShapeDtypeStruct((B, NSA_KV_HEADS, GW, ng), jnp.bfloat16),
                   jax.ShapeDtypeStruct((B, NSA_KV_HEADS, ng, GW), jnp.bfloat16)],
        compiler_params=pltpu.CompilerParams(dimension_semantics=("parallel", "parallel")),
        name="nsa_compress",
    )(g[0], g[1], pe, w1, w2, cs)


def _mix_kernel(hc_ref, on_ref, og_ref, mg_ref, x_ref, gate_ref, wc_ref, wn_ref, wg_ref, wo_ref,
                g2_ref, sc_ref, sh_ref, wq_ref, x1_ref, ut_ref, qh_ref):
    d = D_MODEL
    f32 = jnp.float32
    y = (jax.nn.sigmoid(mg_ref[:, 0:d]) * jnp.dot(hc_ref[...], wc_ref[...], preferred_element_type=f32)
         + jax.nn.sigmoid(mg_ref[:, d:2 * d]) * jnp.dot(on_ref[...], wn_ref[...], preferred_element_type=f32)
         + jax.nn.sigmoid(mg_ref[:, 2 * d:3 * d]) * jnp.dot(og_ref[...], wg_ref[...], preferred_element_type=f32))
    x1 = x_ref[...] + gate_ref[0] * jnp.dot(y.astype(jnp.bfloat16), wo_ref[...], preferred_element_type=f32)
    x1_ref[...] = x1
    u = x1 * lax.rsqrt(jnp.mean(x1 * x1, axis=1, keepdims=True) + EPS) * g2_ref[...] * sc_ref[0] + sh_ref[0]
    ut_ref[...] = u.T.astype(jnp.bfloat16)
    qh_ref[...] = jnp.dot(u.astype(jnp.bfloat16), wq_ref[...], preferred_element_type=f32)


def mix_out(hc, on, og, z, x2, gate1, wc, wn, wg, wo, g2, scale2p, shift2, wq, S):
    n, d = x2.shape
    tm = MIX_TILE
    per_b = S // tm
    row = lambda w: pl.BlockSpec((tm, w), lambda i: (i, 0))
    mod = lambda: pl.BlockSpec((1, 1, d), lambda i: (i // per_b, 0, 0))
    full = lambda a: pl.BlockSpec(a.shape, lambda i: (0,) * a.ndim)
    nq = wq.shape[1]
    return pl.pallas_call(
        _mix_kernel,
        grid=(n // tm,),
        in_specs=[row(CONV_DIM), row(NSA_Q), row(GLA_V),
                  pl.BlockSpec((tm, N_BRANCH * d), lambda i: (i, Z_MG // (N_BRANCH * d))),
                  row(d), mod(), full(wc), full(wn), full(wg), full(wo),
                  pl.BlockSpec((1, d), lambda i: (0, 0)), mod(), mod(), full(wq)],
        out_specs=[row(d), pl.BlockSpec((d, tm), lambda i: (0, i)), row(nq)],
        out_shape=[jax.ShapeDtypeStruct((n, d), jnp.float32), jax.ShapeDtypeStruct((d, n), jnp.bfloat16),
                   jax.ShapeDtypeStruct((n, nq), jnp.float32)],
        compiler_params=pltpu.CompilerParams(dimension_semantics=("parallel",), vmem_limit_bytes=VMEM_LIMIT),
        name="mix_out",
    )(hc, on, og, z, x2, gate1, wc, wn, wg, wo, g2.reshape(1, d), scale2p, shift2, wq)


def _conv_kernel(a_ref, ah_ref, w_ref, b_ref, g_ref, beta_ref, o_ref, hbuf):
    i = pl.program_id(1)
    ts = a_ref.shape[0]
    a = a_ref[...]
    hbuf[pl.ds(CONV_HALO, ts), :] = a[:, :CONV_DIM] * jax.nn.sigmoid(a[:, CONV_DIM:])
    ah = ah_ref[...]
    halo = ah[:, :CONV_DIM] * jax.nn.sigmoid(ah[:, CONV_DIM:])
    hbuf[pl.ds(0, CONV_HALO), :] = jnp.where(i > 0, halo, 0.0)
    acc = jnp.zeros((ts, CONV_DIM), jnp.float32) + b_ref[...]
    off = CONV_HALO - (CONV_WIDTH - 1)
    for k in range(CONV_WIDTH):
        acc = acc + w_ref[k:k + 1, :] * hbuf[pl.ds(off + k, ts), :]
    mu = jnp.mean(acc, axis=1, keepdims=True)
    d = acc - mu
    var = jnp.mean(d * d, axis=1, keepdims=True)
    y = d * lax.rsqrt(var + EPS) * g_ref[...] + beta_ref[...]
    o_ref[...] = (y * jax.nn.sigmoid(y)).astype(o_ref.dtype)


def conv_pallas(z, dw_w, dw_b, ln_g, ln_b, B, S):
    n = B * S
    ts = CONV_TS
    ns = S // ts
    hb = ts // CONV_HALO
    cb = Z_CONV // (2 * CONV_DIM)
    wpad = jnp.pad(dw_w, ((0, 32 - CONV_WIDTH), (0, 0)))
    vec = lambda: pl.BlockSpec((1, CONV_DIM), lambda b, i: (0, 0))
    return pl.pallas_call(
        _conv_kernel,
        grid=(B, ns),
        in_specs=[pl.BlockSpec((ts, 2 * CONV_DIM), lambda b, i: (b * ns + i, cb)),
                  pl.BlockSpec((CONV_HALO, 2 * CONV_DIM), lambda b, i: (jnp.maximum((b * ns + i) * hb - 1, 0), cb)),
                  pl.BlockSpec((32, CONV_DIM), lambda b, i: (0, 0)), vec(), vec(), vec()],
        out_specs=pl.BlockSpec((ts, CONV_DIM), lambda b, i: (b * ns + i, 0)),
        out_shape=jax.ShapeDtypeStruct((n, CONV_DIM), jnp.bfloat16),
        scratch_shapes=[pltpu.VMEM((ts + CONV_HALO, CONV_DIM), jnp.float32)],
        compiler_params=pltpu.CompilerParams(dimension_semantics=("parallel", "parallel")),
        name="conformer_conv",
    )(z, z, wpad, dw_b.reshape(1, -1), ln_g.reshape(1, -1), ln_b.reshape(1, -1))


def _msoftmax(s, mask):
    s = jnp.where(mask, s, NEG)
    m = jnp.max(s, axis=-1, keepdims=True)
    e = jnp.where(mask, jnp.exp(s - m), 0.0)
    l = jnp.sum(e, axis=-1, keepdims=True)
    return e / jnp.where(l > 0.0, l, 1.0)


def _nsa_kernel(q_ref, g_ref, kct_ref, vc_ref, kst_ref, vs_ref, kwt_ref, vw_ref, c2st_ref, o_ref, *, n_sel):
    G = NSA_GROUP
    qb = pl.program_id(2)
    t0 = pl.multiple_of(qb * TQ, TQ)
    q = q_ref[...]
    lane_g = lax.broadcasted_iota(jnp.int32, (TQ, GW), 1) // HEAD_DIM
    q4 = jnp.concatenate([jnp.where(lane_g == h, q, jnp.zeros_like(q)) for h in range(G)], axis=0)
    tpos = t0 + lax.broadcasted_iota(jnp.int32, (TQ, 1), 0)

    def heads(x):
        return x.reshape(G, TQ, x.shape[-1])

    def attend(p, v):
        return heads(jnp.dot(p.reshape(G * TQ, p.shape[-1]).astype(jnp.bfloat16), v,
                             preferred_element_type=jnp.float32))

    starts = [t0 - WINDOW + TQ * c for c in range(WINDOW // TQ + 1)]
    reads = [pl.multiple_of(jnp.maximum(s, 0), TQ) for s in starts]
    kwt = jnp.concatenate([kwt_ref[0, 0, :, pl.ds(r, TQ)] for r in reads], axis=1)
    vw = jnp.concatenate([vw_ref[0, 0, pl.ds(r, TQ), :] for r in reads], axis=0)
    wk = WINDOW + TQ
    kposw = t0 - WINDOW + lax.broadcasted_iota(jnp.int32, (TQ, wk), 1)
    diff = tpos - kposw
    maskw = (kposw >= 0) & (diff >= 0) & (diff < WINDOW)

    kct = kct_ref[0, 0]
    ncp = kct.shape[1]
    ncol = lax.broadcasted_iota(jnp.int32, (TQ, ncp), 1)
    maskc = (ncol * CMP_STRIDE + (CMP_BLOCK - 1)) <= tpos
    sc = heads(jnp.dot(q4, kct, preferred_element_type=jnp.float32))
    sw = heads(jnp.dot(q4, kwt, preferred_element_type=jnp.float32))
    pc = _msoftmax(sc, maskc[None])
    oc = attend(pc, vc_ref[0, 0])
    pw = _msoftmax(sw, maskw[None])
    ow = attend(pw, vw)
    pc_sum = jnp.sum(pc, axis=0)

    c2st = c2st_ref[...]
    jp = c2st.shape[0]
    hi = pc_sum.astype(jnp.bfloat16)
    lo = (pc_sum - hi.astype(jnp.float32)).astype(jnp.bfloat16)
    dn = (((1,), (1,)), ((), ()))
    imp_t = (lax.dot_general(c2st, hi, dn, preferred_element_type=jnp.float32)
             + lax.dot_general(c2st, lo, dn, preferred_element_type=jnp.float32))
    jrow = lax.broadcasted_iota(jnp.int32, (jp, TQ), 0)
    cur = (t0 + lax.broadcasted_iota(jnp.int32, (jp, TQ), 1)) // SLC_BLOCK
    forced = (jrow == 0) | (jrow == cur) | (jrow == cur - 1)
    score = jnp.where(jrow <= cur, imp_t + jnp.where(forced, FORCE_BONUS, 0.0), NEG)
    nblk = kst_ref.shape[3] // SLC_BLOCK
    nrow = -(-nblk // 8) * 8
    sc, jr = score[:nrow], jrow[:nrow]
    rank = jnp.zeros((nrow, TQ), jnp.int32)
    for i in range(nblk):
        si = sc[i:i + 1, :]
        rank = rank + ((si > sc) | ((si == sc) & (jr > i))).astype(jnp.int32)
    sel = ((rank < n_sel) & (sc > 0.5 * NEG)).astype(jnp.float32)
    if nrow < jp:
        sel = jnp.concatenate([sel, jnp.zeros((jp - nrow, TQ), jnp.float32)], axis=0)
    selb = sel.T.astype(jnp.bfloat16)

    nchunks = (t0 + TQ + KC - 1) // KC

    def chunk(c, carry):
        m_old, l_old, acc = carry
        k0 = pl.multiple_of(c * KC, KC)
        kt = kst_ref[0, 0, :, pl.ds(k0, KC)]
        v = vs_ref[0, 0, pl.ds(k0, KC), :]
        kblk = (k0 + lax.broadcasted_iota(jnp.int32, (jp, KC), 1)) // SLC_BLOCK
        expand = (kblk == lax.broadcasted_iota(jnp.int32, (jp, KC), 0)).astype(jnp.bfloat16)
        member = jnp.dot(selb, expand, preferred_element_type=jnp.float32) > 0.5
        kpos = k0 + lax.broadcasted_iota(jnp.int32, (TQ, KC), 1)
        msk = (member & (kpos <= tpos))[None]
        s = jnp.where(msk, heads(jnp.dot(q4, kt, preferred_element_type=jnp.float32)), NEG)
        m_new = jnp.maximum(m_old, jnp.max(s, axis=-1, keepdims=True))
        alpha = jnp.exp(m_old - m_new)
        p = jnp.where(msk, jnp.exp(s - m_new), 0.0)
        l_new = alpha * l_old + jnp.sum(p, axis=-1, keepdims=True)
        return m_new, l_new, alpha * acc + attend(p, v)

    def chunk_pair(i, carry):
        return chunk(2 * i + 1, chunk(2 * i, carry))

    init = (jnp.full((G, TQ, 1), NEG, jnp.float32), jnp.zeros((G, TQ, 1), jnp.float32),
            jnp.zeros((G, TQ, GW), jnp.float32))
    _, l_s, acc_s = lax.fori_loop(0, (nchunks + 1) // 2, chunk_pair, init)
    os_ = acc_s / jnp.where(l_s > 0.0, l_s, 1.0)

    sig = jax.nn.sigmoid(g_ref[...])
    out = jnp.zeros((TQ, GW), jnp.float32)
    for h in range(G):
        o_h = (sig[:, 3 * h:3 * h + 1] * oc[h] + sig[:, 3 * h + 1:3 * h + 2] * os_[h]
               + sig[:, 3 * h + 2:3 * h + 3] * ow[h])
        out = out + jnp.where(lane_g == h, o_h, 0.0)
    o_ref[...] = out.astype(o_ref.dtype)


def nsa_core(q, z, kct, vc, kst, vs, kwt, vw, B, S):
    n = B * S
    ncp = kct.shape[3]
    ncmp = (S - CMP_BLOCK) // CMP_STRIDE + 1
    nslc = S // SLC_BLOCK
    n_sel = min(SLC_TOPK, nslc)
    jp = LANE
    assert nslc <= jp and S % (2 * KC) == 0 and S % TQ == 0 and WINDOW % TQ == 0
    cs = np.arange(ncmp) * CMP_STRIDE
    ss = np.arange(nslc) * SLC_BLOCK
    ov = np.minimum(cs[:, None] + CMP_BLOCK, ss[None, :] + SLC_BLOCK) - np.maximum(cs[:, None], ss[None, :])
    c2s = np.zeros((ncp, jp), np.float32)
    c2s[:ncmp, :nslc] = np.clip(ov, 0, None) / CMP_BLOCK
    c2st = jnp.asarray(c2s.T, dtype=jnp.bfloat16)
    nqb = S // TQ

    def kv_spec(shp):
        return pl.BlockSpec((1, 1) + shp, lambda b, k, i: (b, k, 0, 0))

    return pl.pallas_call(
        functools.partial(_nsa_kernel, n_sel=n_sel),
        grid=(B, NSA_KV_HEADS, nqb),
        in_specs=[pl.BlockSpec((TQ, GW), lambda b, k, i: (b * nqb + i, k)),
                  pl.BlockSpec((TQ, LANE), lambda b, k, i: (b * nqb + i, Z_NG // LANE + k)),
                  kv_spec((GW, ncp)), kv_spec((ncp, GW)),
                  kv_spec((GW, S)), kv_spec((S, GW)),
                  kv_spec((GW, S)), kv_spec((S, GW)),
                  pl.BlockSpec((jp, ncp), lambda b, k, i: (0, 0))],
        out_specs=pl.BlockSpec((TQ, GW), lambda b, k, i: (b * nqb + i, k)),
        out_shape=jax.ShapeDtypeStruct((n, NSA_KV_HEADS * GW), jnp.bfloat16),
        compiler_params=pltpu.CompilerParams(dimension_semantics=("parallel", "parallel", "arbitrary"),
                                             vmem_limit_bytes=VMEM_LIMIT),
        name="nsa_core",
    )(q, z, kct, vc, kst, vs, kwt, vw, c2st)


def _gla_consts():
    C = GLA_CHUNK
    i = np.arange(C)
    mats = [np.tril(np.ones((C, C), np.float32))]
    for s in GLA_LEVELS:
        r = (i // (2 * s)) * 2 * s + s - 1
        upper = (i // s) % 2 == 1
        m = np.arange(C)[None, :]
        mq = ((m > r[:, None]) & (m <= i[:, None]) & upper[:, None]).astype(np.float32)
        mk = ((m > i[:, None]) & (m <= r[:, None]) & (~upper)[:, None]).astype(np.float32)
        mats += [mq, mk]
    return np.concatenate(mats, axis=0)


def _split3(x):
    h = x.astype(jnp.bfloat16)
    r = x - h.astype(jnp.float32)
    m = r.astype(jnp.bfloat16)
    l = (r - m.astype(jnp.float32)).astype(jnp.bfloat16)
    return h, m, l


def _gla_kernel(q_ref, k_ref, v_ref, r_ref, glr_ref, gw_ref, gb_ref, ng_ref, msel_ref, o_ref, state_ref):
    C = GLA_CHUNK

    @pl.when(pl.program_id(1) == 0)
    def _():
        state_ref[...] = jnp.zeros_like(state_ref)

    ii = lax.broadcasted_iota(jnp.int32, (C, C), 0)
    jj = lax.broadcasted_iota(jnp.int32, (C, C), 1)
    lane_h = lax.broadcasted_iota(jnp.int32, (C, GLA_QK), 1) // GLA_DK
    bd = (lax.broadcasted_iota(jnp.int32, (GLA_QK, GLA_V), 0) // GLA_DK
          == lax.broadcasted_iota(jnp.int32, (GLA_QK, GLA_V), 1) // GLA_DV)
    msel = msel_ref[...]
    gw = gw_ref[...].astype(jnp.bfloat16)
    dn = (((1,), (1,)), ((), ()))

    def chunk(c, carry):
        r0 = pl.multiple_of(c * C, C)
        q = q_ref[pl.ds(r0, C), :] * (GLA_DK ** -0.5)
        k = k_ref[pl.ds(r0, C), :]
        v = v_ref[pl.ds(r0, C), :].astype(jnp.bfloat16)
        z = jnp.dot(glr_ref[pl.ds(r0, C), :].astype(jnp.bfloat16), gw, preferred_element_type=jnp.float32) + gb_ref[...]
        a = jax.nn.log_sigmoid(z) / GLA_GATE_TEMP
        ah, am, al = _split3(a)
        ex = (jnp.dot(msel, ah, preferred_element_type=jnp.float32)
              + jnp.dot(msel, am, preferred_element_type=jnp.float32)
              + jnp.dot(msel, al, preferred_element_type=jnp.float32))
        bcum = ex[0:C]
        state = state_ref[...]
        o = jnp.dot((q * jnp.exp(bcum)).astype(jnp.bfloat16), state.astype(jnp.bfloat16),
                    preferred_element_type=jnp.float32)
        kb = k.astype(jnp.bfloat16)
        for h in range(GLA_HEADS):
            hm = lane_h == h
            att = jnp.where(ii == jj, lax.dot_general(jnp.where(hm, q, 0.0).astype(jnp.bfloat16), kb, dn,
                                                      preferred_element_type=jnp.float32), 0.0)
            for li, s in enumerate(GLA_LEVELS):
                eq = jnp.exp(ex[(1 + 2 * li) * C:(2 + 2 * li) * C])
                ek = jnp.exp(ex[(2 + 2 * li) * C:(3 + 2 * li) * C])
                qs = jnp.where(hm, q * eq, 0.0).astype(jnp.bfloat16)
                ks = (k * ek).astype(jnp.bfloat16)
                blk = (ii // (2 * s) == jj // (2 * s)) & ((ii // s) % 2 == 1) & ((jj // s) % 2 == 0)
                att = att + jnp.where(blk, lax.dot_general(qs, ks, dn, preferred_element_type=jnp.float32), 0.0)
            oh = jnp.dot(att.astype(jnp.bfloat16), v, preferred_element_type=jnp.float32)
            o = o + jnp.where(lax.broadcasted_iota(jnp.int32, (C, GLA_V), 1) // GLA_DV == h, oh, 0.0)
        blast = bcum[C - 1:C, :]
        kd_t = (k * jnp.exp(blast - bcum)).T.astype(jnp.bfloat16)
        decay_col = jnp.exp(jnp.sum(a.T, axis=1, keepdims=True))
        upd = jnp.dot(kd_t, v, preferred_element_type=jnp.float32)
        state_ref[...] = state * decay_col + jnp.where(bd, upd, 0.0)
        outs = []
        for h in range(GLA_HEADS):
            oh = o[:, h * GLA_DV:(h + 1) * GLA_DV]
            outs.append(oh * lax.rsqrt(jnp.mean(oh * oh, axis=1, keepdims=True) + EPS) * ng_ref[...])
        on = jnp.concatenate(outs, axis=1)
        r = r_ref[pl.ds(r0, C), :]
        o_ref[pl.ds(r0, C), :] = (on * (r * jax.nn.sigmoid(r))).astype(o_ref.dtype)
        return carry

    lax.fori_loop(0, q_ref.shape[0] // C, chunk, 0)


def gla_pallas(z, gate_w, gate_b, norm_g, B, S):
    n = B * S
    tb = GLA_TB
    nb = S // tb
    gw = jnp.pad(gate_w, ((0, LANE - gate_w.shape[0]), (0, 0)))
    msel = jnp.asarray(_gla_consts(), dtype=jnp.bfloat16)
    zb = lambda col, w: pl.BlockSpec((tb, w), lambda b, i: (b * nb + i, col // w))
    full = lambda shp: pl.BlockSpec(shp, lambda b, i: (0,) * len(shp))
    return pl.pallas_call(
        _gla_kernel,
        grid=(B, nb),
        in_specs=[zb(Z_GQ, GLA_QK), zb(Z_GK, GLA_QK), zb(Z_GV, GLA_V), zb(Z_GR, GLA_V), zb(Z_GLR, LANE),
                  full((LANE, GLA_QK)), full((1, GLA_QK)), full((1, GLA_DV)), full(msel.shape)],
        out_specs=pl.BlockSpec((tb, GLA_V), lambda b, i: (b * nb + i, 0)),
        out_shape=jax.ShapeDtypeStruct((n, GLA_V), jnp.bfloat16),
        scratch_shapes=[pltpu.VMEM((GLA_QK, GLA_V), jnp.float32)],
        compiler_params=pltpu.CompilerParams(dimension_semantics=("parallel", "arbitrary")),
        name="gla",
    )(z, z, z, z, z, gw, gate_b.reshape(1, -1), norm_g.reshape(1, -1), msel)


_FULL_BLOCKS = 8


def _extract_top(s, k, tie_break):
    n, t = s.shape
    row = lax.broadcasted_iota(jnp.int32, (n, t), 0)
    rank = jnp.full((n, t), k, jnp.int32)
    vals = []
    for r in range(k):
        m = jnp.max(s, axis=0, keepdims=True)
        hit = s == m
        if tie_break:
            hit = row == jnp.min(jnp.where(hit, row, n), axis=0, keepdims=True)
        rank = jnp.where(hit, r, rank)
        s = jnp.where(hit, float('-inf'), s)
        vals.append(m)
    return jnp.concatenate(vals, axis=0), rank


def _peer_gate_math(s1, s2, tie_break):
    K = PEER_TOPK
    t = s1.shape[1]
    v1, rank1 = _extract_top(s1, K, tie_break)
    v2, rank2 = _extract_top(s2, K, tie_break)
    blocks, poss = [], []
    row16 = lax.broadcasted_iota(jnp.int32, (K, t), 0)
    for r1 in range(_FULL_BLOCKS):
        c = v1[r1:r1 + 1, :] + v2
        blocks.append(jnp.where(row16 < K // (r1 + 1), c, float('-inf')))
        poss.append(row16 + r1 * K)
    row8 = lax.broadcasted_iota(jnp.int32, (K - _FULL_BLOCKS, t), 0)
    blocks.append(v1[_FULL_BLOCKS:, :] + v2[0:1, :])
    poss.append((row8 + _FULL_BLOCKS) * K)
    sels = [jnp.zeros(b.shape, jnp.bool_) for b in blocks]
    big = K * K
    for _ in range(K):
        m = blocks[0].max(axis=0, keepdims=True)
        for b in blocks[1:]:
            m = jnp.maximum(m, b.max(axis=0, keepdims=True))
        if tie_break:
            first = None
            for b, p in zip(blocks, poss):
                f = jnp.min(jnp.where(b == m, p, big), axis=0, keepdims=True)
                first = f if first is None else jnp.minimum(first, f)
            hits = [p == first for p in poss]
        else:
            hits = [b == m for b in blocks]
        for i, hit in enumerate(hits):
            sels[i] = sels[i] | hit
            blocks[i] = jnp.where(hit, float('-inf'), blocks[i])
    top = v1[0:1, :] + v2[0:1, :]
    cnt_rows = []
    z = jnp.zeros((1, t), jnp.float32)
    for r1 in range(_FULL_BLOCKS):
        sel = sels[r1]
        cnt_rows.append(jnp.sum(sel.astype(jnp.float32), axis=0, keepdims=True))
        c = v1[r1:r1 + 1, :] + v2
        z = z + jnp.sum(jnp.where(sel, jnp.exp(c - top), 0.0), axis=0, keepdims=True)
    sel = sels[_FULL_BLOCKS]
    cnt_tail = sel.astype(jnp.float32)
    c = v1[_FULL_BLOCKS:, :] + v2[0:1, :]
    z = z + jnp.sum(jnp.where(sel, jnp.exp(c - top), 0.0), axis=0, keepdims=True)
    cnt = jnp.concatenate(cnt_rows + [cnt_tail], axis=0)
    a = jnp.zeros(s1.shape, jnp.float32)
    for r in range(K):
        a = jnp.where(rank1 == r, cnt[r:r + 1, :], a)
    taken = (jnp.sum((rank1 < K).astype(jnp.float32), axis=0, keepdims=True),
             jnp.sum((rank2 < K).astype(jnp.float32), axis=0, keepdims=True),
             jnp.sum(cnt, axis=0, keepdims=True))
    return jnp.exp(s1 - v1[0:1, :]), a, rank2.astype(jnp.float32), jnp.exp(s2 - v2[0:1, :]) / z, taken


def _peer_gate_kernel(q_ref, keys_ref, e1_ref, a_ref, r2_ref, e2_ref):
    q = q_ref[...].astype(jnp.bfloat16)
    k1 = keys_ref[0, 0].astype(jnp.bfloat16)
    k2 = keys_ref[0, 1].astype(jnp.bfloat16)
    dn = (((1,), (1,)), ((), ()))
    s1 = lax.dot_general(k1, q[:, :PEER_DKEY // 2], dn, preferred_element_type=jnp.float32)
    s2 = lax.dot_general(k2, q[:, PEER_DKEY // 2:], dn, preferred_element_type=jnp.float32)

    def write(e1, a, r2, e2):
        e1_ref[0] = e1
        a_ref[0] = a
        r2_ref[0] = r2.astype(jnp.bfloat16)
        e2_ref[0] = e2.astype(jnp.bfloat16)

    e1, a, r2, e2, taken = _peer_gate_math(s1, s2, tie_break=False)
    write(e1, a, r2, e2)
    excess = sum(jnp.max(jnp.abs(c - PEER_TOPK)) for c in taken)

    @pl.when(excess > 0.0)
    def _():
        write(*_peer_gate_math(s1, s2, tie_break=True)[:4])


def peer_gates(qh, keys, tt=256):
    n = qh.shape[0]
    H = PEER_HEADS
    shp32 = jax.ShapeDtypeStruct((H, PEER_NKEYS, n), jnp.float32)
    shp16 = jax.ShapeDtypeStruct((H, PEER_NKEYS, n), jnp.bfloat16)
    ospec = pl.BlockSpec((1, PEER_NKEYS, tt), lambda i, h: (h, 0, i))
    return pl.pallas_call(
        _peer_gate_kernel,
        grid=(n // tt, H),
        in_specs=[pl.BlockSpec((tt, PEER_DKEY), lambda i, h: (i, h)),
                  pl.BlockSpec((1, 2, PEER_NKEYS, PEER_DKEY // 2), lambda i, h: (h, 0, 0, 0))],
        out_specs=[ospec, ospec, ospec, ospec],
        out_shape=[shp32, shp32, shp16, shp16],
        compiler_params=pltpu.CompilerParams(dimension_semantics=("parallel", "parallel")),
        name="peer_gates",
    )(qh, keys)


def _peer_dense_kernel(xt_ref, u_ref, vt_ref, e1_ref, a_ref, r2_ref, e2_ref, x1_ref, gate_ref, fg_ref,
                       o_ref, acc_ref, *, ec, final):
    j = pl.program_id(1)
    nsub = ec // PEER_NKEYS

    @pl.when(j == 0)
    def _():
        acc_ref[...] = jnp.zeros_like(acc_ref)

    def gate_mat(i):
        a_idx = j * nsub + i
        g = None
        for h in range(PEER_HEADS):
            cnt = a_ref[h, pl.ds(a_idx, 1), :].astype(jnp.bfloat16)
            e1 = e1_ref[h, pl.ds(a_idx, 1), :].astype(jnp.bfloat16)
            term = jnp.where(r2_ref[h] < cnt, e2_ref[h], jnp.bfloat16(0)) * e1
            g = term if g is None else g + term
        return g

    def scores(i):
        return jnp.dot(u_ref[i * PEER_NKEYS:(i + 1) * PEER_NKEYS, :], xt_ref[...], preferred_element_type=jnp.float32)

    st = scores(0)
    ws = []
    for i in range(nsub):
        st_next = scores(i + 1) if i + 1 < nsub else None
        ws.append(_gelu(st).astype(jnp.bfloat16) * gate_mat(i))
        if i % PEER_GROUPS_PER_DOT == PEER_GROUPS_PER_DOT - 1:
            w = jnp.concatenate(ws, axis=0)
            ws = []
            acc_ref[...] += jnp.dot(vt_ref[:, (i + 1 - PEER_GROUPS_PER_DOT) * PEER_NKEYS:(i + 1) * PEER_NKEYS], w,
                                    preferred_element_type=jnp.float32)
        st = st_next

    @pl.when(j == pl.num_programs(1) - 1)
    def _():
        x2 = x1_ref[...] + gate_ref[0] * acc_ref[...].T
        if final:
            x2 = x2 * lax.rsqrt(jnp.mean(x2 * x2, axis=1, keepdims=True) + EPS) * fg_ref[...]
        o_ref[...] = x2


def peer_dense(xt, u16, vt16, e1, a, r2, e2, x1, gate2, final_g, S, final, tt=512, ec=1024):
    d, n = xt.shape
    ne = u16.shape[0]
    H = PEER_HEADS
    per_b = S // tt
    gspec = lambda: pl.BlockSpec((H, PEER_NKEYS, tt), lambda i, j: (0, 0, i))
    return pl.pallas_call(
        functools.partial(_peer_dense_kernel, ec=ec, final=final),
        grid=(n // tt, ne // ec),
        in_specs=[pl.BlockSpec((d, tt), lambda i, j: (0, i)),
                  pl.BlockSpec((ec, d), lambda i, j: (j, 0)),
                  pl.BlockSpec((d, ec), lambda i, j: (0, j)),
                  gspec(), gspec(), gspec(), gspec(),
                  pl.BlockSpec((tt, d), lambda i, j: (i, 0)),
                  pl.BlockSpec((1, 1, d), lambda i, j: (i // per_b, 0, 0)),
                  pl.BlockSpec((1, d), lambda i, j: (0, 0))],
        out_specs=pl.BlockSpec((tt, d), lambda i, j: (i, 0)),
        out_shape=jax.ShapeDtypeStruct((n, d), jnp.float32),
        scratch_shapes=[pltpu.VMEM((d, tt), jnp.float32)],
        compiler_params=pltpu.CompilerParams(dimension_semantics=("parallel", "arbitrary"),
                                             vmem_limit_bytes=VMEM_LIMIT),
        name="peer_dense",
    )(xt, u16, vt16, e1, a, r2, e2, x1, gate2, final_g.reshape(1, d))


def adaln(c, w, b):
    mod = pmm(jax.nn.silu(c), w) + b
    shift, scale, gate = jnp.split(mod[:, None, :], 3, axis=-1)
    return shift, 1.0 + scale, gate


def kernel(x, c, positions, ada_w, ada_b, norm_g, w_in, conv_dw_w, conv_dw_b, conv_ln_g, conv_ln_b, w_conv_up,
           cmp_pos, cmp_w1, cmp_w2, w_nsa_up, gla_gate_w, gla_gate_b, gla_norm_g, w_gla_up, w_out,
           peer_wq, peer_keys, peer_u, peer_v, final_g):
    B, S, D = x.shape
    bf = jnp.bfloat16
    x2 = x.reshape(B * S, D)
    cs = rope_tables(positions)
    for l in range(DEPTH):
        shift, scale1p, gate = adaln(c, ada_w[l, 0], ada_b[l, 0])
        z = in_proj(x2, norm_g[l, 0], scale1p, shift, pack_w_in(w_in[l]), S)
        hc = conv_pallas(z, conv_dw_w[l], conv_dw_b[l], conv_ln_g[l], conv_ln_b[l], B, S)
        q, kst, vs, kwt, vw = nsa_prep(z, cs, B, S)
        kct, vc = nsa_compress(z, positions, cmp_pos[l], cmp_w1[l], cmp_w2[l], B, S)
        on = nsa_core(q, z, kct, vc, kst, vs, kwt, vw, B, S)
        og = gla_pallas(z, gla_gate_w[l], gla_gate_b[l], gla_norm_g[l], B, S)
        shift2, scale2p, gate2 = adaln(c, ada_w[l, 1], ada_b[l, 1])
        x1, ut, qh = mix_out(hc, on, og, z, x2, gate, w_conv_up[l].astype(bf), w_nsa_up[l].astype(bf),
                             w_gla_up[l].astype(bf), w_out[l].astype(bf), norm_g[l, 1], scale2p, shift2,
                             peer_wq[l].astype(bf), S)
        e1, a, r2, e2 = peer_gates(qh, peer_keys[l])
        x2 = peer_dense(ut, peer_u[l].astype(bf), peer_v[l].astype(bf).T, e1, a, r2, e2, x1, gate2, final_g,
                        S, final=(l == DEPTH - 1))
    return x2.reshape(B, S, D)
```

```python
import functools

import jax
import jax.numpy as jnp
import numpy as np
from jax import lax
from jax.experimental import pallas as pl
from jax.experimental.pallas import tpu as pltpu

D_MODEL = 1024
DEPTH = 2
EPS = 1e-6
NEG = -1e30
N_BRANCH = 3
CONV_DIM = 512
CONV_WIDTH = 31
NSA_HEADS = 8
NSA_KV_HEADS = 2
NSA_GROUP = NSA_HEADS // NSA_KV_HEADS
HEAD_DIM = 64
CMP_BLOCK = 32
CMP_STRIDE = 16
CMP_HIDDEN = 256
SLC_BLOCK = 64
SLC_TOPK = 16
WINDOW = 512
FORCE_BONUS = 1e4
ROPE_THETA = 10000.0
GLA_HEADS = 4
GLA_DK = 64
GLA_DV = 128
GLA_GATE_RANK = 16
GLA_GATE_TEMP = 16.0
GLA_CHUNK = 64
PEER_HEADS = 8
PEER_NKEYS = 128
PEER_DKEY = 256
PEER_TOPK = 16
NSA_Q = NSA_HEADS * HEAD_DIM
NSA_KV = 3 * 2 * NSA_KV_HEADS * HEAD_DIM
NSA_G = 3 * NSA_HEADS
GLA_QK = GLA_HEADS * GLA_DK
GLA_V = GLA_HEADS * GLA_DV
SPLITS = [2 * CONV_DIM, NSA_Q, NSA_KV, NSA_G, GLA_QK, GLA_QK, GLA_V, GLA_V, GLA_GATE_RANK, N_BRANCH * D_MODEL]

LANE = 128
TQ = 128
KC = 512
GW = NSA_GROUP * HEAD_DIM
VMEM_LIMIT = 48 * 1024 * 1024
GLA_LEVELS = (32, 16, 8, 4, 2, 1)
GLA_TB = 512
CONV_TS = 512
CONV_HALO = 32
ROW_TILE = 1024
MIX_TILE = 256
PEER_GROUPS_PER_DOT = 2

Z_MG = 0
Z_CONV = Z_MG + N_BRANCH * D_MODEL
Z_Q = Z_CONV + 2 * CONV_DIM
Z_QSW = Z_Q + NSA_Q
Z_GV = Z_QSW + NSA_Q
Z_GR = Z_GV + GLA_V
Z_GQ = Z_GR + GLA_V
Z_GK = Z_GQ + GLA_QK
Z_CMP = Z_GK + GLA_QK
Z_SLC = Z_CMP + 256
Z_WIN = Z_SLC + 384
Z_NG = Z_WIN + 384
Z_GLR = Z_NG + NSA_KV_HEADS * LANE
NZ = Z_GLR + LANE
assert Z_SLC % 384 == 0 and Z_WIN % 384 == 0 and Z_CONV % (2 * CONV_DIM) == 0 and Z_GQ % GLA_QK == 0


def _mm_kernel(a_ref, b_ref, o_ref):
    o_ref[...] = jnp.dot(a_ref[...].astype(jnp.bfloat16), b_ref[...].astype(jnp.bfloat16),
                         preferred_element_type=jnp.float32)


def _pick(n, cands):
    for c in cands:
        if n % c == 0:
            return c
    return n


def pmm(a, b):
    m, k = a.shape
    n = b.shape[1]
    n_pad = -n % LANE
    if n_pad:
        b = jnp.pad(b, ((0, 0), (0, n_pad)))
    m_pad = -m % 8
    if m_pad:
        a = jnp.pad(a, ((0, m_pad), (0, 0)))
    mp, np_ = m + m_pad, n + n_pad
    tm = _pick(mp, (512, 256, 128, 64, 32, 16, 8))
    tn = _pick(np_, (512, 640, 384, 256, 128))
    out = pl.pallas_call(
        _mm_kernel,
        grid=(mp // tm, np_ // tn),
        in_specs=[pl.BlockSpec((tm, k), lambda i, j: (i, 0)),
                  pl.BlockSpec((k, tn), lambda i, j: (0, j))],
        out_specs=pl.BlockSpec((tm, tn), lambda i, j: (i, j)),
        out_shape=jax.ShapeDtypeStruct((mp, np_), jnp.float32),
        compiler_params=pltpu.CompilerParams(dimension_semantics=("parallel", "parallel")),
        name="pmm",
    )(a, b)
    return out[:m, :n]


def _in_proj_kernel(x_ref, g_ref, sc_ref, sh_ref, w_ref, z_ref, u_ref):
    @pl.when(pl.program_id(1) == 0)
    def _():
        x = x_ref[...]
        y = x * lax.rsqrt(jnp.mean(x * x, axis=1, keepdims=True) + EPS) * g_ref[...]
        u_ref[...] = (y * sc_ref[0] + sh_ref[0]).astype(jnp.bfloat16)

    z_ref[...] = jnp.dot(u_ref[...], w_ref[...], preferred_element_type=jnp.float32)


def in_proj(x2, g, scale1p, shift, w16, S):
    n, d = x2.shape
    nz = w16.shape[1]
    tm = ROW_TILE
    tn = _pick(nz, (1152, 1024, 896, 512, 384, 256, 128))
    per_b = S // tm
    mod = lambda: pl.BlockSpec((1, 1, d), lambda i, j: (i // per_b, 0, 0))
    return pl.pallas_call(
        _in_proj_kernel,
        grid=(n // tm, nz // tn),
        in_specs=[pl.BlockSpec((tm, d), lambda i, j: (i, 0)),
                  pl.BlockSpec((1, d), lambda i, j: (0, 0)), mod(), mod(),
                  pl.BlockSpec((d, tn), lambda i, j: (0, j))],
        out_specs=pl.BlockSpec((tm, tn), lambda i, j: (i, j)),
        out_shape=jax.ShapeDtypeStruct((n, nz), jnp.float32),
        scratch_shapes=[pltpu.VMEM((tm, d), jnp.bfloat16)],
        compiler_params=pltpu.CompilerParams(dimension_semantics=("parallel", "arbitrary"),
                                             vmem_limit_bytes=VMEM_LIMIT),
        name="in_proj",
    )(x2, g.reshape(1, d), scale1p, shift, w16)


def _half_swap(n_heads):
    idx = np.arange(n_heads * HEAD_DIM).reshape(n_heads, 2, HEAD_DIM // 2)
    return idx[:, ::-1, :].reshape(-1)


def pack_w_in(w_in):
    o = np.cumsum([0] + SPLITS)
    a_conv, nq, nkv, ng, gq, gk, gv, gr, glr, mg = [w_in[:, o[i]:o[i + 1]] for i in range(len(SPLITS))]
    kvw = 2 * NSA_KV_HEADS * HEAD_DIM
    kw = NSA_KV_HEADS * HEAD_DIM
    d = w_in.shape[0]
    sw = _half_swap(NSA_KV_HEADS)
    cols = [mg, a_conv, nq, nq[:, _half_swap(NSA_HEADS)], gv, gr, gq, gk, nkv[:, 0:kvw]]
    for br in (1, 2):
        k = nkv[:, br * kvw:br * kvw + kw]
        cols += [k, k[:, sw], nkv[:, br * kvw + kw:(br + 1) * kvw]]
    gpk = NSA_GROUP * 3
    for kvh in range(NSA_KV_HEADS):
        cols += [ng[:, kvh * gpk:(kvh + 1) * gpk], jnp.zeros((d, LANE - gpk), w_in.dtype)]
    cols += [glr, jnp.zeros((d, LANE - GLA_GATE_RANK), w_in.dtype)]
    w = jnp.concatenate(cols, axis=1)
    assert w.shape[1] == NZ
    return w.astype(jnp.bfloat16)


def rope_tables(positions):
    half = HEAD_DIM // 2
    freq = ROPE_THETA ** (-jnp.arange(half, dtype=jnp.float32) / half)
    ang = positions.astype(jnp.float32)[..., None] * freq
    cos, sin = jnp.cos(ang), jnp.sin(ang)
    c = jnp.tile(jnp.concatenate([cos, cos], -1), (1, 1, LANE // HEAD_DIM))
    s = jnp.tile(jnp.concatenate([-sin, sin], -1), (1, 1, LANE // HEAD_DIM))
    return jnp.concatenate([c, s], -1).reshape(-1, 2 * LANE)


def _nsa_prep_kernel(q_ref, qs_ref, slc_ref, win_ref, cs_ref, sel_ref, qo_ref, kst_ref, vs_ref, kwt_ref, vw_ref):
    cos = cs_ref[:, :LANE]
    sin = cs_ref[:, LANE:]
    scale = HEAD_DIM ** -0.5
    for t in range(NSA_Q // LANE):
        sl = slice(t * LANE, (t + 1) * LANE)
        qo_ref[:, sl] = ((q_ref[:, sl] * cos + qs_ref[:, sl] * sin) * scale).astype(jnp.bfloat16)
    dn = (((1,), (1,)), ((), ()))
    for src, kt_ref, v_ref in ((slc_ref, kst_ref, vs_ref), (win_ref, kwt_ref, vw_ref)):
        k = (src[:, 0:LANE] * cos + src[:, LANE:2 * LANE] * sin).astype(jnp.bfloat16)
        v = src[:, 2 * LANE:3 * LANE].astype(jnp.bfloat16)
        for kvh in range(NSA_KV_HEADS):
            sel = sel_ref[kvh]
            kt_ref[0, kvh] = lax.dot_general(sel, k, dn, preferred_element_type=jnp.float32).astype(jnp.bfloat16)
            v_ref[0, kvh] = lax.dot_general(v, sel, dn, preferred_element_type=jnp.float32).astype(jnp.bfloat16)


def _head_repeat_sel():
    sel = np.zeros((NSA_KV_HEADS, GW, LANE), np.float32)
    for kvh in range(NSA_KV_HEADS):
        for r in range(GW):
            sel[kvh, r, kvh * HEAD_DIM + r % HEAD_DIM] = 1.0
    return jnp.asarray(sel, dtype=jnp.bfloat16)


def nsa_prep(z, cs, B, S):
    n = B * S
    tp = ROW_TILE
    nb = S // tp
    zb = lambda col, w: pl.BlockSpec((tp, w), lambda b, i: (b * nb + i, col // w))
    kt = lambda: pl.BlockSpec((1, NSA_KV_HEADS, GW, tp), lambda b, i: (b, 0, 0, i))
    vv = lambda: pl.BlockSpec((1, NSA_KV_HEADS, tp, GW), lambda b, i: (b, 0, i, 0))
    kt_shape = jax.ShapeDtypeStruct((B, NSA_KV_HEADS, GW, S), jnp.bfloat16)
    v_shape = jax.ShapeDtypeStruct((B, NSA_KV_HEADS, S, GW), jnp.bfloat16)
    return pl.pallas_call(
        _nsa_prep_kernel,
        grid=(B, nb),
        in_specs=[zb(Z_Q, NSA_Q), zb(Z_QSW, NSA_Q), zb(Z_SLC, 384), zb(Z_WIN, 384),
                  pl.BlockSpec((tp, 2 * LANE), lambda b, i: (b * nb + i, 0)),
                  pl.BlockSpec((NSA_KV_HEADS, GW, LANE), lambda b, i: (0, 0, 0))],
        out_specs=[pl.BlockSpec((tp, NSA_Q), lambda b, i: (b * nb + i, 0)), kt(), vv(), kt(), vv()],
        out_shape=[jax.ShapeDtypeStruct((n, NSA_Q), jnp.bfloat16), kt_shape, v_shape, kt_shape, v_shape],
        compiler_params=pltpu.CompilerParams(dimension_semantics=("parallel", "parallel")),
        name="nsa_prep",
    )(z, z, z, z, cs, _head_repeat_sel())


def _gelu(x):
    return 0.5 * x * (1.0 + jnp.tanh(0.7978845608028654 * (x + 0.044715 * x * x * x)))


def _compress_kernel(gk_ref, gv_ref, pe_ref, w1_ref, w2_ref, cs_ref, kct_ref, vc_ref):
    nrow = gk_ref.shape[2]
    outs = []
    for kv, g_ref in enumerate((gk_ref, gv_ref)):
        g = g_ref[0, 0]
        lo = jnp.dot((g + pe_ref[kv, 0]).astype(jnp.bfloat16), w1_ref[kv, 0], preferred_element_type=jnp.float32)
        hi = jnp.dot((g + pe_ref[kv, 1]).astype(jnp.bfloat16), w1_ref[kv, 1], preferred_element_type=jnp.float32)
        hid = _gelu(lo + pltpu.roll(hi, nrow - 1, 0)).astype(jnp.bfloat16)
        outs.append(jnp.dot(hid, w2_ref[kv], preferred_element_type=jnp.float32))
    k = outs[0][:, :GW] * cs_ref[0, :, :GW] + outs[0][:, GW:] * cs_ref[0, :, GW:]
    kct_ref[0, 0] = k.T.astype(jnp.bfloat16)
    vc_ref[0, 0] = outs[1][:, :GW].astype(jnp.bfloat16)


def nsa_compress(z, positions, cmp_pos, cmp_w1, cmp_w2, B, S):
    ng = S // CMP_STRIDE
    grp = CMP_STRIDE * HEAD_DIM
    c = z[:, Z_CMP:Z_CMP + 256].reshape(B, ng, CMP_STRIDE, 2, NSA_KV_HEADS, HEAD_DIM)
    g = c.transpose(3, 0, 4, 1, 2, 5).reshape(2, B, NSA_KV_HEADS, ng, grp)
    pe = cmp_pos.reshape(2, 2, 1, grp)
    w1 = cmp_w1.reshape(2, 2, grp, CMP_HIDDEN).astype(jnp.bfloat16)
    rep = jnp.tile(cmp_w2, (1, 1, NSA_GROUP))
    sw = np.tile(_half_swap(1), NSA_GROUP) + np.repeat(np.arange(NSA_GROUP) * HEAD_DIM, HEAD_DIM)
    w2 = jnp.concatenate([rep, rep[:, :, sw]], axis=-1).astype(jnp.bfloat16)
    end = jnp.minimum(jnp.arange(ng) * CMP_STRIDE + CMP_BLOCK - 1, S - 1)
    half = HEAD_DIM // 2
    freq = ROPE_THETA ** (-jnp.arange(half, dtype=jnp.float32) / half)
    ang = positions[:, end].astype(jnp.float32)[..., None] * freq
    cos, sin = jnp.cos(ang), jnp.sin(ang)
    cs = jnp.concatenate([jnp.tile(jnp.concatenate([cos, cos], -1), (1, 1, NSA_GROUP)),
                          jnp.tile(jnp.concatenate([-sin, sin], -1), (1, 1, NSA_GROUP))], -1)
    gspec = lambda: pl.BlockSpec((1, 1, ng, grp), lambda b, k: (b, k, 0, 0))
    return pl.pallas_call(
        _compress_kernel,
        grid=(B, NSA_KV_HEADS),
        in_specs=[gspec(), gspec(),
                  pl.BlockSpec((2, 2, 1, grp), lambda b, k: (0, 0, 0, 0)),
                  pl.BlockSpec((2, 2, grp, CMP_HIDDEN), lambda b, k: (0, 0, 0, 0)),
                  pl.BlockSpec((2, CMP_HIDDEN, 2 * GW), lambda b, k: (0, 0, 0)),
                  pl.BlockSpec((1, ng, 2 * GW), lambda b, k: (b, 0, 0))],
        out_specs=[pl.BlockSpec((1, 1, GW, ng), lambda b, k: (b, k, 0, 0)),
                   pl.BlockSpec((1, 1, ng, GW), lambda b, k: (b, k, 0, 0))],
        out_shape=[jax.ShapeDtypeStruct((B, NSA_KV_HEADS, GW, ng), jnp.bfloat16),
                   jax.ShapeDtypeStruct((B, NSA_KV_HEADS, ng, GW), jnp.bfloat16)],
        compiler_params=pltpu.CompilerParams(dimension_semantics=("parallel", "parallel")),
        name="nsa_compress",
    )(g[0], g[1], pe, w1, w2, cs)


def _mix_kernel(hc_ref, on_ref, og_ref, mg_ref, x_ref, gate_ref, wc_ref, wn_ref, wg_ref, wo_ref,
                g2_ref, sc_ref, sh_ref, wq_ref, x1_ref, ut_ref, qh_ref):
    d = D_MODEL
    f32 = jnp.float32
    y = (jax.nn.sigmoid(mg_ref[:, 0:d]) * jnp.dot(hc_ref[...], wc_ref[...], preferred_element_type=f32)
         + jax.nn.sigmoid(mg_ref[:, d:2 * d]) * jnp.dot(on_ref[...], wn_ref[...], preferred_element_type=f32)
         + jax.nn.sigmoid(mg_ref[:, 2 * d:3 * d]) * jnp.dot(og_ref[...], wg_ref[...], preferred_element_type=f32))
    x1 = x_ref[...] + gate_ref[0] * jnp.dot(y.astype(jnp.bfloat16), wo_ref[...], preferred_element_type=f32)
    x1_ref[...] = x1
    u = x1 * lax.rsqrt(jnp.mean(x1 * x1, axis=1, keepdims=True) + EPS) * g2_ref[...] * sc_ref[0] + sh_ref[0]
    ut_ref[...] = u.T.astype(jnp.bfloat16)
    qh_ref[...] = jnp.dot(u.astype(jnp.bfloat16), wq_ref[...], preferred_element_type=f32).astype(qh_ref.dtype)


def mix_out(hc, on, og, z, x2, gate1, wc, wn, wg, wo, g2, scale2p, shift2, wq, S):
    n, d = x2.shape
    tm = MIX_TILE
    per_b = S // tm
    row = lambda w: pl.BlockSpec((tm, w), lambda i: (i, 0))
    mod = lambda: pl.BlockSpec((1, 1, d), lambda i: (i // per_b, 0, 0))
    full = lambda a: pl.BlockSpec(a.shape, lambda i: (0,) * a.ndim)
    nq = wq.shape[1]
    return pl.pallas_call(
        _mix_kernel,
        grid=(n // tm,),
        in_specs=[row(CONV_DIM), row(NSA_Q), row(GLA_V),
                  pl.BlockSpec((tm, N_BRANCH * d), lambda i: (i, Z_MG // (N_BRANCH * d))),
                  row(d), mod(), full(wc), full(wn), full(wg), full(wo),
                  pl.BlockSpec((1, d), lambda i: (0, 0)), mod(), mod(), full(wq)],
        out_specs=[row(d), pl.BlockSpec((d, tm), lambda i: (0, i)), row(nq)],
        out_shape=[jax.ShapeDtypeStruct((n, d), jnp.float32), jax.ShapeDtypeStruct((d, n), jnp.bfloat16),
                   jax.ShapeDtypeStruct((n, nq), jnp.bfloat16)],
        compiler_params=pltpu.CompilerParams(dimension_semantics=("parallel",), vmem_limit_bytes=VMEM_LIMIT),
        name="mix_out",
    )(hc, on, og, z, x2, gate1, wc, wn, wg, wo, g2.reshape(1, d), scale2p, shift2, wq)


def _conv_kernel(a_ref, ah_ref, w_ref, b_ref, g_ref, beta_ref, o_ref, hbuf):
    i = pl.program_id(1)
    ts = a_ref.shape[0]
    a = a_ref[...]
    hbuf[pl.ds(CONV_HALO, ts), :] = a[:, :CONV_DIM] * jax.nn.sigmoid(a[:, CONV_DIM:])
    ah = ah_ref[...]
    halo = ah[:, :CONV_DIM] * jax.nn.sigmoid(ah[:, CONV_DIM:])
    hbuf[pl.ds(0, CONV_HALO), :] = jnp.where(i > 0, halo, 0.0)
    acc = jnp.zeros((ts, CONV_DIM), jnp.float32) + b_ref[...]
    off = CONV_HALO - (CONV_WIDTH - 1)
    for k in range(CONV_WIDTH):
        acc = acc + w_ref[k:k + 1, :] * hbuf[pl.ds(off + k, ts), :]
    mu = jnp.mean(acc, axis=1, keepdims=True)
    d = acc - mu
    var = jnp.mean(d * d, axis=1, keepdims=True)
    y = d * lax.rsqrt(var + EPS) * g_ref[...] + beta_ref[...]
    o_ref[...] = (y * jax.nn.sigmoid(y)).astype(o_ref.dtype)


def conv_pallas(z, dw_w, dw_b, ln_g, ln_b, B, S):
    n = B * S
    ts = CONV_TS
    ns = S // ts
    hb = ts // CONV_HALO
    cb = Z_CONV // (2 * CONV_DIM)
    wpad = jnp.pad(dw_w, ((0, 32 - CONV_WIDTH), (0, 0)))
    vec = lambda: pl.BlockSpec((1, CONV_DIM), lambda b, i: (0, 0))
    return pl.pallas_call(
        _conv_kernel,
        grid=(B, ns),
        in_specs=[pl.BlockSpec((ts, 2 * CONV_DIM), lambda b, i: (b * ns + i, cb)),
                  pl.BlockSpec((CONV_HALO, 2 * CONV_DIM), lambda b, i: (jnp.maximum((b * ns + i) * hb - 1, 0), cb)),
                  pl.BlockSpec((32, CONV_DIM), lambda b, i: (0, 0)), vec(), vec(), vec()],
        out_specs=pl.BlockSpec((ts, CONV_DIM), lambda b, i: (b * ns + i, 0)),
        out_shape=jax.ShapeDtypeStruct((n, CONV_DIM), jnp.bfloat16),
        scratch_shapes=[pltpu.VMEM((ts + CONV_HALO, CONV_DIM), jnp.float32)],
        compiler_params=pltpu.CompilerParams(dimension_semantics=("parallel", "parallel")),
        name="conformer_conv",
    )(z, z, wpad, dw_b.reshape(1, -1), ln_g.reshape(1, -1), ln_b.reshape(1, -1))


def _msoftmax(s, mask):
    s = jnp.where(mask, s, NEG)
    m = jnp.max(s, axis=-1, keepdims=True)
    e = jnp.where(mask, jnp.exp(s - m), 0.0)
    l = jnp.sum(e, axis=-1, keepdims=True)
    return e / jnp.where(l > 0.0, l, 1.0)


def _nsa_kernel(q_ref, g_ref, kct_ref, vc_ref, kst_ref, vs_ref, kwt_ref, vw_ref, c2st_ref, o_ref, *, n_sel):
    G = NSA_GROUP
    qb = pl.program_id(2)
    t0 = pl.multiple_of(qb * TQ, TQ)
    q = q_ref[...]
    lane_g = lax.broadcasted_iota(jnp.int32, (TQ, GW), 1) // HEAD_DIM
    q4 = jnp.concatenate([jnp.where(lane_g == h, q, jnp.zeros_like(q)) for h in range(G)], axis=0)
    tpos = t0 + lax.broadcasted_iota(jnp.int32, (TQ, 1), 0)

    def heads(x):
        return x.reshape(G, TQ, x.shape[-1])

    def attend(p, v):
        return heads(jnp.dot(p.reshape(G * TQ, p.shape[-1]).astype(jnp.bfloat16), v,
                             preferred_element_type=jnp.float32))

    starts = [t0 - WINDOW + TQ * c for c in range(WINDOW // TQ + 1)]
    reads = [pl.multiple_of(jnp.maximum(s, 0), TQ) for s in starts]
    kwt = jnp.concatenate([kwt_ref[0, 0, :, pl.ds(r, TQ)] for r in reads], axis=1)
    vw = jnp.concatenate([vw_ref[0, 0, pl.ds(r, TQ), :] for r in reads], axis=0)
    wk = WINDOW + TQ
    kposw = t0 - WINDOW + lax.broadcasted_iota(jnp.int32, (TQ, wk), 1)
    diff = tpos - kposw
    maskw = (kposw >= 0) & (diff >= 0) & (diff < WINDOW)

    kct = kct_ref[0, 0]
    ncp = kct.shape[1]
    ncol = lax.broadcasted_iota(jnp.int32, (TQ, ncp), 1)
    maskc = (ncol * CMP_STRIDE + (CMP_BLOCK - 1)) <= tpos
    sc = heads(jnp.dot(q4, kct, preferred_element_type=jnp.float32))
    sw = heads(jnp.dot(q4, kwt, preferred_element_type=jnp.float32))
    pc = _msoftmax(sc, maskc[None])
    oc = attend(pc, vc_ref[0, 0])
    pw = _msoftmax(sw, maskw[None])
    ow = attend(pw, vw)
    pc_sum = jnp.sum(pc, axis=0)

    c2st = c2st_ref[...]
    jp = c2st.shape[0]
    hi = pc_sum.astype(jnp.bfloat16)
    lo = (pc_sum - hi.astype(jnp.float32)).astype(jnp.bfloat16)
    dn = (((1,), (1,)), ((), ()))
    imp_t = (lax.dot_general(c2st, hi, dn, preferred_element_type=jnp.float32)
             + lax.dot_general(c2st, lo, dn, preferred_element_type=jnp.float32))
    jrow = lax.broadcasted_iota(jnp.int32, (jp, TQ), 0)
    cur = (t0 + lax.broadcasted_iota(jnp.int32, (jp, TQ), 1)) // SLC_BLOCK
    forced = (jrow == 0) | (jrow == cur) | (jrow == cur - 1)
    score = jnp.where(jrow <= cur, imp_t + jnp.where(forced, FORCE_BONUS, 0.0), NEG)
    nblk = kst_ref.shape[3] // SLC_BLOCK
    nrow = -(-nblk // 8) * 8
    sc, jr = score[:nrow], jrow[:nrow]
    rank = jnp.zeros((nrow, TQ), jnp.int32)
    for i in range(nblk):
        si = sc[i:i + 1, :]
        rank = rank + ((si > sc) | ((si == sc) & (jr > i))).astype(jnp.int32)
    sel = ((rank < n_sel) & (sc > 0.5 * NEG)).astype(jnp.float32)
    if nrow < jp:
        sel = jnp.concatenate([sel, jnp.zeros((jp - nrow, TQ), jnp.float32)], axis=0)
    selb = sel.T.astype(jnp.bfloat16)

    nchunks = (t0 + TQ + KC - 1) // KC

    def chunk(c, carry):
        m_old, l_old, acc = carry
        k0 = pl.multiple_of(c * KC, KC)
        kt = kst_ref[0, 0, :, pl.ds(k0, KC)]
        v = vs_ref[0, 0, pl.ds(k0, KC), :]
        kblk = (k0 + lax.broadcasted_iota(jnp.int32, (jp, KC), 1)) // SLC_BLOCK
        expand = (kblk == lax.broadcasted_iota(jnp.int32, (jp, KC), 0)).astype(jnp.bfloat16)
        member = jnp.dot(selb, expand, preferred_element_type=jnp.float32) > 0.5
        kpos = k0 + lax.broadcasted_iota(jnp.int32, (TQ, KC), 1)
        msk = (member & (kpos <= tpos))[None]
        s = jnp.where(msk, heads(jnp.dot(q4, kt, preferred_element_type=jnp.float32)), NEG)
        m_new = jnp.maximum(m_old, jnp.max(s, axis=-1, keepdims=True))
        alpha = jnp.exp(m_old - m_new)
        p = jnp.exp(s - m_new)
        l_new = alpha * l_old + jnp.sum(p, axis=-1, keepdims=True)
        return m_new, l_new, alpha * acc + attend(p, v)

    def chunk_pair(i, carry):
        return chunk(2 * i + 1, chunk(2 * i, carry))

    init = (jnp.full((G, TQ, 1), NEG, jnp.float32), jnp.zeros((G, TQ, 1), jnp.float32),
            jnp.zeros((G, TQ, GW), jnp.float32))
    _, l_s, acc_s = lax.fori_loop(0, (nchunks + 1) // 2, chunk_pair, init)
    os_ = acc_s / jnp.where(l_s > 0.0, l_s, 1.0)

    sig = jax.nn.sigmoid(g_ref[...])
    out = jnp.zeros((TQ, GW), jnp.float32)
    for h in range(G):
        o_h = (sig[:, 3 * h:3 * h + 1] * oc[h] + sig[:, 3 * h + 1:3 * h + 2] * os_[h]
               + sig[:, 3 * h + 2:3 * h + 3] * ow[h])
        out = out + jnp.where(lane_g == h, o_h, 0.0)
    o_ref[...] = out.astype(o_ref.dtype)


def nsa_core(q, z, kct, vc, kst, vs, kwt, vw, B, S):
    n = B * S
    ncp = kct.shape[3]
    ncmp = (S - CMP_BLOCK) // CMP_STRIDE + 1
    nslc = S // SLC_BLOCK
    n_sel = min(SLC_TOPK, nslc)
    jp = LANE
    assert nslc <= jp and S % (2 * KC) == 0 and S % TQ == 0 and WINDOW % TQ == 0
    cs = np.arange(ncmp) * CMP_STRIDE
    ss = np.arange(nslc) * SLC_BLOCK
    ov = np.minimum(cs[:, None] + CMP_BLOCK, ss[None, :] + SLC_BLOCK) - np.maximum(cs[:, None], ss[None, :])
    c2s = np.zeros((ncp, jp), np.float32)
    c2s[:ncmp, :nslc] = np.clip(ov, 0, None) / CMP_BLOCK
    c2st = jnp.asarray(c2s.T, dtype=jnp.bfloat16)
    nqb = S // TQ

    def kv_spec(shp):
        return pl.BlockSpec((1, 1) + shp, lambda b, k, i: (b, k, 0, 0))

    return pl.pallas_call(
        functools.partial(_nsa_kernel, n_sel=n_sel),
        grid=(B, NSA_KV_HEADS, nqb),
        in_specs=[pl.BlockSpec((TQ, GW), lambda b, k, i: (b * nqb + i, k)),
                  pl.BlockSpec((TQ, LANE), lambda b, k, i: (b * nqb + i, Z_NG // LANE + k)),
                  kv_spec((GW, ncp)), kv_spec((ncp, GW)),
                  kv_spec((GW, S)), kv_spec((S, GW)),
                  kv_spec((GW, S)), kv_spec((S, GW)),
                  pl.BlockSpec((jp, ncp), lambda b, k, i: (0, 0))],
        out_specs=pl.BlockSpec((TQ, GW), lambda b, k, i: (b * nqb + i, k)),
        out_shape=jax.ShapeDtypeStruct((n, NSA_KV_HEADS * GW), jnp.bfloat16),
        compiler_params=pltpu.CompilerParams(dimension_semantics=("parallel", "parallel", "arbitrary"),
                                             vmem_limit_bytes=VMEM_LIMIT),
        name="nsa_core",
    )(q, z, kct, vc, kst, vs, kwt, vw, c2st)


def _gla_consts():
    C = GLA_CHUNK
    i = np.arange(C)
    mats = [np.tril(np.ones((C, C), np.float32))]
    for s in GLA_LEVELS:
        r = (i // (2 * s)) * 2 * s + s - 1
        upper = (i // s) % 2 == 1
        m = np.arange(C)[None, :]
        mq = ((m > r[:, None]) & (m <= i[:, None]) & upper[:, None]).astype(np.float32)
        mk = ((m > i[:, None]) & (m <= r[:, None]) & (~upper)[:, None]).astype(np.float32)
        mats += [mq, mk]
    return np.concatenate(mats, axis=0)


def _split3(x):
    h = x.astype(jnp.bfloat16)
    r = x - h.astype(jnp.float32)
    m = r.astype(jnp.bfloat16)
    l = (r - m.astype(jnp.float32)).astype(jnp.bfloat16)
    return h, m, l


def _gla_kernel(q_ref, k_ref, v_ref, r_ref, glr_ref, gw_ref, gb_ref, ng_ref, msel_ref, o_ref, state_ref):
    C = GLA_CHUNK

    @pl.when(pl.program_id(1) == 0)
    def _():
        state_ref[...] = jnp.zeros_like(state_ref)

    ii = lax.broadcasted_iota(jnp.int32, (C, C), 0)
    jj = lax.broadcasted_iota(jnp.int32, (C, C), 1)
    lane_h = lax.broadcasted_iota(jnp.int32, (C, GLA_QK), 1) // GLA_DK
    bd = (lax.broadcasted_iota(jnp.int32, (GLA_QK, GLA_V), 0) // GLA_DK
          == lax.broadcasted_iota(jnp.int32, (GLA_QK, GLA_V), 1) // GLA_DV)
    msel = msel_ref[...]
    gw = gw_ref[...].astype(jnp.bfloat16)
    dn = (((1,), (1,)), ((), ()))

    def chunk(c, carry):
        r0 = pl.multiple_of(c * C, C)
        q = q_ref[pl.ds(r0, C), :] * (GLA_DK ** -0.5)
        k = k_ref[pl.ds(r0, C), :]
        v = v_ref[pl.ds(r0, C), :].astype(jnp.bfloat16)
        z = jnp.dot(glr_ref[pl.ds(r0, C), :].astype(jnp.bfloat16), gw, preferred_element_type=jnp.float32) + gb_ref[...]
        a = jax.nn.log_sigmoid(z) / GLA_GATE_TEMP
        ah, am, al = _split3(a)
        ex = (jnp.dot(msel, ah, preferred_element_type=jnp.float32)
              + jnp.dot(msel, am, preferred_element_type=jnp.float32)
              + jnp.dot(msel, al, preferred_element_type=jnp.float32))
        bcum = ex[0:C]
        state = state_ref[...]
        o = jnp.dot((q * jnp.exp(bcum)).astype(jnp.bfloat16), state.astype(jnp.bfloat16),
                    preferred_element_type=jnp.float32)
        kb = k.astype(jnp.bfloat16)
        for h in range(GLA_HEADS):
            hm = lane_h == h
            att = jnp.where(ii == jj, lax.dot_general(jnp.where(hm, q, 0.0).astype(jnp.bfloat16), kb, dn,
                                                      preferred_element_type=jnp.float32), 0.0)
            for li, s in enumerate(GLA_LEVELS):
                eq = jnp.exp(ex[(1 + 2 * li) * C:(2 + 2 * li) * C])
                ek = jnp.exp(ex[(2 + 2 * li) * C:(3 + 2 * li) * C])
                qs = jnp.where(hm, q * eq, 0.0).astype(jnp.bfloat16)
                ks = (k * ek).astype(jnp.bfloat16)
                blk = (ii // (2 * s) == jj // (2 * s)) & ((ii // s) % 2 == 1) & ((jj // s) % 2 == 0)
                att = att + jnp.where(blk, lax.dot_general(qs, ks, dn, preferred_element_type=jnp.float32), 0.0)
            oh = jnp.dot(att.astype(jnp.bfloat16), v, preferred_element_type=jnp.float32)
            o = o + jnp.where(lax.broadcasted_iota(jnp.int32, (C, GLA_V), 1) // GLA_DV == h, oh, 0.0)
        blast = bcum[C - 1:C, :]
        kd_t = (k * jnp.exp(blast - bcum)).T.astype(jnp.bfloat16)
        decay_col = jnp.exp(jnp.sum(a.T, axis=1, keepdims=True))
        upd = jnp.dot(kd_t, v, preferred_element_type=jnp.float32)
        state_ref[...] = state * decay_col + jnp.where(bd, upd, 0.0)
        outs = []
        for h in range(GLA_HEADS):
            oh = o[:, h * GLA_DV:(h + 1) * GLA_DV]
            outs.append(oh * lax.rsqrt(jnp.mean(oh * oh, axis=1, keepdims=True) + EPS) * ng_ref[...])
        on = jnp.concatenate(outs, axis=1)
        r = r_ref[pl.ds(r0, C), :]
        o_ref[pl.ds(r0, C), :] = (on * (r * jax.nn.sigmoid(r))).astype(o_ref.dtype)
        return carry

    lax.fori_loop(0, q_ref.shape[0] // C, chunk, 0)


def gla_pallas(z, gate_w, gate_b, norm_g, B, S):
    n = B * S
    tb = GLA_TB
    nb = S // tb
    gw = jnp.pad(gate_w, ((0, LANE - gate_w.shape[0]), (0, 0)))
    msel = jnp.asarray(_gla_consts(), dtype=jnp.bfloat16)
    zb = lambda col, w: pl.BlockSpec((tb, w), lambda b, i: (b * nb + i, col // w))
    full = lambda shp: pl.BlockSpec(shp, lambda b, i: (0,) * len(shp))
    return pl.pallas_call(
        _gla_kernel,
        grid=(B, nb),
        in_specs=[zb(Z_GQ, GLA_QK), zb(Z_GK, GLA_QK), zb(Z_GV, GLA_V), zb(Z_GR, GLA_V), zb(Z_GLR, LANE),
                  full((LANE, GLA_QK)), full((1, GLA_QK)), full((1, GLA_DV)), full(msel.shape)],
        out_specs=pl.BlockSpec((tb, GLA_V), lambda b, i: (b * nb + i, 0)),
        out_shape=jax.ShapeDtypeStruct((n, GLA_V), jnp.bfloat16),
        scratch_shapes=[pltpu.VMEM((GLA_QK, GLA_V), jnp.float32)],
        compiler_params=pltpu.CompilerParams(dimension_semantics=("parallel", "arbitrary")),
        name="gla",
    )(z, z, z, z, z, gw, gate_b.reshape(1, -1), norm_g.reshape(1, -1), msel)


_FULL_BLOCKS = 8


def _extract_top(s, k, tie_break):
    n, t = s.shape
    row = lax.broadcasted_iota(jnp.int32, (n, t), 0)
    rank = jnp.full((n, t), k, jnp.int32)
    vals = []
    for r in range(k):
        m = jnp.max(s, axis=0, keepdims=True)
        hit = s == m
        if tie_break:
            hit = row == jnp.min(jnp.where(hit, row, n), axis=0, keepdims=True)
        rank = jnp.where(hit, r, rank)
        s = jnp.where(hit, float('-inf'), s)
        vals.append(m)
    return jnp.concatenate(vals, axis=0), rank


def _peer_gate_math(s1, s2, tie_break):
    K = PEER_TOPK
    t = s1.shape[1]
    v1, rank1 = _extract_top(s1, K, tie_break)
    v2, rank2 = _extract_top(s2, K, tie_break)
    blocks, poss = [], []
    row16 = lax.broadcasted_iota(jnp.int32, (K, t), 0)
    for r1 in range(_FULL_BLOCKS):
        c = v1[r1:r1 + 1, :] + v2
        blocks.append(jnp.where(row16 < K // (r1 + 1), c, float('-inf')))
        poss.append(row16 + r1 * K)
    row8 = lax.broadcasted_iota(jnp.int32, (K - _FULL_BLOCKS, t), 0)
    blocks.append(v1[_FULL_BLOCKS:, :] + v2[0:1, :])
    poss.append((row8 + _FULL_BLOCKS) * K)
    sels = [jnp.zeros(b.shape, jnp.bool_) for b in blocks]
    big = K * K
    for _ in range(K):
        m = blocks[0].max(axis=0, keepdims=True)
        for b in blocks[1:]:
            m = jnp.maximum(m, b.max(axis=0, keepdims=True))
        if tie_break:
            first = None
            for b, p in zip(blocks, poss):
                f = jnp.min(jnp.where(b == m, p, big), axis=0, keepdims=True)
                first = f if first is None else jnp.minimum(first, f)
            hits = [p == first for p in poss]
        else:
            hits = [b == m for b in blocks]
        for i, hit in enumerate(hits):
            sels[i] = sels[i] | hit
            blocks[i] = jnp.where(hit, float('-inf'), blocks[i])
    top = v1[0:1, :] + v2[0:1, :]
    cnt_rows = []
    z = jnp.zeros((1, t), jnp.float32)
    for r1 in range(_FULL_BLOCKS):
        sel = sels[r1]
        cnt_rows.append(jnp.sum(sel.astype(jnp.float32), axis=0, keepdims=True))
        c = v1[r1:r1 + 1, :] + v2
        z = z + jnp.sum(jnp.where(sel, jnp.exp(c - top), 0.0), axis=0, keepdims=True)
    sel = sels[_FULL_BLOCKS]
    cnt_tail = sel.astype(jnp.float32)
    c = v1[_FULL_BLOCKS:, :] + v2[0:1, :]
    z = z + jnp.sum(jnp.where(sel, jnp.exp(c - top), 0.0), axis=0, keepdims=True)
    cnt = jnp.concatenate(cnt_rows + [cnt_tail], axis=0)
    a = jnp.zeros(s1.shape, jnp.float32)
    for r in range(K):
        a = jnp.where(rank1 == r, cnt[r:r + 1, :], a)
    taken = (jnp.sum((rank1 < K).astype(jnp.float32), axis=0, keepdims=True),
             jnp.sum((rank2 < K).astype(jnp.float32), axis=0, keepdims=True),
             jnp.sum(cnt, axis=0, keepdims=True))
    return jnp.exp(s1 - v1[0:1, :]), a, rank2.astype(jnp.float32), jnp.exp(s2 - v2[0:1, :]) / z, taken


def _peer_gate_kernel(q_ref, keys_ref, e1_ref, a_ref, r2_ref, e2_ref):
    q = q_ref[...].astype(jnp.bfloat16)
    k1 = keys_ref[0, 0].astype(jnp.bfloat16)
    k2 = keys_ref[0, 1].astype(jnp.bfloat16)
    dn = (((1,), (1,)), ((), ()))
    s1 = lax.dot_general(k1, q[:, :PEER_DKEY // 2], dn, preferred_element_type=jnp.float32)
    s2 = lax.dot_general(k2, q[:, PEER_DKEY // 2:], dn, preferred_element_type=jnp.float32)

    def write(e1, a, r2, e2):
        e1_ref[0] = e1
        a_ref[0] = a
        r2_ref[0] = r2.astype(jnp.bfloat16)
        e2_ref[0] = e2.astype(jnp.bfloat16)

    e1, a, r2, e2, taken = _peer_gate_math(s1, s2, tie_break=False)
    write(e1, a, r2, e2)
    excess = sum(jnp.max(jnp.abs(c - PEER_TOPK)) for c in taken)

    @pl.when(excess > 0.0)
    def _():
        write(*_peer_gate_math(s1, s2, tie_break=True)[:4])


def peer_gates(qh, keys, tt=256):
    n = qh.shape[0]
    H = PEER_HEADS
    shp32 = jax.ShapeDtypeStruct((H, PEER_NKEYS, n), jnp.float32)
    shp16 = jax.ShapeDtypeStruct((H, PEER_NKEYS, n), jnp.bfloat16)
    ospec = pl.BlockSpec((1, PEER_NKEYS, tt), lambda i, h: (h, 0, i))
    return pl.pallas_call(
        _peer_gate_kernel,
        grid=(n // tt, H),
        in_specs=[pl.BlockSpec((tt, PEER_DKEY), lambda i, h: (i, h)),
                  pl.BlockSpec((1, 2, PEER_NKEYS, PEER_DKEY // 2), lambda i, h: (h, 0, 0, 0))],
        out_specs=[ospec, ospec, ospec, ospec],
        out_shape=[shp32, shp32, shp16, shp16],
        compiler_params=pltpu.CompilerParams(dimension_semantics=("parallel", "parallel")),
        name="peer_gates",
    )(qh, keys)


def _peer_dense_kernel(xt_ref, u_ref, vt_ref, e1_ref, a_ref, r2_ref, e2_ref, x1_ref, gate_ref, fg_ref,
                       o_ref, acc_ref, *, ec, final):
    j = pl.program_id(1)
    nsub = ec // PEER_NKEYS

    @pl.when(j == 0)
    def _():
        acc_ref[...] = jnp.zeros_like(acc_ref)

    def gate_mat(i):
        a_idx = j * nsub + i
        g = None
        for h in range(PEER_HEADS):
            cnt = a_ref[h, pl.ds(a_idx, 1), :].astype(jnp.bfloat16)
            e1 = e1_ref[h, pl.ds(a_idx, 1), :].astype(jnp.bfloat16)
            term = jnp.where(r2_ref[h] < cnt, e2_ref[h], jnp.bfloat16(0)) * e1
            g = term if g is None else g + term
        return g

    def scores(i):
        return jnp.dot(u_ref[i * PEER_NKEYS:(i + 1) * PEER_NKEYS, :], xt_ref[...], preferred_element_type=jnp.float32)

    st = scores(0)
    ws = []
    for i in range(nsub):
        st_next = scores(i + 1) if i + 1 < nsub else None
        ws.append(_gelu(st.astype(jnp.bfloat16)) * gate_mat(i))
        if i % PEER_GROUPS_PER_DOT == PEER_GROUPS_PER_DOT - 1:
            w = jnp.concatenate(ws, axis=0)
            ws = []
            acc_ref[...] += jnp.dot(vt_ref[:, (i + 1 - PEER_GROUPS_PER_DOT) * PEER_NKEYS:(i + 1) * PEER_NKEYS], w,
                                    preferred_element_type=jnp.float32)
        st = st_next

    @pl.when(j == pl.num_programs(1) - 1)
    def _():
        x2 = x1_ref[...] + gate_ref[0] * acc_ref[...].T
        if final:
            x2 = x2 * lax.rsqrt(jnp.mean(x2 * x2, axis=1, keepdims=True) + EPS) * fg_ref[...]
        o_ref[...] = x2


def peer_dense(xt, u16, vt16, e1, a, r2, e2, x1, gate2, final_g, S, final, tt=512, ec=1024):
    d, n = xt.shape
    ne = u16.shape[0]
    H = PEER_HEADS
    per_b = S // tt
    gspec = lambda: pl.BlockSpec((H, PEER_NKEYS, tt), lambda i, j: (0, 0, i))
    return pl.pallas_call(
        functools.partial(_peer_dense_kernel, ec=ec, final=final),
        grid=(n // tt, ne // ec),
        in_specs=[pl.BlockSpec((d, tt), lambda i, j: (0, i)),
                  pl.BlockSpec((ec, d), lambda i, j: (j, 0)),
                  pl.BlockSpec((d, ec), lambda i, j: (0, j)),
                  gspec(), gspec(), gspec(), gspec(),
                  pl.BlockSpec((tt, d), lambda i, j: (i, 0)),
                  pl.BlockSpec((1, 1, d), lambda i, j: (i // per_b, 0, 0)),
                  pl.BlockSpec((1, d), lambda i, j: (0, 0))],
        out_specs=pl.BlockSpec((tt, d), lambda i, j: (i, 0)),
        out_shape=jax.ShapeDtypeStruct((n, d), jnp.float32),
        scratch_shapes=[pltpu.VMEM((d, tt), jnp.float32)],
        compiler_params=pltpu.CompilerParams(dimension_semantics=("parallel", "arbitrary"),
                                             vmem_limit_bytes=VMEM_LIMIT),
        name="peer_dense",
    )(xt, u16, vt16, e1, a, r2, e2, x1, gate2, final_g.reshape(1, d))


def adaln(c, w, b):
    mod = pmm(jax.nn.silu(c), w) + b
    shift, scale, gate = jnp.split(mod[:, None, :], 3, axis=-1)
    return shift, 1.0 + scale, gate


def kernel(x, c, positions, ada_w, ada_b, norm_g, w_in, conv_dw_w, conv_dw_b, conv_ln_g, conv_ln_b, w_conv_up,
           cmp_pos, cmp_w1, cmp_w2, w_nsa_up, gla_gate_w, gla_gate_b, gla_norm_g, w_gla_up, w_out,
           peer_wq, peer_keys, peer_u, peer_v, final_g):
    B, S, D = x.shape
    bf = jnp.bfloat16
    x2 = x.reshape(B * S, D)
    cs = rope_tables(positions)
    for l in range(DEPTH):
        shift, scale1p, gate = adaln(c, ada_w[l, 0], ada_b[l, 0])
        z = in_proj(x2, norm_g[l, 0], scale1p, shift, pack_w_in(w_in[l]), S)
        hc = conv_pallas(z, conv_dw_w[l], conv_dw_b[l], conv_ln_g[l], conv_ln_b[l], B, S)
        q, kst, vs, kwt, vw = nsa_prep(z, cs, B, S)
        kct, vc = nsa_compress(z, positions, cmp_pos[l], cmp_w1[l], cmp_w2[l], B, S)
        on = nsa_core(q, z, kct, vc, kst, vs, kwt, vw, B, S)
        og = gla_pallas(z, gla_gate_w[l], gla_gate_b[l], gla_norm_g[l], B, S)
        shift2, scale2p, gate2 = adaln(c, ada_w[l, 1], ada_b[l, 1])
        x1, ut, qh = mix_out(hc, on, og, z, x2, gate, w_conv_up[l].astype(bf), w_nsa_up[l].astype(bf),
                             w_gla_up[l].astype(bf), w_out[l].astype(bf), norm_g[l, 1], scale2p, shift2,
                             peer_wq[l].astype(bf), S)
        e1, a, r2, e2 = peer_gates(qh, peer_keys[l])
        x2 = peer_dense(ut, peer_u[l].astype(bf), peer_v[l].astype(bf).T, e1, a, r2, e2, x1, gate2, final_g,
                        S, final=(l == DEPTH - 1))
    return x2.reshape(B, S, D)
```

```python
import functools

import jax
import jax.numpy as jnp
import numpy as np
from jax import lax
from jax.experimental import pallas as pl
from jax.experimental.pallas import tpu as pltpu

D_MODEL = 1024
DEPTH = 2
EPS = 1e-6
NEG = -1e30
N_BRANCH = 3
CONV_DIM = 512
CONV_WIDTH = 31
NSA_HEADS = 8
NSA_KV_HEADS = 2
NSA_GROUP = NSA_HEADS // NSA_KV_HEADS
HEAD_DIM = 64
CMP_BLOCK = 32
CMP_STRIDE = 16
CMP_HIDDEN = 256
SLC_BLOCK = 64
SLC_TOPK = 16
WINDOW = 512
FORCE_BONUS = 1e4
ROPE_THETA = 10000.0
GLA_HEADS = 4
GLA_DK = 64
GLA_DV = 128
GLA_GATE_RANK = 16
GLA_GATE_TEMP = 16.0
GLA_CHUNK = 256
PEER_HEADS = 8
PEER_NKEYS = 128
PEER_DKEY = 256
PEER_TOPK = 16
NSA_Q = NSA_HEADS * HEAD_DIM
NSA_KV = 3 * 2 * NSA_KV_HEADS * HEAD_DIM
NSA_G = 3 * NSA_HEADS
GLA_QK = GLA_HEADS * GLA_DK
GLA_V = GLA_HEADS * GLA_DV
SPLITS = [2 * CONV_DIM, NSA_Q, NSA_KV, NSA_G, GLA_QK, GLA_QK, GLA_V, GLA_V, GLA_GATE_RANK, N_BRANCH * D_MODEL]

LANE = 128
TQ = 128
KC = 512
GW = NSA_GROUP * HEAD_DIM
VMEM_LIMIT = 48 * 1024 * 1024
GLA_LEVELS = tuple(GLA_CHUNK >> s for s in range(1, GLA_CHUNK.bit_length()))
GLA_TB = 512
CONV_TS = 512
CONV_HALO = 32
ROW_TILE = 1024
MIX_TILE = 256
PEER_GROUPS_PER_DOT = 2

Z_MG = 0
Z_CONV = Z_MG + N_BRANCH * D_MODEL
Z_Q = Z_CONV + 2 * CONV_DIM
Z_QSW = Z_Q + NSA_Q
Z_GV = Z_QSW + NSA_Q
Z_GR = Z_GV + GLA_V
Z_GQ = Z_GR + GLA_V
Z_GK = Z_GQ + GLA_QK
Z_CMP = Z_GK + GLA_QK
Z_SLC = Z_CMP + 256
Z_WIN = Z_SLC + 384
Z_NG = Z_WIN + 384
Z_GLR = Z_NG + NSA_KV_HEADS * LANE
NZ = Z_GLR + LANE
assert Z_SLC % 384 == 0 and Z_WIN % 384 == 0 and Z_CONV % (2 * CONV_DIM) == 0 and Z_GQ % GLA_QK == 0


def _mm_kernel(a_ref, b_ref, o_ref):
    o_ref[...] = jnp.dot(a_ref[...].astype(jnp.bfloat16), b_ref[...].astype(jnp.bfloat16),
                         preferred_element_type=jnp.float32)


def _pick(n, cands):
    for c in cands:
        if n % c == 0:
            return c
    return n


def pmm(a, b):
    m, k = a.shape
    n = b.shape[1]
    n_pad = -n % LANE
    if n_pad:
        b = jnp.pad(b, ((0, 0), (0, n_pad)))
    m_pad = -m % 8
    if m_pad:
        a = jnp.pad(a, ((0, m_pad), (0, 0)))
    mp, np_ = m + m_pad, n + n_pad
    tm = _pick(mp, (512, 256, 128, 64, 32, 16, 8))
    tn = _pick(np_, (512, 640, 384, 256, 128))
    out = pl.pallas_call(
        _mm_kernel,
        grid=(mp // tm, np_ // tn),
        in_specs=[pl.BlockSpec((tm, k), lambda i, j: (i, 0)),
                  pl.BlockSpec((k, tn), lambda i, j: (0, j))],
        out_specs=pl.BlockSpec((tm, tn), lambda i, j: (i, j)),
        out_shape=jax.ShapeDtypeStruct((mp, np_), jnp.float32),
        compiler_params=pltpu.CompilerParams(dimension_semantics=("parallel", "parallel")),
        name="pmm",
    )(a, b)
    return out[:m, :n]


def _in_proj_kernel(x_ref, g_ref, sc_ref, sh_ref, w_ref, z_ref, u_ref):
    @pl.when(pl.program_id(1) == 0)
    def _():
        x = x_ref[...]
        y = x * lax.rsqrt(jnp.mean(x * x, axis=1, keepdims=True) + EPS) * g_ref[...]
        u_ref[...] = (y * sc_ref[0] + sh_ref[0]).astype(jnp.bfloat16)

    z_ref[...] = jnp.dot(u_ref[...], w_ref[...], preferred_element_type=jnp.float32)


def in_proj(x2, g, scale1p, shift, w16, S):
    n, d = x2.shape
    nz = w16.shape[1]
    tm = ROW_TILE
    tn = _pick(nz, (1152, 1024, 896, 512, 384, 256, 128))
    per_b = S // tm
    mod = lambda: pl.BlockSpec((1, 1, d), lambda i, j: (i // per_b, 0, 0))
    return pl.pallas_call(
        _in_proj_kernel,
        grid=(n // tm, nz // tn),
        in_specs=[pl.BlockSpec((tm, d), lambda i, j: (i, 0)),
                  pl.BlockSpec((1, d), lambda i, j: (0, 0)), mod(), mod(),
                  pl.BlockSpec((d, tn), lambda i, j: (0, j))],
        out_specs=pl.BlockSpec((tm, tn), lambda i, j: (i, j)),
        out_shape=jax.ShapeDtypeStruct((n, nz), jnp.float32),
        scratch_shapes=[pltpu.VMEM((tm, d), jnp.bfloat16)],
        compiler_params=pltpu.CompilerParams(dimension_semantics=("parallel", "arbitrary"),
                                             vmem_limit_bytes=VMEM_LIMIT),
        name="in_proj",
    )(x2, g.reshape(1, d), scale1p, shift, w16)


def _half_swap(n_heads):
    idx = np.arange(n_heads * HEAD_DIM).reshape(n_heads, 2, HEAD_DIM // 2)
    return idx[:, ::-1, :].reshape(-1)


def pack_w_in(w_in):
    o = np.cumsum([0] + SPLITS)
    a_conv, nq, nkv, ng, gq, gk, gv, gr, glr, mg = [w_in[:, o[i]:o[i + 1]] for i in range(len(SPLITS))]
    kvw = 2 * NSA_KV_HEADS * HEAD_DIM
    kw = NSA_KV_HEADS * HEAD_DIM
    d = w_in.shape[0]
    sw = _half_swap(NSA_KV_HEADS)
    cols = [mg, a_conv, nq, nq[:, _half_swap(NSA_HEADS)], gv, gr, gq, gk, nkv[:, 0:kvw]]
    for br in (1, 2):
        k = nkv[:, br * kvw:br * kvw + kw]
        cols += [k, k[:, sw], nkv[:, br * kvw + kw:(br + 1) * kvw]]
    gpk = NSA_GROUP * 3
    for kvh in range(NSA_KV_HEADS):
        cols += [ng[:, kvh * gpk:(kvh + 1) * gpk], jnp.zeros((d, LANE - gpk), w_in.dtype)]
    cols += [glr, jnp.zeros((d, LANE - GLA_GATE_RANK), w_in.dtype)]
    w = jnp.concatenate(cols, axis=1)
    assert w.shape[1] == NZ
    return w.astype(jnp.bfloat16)


def rope_tables(positions):
    half = HEAD_DIM // 2
    freq = ROPE_THETA ** (-jnp.arange(half, dtype=jnp.float32) / half)
    ang = positions.astype(jnp.float32)[..., None] * freq
    cos, sin = jnp.cos(ang), jnp.sin(ang)
    c = jnp.tile(jnp.concatenate([cos, cos], -1), (1, 1, LANE // HEAD_DIM))
    s = jnp.tile(jnp.concatenate([-sin, sin], -1), (1, 1, LANE // HEAD_DIM))
    return jnp.concatenate([c, s], -1).reshape(-1, 2 * LANE)


def _nsa_prep_kernel(q_ref, qs_ref, slc_ref, win_ref, cs_ref, sel_ref, qo_ref, kst_ref, vs_ref, kwt_ref, vw_ref):
    cos = cs_ref[:, :LANE]
    sin = cs_ref[:, LANE:]
    scale = HEAD_DIM ** -0.5
    for t in range(NSA_Q // LANE):
        sl = slice(t * LANE, (t + 1) * LANE)
        qo_ref[:, sl] = ((q_ref[:, sl] * cos + qs_ref[:, sl] * sin) * scale).astype(jnp.bfloat16)
    dn = (((1,), (1,)), ((), ()))
    for src, kt_ref, v_ref in ((slc_ref, kst_ref, vs_ref), (win_ref, kwt_ref, vw_ref)):
        k = (src[:, 0:LANE] * cos + src[:, LANE:2 * LANE] * sin).astype(jnp.bfloat16)
        v = src[:, 2 * LANE:3 * LANE].astype(jnp.bfloat16)
        for kvh in range(NSA_KV_HEADS):
            sel = sel_ref[kvh]
            kt_ref[0, kvh] = lax.dot_general(sel, k, dn, preferred_element_type=jnp.float32).astype(jnp.bfloat16)
            v_ref[0, kvh] = lax.dot_general(v, sel, dn, preferred_element_type=jnp.float32).astype(jnp.bfloat16)


def _head_repeat_sel():
    sel = np.zeros((NSA_KV_HEADS, GW, LANE), np.float32)
    for kvh in range(NSA_KV_HEADS):
        for r in range(GW):
            sel[kvh, r, kvh * HEAD_DIM + r % HEAD_DIM] = 1.0
    return jnp.asarray(sel, dtype=jnp.bfloat16)


def nsa_prep(z, cs, B, S):
    n = B * S
    tp = ROW_TILE
    nb = S // tp
    zb = lambda col, w: pl.BlockSpec((tp, w), lambda b, i: (b * nb + i, col // w))
    kt = lambda: pl.BlockSpec((1, NSA_KV_HEADS, GW, tp), lambda b, i: (b, 0, 0, i))
    vv = lambda: pl.BlockSpec((1, NSA_KV_HEADS, tp, GW), lambda b, i: (b, 0, i, 0))
    kt_shape = jax.ShapeDtypeStruct((B, NSA_KV_HEADS, GW, S), jnp.bfloat16)
    v_shape = jax.ShapeDtypeStruct((B, NSA_KV_HEADS, S, GW), jnp.bfloat16)
    return pl.pallas_call(
        _nsa_prep_kernel,
        grid=(B, nb),
        in_specs=[zb(Z_Q, NSA_Q), zb(Z_QSW, NSA_Q), zb(Z_SLC, 384), zb(Z_WIN, 384),
                  pl.BlockSpec((tp, 2 * LANE), lambda b, i: (b * nb + i, 0)),
                  pl.BlockSpec((NSA_KV_HEADS, GW, LANE), lambda b, i: (0, 0, 0))],
        out_specs=[pl.BlockSpec((tp, NSA_Q), lambda b, i: (b * nb + i, 0)), kt(), vv(), kt(), vv()],
        out_shape=[jax.ShapeDtypeStruct((n, NSA_Q), jnp.bfloat16), kt_shape, v_shape, kt_shape, v_shape],
        compiler_params=pltpu.CompilerParams(dimension_semantics=("parallel", "parallel")),
        name="nsa_prep",
    )(z, z, z, z, cs, _head_repeat_sel())


def _gelu(x):
    return 0.5 * x * (1.0 + jnp.tanh(0.7978845608028654 * (x + 0.044715 * x * x * x)))


def _compress_kernel(gk_ref, gv_ref, pe_ref, w1_ref, w2_ref, cs_ref, kct_ref, vc_ref):
    nrow = gk_ref.shape[2]
    outs = []
    for kv, g_ref in enumerate((gk_ref, gv_ref)):
        g = g_ref[0, 0]
        lo = jnp.dot((g + pe_ref[kv, 0]).astype(jnp.bfloat16), w1_ref[kv, 0], preferred_element_type=jnp.float32)
        hi = jnp.dot((g + pe_ref[kv, 1]).astype(jnp.bfloat16), w1_ref[kv, 1], preferred_element_type=jnp.float32)
        hid = _gelu(lo + pltpu.roll(hi, nrow - 1, 0)).astype(jnp.bfloat16)
        outs.append(jnp.dot(hid, w2_ref[kv], preferred_element_type=jnp.float32))
    k = outs[0][:, :GW] * cs_ref[0, :, :GW] + outs[0][:, GW:] * cs_ref[0, :, GW:]
    kct_ref[0, 0] = k.T.astype(jnp.bfloat16)
    vc_ref[0, 0] = outs[1][:, :GW].astype(jnp.bfloat16)


def nsa_compress(z, positions, cmp_pos, cmp_w1, cmp_w2, B, S):
    ng = S // CMP_STRIDE
    grp = CMP_STRIDE * HEAD_DIM
    c = z[:, Z_CMP:Z_CMP + 256].reshape(B, ng, CMP_STRIDE, 2, NSA_KV_HEADS, HEAD_DIM)
    g = c.transpose(3, 0, 4, 1, 2, 5).reshape(2, B, NSA_KV_HEADS, ng, grp)
    pe = cmp_pos.reshape(2, 2, 1, grp)
    w1 = cmp_w1.reshape(2, 2, grp, CMP_HIDDEN).astype(jnp.bfloat16)
    rep = jnp.tile(cmp_w2, (1, 1, NSA_GROUP))
    sw = np.tile(_half_swap(1), NSA_GROUP) + np.repeat(np.arange(NSA_GROUP) * HEAD_DIM, HEAD_DIM)
    w2 = jnp.concatenate([rep, rep[:, :, sw]], axis=-1).astype(jnp.bfloat16)
    end = jnp.minimum(jnp.arange(ng) * CMP_STRIDE + CMP_BLOCK - 1, S - 1)
    half = HEAD_DIM // 2
    freq = ROPE_THETA ** (-jnp.arange(half, dtype=jnp.float32) / half)
    ang = positions[:, end].astype(jnp.float32)[..., None] * freq
    cos, sin = jnp.cos(ang), jnp.sin(ang)
    cs = jnp.concatenate([jnp.tile(jnp.concatenate([cos, cos], -1), (1, 1, NSA_GROUP)),
                          jnp.tile(jnp.concatenate([-sin, sin], -1), (1, 1, NSA_GROUP))], -1)
    gspec = lambda: pl.BlockSpec((1, 1, ng, grp), lambda b, k: (b, k, 0, 0))
    return pl.pallas_call(
        _compress_kernel,
        grid=(B, NSA_KV_HEADS),
        in_specs=[gspec(), gspec(),
                  pl.BlockSpec((2, 2, 1, grp), lambda b, k: (0, 0, 0, 0)),
                  pl.BlockSpec((2, 2, grp, CMP_HIDDEN), lambda b, k: (0, 0, 0, 0)),
                  pl.BlockSpec((2, CMP_HIDDEN, 2 * GW), lambda b, k: (0, 0, 0)),
                  pl.BlockSpec((1, ng, 2 * GW), lambda b, k: (b, 0, 0))],
        out_specs=[pl.BlockSpec((1, 1, GW, ng), lambda b, k: (b, k, 0, 0)),
                   pl.BlockSpec((1, 1, ng, GW), lambda b, k: (b, k, 0, 0))],
        out_shape=[jax.ShapeDtypeStruct((B, NSA_KV_HEADS, GW, ng), jnp.bfloat16),
                   jax.ShapeDtypeStruct((B, NSA_KV_HEADS, ng, GW), jnp.bfloat16)],
        compiler_params=pltpu.CompilerParams(dimension_semantics=("parallel", "parallel")),
        name="nsa_compress",
    )(g[0], g[1], pe, w1, w2, cs)


def _mix_kernel(hc_ref, on_ref, og_ref, mg_ref, x_ref, gate_ref, wc_ref, wn_ref, wg_ref, wo_ref,
                g2_ref, sc_ref, sh_ref, wq_ref, x1_ref, ut_ref, qh_ref):
    d = D_MODEL
    f32 = jnp.float32
    y = (jax.nn.sigmoid(mg_ref[:, 0:d]) * jnp.dot(hc_ref[...], wc_ref[...], preferred_element_type=f32)
         + jax.nn.sigmoid(mg_ref[:, d:2 * d]) * jnp.dot(on_ref[...], wn_ref[...], preferred_element_type=f32)
         + jax.nn.sigmoid(mg_ref[:, 2 * d:3 * d]) * jnp.dot(og_ref[...], wg_ref[...], preferred_element_type=f32))
    x1 = x_ref[...] + gate_ref[0] * jnp.dot(y.astype(jnp.bfloat16), wo_ref[...], preferred_element_type=f32)
    x1_ref[...] = x1
    u = x1 * lax.rsqrt(jnp.mean(x1 * x1, axis=1, keepdims=True) + EPS) * g2_ref[...] * sc_ref[0] + sh_ref[0]
    ut_ref[...] = u.T.astype(jnp.bfloat16)
    qh_ref[...] = jnp.dot(u.astype(jnp.bfloat16), wq_ref[...], preferred_element_type=f32).astype(qh_ref.dtype)


def mix_out(hc, on, og, z, x2, gate1, wc, wn, wg, wo, g2, scale2p, shift2, wq, S):
    n, d = x2.shape
    tm = MIX_TILE
    per_b = S // tm
    row = lambda w: pl.BlockSpec((tm, w), lambda i: (i, 0))
    mod = lambda: pl.BlockSpec((1, 1, d), lambda i: (i // per_b, 0, 0))
    full = lambda a: pl.BlockSpec(a.shape, lambda i: (0,) * a.ndim)
    nq = wq.shape[1]
    return pl.pallas_call(
        _mix_kernel,
        grid=(n // tm,),
        in_specs=[row(CONV_DIM), row(NSA_Q), row(GLA_V),
                  pl.BlockSpec((tm, N_BRANCH * d), lambda i: (i, Z_MG // (N_BRANCH * d))),
                  row(d), mod(), full(wc), full(wn), full(wg), full(wo),
                  pl.BlockSpec((1, d), lambda i: (0, 0)), mod(), mod(), full(wq)],
        out_specs=[row(d), pl.BlockSpec((d, tm), lambda i: (0, i)), row(nq)],
        out_shape=[jax.ShapeDtypeStruct((n, d), jnp.float32), jax.ShapeDtypeStruct((d, n), jnp.bfloat16),
                   jax.ShapeDtypeStruct((n, nq), jnp.bfloat16)],
        compiler_params=pltpu.CompilerParams(dimension_semantics=("parallel",), vmem_limit_bytes=VMEM_LIMIT),
        name="mix_out",
    )(hc, on, og, z, x2, gate1, wc, wn, wg, wo, g2.reshape(1, d), scale2p, shift2, wq)


def _conv_kernel(a_ref, ah_ref, w_ref, b_ref, g_ref, beta_ref, o_ref, hbuf):
    i = pl.program_id(1)
    ts = a_ref.shape[0]
    a = a_ref[...]
    hbuf[pl.ds(CONV_HALO, ts), :] = a[:, :CONV_DIM] * jax.nn.sigmoid(a[:, CONV_DIM:])
    ah = ah_ref[...]
    halo = ah[:, :CONV_DIM] * jax.nn.sigmoid(ah[:, CONV_DIM:])
    hbuf[pl.ds(0, CONV_HALO), :] = jnp.where(i > 0, halo, 0.0)
    acc = jnp.zeros((ts, CONV_DIM), jnp.float32) + b_ref[...]
    off = CONV_HALO - (CONV_WIDTH - 1)
    for k in range(CONV_WIDTH):
        acc = acc + w_ref[k:k + 1, :] * hbuf[pl.ds(off + k, ts), :]
    mu = jnp.mean(acc, axis=1, keepdims=True)
    d = acc - mu
    var = jnp.mean(d * d, axis=1, keepdims=True)
    y = d * lax.rsqrt(var + EPS) * g_ref[...] + beta_ref[...]
    o_ref[...] = (y * jax.nn.sigmoid(y)).astype(o_ref.dtype)


def conv_pallas(z, dw_w, dw_b, ln_g, ln_b, B, S):
    n = B * S
    ts = CONV_TS
    ns = S // ts
    hb = ts // CONV_HALO
    cb = Z_CONV // (2 * CONV_DIM)
    wpad = jnp.pad(dw_w, ((0, 32 - CONV_WIDTH), (0, 0)))
    vec = lambda: pl.BlockSpec((1, CONV_DIM), lambda b, i: (0, 0))
    return pl.pallas_call(
        _conv_kernel,
        grid=(B, ns),
        in_specs=[pl.BlockSpec((ts, 2 * CONV_DIM), lambda b, i: (b * ns + i, cb)),
                  pl.BlockSpec((CONV_HALO, 2 * CONV_DIM), lambda b, i: (jnp.maximum((b * ns + i) * hb - 1, 0), cb)),
                  pl.BlockSpec((32, CONV_DIM), lambda b, i: (0, 0)), vec(), vec(), vec()],
        out_specs=pl.BlockSpec((ts, CONV_DIM), lambda b, i: (b * ns + i, 0)),
        out_shape=jax.ShapeDtypeStruct((n, CONV_DIM), jnp.bfloat16),
        scratch_shapes=[pltpu.VMEM((ts + CONV_HALO, CONV_DIM), jnp.float32)],
        compiler_params=pltpu.CompilerParams(dimension_semantics=("parallel", "parallel")),
        name="conformer_conv",
    )(z, z, wpad, dw_b.reshape(1, -1), ln_g.reshape(1, -1), ln_b.reshape(1, -1))


def _msoftmax(s, mask):
    s = jnp.where(mask, s, NEG)
    m = jnp.max(s, axis=-1, keepdims=True)
    e = jnp.where(mask, jnp.exp(s - m), 0.0)
    l = jnp.sum(e, axis=-1, keepdims=True)
    return e / jnp.where(l > 0.0, l, 1.0)


def _nsa_kernel(q_ref, g_ref, kct_ref, vc_ref, kst_ref, vs_ref, kwt_ref, vw_ref, c2st_ref, o_ref, *, n_sel):
    G = NSA_GROUP
    qb = pl.program_id(2)
    t0 = pl.multiple_of(qb * TQ, TQ)
    q = q_ref[...]
    lane_g = lax.broadcasted_iota(jnp.int32, (TQ, GW), 1) // HEAD_DIM
    q4 = jnp.concatenate([jnp.where(lane_g == h, q, jnp.zeros_like(q)) for h in range(G)], axis=0)
    tpos = t0 + lax.broadcasted_iota(jnp.int32, (TQ, 1), 0)

    def heads(x):
        return x.reshape(G, TQ, x.shape[-1])

    def attend(p, v):
        return heads(jnp.dot(p.reshape(G * TQ, p.shape[-1]).astype(jnp.bfloat16), v,
                             preferred_element_type=jnp.float32))

    starts = [t0 - WINDOW + TQ * c for c in range(WINDOW // TQ + 1)]
    reads = [pl.multiple_of(jnp.maximum(s, 0), TQ) for s in starts]
    kwt = jnp.concatenate([kwt_ref[0, 0, :, pl.ds(r, TQ)] for r in reads], axis=1)
    vw = jnp.concatenate([vw_ref[0, 0, pl.ds(r, TQ), :] for r in reads], axis=0)
    wk = WINDOW + TQ
    kposw = t0 - WINDOW + lax.broadcasted_iota(jnp.int32, (TQ, wk), 1)
    diff = tpos - kposw
    maskw = (kposw >= 0) & (diff >= 0) & (diff < WINDOW)

    kct = kct_ref[0, 0]
    ncp = kct.shape[1]
    ncol = lax.broadcasted_iota(jnp.int32, (TQ, ncp), 1)
    maskc = (ncol * CMP_STRIDE + (CMP_BLOCK - 1)) <= tpos
    sc = heads(jnp.dot(q4, kct, preferred_element_type=jnp.float32))
    sw = heads(jnp.dot(q4, kwt, preferred_element_type=jnp.float32))
    pc = _msoftmax(sc, maskc[None])
    oc = attend(pc, vc_ref[0, 0])
    pw = _msoftmax(sw, maskw[None])
    ow = attend(pw, vw)
    pc_sum = jnp.sum(pc, axis=0)

    c2st = c2st_ref[...]
    jp = c2st.shape[0]
    hi = pc_sum.astype(jnp.bfloat16)
    lo = (pc_sum - hi.astype(jnp.float32)).astype(jnp.bfloat16)
    dn = (((1,), (1,)), ((), ()))
    imp_t = (lax.dot_general(c2st, hi, dn, preferred_element_type=jnp.float32)
             + lax.dot_general(c2st, lo, dn, preferred_element_type=jnp.float32))
    jrow = lax.broadcasted_iota(jnp.int32, (jp, TQ), 0)
    cur = (t0 + lax.broadcasted_iota(jnp.int32, (jp, TQ), 1)) // SLC_BLOCK
    forced = (jrow == 0) | (jrow == cur) | (jrow == cur - 1)
    score = jnp.where(jrow <= cur, imp_t + jnp.where(forced, FORCE_BONUS, 0.0), NEG)
    nblk = kst_ref.shape[3] // SLC_BLOCK
    nrow = -(-nblk // 8) * 8
    sc, jr = score[:nrow], jrow[:nrow]
    rank = jnp.zeros((nrow, TQ), jnp.int32)
    for i in range(nblk):
        si = sc[i:i + 1, :]
        rank = rank + ((si > sc) | ((si == sc) & (jr > i))).astype(jnp.int32)
    sel = ((rank < n_sel) & (sc > 0.5 * NEG)).astype(jnp.float32)
    if nrow < jp:
        sel = jnp.concatenate([sel, jnp.zeros((jp - nrow, TQ), jnp.float32)], axis=0)
    selb = sel.T.astype(jnp.bfloat16)

    nchunks = (t0 + TQ + KC - 1) // KC

    def chunk(c, carry):
        m_old, l_old, acc = carry
        k0 = pl.multiple_of(c * KC, KC)
        kt = kst_ref[0, 0, :, pl.ds(k0, KC)]
        v = vs_ref[0, 0, pl.ds(k0, KC), :]
        kblk = (k0 + lax.broadcasted_iota(jnp.int32, (jp, KC), 1)) // SLC_BLOCK
        expand = (kblk == lax.broadcasted_iota(jnp.int32, (jp, KC), 0)).astype(jnp.bfloat16)
        member = jnp.dot(selb, expand, preferred_element_type=jnp.float32) > 0.5
        kpos = k0 + lax.broadcasted_iota(jnp.int32, (TQ, KC), 1)
        msk = (member & (kpos <= tpos))[None]
        s = jnp.where(msk, heads(jnp.dot(q4, kt, preferred_element_type=jnp.float32)), NEG)
        m_new = jnp.maximum(m_old, jnp.max(s, axis=-1, keepdims=True))
        alpha = jnp.exp(m_old - m_new)
        p = jnp.exp(s - m_new)
        l_new = alpha * l_old + jnp.sum(p, axis=-1, keepdims=True)
        return m_new, l_new, alpha * acc + attend(p, v)

    def chunk_pair(i, carry):
        return chunk(2 * i + 1, chunk(2 * i, carry))

    init = (jnp.full((G, TQ, 1), NEG, jnp.float32), jnp.zeros((G, TQ, 1), jnp.float32),
            jnp.zeros((G, TQ, GW), jnp.float32))
    _, l_s, acc_s = lax.fori_loop(0, (nchunks + 1) // 2, chunk_pair, init)
    os_ = acc_s / jnp.where(l_s > 0.0, l_s, 1.0)

    sig = jax.nn.sigmoid(g_ref[...])
    out = jnp.zeros((TQ, GW), jnp.float32)
    for h in range(G):
        o_h = (sig[:, 3 * h:3 * h + 1] * oc[h] + sig[:, 3 * h + 1:3 * h + 2] * os_[h]
               + sig[:, 3 * h + 2:3 * h + 3] * ow[h])
        out = out + jnp.where(lane_g == h, o_h, 0.0)
    o_ref[...] = out.astype(o_ref.dtype)


def nsa_core(q, z, kct, vc, kst, vs, kwt, vw, B, S):
    n = B * S
    ncp = kct.shape[3]
    ncmp = (S - CMP_BLOCK) // CMP_STRIDE + 1
    nslc = S // SLC_BLOCK
    n_sel = min(SLC_TOPK, nslc)
    jp = LANE
    assert nslc <= jp and S % (2 * KC) == 0 and S % TQ == 0 and WINDOW % TQ == 0
    cs = np.arange(ncmp) * CMP_STRIDE
    ss = np.arange(nslc) * SLC_BLOCK
    ov = np.minimum(cs[:, None] + CMP_BLOCK, ss[None, :] + SLC_BLOCK) - np.maximum(cs[:, None], ss[None, :])
    c2s = np.zeros((ncp, jp), np.float32)
    c2s[:ncmp, :nslc] = np.clip(ov, 0, None) / CMP_BLOCK
    c2st = jnp.asarray(c2s.T, dtype=jnp.bfloat16)
    nqb = S // TQ

    def kv_spec(shp):
        return pl.BlockSpec((1, 1) + shp, lambda b, k, i: (b, k, 0, 0))

    return pl.pallas_call(
        functools.partial(_nsa_kernel, n_sel=n_sel),
        grid=(B, NSA_KV_HEADS, nqb),
        in_specs=[pl.BlockSpec((TQ, GW), lambda b, k, i: (b * nqb + i, k)),
                  pl.BlockSpec((TQ, LANE), lambda b, k, i: (b * nqb + i, Z_NG // LANE + k)),
                  kv_spec((GW, ncp)), kv_spec((ncp, GW)),
                  kv_spec((GW, S)), kv_spec((S, GW)),
                  kv_spec((GW, S)), kv_spec((S, GW)),
                  pl.BlockSpec((jp, ncp), lambda b, k, i: (0, 0))],
        out_specs=pl.BlockSpec((TQ, GW), lambda b, k, i: (b * nqb + i, k)),
        out_shape=jax.ShapeDtypeStruct((n, NSA_KV_HEADS * GW), jnp.bfloat16),
        compiler_params=pltpu.CompilerParams(dimension_semantics=("parallel", "parallel", "arbitrary"),
                                             vmem_limit_bytes=VMEM_LIMIT),
        name="nsa_core",
    )(q, z, kct, vc, kst, vs, kwt, vw, c2st)


def _gla_consts():
    C = GLA_CHUNK
    i = np.arange(C)
    mats = [np.tril(np.ones((C, C), np.float32))]
    for s in GLA_LEVELS:
        r = (i // (2 * s)) * 2 * s + s - 1
        upper = (i // s) % 2 == 1
        m = np.arange(C)[None, :]
        mq = ((m > r[:, None]) & (m <= i[:, None]) & upper[:, None]).astype(np.float32)
        mk = ((m > i[:, None]) & (m <= r[:, None]) & (~upper)[:, None]).astype(np.float32)
        mats += [mq, mk]
    return np.concatenate(mats, axis=0)


def _split3(x):
    h = x.astype(jnp.bfloat16)
    r = x - h.astype(jnp.float32)
    m = r.astype(jnp.bfloat16)
    l = (r - m.astype(jnp.float32)).astype(jnp.bfloat16)
    return h, m, l


def _gla_kernel(q_ref, k_ref, v_ref, r_ref, glr_ref, gw_ref, gb_ref, ng_ref, msel_ref, o_ref, state_ref):
    C = GLA_CHUNK

    @pl.when(pl.program_id(1) == 0)
    def _():
        state_ref[...] = jnp.zeros_like(state_ref)

    ii = lax.broadcasted_iota(jnp.int32, (C, C), 0)
    jj = lax.broadcasted_iota(jnp.int32, (C, C), 1)
    lane_h = lax.broadcasted_iota(jnp.int32, (C, GLA_QK), 1) // GLA_DK
    bd = (lax.broadcasted_iota(jnp.int32, (GLA_QK, GLA_V), 0) // GLA_DK
          == lax.broadcasted_iota(jnp.int32, (GLA_QK, GLA_V), 1) // GLA_DV)
    msel = msel_ref[...]
    gw = gw_ref[...].astype(jnp.bfloat16)
    dn = (((1,), (1,)), ((), ()))

    def chunk(c, carry):
        r0 = pl.multiple_of(c * C, C)
        q = q_ref[pl.ds(r0, C), :] * (GLA_DK ** -0.5)
        k = k_ref[pl.ds(r0, C), :]
        v = v_ref[pl.ds(r0, C), :].astype(jnp.bfloat16)
        z = jnp.dot(glr_ref[pl.ds(r0, C), :].astype(jnp.bfloat16), gw, preferred_element_type=jnp.float32) + gb_ref[...]
        a = jax.nn.log_sigmoid(z) / GLA_GATE_TEMP
        ah, am, al = _split3(a)
        ex = (jnp.dot(msel, ah, preferred_element_type=jnp.float32)
              + jnp.dot(msel, am, preferred_element_type=jnp.float32)
              + jnp.dot(msel, al, preferred_element_type=jnp.float32))
        bcum = ex[0:C]
        state = state_ref[...]
        o = jnp.dot((q * jnp.exp(bcum)).astype(jnp.bfloat16), state.astype(jnp.bfloat16),
                    preferred_element_type=jnp.float32)
        kb = k.astype(jnp.bfloat16)
        for h in range(GLA_HEADS):
            hm = lane_h == h
            att = jnp.where(ii == jj, lax.dot_general(jnp.where(hm, q, 0.0).astype(jnp.bfloat16), kb, dn,
                                                      preferred_element_type=jnp.float32), 0.0)
            for li, s in enumerate(GLA_LEVELS):
                eq = jnp.exp(ex[(1 + 2 * li) * C:(2 + 2 * li) * C])
                ek = jnp.exp(ex[(2 + 2 * li) * C:(3 + 2 * li) * C])
                qs = jnp.where(hm, q * eq, 0.0).astype(jnp.bfloat16)
                ks = (k * ek).astype(jnp.bfloat16)
                blk = (ii // (2 * s) == jj // (2 * s)) & ((ii // s) % 2 == 1) & ((jj // s) % 2 == 0)
                att = att + jnp.where(blk, lax.dot_general(qs, ks, dn, preferred_element_type=jnp.float32), 0.0)
            oh = jnp.dot(att.astype(jnp.bfloat16), v, preferred_element_type=jnp.float32)
            o = o + jnp.where(lax.broadcasted_iota(jnp.int32, (C, GLA_V), 1) // GLA_DV == h, oh, 0.0)
        blast = bcum[C - 1:C, :]
        kd_t = (k * jnp.exp(blast - bcum)).T.astype(jnp.bfloat16)
        decay_col = jnp.exp(jnp.sum(a.T, axis=1, keepdims=True))
        upd = jnp.dot(kd_t, v, preferred_element_type=jnp.float32)
        state_ref[...] = state * decay_col + jnp.where(bd, upd, 0.0)
        outs = []
        for h in range(GLA_HEADS):
            oh = o[:, h * GLA_DV:(h + 1) * GLA_DV]
            outs.append(oh * lax.rsqrt(jnp.mean(oh * oh, axis=1, keepdims=True) + EPS) * ng_ref[...])
        on = jnp.concatenate(outs, axis=1)
        r = r_ref[pl.ds(r0, C), :]
        o_ref[pl.ds(r0, C), :] = (on * (r * jax.nn.sigmoid(r))).astype(o_ref.dtype)
        return carry

    lax.fori_loop(0, q_ref.shape[0] // C, chunk, 0)


def gla_pallas(z, gate_w, gate_b, norm_g, B, S):
    n = B * S
    tb = GLA_TB
    nb = S // tb
    gw = jnp.pad(gate_w, ((0, LANE - gate_w.shape[0]), (0, 0)))
    msel = jnp.asarray(_gla_consts(), dtype=jnp.bfloat16)
    zb = lambda col, w: pl.BlockSpec((tb, w), lambda b, i: (b * nb + i, col // w))
    full = lambda shp: pl.BlockSpec(shp, lambda b, i: (0,) * len(shp))
    return pl.pallas_call(
        _gla_kernel,
        grid=(B, nb),
        in_specs=[zb(Z_GQ, GLA_QK), zb(Z_GK, GLA_QK), zb(Z_GV, GLA_V), zb(Z_GR, GLA_V), zb(Z_GLR, LANE),
                  full((LANE, GLA_QK)), full((1, GLA_QK)), full((1, GLA_DV)), full(msel.shape)],
        out_specs=pl.BlockSpec((tb, GLA_V), lambda b, i: (b * nb + i, 0)),
        out_shape=jax.ShapeDtypeStruct((n, GLA_V), jnp.bfloat16),
        scratch_shapes=[pltpu.VMEM((GLA_QK, GLA_V), jnp.float32)],
        compiler_params=pltpu.CompilerParams(dimension_semantics=("parallel", "arbitrary")),
        name="gla",
    )(z, z, z, z, z, gw, gate_b.reshape(1, -1), norm_g.reshape(1, -1), msel)


_FULL_BLOCKS = 8


def _extract_top(s, k, tie_break):
    n, t = s.shape
    row = lax.broadcasted_iota(jnp.int32, (n, t), 0)
    rank = jnp.full((n, t), k, jnp.int32)
    vals = []
    for r in range(k):
        m = jnp.max(s, axis=0, keepdims=True)
        hit = s == m
        if tie_break:
            hit = row == jnp.min(jnp.where(hit, row, n), axis=0, keepdims=True)
        rank = jnp.where(hit, r, rank)
        s = jnp.where(hit, float('-inf'), s)
        vals.append(m)
    return jnp.concatenate(vals, axis=0), rank


def _peer_gate_math(s1, s2, tie_break):
    K = PEER_TOPK
    t = s1.shape[1]
    v1, rank1 = _extract_top(s1, K, tie_break)
    v2, rank2 = _extract_top(s2, K, tie_break)
    blocks, poss = [], []
    row16 = lax.broadcasted_iota(jnp.int32, (K, t), 0)
    for r1 in range(_FULL_BLOCKS):
        c = v1[r1:r1 + 1, :] + v2
        blocks.append(jnp.where(row16 < K // (r1 + 1), c, float('-inf')))
        poss.append(row16 + r1 * K)
    row8 = lax.broadcasted_iota(jnp.int32, (K - _FULL_BLOCKS, t), 0)
    blocks.append(v1[_FULL_BLOCKS:, :] + v2[0:1, :])
    poss.append((row8 + _FULL_BLOCKS) * K)
    sels = [jnp.zeros(b.shape, jnp.bool_) for b in blocks]
    big = K * K
    for _ in range(K):
        m = blocks[0].max(axis=0, keepdims=True)
        for b in blocks[1:]:
            m = jnp.maximum(m, b.max(axis=0, keepdims=True))
        if tie_break:
            first = None
            for b, p in zip(blocks, poss):
                f = jnp.min(jnp.where(b == m, p, big), axis=0, keepdims=True)
                first = f if first is None else jnp.minimum(first, f)
            hits = [p == first for p in poss]
        else:
            hits = [b == m for b in blocks]
        for i, hit in enumerate(hits):
            sels[i] = sels[i] | hit
            blocks[i] = jnp.where(hit, float('-inf'), blocks[i])
    top = v1[0:1, :] + v2[0:1, :]
    cnt_rows = []
    z = jnp.zeros((1, t), jnp.float32)
    for r1 in range(_FULL_BLOCKS):
        sel = sels[r1]
        cnt_rows.append(jnp.sum(sel.astype(jnp.float32), axis=0, keepdims=True))
        c = v1[r1:r1 + 1, :] + v2
        z = z + jnp.sum(jnp.where(sel, jnp.exp(c - top), 0.0), axis=0, keepdims=True)
    sel = sels[_FULL_BLOCKS]
    cnt_tail = sel.astype(jnp.float32)
    c = v1[_FULL_BLOCKS:, :] + v2[0:1, :]
    z = z + jnp.sum(jnp.where(sel, jnp.exp(c - top), 0.0), axis=0, keepdims=True)
    cnt = jnp.concatenate(cnt_rows + [cnt_tail], axis=0)
    a = jnp.zeros(s1.shape, jnp.float32)
    for r in range(K):
        a = jnp.where(rank1 == r, cnt[r:r + 1, :], a)
    taken = (jnp.sum((rank1 < K).astype(jnp.float32), axis=0, keepdims=True),
             jnp.sum((rank2 < K).astype(jnp.float32), axis=0, keepdims=True),
             jnp.sum(cnt, axis=0, keepdims=True))
    return jnp.exp(s1 - v1[0:1, :]), a, rank2.astype(jnp.float32), jnp.exp(s2 - v2[0:1, :]) / z, taken


def _peer_gate_kernel(q_ref, keys_ref, e1_ref, a_ref, r2_ref, e2_ref):
    q = q_ref[...].astype(jnp.bfloat16)
    k1 = keys_ref[0, 0].astype(jnp.bfloat16)
    k2 = keys_ref[0, 1].astype(jnp.bfloat16)
    dn = (((1,), (1,)), ((), ()))
    s1 = lax.dot_general(k1, q[:, :PEER_DKEY // 2], dn, preferred_element_type=jnp.float32)
    s2 = lax.dot_general(k2, q[:, PEER_DKEY // 2:], dn, preferred_element_type=jnp.float32)

    def write(e1, a, r2, e2):
        e1_ref[0] = e1
        a_ref[0] = a
        r2_ref[0] = r2.astype(jnp.bfloat16)
        e2_ref[0] = e2.astype(jnp.bfloat16)

    e1, a, r2, e2, taken = _peer_gate_math(s1, s2, tie_break=False)
    write(e1, a, r2, e2)
    excess = sum(jnp.max(jnp.abs(c - PEER_TOPK)) for c in taken)

    @pl.when(excess > 0.0)
    def _():
        write(*_peer_gate_math(s1, s2, tie_break=True)[:4])


def peer_gates(qh, keys, tt=256):
    n = qh.shape[0]
    H = PEER_HEADS
    shp32 = jax.ShapeDtypeStruct((H, PEER_NKEYS, n), jnp.float32)
    shp16 = jax.ShapeDtypeStruct((H, PEER_NKEYS, n), jnp.bfloat16)
    ospec = pl.BlockSpec((1, PEER_NKEYS, tt), lambda i, h: (h, 0, i))
    return pl.pallas_call(
        _peer_gate_kernel,
        grid=(n // tt, H),
        in_specs=[pl.BlockSpec((tt, PEER_DKEY), lambda i, h: (i, h)),
                  pl.BlockSpec((1, 2, PEER_NKEYS, PEER_DKEY // 2), lambda i, h: (h, 0, 0, 0))],
        out_specs=[ospec, ospec, ospec, ospec],
        out_shape=[shp32, shp32, shp16, shp16],
        compiler_params=pltpu.CompilerParams(dimension_semantics=("parallel", "parallel")),
        name="peer_gates",
    )(qh, keys)


def _peer_dense_kernel(xt_ref, u_ref, vt_ref, e1_ref, a_ref, r2_ref, e2_ref, x1_ref, gate_ref, fg_ref,
                       o_ref, acc_ref, *, ec, final):
    j = pl.program_id(1)
    nsub = ec // PEER_NKEYS

    @pl.when(j == 0)
    def _():
        acc_ref[...] = jnp.zeros_like(acc_ref)

    def gate_mat(i):
        a_idx = j * nsub + i
        g = None
        for h in range(PEER_HEADS):
            cnt = a_ref[h, pl.ds(a_idx, 1), :].astype(jnp.bfloat16)
            e1 = e1_ref[h, pl.ds(a_idx, 1), :].astype(jnp.bfloat16)
            term = jnp.where(r2_ref[h] < cnt, e2_ref[h], jnp.bfloat16(0)) * e1
            g = term if g is None else g + term
        return g

    def scores(i):
        return jnp.dot(u_ref[i * PEER_NKEYS:(i + 1) * PEER_NKEYS, :], xt_ref[...], preferred_element_type=jnp.float32)

    st = scores(0)
    ws = []
    for i in range(nsub):
        st_next = scores(i + 1) if i + 1 < nsub else None
        ws.append(_gelu(st.astype(jnp.bfloat16)) * gate_mat(i))
        if i % PEER_GROUPS_PER_DOT == PEER_GROUPS_PER_DOT - 1:
            w = jnp.concatenate(ws, axis=0)
            ws = []
            acc_ref[...] += jnp.dot(vt_ref[:, (i + 1 - PEER_GROUPS_PER_DOT) * PEER_NKEYS:(i + 1) * PEER_NKEYS], w,
                                    preferred_element_type=jnp.float32)
        st = st_next

    @pl.when(j == pl.num_programs(1) - 1)
    def _():
        x2 = x1_ref[...] + gate_ref[0] * acc_ref[...].T
        if final:
            x2 = x2 * lax.rsqrt(jnp.mean(x2 * x2, axis=1, keepdims=True) + EPS) * fg_ref[...]
        o_ref[...] = x2


def peer_dense(xt, u16, vt16, e1, a, r2, e2, x1, gate2, final_g, S, final, tt=512, ec=1024):
    d, n = xt.shape
    ne = u16.shape[0]
    H = PEER_HEADS
    per_b = S // tt
    gspec = lambda: pl.BlockSpec((H, PEER_NKEYS, tt), lambda i, j: (0, 0, i))
    return pl.pallas_call(
        functools.partial(_peer_dense_kernel, ec=ec, final=final),
        grid=(n // tt, ne // ec),
        in_specs=[pl.BlockSpec((d, tt), lambda i, j: (0, i)),
                  pl.BlockSpec((ec, d), lambda i, j: (j, 0)),
                  pl.BlockSpec((d, ec), lambda i, j: (0, j)),
                  gspec(), gspec(), gspec(), gspec(),
                  pl.BlockSpec((tt, d), lambda i, j: (i, 0)),
                  pl.BlockSpec((1, 1, d), lambda i, j: (i // per_b, 0, 0)),
                  pl.BlockSpec((1, d), lambda i, j: (0, 0))],
        out_specs=pl.BlockSpec((tt, d), lambda i, j: (i, 0)),
        out_shape=jax.ShapeDtypeStruct((n, d), jnp.float32),
        scratch_shapes=[pltpu.VMEM((d, tt), jnp.float32)],
        compiler_params=pltpu.CompilerParams(dimension_semantics=("parallel", "arbitrary"),
                                             vmem_limit_bytes=VMEM_LIMIT),
        name="peer_dense",
    )(xt, u16, vt16, e1, a, r2, e2, x1, gate2, final_g.reshape(1, d))


def adaln(c, w, b):
    mod = pmm(jax.nn.silu(c), w) + b
    shift, scale, gate = jnp.split(mod[:, None, :], 3, axis=-1)
    return shift, 1.0 + scale, gate


def kernel(x, c, positions, ada_w, ada_b, norm_g, w_in, conv_dw_w, conv_dw_b, conv_ln_g, conv_ln_b, w_conv_up,
           cmp_pos, cmp_w1, cmp_w2, w_nsa_up, gla_gate_w, gla_gate_b, gla_norm_g, w_gla_up, w_out,
           peer_wq, peer_keys, peer_u, peer_v, final_g):
    B, S, D = x.shape
    bf = jnp.bfloat16
    x2 = x.reshape(B * S, D)
    cs = rope_tables(positions)
    for l in range(DEPTH):
        shift, scale1p, gate = adaln(c, ada_w[l, 0], ada_b[l, 0])
        z = in_proj(x2, norm_g[l, 0], scale1p, shift, pack_w_in(w_in[l]), S)
        hc = conv_pallas(z, conv_dw_w[l], conv_dw_b[l], conv_ln_g[l], conv_ln_b[l], B, S)
        q, kst, vs, kwt, vw = nsa_prep(z, cs, B, S)
        kct, vc = nsa_compress(z, positions, cmp_pos[l], cmp_w1[l], cmp_w2[l], B, S)
        on = nsa_core(q, z, kct, vc, kst, vs, kwt, vw, B, S)
        og = gla_pallas(z, gla_gate_w[l], gla_gate_b[l], gla_norm_g[l], B, S)
        shift2, scale2p, gate2 = adaln(c, ada_w[l, 1], ada_b[l, 1])
        x1, ut, qh = mix_out(hc, on, og, z, x2, gate, w_conv_up[l].astype(bf), w_nsa_up[l].astype(bf),
                             w_gla_up[l].astype(bf), w_out[l].astype(bf), norm_g[l, 1], scale2p, shift2,
                             peer_wq[l].astype(bf), S)
        e1, a, r2, e2 = peer_gates(qh, peer_keys[l])
        x2 = peer_dense(ut, peer_u[l].astype(bf), peer_v[l].astype(bf).T, e1, a, r2, e2, x1, gate2, final_g,
                        S, final=(l == DEPTH - 1))
    return x2.reshape(B, S, D)
```

```python
import functools

import jax
import jax.numpy as jnp
import numpy as np
from jax import lax
from jax.experimental import pallas as pl
from jax.experimental.pallas import tpu as pltpu

D_MODEL = 1024
DEPTH = 2
EPS = 1e-6
NEG = -1e30
N_BRANCH = 3
CONV_DIM = 512
CONV_WIDTH = 31
NSA_HEADS = 8
NSA_KV_HEADS = 2
NSA_GROUP = NSA_HEADS // NSA_KV_HEADS
HEAD_DIM = 64
CMP_BLOCK = 32
CMP_STRIDE = 16
CMP_HIDDEN = 256
SLC_BLOCK = 64
SLC_TOPK = 16
WINDOW = 512
FORCE_BONUS = 1e4
ROPE_THETA = 10000.0
GLA_HEADS = 4
GLA_DK = 64
GLA_DV = 128
GLA_GATE_RANK = 16
GLA_GATE_TEMP = 16.0
GLA_CHUNK = 256
PEER_HEADS = 8
PEER_NKEYS = 128
PEER_DKEY = 256
PEER_TOPK = 16
NSA_Q = NSA_HEADS * HEAD_DIM
NSA_KV = 3 * 2 * NSA_KV_HEADS * HEAD_DIM
NSA_G = 3 * NSA_HEADS
GLA_QK = GLA_HEADS * GLA_DK
GLA_V = GLA_HEADS * GLA_DV
SPLITS = [2 * CONV_DIM, NSA_Q, NSA_KV, NSA_G, GLA_QK, GLA_QK, GLA_V, GLA_V, GLA_GATE_RANK, N_BRANCH * D_MODEL]

LANE = 128
TQ = 128
KC = 512
GW = NSA_GROUP * HEAD_DIM
VMEM_LIMIT = 48 * 1024 * 1024
GLA_LEVELS = tuple(GLA_CHUNK >> s for s in range(1, GLA_CHUNK.bit_length()))
GLA_TB = 512
CONV_TS = 512
CONV_HALO = 32
ROW_TILE = 1024
MIX_TILE = 256
PEER_GROUPS_PER_DOT = 2

Z_MG = 0
Z_CONV = Z_MG + N_BRANCH * D_MODEL
Z_Q = Z_CONV + 2 * CONV_DIM
Z_QSW = Z_Q + NSA_Q
Z_GV = Z_QSW + NSA_Q
Z_GR = Z_GV + GLA_V
Z_GQ = Z_GR + GLA_V
Z_GK = Z_GQ + GLA_QK
Z_CMP = Z_GK + GLA_QK
Z_SLC = Z_CMP + 256
Z_WIN = Z_SLC + 384
Z_NG = Z_WIN + 384
Z_GLR = Z_NG + NSA_KV_HEADS * LANE
NZ = Z_GLR + LANE
assert Z_SLC % 384 == 0 and Z_WIN % 384 == 0 and Z_CONV % (2 * CONV_DIM) == 0 and Z_GQ % GLA_QK == 0


def _mm_kernel(a_ref, b_ref, o_ref):
    o_ref[...] = jnp.dot(a_ref[...].astype(jnp.bfloat16), b_ref[...].astype(jnp.bfloat16),
                         preferred_element_type=jnp.float32)


def _pick(n, cands):
    for c in cands:
        if n % c == 0:
            return c
    return n


def pmm(a, b):
    m, k = a.shape
    n = b.shape[1]
    n_pad = -n % LANE
    if n_pad:
        b = jnp.pad(b, ((0, 0), (0, n_pad)))
    m_pad = -m % 8
    if m_pad:
        a = jnp.pad(a, ((0, m_pad), (0, 0)))
    mp, np_ = m + m_pad, n + n_pad
    tm = _pick(mp, (512, 256, 128, 64, 32, 16, 8))
    tn = _pick(np_, (512, 640, 384, 256, 128))
    out = pl.pallas_call(
        _mm_kernel,
        grid=(mp // tm, np_ // tn),
        in_specs=[pl.BlockSpec((tm, k), lambda i, j: (i, 0)),
                  pl.BlockSpec((k, tn), lambda i, j: (0, j))],
        out_specs=pl.BlockSpec((tm, tn), lambda i, j: (i, j)),
        out_shape=jax.ShapeDtypeStruct((mp, np_), jnp.float32),
        compiler_params=pltpu.CompilerParams(dimension_semantics=("parallel", "parallel")),
        name="pmm",
    )(a, b)
    return out[:m, :n]


def _in_proj_kernel(x_ref, g_ref, sc_ref, sh_ref, w_ref, z_ref, u_ref):
    @pl.when(pl.program_id(1) == 0)
    def _():
        x = x_ref[...]
        y = x * lax.rsqrt(jnp.mean(x * x, axis=1, keepdims=True) + EPS) * g_ref[...]
        u_ref[...] = (y * sc_ref[0] + sh_ref[0]).astype(jnp.bfloat16)

    z_ref[...] = jnp.dot(u_ref[...], w_ref[...], preferred_element_type=jnp.float32)


def in_proj(x2, g, scale1p, shift, w16, S):
    n, d = x2.shape
    nz = w16.shape[1]
    tm = ROW_TILE
    tn = _pick(nz, (1152, 1024, 896, 512, 384, 256, 128))
    per_b = S // tm
    mod = lambda: pl.BlockSpec((1, 1, d), lambda i, j: (i // per_b, 0, 0))
    return pl.pallas_call(
        _in_proj_kernel,
        grid=(n // tm, nz // tn),
        in_specs=[pl.BlockSpec((tm, d), lambda i, j: (i, 0)),
                  pl.BlockSpec((1, d), lambda i, j: (0, 0)), mod(), mod(),
                  pl.BlockSpec((d, tn), lambda i, j: (0, j))],
        out_specs=pl.BlockSpec((tm, tn), lambda i, j: (i, j)),
        out_shape=jax.ShapeDtypeStruct((n, nz), jnp.float32),
        scratch_shapes=[pltpu.VMEM((tm, d), jnp.bfloat16)],
        compiler_params=pltpu.CompilerParams(dimension_semantics=("parallel", "arbitrary"),
                                             vmem_limit_bytes=VMEM_LIMIT),
        name="in_proj",
    )(x2, g.reshape(1, d), scale1p, shift, w16)


def _half_swap(n_heads):
    idx = np.arange(n_heads * HEAD_DIM).reshape(n_heads, 2, HEAD_DIM // 2)
    return idx[:, ::-1, :].reshape(-1)


def pack_w_in(w_in):
    o = np.cumsum([0] + SPLITS)
    a_conv, nq, nkv, ng, gq, gk, gv, gr, glr, mg = [w_in[:, o[i]:o[i + 1]] for i in range(len(SPLITS))]
    kvw = 2 * NSA_KV_HEADS * HEAD_DIM
    kw = NSA_KV_HEADS * HEAD_DIM
    d = w_in.shape[0]
    sw = _half_swap(NSA_KV_HEADS)
    cols = [mg, a_conv, nq, nq[:, _half_swap(NSA_HEADS)], gv, gr, gq, gk, nkv[:, 0:kvw]]
    for br in (1, 2):
        k = nkv[:, br * kvw:br * kvw + kw]
        cols += [k, k[:, sw], nkv[:, br * kvw + kw:(br + 1) * kvw]]
    gpk = NSA_GROUP * 3
    for kvh in range(NSA_KV_HEADS):
        cols += [ng[:, kvh * gpk:(kvh + 1) * gpk], jnp.zeros((d, LANE - gpk), w_in.dtype)]
    cols += [glr, jnp.zeros((d, LANE - GLA_GATE_RANK), w_in.dtype)]
    w = jnp.concatenate(cols, axis=1)
    assert w.shape[1] == NZ
    return w.astype(jnp.bfloat16)


def rope_tables(positions):
    half = HEAD_DIM // 2
    freq = ROPE_THETA ** (-jnp.arange(half, dtype=jnp.float32) / half)
    ang = positions.astype(jnp.float32)[..., None] * freq
    cos, sin = jnp.cos(ang), jnp.sin(ang)
    c = jnp.tile(jnp.concatenate([cos, cos], -1), (1, 1, LANE // HEAD_DIM))
    s = jnp.tile(jnp.concatenate([-sin, sin], -1), (1, 1, LANE // HEAD_DIM))
    return jnp.concatenate([c, s], -1).reshape(-1, 2 * LANE)


def _nsa_prep_kernel(q_ref, qs_ref, slc_ref, win_ref, cs_ref, sel_ref, qo_ref, kst_ref, vs_ref, kwt_ref, vw_ref):
    cos = cs_ref[:, :LANE]
    sin = cs_ref[:, LANE:]
    scale = HEAD_DIM ** -0.5
    for t in range(NSA_Q // LANE):
        sl = slice(t * LANE, (t + 1) * LANE)
        qo_ref[:, sl] = ((q_ref[:, sl] * cos + qs_ref[:, sl] * sin) * scale).astype(jnp.bfloat16)
    dn = (((1,), (1,)), ((), ()))
    for src, kt_ref, v_ref in ((slc_ref, kst_ref, vs_ref), (win_ref, kwt_ref, vw_ref)):
        k = (src[:, 0:LANE] * cos + src[:, LANE:2 * LANE] * sin).astype(jnp.bfloat16)
        v = src[:, 2 * LANE:3 * LANE].astype(jnp.bfloat16)
        for kvh in range(NSA_KV_HEADS):
            sel = sel_ref[kvh]
            kt_ref[0, kvh] = lax.dot_general(sel, k, dn, preferred_element_type=jnp.float32).astype(jnp.bfloat16)
            v_ref[0, kvh] = lax.dot_general(v, sel, dn, preferred_element_type=jnp.float32).astype(jnp.bfloat16)


def _head_repeat_sel():
    sel = np.zeros((NSA_KV_HEADS, GW, LANE), np.float32)
    for kvh in range(NSA_KV_HEADS):
        for r in range(GW):
            sel[kvh, r, kvh * HEAD_DIM + r % HEAD_DIM] = 1.0
    return jnp.asarray(sel, dtype=jnp.bfloat16)


def nsa_prep(z, cs, B, S):
    n = B * S
    tp = ROW_TILE
    nb = S // tp
    zb = lambda col, w: pl.BlockSpec((tp, w), lambda b, i: (b * nb + i, col // w))
    kt = lambda: pl.BlockSpec((1, NSA_KV_HEADS, GW, tp), lambda b, i: (b, 0, 0, i))
    vv = lambda: pl.BlockSpec((1, NSA_KV_HEADS, tp, GW), lambda b, i: (b, 0, i, 0))
    kt_shape = jax.ShapeDtypeStruct((B, NSA_KV_HEADS, GW, S), jnp.bfloat16)
    v_shape = jax.ShapeDtypeStruct((B, NSA_KV_HEADS, S, GW), jnp.bfloat16)
    return pl.pallas_call(
        _nsa_prep_kernel,
        grid=(B, nb),
        in_specs=[zb(Z_Q, NSA_Q), zb(Z_QSW, NSA_Q), zb(Z_SLC, 384), zb(Z_WIN, 384),
                  pl.BlockSpec((tp, 2 * LANE), lambda b, i: (b * nb + i, 0)),
                  pl.BlockSpec((NSA_KV_HEADS, GW, LANE), lambda b, i: (0, 0, 0))],
        out_specs=[pl.BlockSpec((tp, NSA_Q), lambda b, i: (b * nb + i, 0)), kt(), vv(), kt(), vv()],
        out_shape=[jax.ShapeDtypeStruct((n, NSA_Q), jnp.bfloat16), kt_shape, v_shape, kt_shape, v_shape],
        compiler_params=pltpu.CompilerParams(dimension_semantics=("parallel", "parallel")),
        name="nsa_prep",
    )(z, z, z, z, cs, _head_repeat_sel())


def _gelu(x):
    return 0.5 * x * (1.0 + jnp.tanh(0.7978845608028654 * (x + 0.044715 * x * x * x)))


def _compress_kernel(gk_ref, gv_ref, pe_ref, w1_ref, w2_ref, cs_ref, kct_ref, vc_ref):
    nrow = gk_ref.shape[2]
    outs = []
    for kv, g_ref in enumerate((gk_ref, gv_ref)):
        g = g_ref[0, 0]
        lo = jnp.dot((g + pe_ref[kv, 0]).astype(jnp.bfloat16), w1_ref[kv, 0], preferred_element_type=jnp.float32)
        hi = jnp.dot((g + pe_ref[kv, 1]).astype(jnp.bfloat16), w1_ref[kv, 1], preferred_element_type=jnp.float32)
        hid = _gelu(lo + pltpu.roll(hi, nrow - 1, 0)).astype(jnp.bfloat16)
        outs.append(jnp.dot(hid, w2_ref[kv], preferred_element_type=jnp.float32))
    k = outs[0][:, :GW] * cs_ref[0, :, :GW] + outs[0][:, GW:] * cs_ref[0, :, GW:]
    kct_ref[0, 0] = k.T.astype(jnp.bfloat16)
    vc_ref[0, 0] = outs[1][:, :GW].astype(jnp.bfloat16)


def nsa_compress(z, positions, cmp_pos, cmp_w1, cmp_w2, B, S):
    ng = S // CMP_STRIDE
    grp = CMP_STRIDE * HEAD_DIM
    c = z[:, Z_CMP:Z_CMP + 256].reshape(B, ng, CMP_STRIDE, 2, NSA_KV_HEADS, HEAD_DIM)
    g = c.transpose(3, 0, 4, 1, 2, 5).reshape(2, B, NSA_KV_HEADS, ng, grp)
    pe = cmp_pos.reshape(2, 2, 1, grp)
    w1 = cmp_w1.reshape(2, 2, grp, CMP_HIDDEN).astype(jnp.bfloat16)
    rep = jnp.tile(cmp_w2, (1, 1, NSA_GROUP))
    sw = np.tile(_half_swap(1), NSA_GROUP) + np.repeat(np.arange(NSA_GROUP) * HEAD_DIM, HEAD_DIM)
    w2 = jnp.concatenate([rep, rep[:, :, sw]], axis=-1).astype(jnp.bfloat16)
    end = jnp.minimum(jnp.arange(ng) * CMP_STRIDE + CMP_BLOCK - 1, S - 1)
    half = HEAD_DIM // 2
    freq = ROPE_THETA ** (-jnp.arange(half, dtype=jnp.float32) / half)
    ang = positions[:, end].astype(jnp.float32)[..., None] * freq
    cos, sin = jnp.cos(ang), jnp.sin(ang)
    cs = jnp.concatenate([jnp.tile(jnp.concatenate([cos, cos], -1), (1, 1, NSA_GROUP)),
                          jnp.tile(jnp.concatenate([-sin, sin], -1), (1, 1, NSA_GROUP))], -1)
    gspec = lambda: pl.BlockSpec((1, 1, ng, grp), lambda b, k: (b, k, 0, 0))
    return pl.pallas_call(
        _compress_kernel,
        grid=(B, NSA_KV_HEADS),
        in_specs=[gspec(), gspec(),
                  pl.BlockSpec((2, 2, 1, grp), lambda b, k: (0, 0, 0, 0)),
                  pl.BlockSpec((2, 2, grp, CMP_HIDDEN), lambda b, k: (0, 0, 0, 0)),
                  pl.BlockSpec((2, CMP_HIDDEN, 2 * GW), lambda b, k: (0, 0, 0)),
                  pl.BlockSpec((1, ng, 2 * GW), lambda b, k: (b, 0, 0))],
        out_specs=[pl.BlockSpec((1, 1, GW, ng), lambda b, k: (b, k, 0, 0)),
                   pl.BlockSpec((1, 1, ng, GW), lambda b, k: (b, k, 0, 0))],
        out_shape=[jax.ShapeDtypeStruct((B, NSA_KV_HEADS, GW, ng), jnp.bfloat16),
                   jax.ShapeDtypeStruct((B, NSA_KV_HEADS, ng, GW), jnp.bfloat16)],
        compiler_params=pltpu.CompilerParams(dimension_semantics=("parallel", "parallel")),
        name="nsa_compress",
    )(g[0], g[1], pe, w1, w2, cs)


def _mix_kernel(hc_ref, on_ref, og_ref, mg_ref, x_ref, gate_ref, wc_ref, wn_ref, wg_ref, wo_ref,
                g2_ref, sc_ref, sh_ref, wq_ref, x1_ref, ut_ref, qh_ref):
    d = D_MODEL
    f32 = jnp.float32
    y = (jax.nn.sigmoid(mg_ref[:, 0:d]) * jnp.dot(hc_ref[...], wc_ref[...], preferred_element_type=f32)
         + jax.nn.sigmoid(mg_ref[:, d:2 * d]) * jnp.dot(on_ref[...], wn_ref[...], preferred_element_type=f32)
         + jax.nn.sigmoid(mg_ref[:, 2 * d:3 * d]) * jnp.dot(og_ref[...], wg_ref[...], preferred_element_type=f32))
    x1 = x_ref[...] + gate_ref[0] * jnp.dot(y.astype(jnp.bfloat16), wo_ref[...], preferred_element_type=f32)
    x1_ref[...] = x1
    u = x1 * lax.rsqrt(jnp.mean(x1 * x1, axis=1, keepdims=True) + EPS) * g2_ref[...] * sc_ref[0] + sh_ref[0]
    ut_ref[...] = u.T.astype(jnp.bfloat16)
    qh_ref[...] = jnp.dot(u.astype(jnp.bfloat16), wq_ref[...], preferred_element_type=f32).astype(qh_ref.dtype)


def mix_out(hc, on, og, z, x2, gate1, wc, wn, wg, wo, g2, scale2p, shift2, wq, S):
    n, d = x2.shape
    tm = MIX_TILE
    per_b = S // tm
    row = lambda w: pl.BlockSpec((tm, w), lambda i: (i, 0))
    mod = lambda: pl.BlockSpec((1, 1, d), lambda i: (i // per_b, 0, 0))
    full = lambda a: pl.BlockSpec(a.shape, lambda i: (0,) * a.ndim)
    nq = wq.shape[1]
    return pl.pallas_call(
        _mix_kernel,
        grid=(n // tm,),
        in_specs=[row(CONV_DIM), row(NSA_Q), row(GLA_V),
                  pl.BlockSpec((tm, N_BRANCH * d), lambda i: (i, Z_MG // (N_BRANCH * d))),
                  row(d), mod(), full(wc), full(wn), full(wg), full(wo),
                  pl.BlockSpec((1, d), lambda i: (0, 0)), mod(), mod(), full(wq)],
        out_specs=[row(d), pl.BlockSpec((d, tm), lambda i: (0, i)), row(nq)],
        out_shape=[jax.ShapeDtypeStruct((n, d), jnp.float32), jax.ShapeDtypeStruct((d, n), jnp.bfloat16),
                   jax.ShapeDtypeStruct((n, nq), jnp.bfloat16)],
        compiler_params=pltpu.CompilerParams(dimension_semantics=("parallel",), vmem_limit_bytes=VMEM_LIMIT),
        name="mix_out",
    )(hc, on, og, z, x2, gate1, wc, wn, wg, wo, g2.reshape(1, d), scale2p, shift2, wq)


def _conv_kernel(a_ref, ah_ref, w_ref, b_ref, g_ref, beta_ref, o_ref, hbuf):
    i = pl.program_id(1)
    ts = a_ref.shape[0]
    a = a_ref[...]
    hbuf[pl.ds(CONV_HALO, ts), :] = a[:, :CONV_DIM] * jax.nn.sigmoid(a[:, CONV_DIM:])
    ah = ah_ref[...]
    halo = ah[:, :CONV_DIM] * jax.nn.sigmoid(ah[:, CONV_DIM:])
    hbuf[pl.ds(0, CONV_HALO), :] = jnp.where(i > 0, halo, 0.0)
    acc = jnp.zeros((ts, CONV_DIM), jnp.float32) + b_ref[...]
    off = CONV_HALO - (CONV_WIDTH - 1)
    for k in range(CONV_WIDTH):
        acc = acc + w_ref[k:k + 1, :] * hbuf[pl.ds(off + k, ts), :]
    mu = jnp.mean(acc, axis=1, keepdims=True)
    d = acc - mu
    var = jnp.mean(d * d, axis=1, keepdims=True)
    y = d * lax.rsqrt(var + EPS) * g_ref[...] + beta_ref[...]
    o_ref[...] = (y * jax.nn.sigmoid(y)).astype(o_ref.dtype)


def conv_pallas(z, dw_w, dw_b, ln_g, ln_b, B, S):
    n = B * S
    ts = CONV_TS
    ns = S // ts
    hb = ts // CONV_HALO
    cb = Z_CONV // (2 * CONV_DIM)
    wpad = jnp.pad(dw_w, ((0, 32 - CONV_WIDTH), (0, 0)))
    vec = lambda: pl.BlockSpec((1, CONV_DIM), lambda b, i: (0, 0))
    return pl.pallas_call(
        _conv_kernel,
        grid=(B, ns),
        in_specs=[pl.BlockSpec((ts, 2 * CONV_DIM), lambda b, i: (b * ns + i, cb)),
                  pl.BlockSpec((CONV_HALO, 2 * CONV_DIM), lambda b, i: (jnp.maximum((b * ns + i) * hb - 1, 0), cb)),
                  pl.BlockSpec((32, CONV_DIM), lambda b, i: (0, 0)), vec(), vec(), vec()],
        out_specs=pl.BlockSpec((ts, CONV_DIM), lambda b, i: (b * ns + i, 0)),
        out_shape=jax.ShapeDtypeStruct((n, CONV_DIM), jnp.bfloat16),
        scratch_shapes=[pltpu.VMEM((ts + CONV_HALO, CONV_DIM), jnp.float32)],
        compiler_params=pltpu.CompilerParams(dimension_semantics=("parallel", "parallel")),
        name="conformer_conv",
    )(z, z, wpad, dw_b.reshape(1, -1), ln_g.reshape(1, -1), ln_b.reshape(1, -1))


def _msoftmax(s, mask):
    s = jnp.where(mask, s, NEG)
    m = jnp.max(s, axis=-1, keepdims=True)
    e = jnp.where(mask, jnp.exp(s - m), 0.0)
    l = jnp.sum(e, axis=-1, keepdims=True)
    return e / jnp.where(l > 0.0, l, 1.0)


def _nsa_kernel(q_ref, g_ref, kct_ref, vc_ref, kst_ref, vs_ref, kwt_ref, vw_ref, c2st_ref, o_ref, *, n_sel):
    G = NSA_GROUP
    qb = pl.program_id(2)
    t0 = pl.multiple_of(qb * TQ, TQ)
    q = q_ref[...]
    lane_g = lax.broadcasted_iota(jnp.int32, (TQ, GW), 1) // HEAD_DIM
    q4 = jnp.concatenate([jnp.where(lane_g == h, q, jnp.zeros_like(q)) for h in range(G)], axis=0)
    tpos = t0 + lax.broadcasted_iota(jnp.int32, (TQ, 1), 0)

    def heads(x):
        return x.reshape(G, TQ, x.shape[-1])

    def attend(p, v):
        return heads(jnp.dot(p.reshape(G * TQ, p.shape[-1]).astype(jnp.bfloat16), v,
                             preferred_element_type=jnp.float32))

    starts = [t0 - WINDOW + TQ * c for c in range(WINDOW // TQ + 1)]
    reads = [pl.multiple_of(jnp.maximum(s, 0), TQ) for s in starts]
    kwt = jnp.concatenate([kwt_ref[0, 0, :, pl.ds(r, TQ)] for r in reads], axis=1)
    vw = jnp.concatenate([vw_ref[0, 0, pl.ds(r, TQ), :] for r in reads], axis=0)
    wk = WINDOW + TQ
    kposw = t0 - WINDOW + lax.broadcasted_iota(jnp.int32, (TQ, wk), 1)
    diff = tpos - kposw
    maskw = (kposw >= 0) & (diff >= 0) & (diff < WINDOW)

    kct = kct_ref[0, 0]
    ncp = kct.shape[1]
    ncol = lax.broadcasted_iota(jnp.int32, (TQ, ncp), 1)
    maskc = (ncol * CMP_STRIDE + (CMP_BLOCK - 1)) <= tpos
    sc = heads(jnp.dot(q4, kct, preferred_element_type=jnp.float32))
    sw = heads(jnp.dot(q4, kwt, preferred_element_type=jnp.float32))
    pc = _msoftmax(sc, maskc[None])
    oc = attend(pc, vc_ref[0, 0])
    pw = _msoftmax(sw, maskw[None])
    ow = attend(pw, vw)
    pc_sum = jnp.sum(pc, axis=0)

    c2st = c2st_ref[...]
    jp = c2st.shape[0]
    hi = pc_sum.astype(jnp.bfloat16)
    lo = (pc_sum - hi.astype(jnp.float32)).astype(jnp.bfloat16)
    dn = (((1,), (1,)), ((), ()))
    imp_t = (lax.dot_general(c2st, hi, dn, preferred_element_type=jnp.float32)
             + lax.dot_general(c2st, lo, dn, preferred_element_type=jnp.float32))
    jrow = lax.broadcasted_iota(jnp.int32, (jp, TQ), 0)
    cur = (t0 + lax.broadcasted_iota(jnp.int32, (jp, TQ), 1)) // SLC_BLOCK
    forced = (jrow == 0) | (jrow == cur) | (jrow == cur - 1)
    score = jnp.where(jrow <= cur, imp_t + jnp.where(forced, FORCE_BONUS, 0.0), NEG)
    nblk = kst_ref.shape[3] // SLC_BLOCK
    nrow = -(-nblk // 8) * 8
    sc, jr = score[:nrow], jrow[:nrow]
    rank = jnp.zeros((nrow, TQ), jnp.int32)
    for i in range(nblk):
        si = sc[i:i + 1, :]
        rank = rank + ((si > sc) | ((si == sc) & (jr > i))).astype(jnp.int32)
    sel = ((rank < n_sel) & (sc > 0.5 * NEG)).astype(jnp.float32)
    if nrow < jp:
        sel = jnp.concatenate([sel, jnp.zeros((jp - nrow, TQ), jnp.float32)], axis=0)
    selb = sel.T.astype(jnp.bfloat16)

    nchunks = (t0 + TQ + KC - 1) // KC

    def chunk(c, carry):
        m_old, l_old, acc = carry
        k0 = pl.multiple_of(c * KC, KC)
        kt = kst_ref[0, 0, :, pl.ds(k0, KC)]
        v = vs_ref[0, 0, pl.ds(k0, KC), :]
        kblk = (k0 + lax.broadcasted_iota(jnp.int32, (jp, KC), 1)) // SLC_BLOCK
        expand = (kblk == lax.broadcasted_iota(jnp.int32, (jp, KC), 0)).astype(jnp.bfloat16)
        member = jnp.dot(selb, expand, preferred_element_type=jnp.float32) > 0.5
        kpos = k0 + lax.broadcasted_iota(jnp.int32, (TQ, KC), 1)
        msk = (member & (kpos <= tpos))[None]
        s = jnp.where(msk, heads(jnp.dot(q4, kt, preferred_element_type=jnp.float32)), NEG)
        m_new = jnp.maximum(m_old, jnp.max(s, axis=-1, keepdims=True))
        alpha = jnp.exp(m_old - m_new)
        p = jnp.exp(s - m_new)
        l_new = alpha * l_old + jnp.sum(p, axis=-1, keepdims=True)
        return m_new, l_new, alpha * acc + attend(p, v)

    def chunk_pair(i, carry):
        return chunk(2 * i + 1, chunk(2 * i, carry))

    init = (jnp.full((G, TQ, 1), NEG, jnp.float32), jnp.zeros((G, TQ, 1), jnp.float32),
            jnp.zeros((G, TQ, GW), jnp.float32))
    _, l_s, acc_s = lax.fori_loop(0, (nchunks + 1) // 2, chunk_pair, init)
    os_ = acc_s / jnp.where(l_s > 0.0, l_s, 1.0)

    sig = jax.nn.sigmoid(g_ref[...])
    out = jnp.zeros((TQ, GW), jnp.float32)
    for h in range(G):
        o_h = (sig[:, 3 * h:3 * h + 1] * oc[h] + sig[:, 3 * h + 1:3 * h + 2] * os_[h]
               + sig[:, 3 * h + 2:3 * h + 3] * ow[h])
        out = out + jnp.where(lane_g == h, o_h, 0.0)
    o_ref[...] = out.astype(o_ref.dtype)


def nsa_core(q, z, kct, vc, kst, vs, kwt, vw, B, S):
    n = B * S
    ncp = kct.shape[3]
    ncmp = (S - CMP_BLOCK) // CMP_STRIDE + 1
    nslc = S // SLC_BLOCK
    n_sel = min(SLC_TOPK, nslc)
    jp = LANE
    assert nslc <= jp and S % (2 * KC) == 0 and S % TQ == 0 and WINDOW % TQ == 0
    cs = np.arange(ncmp) * CMP_STRIDE
    ss = np.arange(nslc) * SLC_BLOCK
    ov = np.minimum(cs[:, None] + CMP_BLOCK, ss[None, :] + SLC_BLOCK) - np.maximum(cs[:, None], ss[None, :])
    c2s = np.zeros((ncp, jp), np.float32)
    c2s[:ncmp, :nslc] = np.clip(ov, 0, None) / CMP_BLOCK
    c2st = jnp.asarray(c2s.T, dtype=jnp.bfloat16)
    nqb = S // TQ

    def kv_spec(shp):
        return pl.BlockSpec((1, 1) + shp, lambda b, k, i: (b, k, 0, 0))

    return pl.pallas_call(
        functools.partial(_nsa_kernel, n_sel=n_sel),
        grid=(B, NSA_KV_HEADS, nqb),
        in_specs=[pl.BlockSpec((TQ, GW), lambda b, k, i: (b * nqb + i, k)),
                  pl.BlockSpec((TQ, LANE), lambda b, k, i: (b * nqb + i, Z_NG // LANE + k)),
                  kv_spec((GW, ncp)), kv_spec((ncp, GW)),
                  kv_spec((GW, S)), kv_spec((S, GW)),
                  kv_spec((GW, S)), kv_spec((S, GW)),
                  pl.BlockSpec((jp, ncp), lambda b, k, i: (0, 0))],
        out_specs=pl.BlockSpec((TQ, GW), lambda b, k, i: (b * nqb + i, k)),
        out_shape=jax.ShapeDtypeStruct((n, NSA_KV_HEADS * GW), jnp.bfloat16),
        compiler_params=pltpu.CompilerParams(dimension_semantics=("parallel", "parallel", "arbitrary"),
                                             vmem_limit_bytes=VMEM_LIMIT),
        name="nsa_core",
    )(q, z, kct, vc, kst, vs, kwt, vw, c2st)


def _gla_consts():
    C = GLA_CHUNK
    i = np.arange(C)
    mats = [np.tril(np.ones((C, C), np.float32))]
    for s in GLA_LEVELS:
        r = (i // (2 * s)) * 2 * s + s - 1
        upper = (i // s) % 2 == 1
        m = np.arange(C)[None, :]
        mq = ((m > r[:, None]) & (m <= i[:, None]) & upper[:, None]).astype(np.float32)
        mk = ((m > i[:, None]) & (m <= r[:, None]) & (~upper)[:, None]).astype(np.float32)
        mats += [mq, mk]
    return np.concatenate(mats, axis=0)


def _split3(x):
    h = x.astype(jnp.bfloat16)
    r = x - h.astype(jnp.float32)
    m = r.astype(jnp.bfloat16)
    l = (r - m.astype(jnp.float32)).astype(jnp.bfloat16)
    return h, m, l


def _gla_kernel(q_ref, k_ref, v_ref, r_ref, glr_ref, gw_ref, gb_ref, ng_ref, msel_ref, o_ref, state_ref):
    C = GLA_CHUNK

    @pl.when(pl.program_id(1) == 0)
    def _():
        state_ref[...] = jnp.zeros_like(state_ref)

    ii = lax.broadcasted_iota(jnp.int32, (C, C), 0)
    jj = lax.broadcasted_iota(jnp.int32, (C, C), 1)
    lane_h = lax.broadcasted_iota(jnp.int32, (C, GLA_QK), 1) // GLA_DK
    bd = (lax.broadcasted_iota(jnp.int32, (GLA_QK, GLA_V), 0) // GLA_DK
          == lax.broadcasted_iota(jnp.int32, (GLA_QK, GLA_V), 1) // GLA_DV)
    msel = msel_ref[...]
    gw = gw_ref[...].astype(jnp.bfloat16)
    dn = (((1,), (1,)), ((), ()))

    def chunk(c, carry):
        r0 = pl.multiple_of(c * C, C)
        q = q_ref[pl.ds(r0, C), :] * (GLA_DK ** -0.5)
        k = k_ref[pl.ds(r0, C), :]
        v = v_ref[pl.ds(r0, C), :].astype(jnp.bfloat16)
        z = jnp.dot(glr_ref[pl.ds(r0, C), :].astype(jnp.bfloat16), gw, preferred_element_type=jnp.float32) + gb_ref[...]
        a = jax.nn.log_sigmoid(z) / GLA_GATE_TEMP
        ah, am, al = _split3(a)
        ex = (jnp.dot(msel, ah, preferred_element_type=jnp.float32)
              + jnp.dot(msel, am, preferred_element_type=jnp.float32)
              + jnp.dot(msel, al, preferred_element_type=jnp.float32))
        bcum = ex[0:C]
        state = state_ref[...]
        o = jnp.dot((q * jnp.exp(bcum)).astype(jnp.bfloat16), state.astype(jnp.bfloat16),
                    preferred_element_type=jnp.float32)
        kb = k.astype(jnp.bfloat16)
        for h in range(GLA_HEADS):
            hm = lane_h == h
            att = jnp.where(ii == jj, lax.dot_general(jnp.where(hm, q, 0.0).astype(jnp.bfloat16), kb, dn,
                                                      preferred_element_type=jnp.float32), 0.0)
            for li, s in enumerate(GLA_LEVELS):
                eq = jnp.exp(ex[(1 + 2 * li) * C:(2 + 2 * li) * C])
                ek = jnp.exp(ex[(2 + 2 * li) * C:(3 + 2 * li) * C])
                qs = jnp.where(hm, q * eq, 0.0).astype(jnp.bfloat16)
                ks = (k * ek).astype(jnp.bfloat16)
                blk = (ii // (2 * s) == jj // (2 * s)) & ((ii // s) % 2 == 1) & ((jj // s) % 2 == 0)
                att = att + jnp.where(blk, lax.dot_general(qs, ks, dn, preferred_element_type=jnp.float32), 0.0)
            oh = jnp.dot(att.astype(jnp.bfloat16), v, preferred_element_type=jnp.float32)
            o = o + jnp.where(lax.broadcasted_iota(jnp.int32, (C, GLA_V), 1) // GLA_DV == h, oh, 0.0)
        blast = bcum[C - 1:C, :]
        kd_t = (k * jnp.exp(blast - bcum)).T.astype(jnp.bfloat16)
        decay_col = jnp.exp(jnp.sum(a.T, axis=1, keepdims=True))
        upd = jnp.dot(kd_t, v, preferred_element_type=jnp.float32)
        state_ref[...] = state * decay_col + jnp.where(bd, upd, 0.0)
        outs = []
        for h in range(GLA_HEADS):
            oh = o[:, h * GLA_DV:(h + 1) * GLA_DV]
            outs.append(oh * lax.rsqrt(jnp.mean(oh * oh, axis=1, keepdims=True) + EPS) * ng_ref[...])
        on = jnp.concatenate(outs, axis=1)
        r = r_ref[pl.ds(r0, C), :]
        o_ref[pl.ds(r0, C), :] = (on * (r * jax.nn.sigmoid(r))).astype(o_ref.dtype)
        return carry

    lax.fori_loop(0, q_ref.shape[0] // C, chunk, 0)


def gla_pallas(z, gate_w, gate_b, norm_g, B, S):
    n = B * S
    tb = GLA_TB
    nb = S // tb
    gw = jnp.pad(gate_w, ((0, LANE - gate_w.shape[0]), (0, 0)))
    msel = jnp.asarray(_gla_consts(), dtype=jnp.bfloat16)
    zb = lambda col, w: pl.BlockSpec((tb, w), lambda b, i: (b * nb + i, col // w))
    full = lambda shp: pl.BlockSpec(shp, lambda b, i: (0,) * len(shp))
    return pl.pallas_call(
        _gla_kernel,
        grid=(B, nb),
        in_specs=[zb(Z_GQ, GLA_QK), zb(Z_GK, GLA_QK), zb(Z_GV, GLA_V), zb(Z_GR, GLA_V), zb(Z_GLR, LANE),
                  full((LANE, GLA_QK)), full((1, GLA_QK)), full((1, GLA_DV)), full(msel.shape)],
        out_specs=pl.BlockSpec((tb, GLA_V), lambda b, i: (b * nb + i, 0)),
        out_shape=jax.ShapeDtypeStruct((n, GLA_V), jnp.bfloat16),
        scratch_shapes=[pltpu.VMEM((GLA_QK, GLA_V), jnp.float32)],
        compiler_params=pltpu.CompilerParams(dimension_semantics=("parallel", "arbitrary")),
        name="gla",
    )(z, z, z, z, z, gw, gate_b.reshape(1, -1), norm_g.reshape(1, -1), msel)


_FULL_BLOCKS = 8


def _extract_top(s, k, tie_break):
    n, t = s.shape
    row = lax.broadcasted_iota(jnp.int32, (n, t), 0)
    rank = jnp.full((n, t), k, jnp.int32)
    vals = []
    for r in range(k):
        m = jnp.max(s, axis=0, keepdims=True)
        hit = s == m
        if tie_break:
            hit = row == jnp.min(jnp.where(hit, row, n), axis=0, keepdims=True)
        rank = jnp.where(hit, r, rank)
        s = jnp.where(hit, float('-inf'), s)
        vals.append(m)
    return jnp.concatenate(vals, axis=0), rank


def _peer_gate_math(s1, s2, tie_break):
    K = PEER_TOPK
    t = s1.shape[1]
    v1, rank1 = _extract_top(s1, K, tie_break)
    v2, rank2 = _extract_top(s2, K, tie_break)
    blocks, poss = [], []
    row16 = lax.broadcasted_iota(jnp.int32, (K, t), 0)
    for r1 in range(_FULL_BLOCKS):
        c = v1[r1:r1 + 1, :] + v2
        blocks.append(jnp.where(row16 < K // (r1 + 1), c, float('-inf')))
        poss.append(row16 + r1 * K)
    row8 = lax.broadcasted_iota(jnp.int32, (K - _FULL_BLOCKS, t), 0)
    blocks.append(v1[_FULL_BLOCKS:, :] + v2[0:1, :])
    poss.append((row8 + _FULL_BLOCKS) * K)
    sels = [jnp.zeros(b.shape, jnp.bool_) for b in blocks]
    big = K * K
    for _ in range(K):
        m = blocks[0].max(axis=0, keepdims=True)
        for b in blocks[1:]:
            m = jnp.maximum(m, b.max(axis=0, keepdims=True))
        if tie_break:
            first = None
            for b, p in zip(blocks, poss):
                f = jnp.min(jnp.where(b == m, p, big), axis=0, keepdims=True)
                first = f if first is None else jnp.minimum(first, f)
            hits = [p == first for p in poss]
        else:
            hits = [b == m for b in blocks]
        for i, hit in enumerate(hits):
            sels[i] = sels[i] | hit
            blocks[i] = jnp.where(hit, float('-inf'), blocks[i])
    top = v1[0:1, :] + v2[0:1, :]
    cnt_rows = []
    z = jnp.zeros((1, t), jnp.float32)
    for r1 in range(_FULL_BLOCKS):
        sel = sels[r1]
        cnt_rows.append(jnp.sum(sel.astype(jnp.float32), axis=0, keepdims=True))
        c = v1[r1:r1 + 1, :] + v2
        z = z + jnp.sum(jnp.where(sel, jnp.exp(c - top), 0.0), axis=0, keepdims=True)
    sel = sels[_FULL_BLOCKS]
    cnt_tail = sel.astype(jnp.float32)
    c = v1[_FULL_BLOCKS:, :] + v2[0:1, :]
    z = z + jnp.sum(jnp.where(sel, jnp.exp(c - top), 0.0), axis=0, keepdims=True)
    cnt = jnp.concatenate(cnt_rows + [cnt_tail], axis=0)
    a = jnp.zeros(s1.shape, jnp.float32)
    for r in range(K):
        a = jnp.where(rank1 == r, cnt[r:r + 1, :], a)
    taken = (jnp.sum((rank1 < K).astype(jnp.float32), axis=0, keepdims=True),
             jnp.sum((rank2 < K).astype(jnp.float32), axis=0, keepdims=True),
             jnp.sum(cnt, axis=0, keepdims=True))
    return jnp.exp(s1 - v1[0:1, :]), a, rank2.astype(jnp.float32), jnp.exp(s2 - v2[0:1, :]) / z, taken


def _peer_gate_kernel(q_ref, keys_ref, e1_ref, a_ref, r2_ref, e2_ref):
    q = q_ref[...].astype(jnp.bfloat16)
    k1 = keys_ref[0, 0].astype(jnp.bfloat16)
    k2 = keys_ref[0, 1].astype(jnp.bfloat16)
    dn = (((1,), (1,)), ((), ()))
    s1 = lax.dot_general(k1, q[:, :PEER_DKEY // 2], dn, preferred_element_type=jnp.float32)
    s2 = lax.dot_general(k2, q[:, PEER_DKEY // 2:], dn, preferred_element_type=jnp.float32)

    def write(e1, a, r2, e2):
        e1_ref[0] = e1
        a_ref[0] = a
        r2_ref[0] = r2.astype(jnp.bfloat16)
        e2_ref[0] = e2.astype(jnp.bfloat16)

    e1, a, r2, e2, taken = _peer_gate_math(s1, s2, tie_break=False)
    write(e1, a, r2, e2)
    excess = sum(jnp.max(jnp.abs(c - PEER_TOPK)) for c in taken)

    @pl.when(excess > 0.0)
    def _():
        write(*_peer_gate_math(s1, s2, tie_break=True)[:4])


def peer_gates(qh, keys, tt=512):
    n = qh.shape[0]
    H = PEER_HEADS
    shp32 = jax.ShapeDtypeStruct((H, PEER_NKEYS, n), jnp.float32)
    shp16 = jax.ShapeDtypeStruct((H, PEER_NKEYS, n), jnp.bfloat16)
    ospec = pl.BlockSpec((1, PEER_NKEYS, tt), lambda i, h: (h, 0, i))
    return pl.pallas_call(
        _peer_gate_kernel,
        grid=(n // tt, H),
        in_specs=[pl.BlockSpec((tt, PEER_DKEY), lambda i, h: (i, h)),
                  pl.BlockSpec((1, 2, PEER_NKEYS, PEER_DKEY // 2), lambda i, h: (h, 0, 0, 0))],
        out_specs=[ospec, ospec, ospec, ospec],
        out_shape=[shp32, shp32, shp16, shp16],
        compiler_params=pltpu.CompilerParams(dimension_semantics=("parallel", "parallel")),
        name="peer_gates",
    )(qh, keys)


def _row_bf16(row, n):
    t = row.shape[1]
    tile = jnp.broadcast_to(row, (16, t)).astype(jnp.bfloat16)
    return jnp.concatenate([tile] * (n // 16), axis=0)


def _peer_dense_kernel(xt_ref, u_ref, vt_ref, e1_ref, a_ref, r2_ref, e2_ref, x1_ref, gate_ref, fg_ref,
                       o_ref, acc_ref, *, ec, final):
    j = pl.program_id(1)
    nsub = ec // PEER_NKEYS

    @pl.when(j == 0)
    def _():
        acc_ref[...] = jnp.zeros_like(acc_ref)

    assert nsub == 8
    r8 = pl.multiple_of(j * nsub, 8)
    a_tiles = [a_ref[h, pl.ds(r8, 8), :] for h in range(PEER_HEADS)]
    e1_tiles = [e1_ref[h, pl.ds(r8, 8), :] for h in range(PEER_HEADS)]

    def gate_mat(i):
        g = None
        for h in range(PEER_HEADS):
            cnt = _row_bf16(a_tiles[h][i:i + 1, :], PEER_NKEYS)
            e1 = _row_bf16(e1_tiles[h][i:i + 1, :], PEER_NKEYS)
            term = jnp.where(r2_ref[h] < cnt, e2_ref[h], jnp.bfloat16(0)) * e1
            g = term if g is None else g + term
        return g

    def scores(i):
        return jnp.dot(u_ref[i * PEER_NKEYS:(i + 1) * PEER_NKEYS, :], xt_ref[...], preferred_element_type=jnp.float32)

    st = scores(0)
    ws = []
    for i in range(nsub):
        st_next = scores(i + 1) if i + 1 < nsub else None
        ws.append(_gelu(st.astype(jnp.bfloat16)) * gate_mat(i))
        if i % PEER_GROUPS_PER_DOT == PEER_GROUPS_PER_DOT - 1:
            w = jnp.concatenate(ws, axis=0)
            ws = []
            acc_ref[...] += jnp.dot(vt_ref[:, (i + 1 - PEER_GROUPS_PER_DOT) * PEER_NKEYS:(i + 1) * PEER_NKEYS], w,
                                    preferred_element_type=jnp.float32)
        st = st_next

    @pl.when(j == pl.num_programs(1) - 1)
    def _():
        x2 = x1_ref[...] + gate_ref[0] * acc_ref[...].T
        if final:
            x2 = x2 * lax.rsqrt(jnp.mean(x2 * x2, axis=1, keepdims=True) + EPS) * fg_ref[...]
        o_ref[...] = x2


def peer_dense(xt, u16, vt16, e1, a, r2, e2, x1, gate2, final_g, S, final, tt=512, ec=1024):
    d, n = xt.shape
    ne = u16.shape[0]
    H = PEER_HEADS
    per_b = S // tt
    gspec = lambda: pl.BlockSpec((H, PEER_NKEYS, tt), lambda i, j: (0, 0, i))
    return pl.pallas_call(
        functools.partial(_peer_dense_kernel, ec=ec, final=final),
        grid=(n // tt, ne // ec),
        in_specs=[pl.BlockSpec((d, tt), lambda i, j: (0, i)),
                  pl.BlockSpec((ec, d), lambda i, j: (j, 0)),
                  pl.BlockSpec((d, ec), lambda i, j: (0, j)),
                  gspec(), gspec(), gspec(), gspec(),
                  pl.BlockSpec((tt, d), lambda i, j: (i, 0)),
                  pl.BlockSpec((1, 1, d), lambda i, j: (i // per_b, 0, 0)),
                  pl.BlockSpec((1, d), lambda i, j: (0, 0))],
        out_specs=pl.BlockSpec((tt, d), lambda i, j: (i, 0)),
        out_shape=jax.ShapeDtypeStruct((n, d), jnp.float32),
        scratch_shapes=[pltpu.VMEM((d, tt), jnp.float32)],
        compiler_params=pltpu.CompilerParams(dimension_semantics=("parallel", "arbitrary"),
                                             vmem_limit_bytes=VMEM_LIMIT),
        name="peer_dense",
    )(xt, u16, vt16, e1, a, r2, e2, x1, gate2, final_g.reshape(1, d))


def adaln(c, w, b):
    mod = pmm(jax.nn.silu(c), w) + b
    shift, scale, gate = jnp.split(mod[:, None, :], 3, axis=-1)
    return shift, 1.0 + scale, gate


def kernel(x, c, positions, ada_w, ada_b, norm_g, w_in, conv_dw_w, conv_dw_b, conv_ln_g, conv_ln_b, w_conv_up,
           cmp_pos, cmp_w1, cmp_w2, w_nsa_up, gla_gate_w, gla_gate_b, gla_norm_g, w_gla_up, w_out,
           peer_wq, peer_keys, peer_u, peer_v, final_g):
    B, S, D = x.shape
    bf = jnp.bfloat16
    x2 = x.reshape(B * S, D)
    cs = rope_tables(positions)
    for l in range(DEPTH):
        shift, scale1p, gate = adaln(c, ada_w[l, 0], ada_b[l, 0])
        z = in_proj(x2, norm_g[l, 0], scale1p, shift, pack_w_in(w_in[l]), S)
        hc = conv_pallas(z, conv_dw_w[l], conv_dw_b[l], conv_ln_g[l], conv_ln_b[l], B, S)
        q, kst, vs, kwt, vw = nsa_prep(z, cs, B, S)
        kct, vc = nsa_compress(z, positions, cmp_pos[l], cmp_w1[l], cmp_w2[l], B, S)
        on = nsa_core(q, z, kct, vc, kst, vs, kwt, vw, B, S)
        og = gla_pallas(z, gla_gate_w[l], gla_gate_b[l], gla_norm_g[l], B, S)
        shift2, scale2p, gate2 = adaln(c, ada_w[l, 1], ada_b[l, 1])
        x1, ut, qh = mix_out(hc, on, og, z, x2, gate, w_conv_up[l].astype(bf), w_nsa_up[l].astype(bf),
                             w_gla_up[l].astype(bf), w_out[l].astype(bf), norm_g[l, 1], scale2p, shift2,
                             peer_wq[l].astype(bf), S)
        e1, a, r2, e2 = peer_gates(qh, peer_keys[l])
        x2 = peer_dense(ut, peer_u[l].astype(bf), peer_v[l].astype(bf).T, e1, a, r2, e2, x1, gate2, final_g,
                        S, final=(l == DEPTH - 1))
    return x2.reshape(B, S, D)
```

```python
import functools

import jax
import jax.numpy as jnp
import numpy as np
from jax import lax
from jax.experimental import pallas as pl
from jax.experimental.pallas import tpu as pltpu

D_MODEL = 1024
DEPTH = 2
EPS = 1e-6
NEG = -1e30
N_BRANCH = 3
CONV_DIM = 512
CONV_WIDTH = 31
NSA_HEADS = 8
NSA_KV_HEADS = 2
NSA_GROUP = NSA_HEADS // NSA_KV_HEADS
HEAD_DIM = 64
CMP_BLOCK = 32
CMP_STRIDE = 16
CMP_HIDDEN = 256
SLC_BLOCK = 64
SLC_TOPK = 16
WINDOW = 512
FORCE_BONUS = 1e4
ROPE_THETA = 10000.0
GLA_HEADS = 4
GLA_DK = 64
GLA_DV = 128
GLA_GATE_RANK = 16
GLA_GATE_TEMP = 16.0
GLA_CHUNK = 256
PEER_HEADS = 8
PEER_NKEYS = 128
PEER_DKEY = 256
PEER_TOPK = 16
NSA_Q = NSA_HEADS * HEAD_DIM
NSA_KV = 3 * 2 * NSA_KV_HEADS * HEAD_DIM
NSA_G = 3 * NSA_HEADS
GLA_QK = GLA_HEADS * GLA_DK
GLA_V = GLA_HEADS * GLA_DV
SPLITS = [2 * CONV_DIM, NSA_Q, NSA_KV, NSA_G, GLA_QK, GLA_QK, GLA_V, GLA_V, GLA_GATE_RANK, N_BRANCH * D_MODEL]

LANE = 128
TQ = 128
KC = 512
GW = NSA_GROUP * HEAD_DIM
VMEM_LIMIT = 48 * 1024 * 1024
GLA_LEVELS = tuple(GLA_CHUNK >> s for s in range(1, GLA_CHUNK.bit_length()))
GLA_TB = 512
CONV_TS = 512
CONV_HALO = 32
ROW_TILE = 1024
MIX_TILE = 256
PEER_GROUPS_PER_DOT = 2
PEER_GROUPS_PER_SCORE = 4

Z_MG = 0
Z_CONV = Z_MG + N_BRANCH * D_MODEL
Z_Q = Z_CONV + 2 * CONV_DIM
Z_QSW = Z_Q + NSA_Q
Z_GV = Z_QSW + NSA_Q
Z_GR = Z_GV + GLA_V
Z_GQ = Z_GR + GLA_V
Z_GK = Z_GQ + GLA_QK
Z_CMP = Z_GK + GLA_QK
Z_SLC = Z_CMP + 256
Z_WIN = Z_SLC + 384
Z_NG = Z_WIN + 384
Z_GLR = Z_NG + NSA_KV_HEADS * LANE
NZ = Z_GLR + LANE
assert Z_SLC % 384 == 0 and Z_WIN % 384 == 0 and Z_CONV % (2 * CONV_DIM) == 0 and Z_GQ % GLA_QK == 0


def _mm_kernel(a_ref, b_ref, o_ref):
    o_ref[...] = jnp.dot(a_ref[...].astype(jnp.bfloat16), b_ref[...].astype(jnp.bfloat16),
                         preferred_element_type=jnp.float32)


def _pick(n, cands):
    for c in cands:
        if n % c == 0:
            return c
    return n


def pmm(a, b):
    m, k = a.shape
    n = b.shape[1]
    n_pad = -n % LANE
    if n_pad:
        b = jnp.pad(b, ((0, 0), (0, n_pad)))
    m_pad = -m % 8
    if m_pad:
        a = jnp.pad(a, ((0, m_pad), (0, 0)))
    mp, np_ = m + m_pad, n + n_pad
    tm = _pick(mp, (512, 256, 128, 64, 32, 16, 8))
    tn = _pick(np_, (512, 640, 384, 256, 128))
    out = pl.pallas_call(
        _mm_kernel,
        grid=(mp // tm, np_ // tn),
        in_specs=[pl.BlockSpec((tm, k), lambda i, j: (i, 0)),
                  pl.BlockSpec((k, tn), lambda i, j: (0, j))],
        out_specs=pl.BlockSpec((tm, tn), lambda i, j: (i, j)),
        out_shape=jax.ShapeDtypeStruct((mp, np_), jnp.float32),
        compiler_params=pltpu.CompilerParams(dimension_semantics=("parallel", "parallel")),
        name="pmm",
    )(a, b)
    return out[:m, :n]


def _in_proj_kernel(x_ref, g_ref, sc_ref, sh_ref, w_ref, z_ref, u_ref):
    @pl.when(pl.program_id(1) == 0)
    def _():
        x = x_ref[...]
        y = x * lax.rsqrt(jnp.mean(x * x, axis=1, keepdims=True) + EPS) * g_ref[...]
        u_ref[...] = (y * sc_ref[0] + sh_ref[0]).astype(jnp.bfloat16)

    z_ref[...] = jnp.dot(u_ref[...], w_ref[...], preferred_element_type=jnp.float32)


def in_proj(x2, g, scale1p, shift, w16, S):
    n, d = x2.shape
    nz = w16.shape[1]
    tm = ROW_TILE
    tn = _pick(nz, (1152, 1024, 896, 512, 384, 256, 128))
    per_b = S // tm
    mod = lambda: pl.BlockSpec((1, 1, d), lambda i, j: (i // per_b, 0, 0))
    return pl.pallas_call(
        _in_proj_kernel,
        grid=(n // tm, nz // tn),
        in_specs=[pl.BlockSpec((tm, d), lambda i, j: (i, 0)),
                  pl.BlockSpec((1, d), lambda i, j: (0, 0)), mod(), mod(),
                  pl.BlockSpec((d, tn), lambda i, j: (0, j))],
        out_specs=pl.BlockSpec((tm, tn), lambda i, j: (i, j)),
        out_shape=jax.ShapeDtypeStruct((n, nz), jnp.float32),
        scratch_shapes=[pltpu.VMEM((tm, d), jnp.bfloat16)],
        compiler_params=pltpu.CompilerParams(dimension_semantics=("parallel", "arbitrary"),
                                             vmem_limit_bytes=VMEM_LIMIT),
        name="in_proj",
    )(x2, g.reshape(1, d), scale1p, shift, w16)


def _half_swap(n_heads):
    idx = np.arange(n_heads * HEAD_DIM).reshape(n_heads, 2, HEAD_DIM // 2)
    return idx[:, ::-1, :].reshape(-1)


def pack_w_in(w_in):
    o = np.cumsum([0] + SPLITS)
    a_conv, nq, nkv, ng, gq, gk, gv, gr, glr, mg = [w_in[:, o[i]:o[i + 1]] for i in range(len(SPLITS))]
    kvw = 2 * NSA_KV_HEADS * HEAD_DIM
    kw = NSA_KV_HEADS * HEAD_DIM
    d = w_in.shape[0]
    sw = _half_swap(NSA_KV_HEADS)
    cols = [mg, a_conv, nq, nq[:, _half_swap(NSA_HEADS)], gv, gr, gq, gk, nkv[:, 0:kvw]]
    for br in (1, 2):
        k = nkv[:, br * kvw:br * kvw + kw]
        cols += [k, k[:, sw], nkv[:, br * kvw + kw:(br + 1) * kvw]]
    gpk = NSA_GROUP * 3
    for kvh in range(NSA_KV_HEADS):
        cols += [ng[:, kvh * gpk:(kvh + 1) * gpk], jnp.zeros((d, LANE - gpk), w_in.dtype)]
    cols += [glr, jnp.zeros((d, LANE - GLA_GATE_RANK), w_in.dtype)]
    w = jnp.concatenate(cols, axis=1)
    assert w.shape[1] == NZ
    return w.astype(jnp.bfloat16)


def rope_tables(positions):
    half = HEAD_DIM // 2
    freq = ROPE_THETA ** (-jnp.arange(half, dtype=jnp.float32) / half)
    ang = positions.astype(jnp.float32)[..., None] * freq
    cos, sin = jnp.cos(ang), jnp.sin(ang)
    c = jnp.tile(jnp.concatenate([cos, cos], -1), (1, 1, LANE // HEAD_DIM))
    s = jnp.tile(jnp.concatenate([-sin, sin], -1), (1, 1, LANE // HEAD_DIM))
    return jnp.concatenate([c, s], -1).reshape(-1, 2 * LANE)


def _nsa_prep_kernel(q_ref, qs_ref, slc_ref, win_ref, cs_ref, sel_ref, qo_ref, kst_ref, vs_ref, kwt_ref, vw_ref):
    cos = cs_ref[:, :LANE]
    sin = cs_ref[:, LANE:]
    scale = HEAD_DIM ** -0.5
    for t in range(NSA_Q // LANE):
        sl = slice(t * LANE, (t + 1) * LANE)
        qo_ref[:, sl] = ((q_ref[:, sl] * cos + qs_ref[:, sl] * sin) * scale).astype(jnp.bfloat16)
    dn = (((1,), (1,)), ((), ()))
    for src, kt_ref, v_ref in ((slc_ref, kst_ref, vs_ref), (win_ref, kwt_ref, vw_ref)):
        k = (src[:, 0:LANE] * cos + src[:, LANE:2 * LANE] * sin).astype(jnp.bfloat16)
        v = src[:, 2 * LANE:3 * LANE].astype(jnp.bfloat16)
        for kvh in range(NSA_KV_HEADS):
            sel = sel_ref[kvh]
            kt_ref[0, kvh] = lax.dot_general(sel, k, dn, preferred_element_type=jnp.float32).astype(jnp.bfloat16)
            v_ref[0, kvh] = lax.dot_general(v, sel, dn, preferred_element_type=jnp.float32).astype(jnp.bfloat16)


def _head_repeat_sel():
    sel = np.zeros((NSA_KV_HEADS, GW, LANE), np.float32)
    for kvh in range(NSA_KV_HEADS):
        for r in range(GW):
            sel[kvh, r, kvh * HEAD_DIM + r % HEAD_DIM] = 1.0
    return jnp.asarray(sel, dtype=jnp.bfloat16)


def nsa_prep(z, cs, B, S):
    n = B * S
    tp = ROW_TILE
    nb = S // tp
    zb = lambda col, w: pl.BlockSpec((tp, w), lambda b, i: (b * nb + i, col // w))
    kt = lambda: pl.BlockSpec((1, NSA_KV_HEADS, GW, tp), lambda b, i: (b, 0, 0, i))
    vv = lambda: pl.BlockSpec((1, NSA_KV_HEADS, tp, GW), lambda b, i: (b, 0, i, 0))
    kt_shape = jax.ShapeDtypeStruct((B, NSA_KV_HEADS, GW, S), jnp.bfloat16)
    v_shape = jax.ShapeDtypeStruct((B, NSA_KV_HEADS, S, GW), jnp.bfloat16)
    return pl.pallas_call(
        _nsa_prep_kernel,
        grid=(B, nb),
        in_specs=[zb(Z_Q, NSA_Q), zb(Z_QSW, NSA_Q), zb(Z_SLC, 384), zb(Z_WIN, 384),
                  pl.BlockSpec((tp, 2 * LANE), lambda b, i: (b * nb + i, 0)),
                  pl.BlockSpec((NSA_KV_HEADS, GW, LANE), lambda b, i: (0, 0, 0))],
        out_specs=[pl.BlockSpec((tp, NSA_Q), lambda b, i: (b * nb + i, 0)), kt(), vv(), kt(), vv()],
        out_shape=[jax.ShapeDtypeStruct((n, NSA_Q), jnp.bfloat16), kt_shape, v_shape, kt_shape, v_shape],
        compiler_params=pltpu.CompilerParams(dimension_semantics=("parallel", "parallel")),
        name="nsa_prep",
    )(z, z, z, z, cs, _head_repeat_sel())


def _gelu(x):
    return 0.5 * x * (1.0 + jnp.tanh(0.7978845608028654 * (x + 0.044715 * x * x * x)))


def _compress_kernel(gk_ref, gv_ref, pe_ref, w1_ref, w2_ref, cs_ref, kct_ref, vc_ref):
    nrow = gk_ref.shape[2]
    outs = []
    for kv, g_ref in enumerate((gk_ref, gv_ref)):
        g = g_ref[0, 0]
        lo = jnp.dot((g + pe_ref[kv, 0]).astype(jnp.bfloat16), w1_ref[kv, 0], preferred_element_type=jnp.float32)
        hi = jnp.dot((g + pe_ref[kv, 1]).astype(jnp.bfloat16), w1_ref[kv, 1], preferred_element_type=jnp.float32)
        hid = _gelu(lo + pltpu.roll(hi, nrow - 1, 0)).astype(jnp.bfloat16)
        outs.append(jnp.dot(hid, w2_ref[kv], preferred_element_type=jnp.float32))
    k = outs[0][:, :GW] * cs_ref[0, :, :GW] + outs[0][:, GW:] * cs_ref[0, :, GW:]
    kct_ref[0, 0] = k.T.astype(jnp.bfloat16)
    vc_ref[0, 0] = outs[1][:, :GW].astype(jnp.bfloat16)


def nsa_compress(z, positions, cmp_pos, cmp_w1, cmp_w2, B, S):
    ng = S // CMP_STRIDE
    grp = CMP_STRIDE * HEAD_DIM
    c = z[:, Z_CMP:Z_CMP + 256].reshape(B, ng, CMP_STRIDE, 2, NSA_KV_HEADS, HEAD_DIM)
    g = c.transpose(3, 0, 4, 1, 2, 5).reshape(2, B, NSA_KV_HEADS, ng, grp)
    pe = cmp_pos.reshape(2, 2, 1, grp)
    w1 = cmp_w1.reshape(2, 2, grp, CMP_HIDDEN).astype(jnp.bfloat16)
    rep = jnp.tile(cmp_w2, (1, 1, NSA_GROUP))
    sw = np.tile(_half_swap(1), NSA_GROUP) + np.repeat(np.arange(NSA_GROUP) * HEAD_DIM, HEAD_DIM)
    w2 = jnp.concatenate([rep, rep[:, :, sw]], axis=-1).astype(jnp.bfloat16)
    end = jnp.minimum(jnp.arange(ng) * CMP_STRIDE + CMP_BLOCK - 1, S - 1)
    half = HEAD_DIM // 2
    freq = ROPE_THETA ** (-jnp.arange(half, dtype=jnp.float32) / half)
    ang = positions[:, end].astype(jnp.float32)[..., None] * freq
    cos, sin = jnp.cos(ang), jnp.sin(ang)
    cs = jnp.concatenate([jnp.tile(jnp.concatenate([cos, cos], -1), (1, 1, NSA_GROUP)),
                          jnp.tile(jnp.concatenate([-sin, sin], -1), (1, 1, NSA_GROUP))], -1)
    gspec = lambda: pl.BlockSpec((1, 1, ng, grp), lambda b, k: (b, k, 0, 0))
    return pl.pallas_call(
        _compress_kernel,
        grid=(B, NSA_KV_HEADS),
        in_specs=[gspec(), gspec(),
                  pl.BlockSpec((2, 2, 1, grp), lambda b, k: (0, 0, 0, 0)),
                  pl.BlockSpec((2, 2, grp, CMP_HIDDEN), lambda b, k: (0, 0, 0, 0)),
                  pl.BlockSpec((2, CMP_HIDDEN, 2 * GW), lambda b, k: (0, 0, 0)),
                  pl.BlockSpec((1, ng, 2 * GW), lambda b, k: (b, 0, 0))],
        out_specs=[pl.BlockSpec((1, 1, GW, ng), lambda b, k: (b, k, 0, 0)),
                   pl.BlockSpec((1, 1, ng, GW), lambda b, k: (b, k, 0, 0))],
        out_shape=[jax.ShapeDtypeStruct((B, NSA_KV_HEADS, GW, ng), jnp.bfloat16),
                   jax.ShapeDtypeStruct((B, NSA_KV_HEADS, ng, GW), jnp.bfloat16)],
        compiler_params=pltpu.CompilerParams(dimension_semantics=("parallel", "parallel")),
        name="nsa_compress",
    )(g[0], g[1], pe, w1, w2, cs)


def _mix_kernel(hc_ref, on_ref, og_ref, mg_ref, x_ref, gate_ref, wc_ref, wn_ref, wg_ref, wo_ref,
                g2_ref, sc_ref, sh_ref, wq_ref, x1_ref, ut_ref, qh_ref):
    d = D_MODEL
    f32 = jnp.float32
    y = (jax.nn.sigmoid(mg_ref[:, 0:d]) * jnp.dot(hc_ref[...], wc_ref[...], preferred_element_type=f32)
         + jax.nn.sigmoid(mg_ref[:, d:2 * d]) * jnp.dot(on_ref[...], wn_ref[...], preferred_element_type=f32)
         + jax.nn.sigmoid(mg_ref[:, 2 * d:3 * d]) * jnp.dot(og_ref[...], wg_ref[...], preferred_element_type=f32))
    x1 = x_ref[...] + gate_ref[0] * jnp.dot(y.astype(jnp.bfloat16), wo_ref[...], preferred_element_type=f32)
    x1_ref[...] = x1
    u = x1 * lax.rsqrt(jnp.mean(x1 * x1, axis=1, keepdims=True) + EPS) * g2_ref[...] * sc_ref[0] + sh_ref[0]
    ut_ref[...] = u.T.astype(jnp.bfloat16)
    qh_ref[...] = jnp.dot(u.astype(jnp.bfloat16), wq_ref[...], preferred_element_type=f32).astype(qh_ref.dtype)


def mix_out(hc, on, og, z, x2, gate1, wc, wn, wg, wo, g2, scale2p, shift2, wq, S):
    n, d = x2.shape
    tm = MIX_TILE
    per_b = S // tm
    row = lambda w: pl.BlockSpec((tm, w), lambda i: (i, 0))
    mod = lambda: pl.BlockSpec((1, 1, d), lambda i: (i // per_b, 0, 0))
    full = lambda a: pl.BlockSpec(a.shape, lambda i: (0,) * a.ndim)
    nq = wq.shape[1]
    return pl.pallas_call(
        _mix_kernel,
        grid=(n // tm,),
        in_specs=[row(CONV_DIM), row(NSA_Q), row(GLA_V),
                  pl.BlockSpec((tm, N_BRANCH * d), lambda i: (i, Z_MG // (N_BRANCH * d))),
                  row(d), mod(), full(wc), full(wn), full(wg), full(wo),
                  pl.BlockSpec((1, d), lambda i: (0, 0)), mod(), mod(), full(wq)],
        out_specs=[row(d), pl.BlockSpec((d, tm), lambda i: (0, i)), row(nq)],
        out_shape=[jax.ShapeDtypeStruct((n, d), jnp.float32), jax.ShapeDtypeStruct((d, n), jnp.bfloat16),
                   jax.ShapeDtypeStruct((n, nq), jnp.bfloat16)],
        compiler_params=pltpu.CompilerParams(dimension_semantics=("parallel",), vmem_limit_bytes=VMEM_LIMIT),
        name="mix_out",
    )(hc, on, og, z, x2, gate1, wc, wn, wg, wo, g2.reshape(1, d), scale2p, shift2, wq)


def _conv_kernel(a_ref, ah_ref, w_ref, b_ref, g_ref, beta_ref, o_ref, hbuf):
    i = pl.program_id(1)
    ts = a_ref.shape[0]
    a = a_ref[...]
    hbuf[pl.ds(CONV_HALO, ts), :] = a[:, :CONV_DIM] * jax.nn.sigmoid(a[:, CONV_DIM:])
    ah = ah_ref[...]
    halo = ah[:, :CONV_DIM] * jax.nn.sigmoid(ah[:, CONV_DIM:])
    hbuf[pl.ds(0, CONV_HALO), :] = jnp.where(i > 0, halo, 0.0)
    acc = jnp.zeros((ts, CONV_DIM), jnp.float32) + b_ref[...]
    off = CONV_HALO - (CONV_WIDTH - 1)
    for k in range(CONV_WIDTH):
        acc = acc + w_ref[k:k + 1, :] * hbuf[pl.ds(off + k, ts), :]
    mu = jnp.mean(acc, axis=1, keepdims=True)
    d = acc - mu
    var = jnp.mean(d * d, axis=1, keepdims=True)
    y = d * lax.rsqrt(var + EPS) * g_ref[...] + beta_ref[...]
    o_ref[...] = (y * jax.nn.sigmoid(y)).astype(o_ref.dtype)


def conv_pallas(z, dw_w, dw_b, ln_g, ln_b, B, S):
    n = B * S
    ts = CONV_TS
    ns = S // ts
    hb = ts // CONV_HALO
    cb = Z_CONV // (2 * CONV_DIM)
    wpad = jnp.pad(dw_w, ((0, 32 - CONV_WIDTH), (0, 0)))
    vec = lambda: pl.BlockSpec((1, CONV_DIM), lambda b, i: (0, 0))
    return pl.pallas_call(
        _conv_kernel,
        grid=(B, ns),
        in_specs=[pl.BlockSpec((ts, 2 * CONV_DIM), lambda b, i: (b * ns + i, cb)),
                  pl.BlockSpec((CONV_HALO, 2 * CONV_DIM), lambda b, i: (jnp.maximum((b * ns + i) * hb - 1, 0), cb)),
                  pl.BlockSpec((32, CONV_DIM), lambda b, i: (0, 0)), vec(), vec(), vec()],
        out_specs=pl.BlockSpec((ts, CONV_DIM), lambda b, i: (b * ns + i, 0)),
        out_shape=jax.ShapeDtypeStruct((n, CONV_DIM), jnp.bfloat16),
        scratch_shapes=[pltpu.VMEM((ts + CONV_HALO, CONV_DIM), jnp.float32)],
        compiler_params=pltpu.CompilerParams(dimension_semantics=("parallel", "parallel")),
        name="conformer_conv",
    )(z, z, wpad, dw_b.reshape(1, -1), ln_g.reshape(1, -1), ln_b.reshape(1, -1))


def _msoftmax(s, mask):
    s = jnp.where(mask, s, NEG)
    m = jnp.max(s, axis=-1, keepdims=True)
    e = jnp.where(mask, jnp.exp(s - m), 0.0)
    l = jnp.sum(e, axis=-1, keepdims=True)
    return e / jnp.where(l > 0.0, l, 1.0)


def _nsa_kernel(q_ref, g_ref, kct_ref, vc_ref, kst_ref, vs_ref, kwt_ref, vw_ref, c2st_ref, o_ref, *, n_sel):
    G = NSA_GROUP
    qb = pl.program_id(2)
    t0 = pl.multiple_of(qb * TQ, TQ)
    q = q_ref[...]
    lane_g = lax.broadcasted_iota(jnp.int32, (TQ, GW), 1) // HEAD_DIM
    q4 = jnp.concatenate([jnp.where(lane_g == h, q, jnp.zeros_like(q)) for h in range(G)], axis=0)
    tpos = t0 + lax.broadcasted_iota(jnp.int32, (TQ, 1), 0)

    def heads(x):
        return x.reshape(G, TQ, x.shape[-1])

    def attend(p, v):
        return heads(jnp.dot(p.reshape(G * TQ, p.shape[-1]).astype(jnp.bfloat16), v,
                             preferred_element_type=jnp.float32))

    starts = [t0 - WINDOW + TQ * c for c in range(WINDOW // TQ + 1)]
    reads = [pl.multiple_of(jnp.maximum(s, 0), TQ) for s in starts]
    kwt = jnp.concatenate([kwt_ref[0, 0, :, pl.ds(r, TQ)] for r in reads], axis=1)
    vw = jnp.concatenate([vw_ref[0, 0, pl.ds(r, TQ), :] for r in reads], axis=0)
    wk = WINDOW + TQ
    kposw = t0 - WINDOW + lax.broadcasted_iota(jnp.int32, (TQ, wk), 1)
    diff = tpos - kposw
    maskw = (kposw >= 0) & (diff >= 0) & (diff < WINDOW)

    kct = kct_ref[0, 0]
    ncp = kct.shape[1]
    ncol = lax.broadcasted_iota(jnp.int32, (TQ, ncp), 1)
    maskc = (ncol * CMP_STRIDE + (CMP_BLOCK - 1)) <= tpos
    sc = heads(jnp.dot(q4, kct, preferred_element_type=jnp.float32))
    sw = heads(jnp.dot(q4, kwt, preferred_element_type=jnp.float32))
    pc = _msoftmax(sc, maskc[None])
    oc = attend(pc, vc_ref[0, 0])
    pw = _msoftmax(sw, maskw[None])
    ow = attend(pw, vw)
    pc_sum = jnp.sum(pc, axis=0)

    c2st = c2st_ref[...]
    jp = c2st.shape[0]
    hi = pc_sum.astype(jnp.bfloat16)
    lo = (pc_sum - hi.astype(jnp.float32)).astype(jnp.bfloat16)
    dn = (((1,), (1,)), ((), ()))
    imp_t = (lax.dot_general(c2st, hi, dn, preferred_element_type=jnp.float32)
             + lax.dot_general(c2st, lo, dn, preferred_element_type=jnp.float32))
    jrow = lax.broadcasted_iota(jnp.int32, (jp, TQ), 0)
    cur = (t0 + lax.broadcasted_iota(jnp.int32, (jp, TQ), 1)) // SLC_BLOCK
    forced = (jrow == 0) | (jrow == cur) | (jrow == cur - 1)
    score = jnp.where(jrow <= cur, imp_t + jnp.where(forced, FORCE_BONUS, 0.0), NEG)
    nblk = kst_ref.shape[3] // SLC_BLOCK
    nrow = -(-nblk // 8) * 8
    sc, jr = score[:nrow], jrow[:nrow]
    rank = jnp.zeros((nrow, TQ), jnp.int32)
    for i in range(nblk):
        si = sc[i:i + 1, :]
        rank = rank + ((si > sc) | ((si == sc) & (jr > i))).astype(jnp.int32)
    sel = ((rank < n_sel) & (sc > 0.5 * NEG)).astype(jnp.float32)
    if nrow < jp:
        sel = jnp.concatenate([sel, jnp.zeros((jp - nrow, TQ), jnp.float32)], axis=0)
    selb = sel.T.astype(jnp.bfloat16)

    nchunks = (t0 + TQ + KC - 1) // KC

    def chunk(c, carry):
        m_old, l_old, acc = carry
        k0 = pl.multiple_of(c * KC, KC)
        kt = kst_ref[0, 0, :, pl.ds(k0, KC)]
        v = vs_ref[0, 0, pl.ds(k0, KC), :]
        kblk = (k0 + lax.broadcasted_iota(jnp.int32, (jp, KC), 1)) // SLC_BLOCK
        expand = (kblk == lax.broadcasted_iota(jnp.int32, (jp, KC), 0)).astype(jnp.bfloat16)
        member = jnp.dot(selb, expand, preferred_element_type=jnp.float32) > 0.5
        kpos = k0 + lax.broadcasted_iota(jnp.int32, (TQ, KC), 1)
        msk = (member & (kpos <= tpos))[None]
        s = jnp.where(msk, heads(jnp.dot(q4, kt, preferred_element_type=jnp.float32)), NEG)
        m_new = jnp.maximum(m_old, jnp.max(s, axis=-1, keepdims=True))
        alpha = jnp.exp(m_old - m_new)
        p = jnp.exp(s - m_new)
        l_new = alpha * l_old + jnp.sum(p, axis=-1, keepdims=True)
        return m_new, l_new, alpha * acc + attend(p, v)

    def chunk_pair(i, carry):
        return chunk(2 * i + 1, chunk(2 * i, carry))

    init = (jnp.full((G, TQ, 1), NEG, jnp.float32), jnp.zeros((G, TQ, 1), jnp.float32),
            jnp.zeros((G, TQ, GW), jnp.float32))
    _, l_s, acc_s = lax.fori_loop(0, (nchunks + 1) // 2, chunk_pair, init)
    os_ = acc_s / jnp.where(l_s > 0.0, l_s, 1.0)

    sig = jax.nn.sigmoid(g_ref[...])
    out = jnp.zeros((TQ, GW), jnp.float32)
    for h in range(G):
        o_h = (sig[:, 3 * h:3 * h + 1] * oc[h] + sig[:, 3 * h + 1:3 * h + 2] * os_[h]
               + sig[:, 3 * h + 2:3 * h + 3] * ow[h])
        out = out + jnp.where(lane_g == h, o_h, 0.0)
    o_ref[...] = out.astype(o_ref.dtype)


def nsa_core(q, z, kct, vc, kst, vs, kwt, vw, B, S):
    n = B * S
    ncp = kct.shape[3]
    ncmp = (S - CMP_BLOCK) // CMP_STRIDE + 1
    nslc = S // SLC_BLOCK
    n_sel = min(SLC_TOPK, nslc)
    jp = LANE
    assert nslc <= jp and S % (2 * KC) == 0 and S % TQ == 0 and WINDOW % TQ == 0
    cs = np.arange(ncmp) * CMP_STRIDE
    ss = np.arange(nslc) * SLC_BLOCK
    ov = np.minimum(cs[:, None] + CMP_BLOCK, ss[None, :] + SLC_BLOCK) - np.maximum(cs[:, None], ss[None, :])
    c2s = np.zeros((ncp, jp), np.float32)
    c2s[:ncmp, :nslc] = np.clip(ov, 0, None) / CMP_BLOCK
    c2st = jnp.asarray(c2s.T, dtype=jnp.bfloat16)
    nqb = S // TQ

    def kv_spec(shp):
        return pl.BlockSpec((1, 1) + shp, lambda b, k, i: (b, k, 0, 0))

    return pl.pallas_call(
        functools.partial(_nsa_kernel, n_sel=n_sel),
        grid=(B, NSA_KV_HEADS, nqb),
        in_specs=[pl.BlockSpec((TQ, GW), lambda b, k, i: (b * nqb + i, k)),
                  pl.BlockSpec((TQ, LANE), lambda b, k, i: (b * nqb + i, Z_NG // LANE + k)),
                  kv_spec((GW, ncp)), kv_spec((ncp, GW)),
                  kv_spec((GW, S)), kv_spec((S, GW)),
                  kv_spec((GW, S)), kv_spec((S, GW)),
                  pl.BlockSpec((jp, ncp), lambda b, k, i: (0, 0))],
        out_specs=pl.BlockSpec((TQ, GW), lambda b, k, i: (b * nqb + i, k)),
        out_shape=jax.ShapeDtypeStruct((n, NSA_KV_HEADS * GW), jnp.bfloat16),
        compiler_params=pltpu.CompilerParams(dimension_semantics=("parallel", "parallel", "arbitrary"),
                                             vmem_limit_bytes=VMEM_LIMIT),
        name="nsa_core",
    )(q, z, kct, vc, kst, vs, kwt, vw, c2st)


def _gla_consts():
    C = GLA_CHUNK
    i = np.arange(C)
    mats = [np.tril(np.ones((C, C), np.float32))]
    for s in GLA_LEVELS:
        r = (i // (2 * s)) * 2 * s + s - 1
        upper = (i // s) % 2 == 1
        m = np.arange(C)[None, :]
        mq = ((m > r[:, None]) & (m <= i[:, None]) & upper[:, None]).astype(np.float32)
        mk = ((m > i[:, None]) & (m <= r[:, None]) & (~upper)[:, None]).astype(np.float32)
        mats += [mq, mk]
    return np.concatenate(mats, axis=0)


def _split3(x):
    h = x.astype(jnp.bfloat16)
    r = x - h.astype(jnp.float32)
    m = r.astype(jnp.bfloat16)
    l = (r - m.astype(jnp.float32)).astype(jnp.bfloat16)
    return h, m, l


def _gla_kernel(q_ref, k_ref, v_ref, r_ref, glr_ref, gw_ref, gb_ref, ng_ref, msel_ref, o_ref, state_ref):
    C = GLA_CHUNK

    @pl.when(pl.program_id(1) == 0)
    def _():
        state_ref[...] = jnp.zeros_like(state_ref)

    ii = lax.broadcasted_iota(jnp.int32, (C, C), 0)
    jj = lax.broadcasted_iota(jnp.int32, (C, C), 1)
    lane_h = lax.broadcasted_iota(jnp.int32, (C, GLA_QK), 1) // GLA_DK
    bd = (lax.broadcasted_iota(jnp.int32, (GLA_QK, GLA_V), 0) // GLA_DK
          == lax.broadcasted_iota(jnp.int32, (GLA_QK, GLA_V), 1) // GLA_DV)
    msel = msel_ref[...]
    gw = gw_ref[...].astype(jnp.bfloat16)
    dn = (((1,), (1,)), ((), ()))

    def chunk(c, carry):
        r0 = pl.multiple_of(c * C, C)
        q = q_ref[pl.ds(r0, C), :] * (GLA_DK ** -0.5)
        k = k_ref[pl.ds(r0, C), :]
        v = v_ref[pl.ds(r0, C), :].astype(jnp.bfloat16)
        z = jnp.dot(glr_ref[pl.ds(r0, C), :].astype(jnp.bfloat16), gw, preferred_element_type=jnp.float32) + gb_ref[...]
        a = jax.nn.log_sigmoid(z) / GLA_GATE_TEMP
        ah, am, al = _split3(a)
        ex = (jnp.dot(msel, ah, preferred_element_type=jnp.float32)
              + jnp.dot(msel, am, preferred_element_type=jnp.float32)
              + jnp.dot(msel, al, preferred_element_type=jnp.float32))
        bcum = ex[0:C]
        state = state_ref[...]
        o = jnp.dot((q * jnp.exp(bcum)).astype(jnp.bfloat16), state.astype(jnp.bfloat16),
                    preferred_element_type=jnp.float32)
        kb = k.astype(jnp.bfloat16)
        for h in range(GLA_HEADS):
            hm = lane_h == h
            att = jnp.where(ii == jj, lax.dot_general(jnp.where(hm, q, 0.0).astype(jnp.bfloat16), kb, dn,
                                                      preferred_element_type=jnp.float32), 0.0)
            for li, s in enumerate(GLA_LEVELS):
                eq = jnp.exp(ex[(1 + 2 * li) * C:(2 + 2 * li) * C])
                ek = jnp.exp(ex[(2 + 2 * li) * C:(3 + 2 * li) * C])
                qs = jnp.where(hm, q * eq, 0.0).astype(jnp.bfloat16)
                ks = (k * ek).astype(jnp.bfloat16)
                blk = (ii // (2 * s) == jj // (2 * s)) & ((ii // s) % 2 == 1) & ((jj // s) % 2 == 0)
                att = att + jnp.where(blk, lax.dot_general(qs, ks, dn, preferred_element_type=jnp.float32), 0.0)
            oh = jnp.dot(att.astype(jnp.bfloat16), v, preferred_element_type=jnp.float32)
            o = o + jnp.where(lax.broadcasted_iota(jnp.int32, (C, GLA_V), 1) // GLA_DV == h, oh, 0.0)
        blast = bcum[C - 1:C, :]
        kd_t = (k * jnp.exp(blast - bcum)).T.astype(jnp.bfloat16)
        decay_col = jnp.exp(jnp.sum(a.T, axis=1, keepdims=True))
        upd = jnp.dot(kd_t, v, preferred_element_type=jnp.float32)
        state_ref[...] = state * decay_col + jnp.where(bd, upd, 0.0)
        outs = []
        for h in range(GLA_HEADS):
            oh = o[:, h * GLA_DV:(h + 1) * GLA_DV]
            outs.append(oh * lax.rsqrt(jnp.mean(oh * oh, axis=1, keepdims=True) + EPS) * ng_ref[...])
        on = jnp.concatenate(outs, axis=1)
        r = r_ref[pl.ds(r0, C), :]
        o_ref[pl.ds(r0, C), :] = (on * (r * jax.nn.sigmoid(r))).astype(o_ref.dtype)
        return carry

    lax.fori_loop(0, q_ref.shape[0] // C, chunk, 0)


def gla_pallas(z, gate_w, gate_b, norm_g, B, S):
    n = B * S
    tb = GLA_TB
    nb = S // tb
    gw = jnp.pad(gate_w, ((0, LANE - gate_w.shape[0]), (0, 0)))
    msel = jnp.asarray(_gla_consts(), dtype=jnp.bfloat16)
    zb = lambda col, w: pl.BlockSpec((tb, w), lambda b, i: (b * nb + i, col // w))
    full = lambda shp: pl.BlockSpec(shp, lambda b, i: (0,) * len(shp))
    return pl.pallas_call(
        _gla_kernel,
        grid=(B, nb),
        in_specs=[zb(Z_GQ, GLA_QK), zb(Z_GK, GLA_QK), zb(Z_GV, GLA_V), zb(Z_GR, GLA_V), zb(Z_GLR, LANE),
                  full((LANE, GLA_QK)), full((1, GLA_QK)), full((1, GLA_DV)), full(msel.shape)],
        out_specs=pl.BlockSpec((tb, GLA_V), lambda b, i: (b * nb + i, 0)),
        out_shape=jax.ShapeDtypeStruct((n, GLA_V), jnp.bfloat16),
        scratch_shapes=[pltpu.VMEM((GLA_QK, GLA_V), jnp.float32)],
        compiler_params=pltpu.CompilerParams(dimension_semantics=("parallel", "arbitrary")),
        name="gla",
    )(z, z, z, z, z, gw, gate_b.reshape(1, -1), norm_g.reshape(1, -1), msel)


_FULL_BLOCKS = 8


def _extract_top(s, k, tie_break):
    n, t = s.shape
    row = lax.broadcasted_iota(jnp.int32, (n, t), 0)
    rank = jnp.full((n, t), k, jnp.int32)
    vals = []
    for r in range(k):
        m = jnp.max(s, axis=0, keepdims=True)
        hit = s == m
        if tie_break:
            hit = row == jnp.min(jnp.where(hit, row, n), axis=0, keepdims=True)
        rank = jnp.where(hit, r, rank)
        s = jnp.where(hit, float('-inf'), s)
        vals.append(m)
    return jnp.concatenate(vals, axis=0), rank


def _peer_gate_math(s1, s2, tie_break):
    K = PEER_TOPK
    t = s1.shape[1]
    v1, rank1 = _extract_top(s1, K, tie_break)
    v2, rank2 = _extract_top(s2, K, tie_break)
    blocks, poss = [], []
    row16 = lax.broadcasted_iota(jnp.int32, (K, t), 0)
    for r1 in range(_FULL_BLOCKS):
        c = v1[r1:r1 + 1, :] + v2
        blocks.append(jnp.where(row16 < K // (r1 + 1), c, float('-inf')))
        poss.append(row16 + r1 * K)
    row8 = lax.broadcasted_iota(jnp.int32, (K - _FULL_BLOCKS, t), 0)
    blocks.append(v1[_FULL_BLOCKS:, :] + v2[0:1, :])
    poss.append((row8 + _FULL_BLOCKS) * K)
    sels = [jnp.zeros(b.shape, jnp.bool_) for b in blocks]
    big = K * K
    for _ in range(K):
        m = blocks[0].max(axis=0, keepdims=True)
        for b in blocks[1:]:
            m = jnp.maximum(m, b.max(axis=0, keepdims=True))
        if tie_break:
            first = None
            for b, p in zip(blocks, poss):
                f = jnp.min(jnp.where(b == m, p, big), axis=0, keepdims=True)
                first = f if first is None else jnp.minimum(first, f)
            hits = [p == first for p in poss]
        else:
            hits = [b == m for b in blocks]
        for i, hit in enumerate(hits):
            sels[i] = sels[i] | hit
            blocks[i] = jnp.where(hit, float('-inf'), blocks[i])
    top = v1[0:1, :] + v2[0:1, :]
    cnt_rows = []
    z = jnp.zeros((1, t), jnp.float32)
    for r1 in range(_FULL_BLOCKS):
        sel = sels[r1]
        cnt_rows.append(jnp.sum(sel.astype(jnp.float32), axis=0, keepdims=True))
        c = v1[r1:r1 + 1, :] + v2
        z = z + jnp.sum(jnp.where(sel, jnp.exp(c - top), 0.0), axis=0, keepdims=True)
    sel = sels[_FULL_BLOCKS]
    cnt_tail = sel.astype(jnp.float32)
    c = v1[_FULL_BLOCKS:, :] + v2[0:1, :]
    z = z + jnp.sum(jnp.where(sel, jnp.exp(c - top), 0.0), axis=0, keepdims=True)
    cnt = jnp.concatenate(cnt_rows + [cnt_tail], axis=0)
    a = jnp.zeros(s1.shape, jnp.float32)
    for r in range(K):
        a = jnp.where(rank1 == r, cnt[r:r + 1, :], a)
    taken = (jnp.sum((rank1 < K).astype(jnp.float32), axis=0, keepdims=True),
             jnp.sum((rank2 < K).astype(jnp.float32), axis=0, keepdims=True),
             jnp.sum(cnt, axis=0, keepdims=True))
    return jnp.exp(s1 - v1[0:1, :]), a, rank2.astype(jnp.float32), jnp.exp(s2 - v2[0:1, :]) / z, taken


def _peer_gate_kernel(q_ref, keys_ref, e1_ref, a_ref, r2_ref, e2_ref):
    q = q_ref[...].astype(jnp.bfloat16)
    k1 = keys_ref[0, 0].astype(jnp.bfloat16)
    k2 = keys_ref[0, 1].astype(jnp.bfloat16)
    dn = (((1,), (1,)), ((), ()))
    s1 = lax.dot_general(k1, q[:, :PEER_DKEY // 2], dn, preferred_element_type=jnp.float32)
    s2 = lax.dot_general(k2, q[:, PEER_DKEY // 2:], dn, preferred_element_type=jnp.float32)

    def write(e1, a, r2, e2):
        e1_ref[0] = e1
        a_ref[0] = a
        r2_ref[0] = r2.astype(jnp.bfloat16)
        e2_ref[0] = e2.astype(jnp.bfloat16)

    e1, a, r2, e2, taken = _peer_gate_math(s1, s2, tie_break=False)
    write(e1, a, r2, e2)
    excess = sum(jnp.max(jnp.abs(c - PEER_TOPK)) for c in taken)

    @pl.when(excess > 0.0)
    def _():
        write(*_peer_gate_math(s1, s2, tie_break=True)[:4])


def peer_gates(qh, keys, tt=512):
    n = qh.shape[0]
    H = PEER_HEADS
    shp32 = jax.ShapeDtypeStruct((H, PEER_NKEYS, n), jnp.float32)
    shp16 = jax.ShapeDtypeStruct((H, PEER_NKEYS, n), jnp.bfloat16)
    ospec = pl.BlockSpec((1, PEER_NKEYS, tt), lambda i, h: (h, 0, i))
    return pl.pallas_call(
        _peer_gate_kernel,
        grid=(n // tt, H),
        in_specs=[pl.BlockSpec((tt, PEER_DKEY), lambda i, h: (i, h)),
                  pl.BlockSpec((1, 2, PEER_NKEYS, PEER_DKEY // 2), lambda i, h: (h, 0, 0, 0))],
        out_specs=[ospec, ospec, ospec, ospec],
        out_shape=[shp32, shp32, shp16, shp16],
        compiler_params=pltpu.CompilerParams(dimension_semantics=("parallel", "parallel")),
        name="peer_gates",
    )(qh, keys)


def _row_bf16(row, n):
    t = row.shape[1]
    tile = jnp.broadcast_to(row, (16, t)).astype(jnp.bfloat16)
    return jnp.concatenate([tile] * (n // 16), axis=0)


def _peer_dense_kernel(xt_ref, u_ref, vt_ref, e1_ref, a_ref, r2_ref, e2_ref, x1_ref, gate_ref, fg_ref,
                       o_ref, acc_ref, *, ec, final):
    j = pl.program_id(1)
    nsub = ec // PEER_NKEYS

    @pl.when(j == 0)
    def _():
        acc_ref[...] = jnp.zeros_like(acc_ref)

    assert nsub == 8
    r8 = pl.multiple_of(j * nsub, 8)
    a_tiles = [a_ref[h, pl.ds(r8, 8), :] for h in range(PEER_HEADS)]
    e1_tiles = [e1_ref[h, pl.ds(r8, 8), :] for h in range(PEER_HEADS)]

    def gate_mat(i):
        g = None
        for h in range(PEER_HEADS):
            cnt = _row_bf16(a_tiles[h][i:i + 1, :], PEER_NKEYS)
            e1 = _row_bf16(e1_tiles[h][i:i + 1, :], PEER_NKEYS)
            term = jnp.where(r2_ref[h] < cnt, e2_ref[h], jnp.bfloat16(0)) * e1
            g = term if g is None else g + term
        return g

    gs = PEER_GROUPS_PER_SCORE

    def scores(p):
        r0 = p * gs * PEER_NKEYS
        return jnp.dot(u_ref[r0:r0 + gs * PEER_NKEYS, :], xt_ref[...], preferred_element_type=jnp.float32)

    st = scores(0)
    ws = []
    for p in range(nsub // gs):
        st_next = scores(p + 1) if (p + 1) * gs < nsub else None
        for k in range(gs):
            i = p * gs + k
            ws.append(_gelu(st[k * PEER_NKEYS:(k + 1) * PEER_NKEYS].astype(jnp.bfloat16)) * gate_mat(i))
            if i % PEER_GROUPS_PER_DOT == PEER_GROUPS_PER_DOT - 1:
                w = jnp.concatenate(ws, axis=0)
                ws = []
                acc_ref[...] += jnp.dot(
                    vt_ref[:, (i + 1 - PEER_GROUPS_PER_DOT) * PEER_NKEYS:(i + 1) * PEER_NKEYS], w,
                    preferred_element_type=jnp.float32)
        st = st_next

    @pl.when(j == pl.num_programs(1) - 1)
    def _():
        x2 = x1_ref[...] + gate_ref[0] * acc_ref[...].T
        if final:
            x2 = x2 * lax.rsqrt(jnp.mean(x2 * x2, axis=1, keepdims=True) + EPS) * fg_ref[...]
        o_ref[...] = x2


def peer_dense(xt, u16, vt16, e1, a, r2, e2, x1, gate2, final_g, S, final, tt=512, ec=1024):
    d, n = xt.shape
    ne = u16.shape[0]
    H = PEER_HEADS
    per_b = S // tt
    gspec = lambda: pl.BlockSpec((H, PEER_NKEYS, tt), lambda i, j: (0, 0, i))
    return pl.pallas_call(
        functools.partial(_peer_dense_kernel, ec=ec, final=final),
        grid=(n // tt, ne // ec),
        in_specs=[pl.BlockSpec((d, tt), lambda i, j: (0, i)),
                  pl.BlockSpec((ec, d), lambda i, j: (j, 0)),
                  pl.BlockSpec((d, ec), lambda i, j: (0, j)),
                  gspec(), gspec(), gspec(), gspec(),
                  pl.BlockSpec((tt, d), lambda i, j: (i, 0)),
                  pl.BlockSpec((1, 1, d), lambda i, j: (i // per_b, 0, 0)),
                  pl.BlockSpec((1, d), lambda i, j: (0, 0))],
        out_specs=pl.BlockSpec((tt, d), lambda i, j: (i, 0)),
        out_shape=jax.ShapeDtypeStruct((n, d), jnp.float32),
        scratch_shapes=[pltpu.VMEM((d, tt), jnp.float32)],
        compiler_params=pltpu.CompilerParams(dimension_semantics=("parallel", "arbitrary"),
                                             vmem_limit_bytes=VMEM_LIMIT),
        name="peer_dense",
    )(xt, u16, vt16, e1, a, r2, e2, x1, gate2, final_g.reshape(1, d))


def adaln(c, w, b):
    mod = pmm(jax.nn.silu(c), w) + b
    shift, scale, gate = jnp.split(mod[:, None, :], 3, axis=-1)
    return shift, 1.0 + scale, gate


def kernel(x, c, positions, ada_w, ada_b, norm_g, w_in, conv_dw_w, conv_dw_b, conv_ln_g, conv_ln_b, w_conv_up,
           cmp_pos, cmp_w1, cmp_w2, w_nsa_up, gla_gate_w, gla_gate_b, gla_norm_g, w_gla_up, w_out,
           peer_wq, peer_keys, peer_u, peer_v, final_g):
    B, S, D = x.shape
    bf = jnp.bfloat16
    x2 = x.reshape(B * S, D)
    cs = rope_tables(positions)
    for l in range(DEPTH):
        shift, scale1p, gate = adaln(c, ada_w[l, 0], ada_b[l, 0])
        z = in_proj(x2, norm_g[l, 0], scale1p, shift, pack_w_in(w_in[l]), S)
        hc = conv_pallas(z, conv_dw_w[l], conv_dw_b[l], conv_ln_g[l], conv_ln_b[l], B, S)
        q, kst, vs, kwt, vw = nsa_prep(z, cs, B, S)
        kct, vc = nsa_compress(z, positions, cmp_pos[l], cmp_w1[l], cmp_w2[l], B, S)
        on = nsa_core(q, z, kct, vc, kst, vs, kwt, vw, B, S)
        og = gla_pallas(z, gla_gate_w[l], gla_gate_b[l], gla_norm_g[l], B, S)
        shift2, scale2p, gate2 = adaln(c, ada_w[l, 1], ada_b[l, 1])
        x1, ut, qh = mix_out(hc, on, og, z, x2, gate, w_conv_up[l].astype(bf), w_nsa_up[l].astype(bf),
                             w_gla_up[l].astype(bf), w_out[l].astype(bf), norm_g[l, 1], scale2p, shift2,
                             peer_wq[l].astype(bf), S)
        e1, a, r2, e2 = peer_gates(qh, peer_keys[l])
        x2 = peer_dense(ut, peer_u[l].astype(bf), peer_v[l].astype(bf).T, e1, a, r2, e2, x1, gate2, final_g,
                        S, final=(l == DEPTH - 1))
    return x2.reshape(B, S, D)
```

```python
import functools

import jax
import jax.numpy as jnp
import numpy as np
from jax import lax
from jax.experimental import pallas as pl
from jax.experimental.pallas import tpu as pltpu

D_MODEL = 1024
DEPTH = 2
EPS = 1e-6
NEG = -1e30
N_BRANCH = 3
CONV_DIM = 512
CONV_WIDTH = 31
NSA_HEADS = 8
NSA_KV_HEADS = 2
NSA_GROUP = NSA_HEADS // NSA_KV_HEADS
HEAD_DIM = 64
CMP_BLOCK = 32
CMP_STRIDE = 16
CMP_HIDDEN = 256
SLC_BLOCK = 64
SLC_TOPK = 16
WINDOW = 512
FORCE_BONUS = 1e4
ROPE_THETA = 10000.0
GLA_HEADS = 4
GLA_DK = 64
GLA_DV = 128
GLA_GATE_RANK = 16
GLA_GATE_TEMP = 16.0
GLA_CHUNK = 256
PEER_HEADS = 8
PEER_NKEYS = 128
PEER_DKEY = 256
PEER_TOPK = 16
NSA_Q = NSA_HEADS * HEAD_DIM
NSA_KV = 3 * 2 * NSA_KV_HEADS * HEAD_DIM
NSA_G = 3 * NSA_HEADS
GLA_QK = GLA_HEADS * GLA_DK
GLA_V = GLA_HEADS * GLA_DV
SPLITS = [2 * CONV_DIM, NSA_Q, NSA_KV, NSA_G, GLA_QK, GLA_QK, GLA_V, GLA_V, GLA_GATE_RANK, N_BRANCH * D_MODEL]

LANE = 128
TQ = 128
KC = 512
GW = NSA_GROUP * HEAD_DIM
VMEM_LIMIT = 48 * 1024 * 1024
GLA_LEVELS = tuple(GLA_CHUNK >> s for s in range(1, GLA_CHUNK.bit_length()))
GLA_TB = 512
CONV_TS = 512
CONV_HALO = 32
ROW_TILE = 1024
MIX_TILE = 512
PEER_GROUPS_PER_DOT = 2
PEER_GROUPS_PER_SCORE = 8

Z_MG = 0
Z_CONV = Z_MG + N_BRANCH * D_MODEL
Z_Q = Z_CONV + 2 * CONV_DIM
Z_QSW = Z_Q + NSA_Q
Z_GV = Z_QSW + NSA_Q
Z_GR = Z_GV + GLA_V
Z_GQ = Z_GR + GLA_V
Z_GK = Z_GQ + GLA_QK
Z_CMP = Z_GK + GLA_QK
Z_SLC = Z_CMP + 256
Z_WIN = Z_SLC + 384
Z_NG = Z_WIN + 384
Z_GLR = Z_NG + NSA_KV_HEADS * LANE
NZ = Z_GLR + LANE
assert Z_SLC % 384 == 0 and Z_WIN % 384 == 0 and Z_CONV % (2 * CONV_DIM) == 0 and Z_GQ % GLA_QK == 0


def _mm_kernel(a_ref, b_ref, o_ref):
    o_ref[...] = jnp.dot(a_ref[...].astype(jnp.bfloat16), b_ref[...].astype(jnp.bfloat16),
                         preferred_element_type=jnp.float32)


def _pick(n, cands):
    for c in cands:
        if n % c == 0:
            return c
    return n


def pmm(a, b):
    m, k = a.shape
    n = b.shape[1]
    n_pad = -n % LANE
    if n_pad:
        b = jnp.pad(b, ((0, 0), (0, n_pad)))
    m_pad = -m % 8
    if m_pad:
        a = jnp.pad(a, ((0, m_pad), (0, 0)))
    mp, np_ = m + m_pad, n + n_pad
    tm = _pick(mp, (512, 256, 128, 64, 32, 16, 8))
    tn = _pick(np_, (512, 640, 384, 256, 128))
    out = pl.pallas_call(
        _mm_kernel,
        grid=(mp // tm, np_ // tn),
        in_specs=[pl.BlockSpec((tm, k), lambda i, j: (i, 0)),
                  pl.BlockSpec((k, tn), lambda i, j: (0, j))],
        out_specs=pl.BlockSpec((tm, tn), lambda i, j: (i, j)),
        out_shape=jax.ShapeDtypeStruct((mp, np_), jnp.float32),
        compiler_params=pltpu.CompilerParams(dimension_semantics=("parallel", "parallel")),
        name="pmm",
    )(a, b)
    return out[:m, :n]


def _in_proj_kernel(x_ref, g_ref, sc_ref, sh_ref, w_ref, z_ref, u_ref):
    @pl.when(pl.program_id(1) == 0)
    def _():
        x = x_ref[...]
        y = x * lax.rsqrt(jnp.mean(x * x, axis=1, keepdims=True) + EPS) * g_ref[...]
        u_ref[...] = (y * sc_ref[0] + sh_ref[0]).astype(jnp.bfloat16)

    z_ref[...] = jnp.dot(u_ref[...], w_ref[...], preferred_element_type=jnp.float32)


def in_proj(x2, g, scale1p, shift, w16, S):
    n, d = x2.shape
    nz = w16.shape[1]
    tm = ROW_TILE
    tn = _pick(nz, (1152, 1024, 896, 512, 384, 256, 128))
    per_b = S // tm
    mod = lambda: pl.BlockSpec((1, 1, d), lambda i, j: (i // per_b, 0, 0))
    return pl.pallas_call(
        _in_proj_kernel,
        grid=(n // tm, nz // tn),
        in_specs=[pl.BlockSpec((tm, d), lambda i, j: (i, 0)),
                  pl.BlockSpec((1, d), lambda i, j: (0, 0)), mod(), mod(),
                  pl.BlockSpec((d, tn), lambda i, j: (0, j))],
        out_specs=pl.BlockSpec((tm, tn), lambda i, j: (i, j)),
        out_shape=jax.ShapeDtypeStruct((n, nz), jnp.float32),
        scratch_shapes=[pltpu.VMEM((tm, d), jnp.bfloat16)],
        compiler_params=pltpu.CompilerParams(dimension_semantics=("parallel", "arbitrary"),
                                             vmem_limit_bytes=VMEM_LIMIT),
        name="in_proj",
    )(x2, g.reshape(1, d), scale1p, shift, w16)


def _half_swap(n_heads):
    idx = np.arange(n_heads * HEAD_DIM).reshape(n_heads, 2, HEAD_DIM // 2)
    return idx[:, ::-1, :].reshape(-1)


def pack_w_in(w_in):
    o = np.cumsum([0] + SPLITS)
    a_conv, nq, nkv, ng, gq, gk, gv, gr, glr, mg = [w_in[:, o[i]:o[i + 1]] for i in range(len(SPLITS))]
    kvw = 2 * NSA_KV_HEADS * HEAD_DIM
    kw = NSA_KV_HEADS * HEAD_DIM
    d = w_in.shape[0]
    sw = _half_swap(NSA_KV_HEADS)
    cols = [mg, a_conv, nq, nq[:, _half_swap(NSA_HEADS)], gv, gr, gq, gk, nkv[:, 0:kvw]]
    for br in (1, 2):
        k = nkv[:, br * kvw:br * kvw + kw]
        cols += [k, k[:, sw], nkv[:, br * kvw + kw:(br + 1) * kvw]]
    gpk = NSA_GROUP * 3
    for kvh in range(NSA_KV_HEADS):
        cols += [ng[:, kvh * gpk:(kvh + 1) * gpk], jnp.zeros((d, LANE - gpk), w_in.dtype)]
    cols += [glr, jnp.zeros((d, LANE - GLA_GATE_RANK), w_in.dtype)]
    w = jnp.concatenate(cols, axis=1)
    assert w.shape[1] == NZ
    return w.astype(jnp.bfloat16)


def rope_tables(positions):
    half = HEAD_DIM // 2
    freq = ROPE_THETA ** (-jnp.arange(half, dtype=jnp.float32) / half)
    ang = positions.astype(jnp.float32)[..., None] * freq
    cos, sin = jnp.cos(ang), jnp.sin(ang)
    c = jnp.tile(jnp.concatenate([cos, cos], -1), (1, 1, LANE // HEAD_DIM))
    s = jnp.tile(jnp.concatenate([-sin, sin], -1), (1, 1, LANE // HEAD_DIM))
    return jnp.concatenate([c, s], -1).reshape(-1, 2 * LANE)


def _nsa_prep_kernel(q_ref, qs_ref, slc_ref, win_ref, cs_ref, sel_ref, qo_ref, kst_ref, vs_ref, kwt_ref, vw_ref):
    cos = cs_ref[:, :LANE]
    sin = cs_ref[:, LANE:]
    scale = HEAD_DIM ** -0.5
    for t in range(NSA_Q // LANE):
        sl = slice(t * LANE, (t + 1) * LANE)
        qo_ref[:, sl] = ((q_ref[:, sl] * cos + qs_ref[:, sl] * sin) * scale).astype(jnp.bfloat16)
    dn = (((1,), (1,)), ((), ()))
    for src, kt_ref, v_ref in ((slc_ref, kst_ref, vs_ref), (win_ref, kwt_ref, vw_ref)):
        k = (src[:, 0:LANE] * cos + src[:, LANE:2 * LANE] * sin).astype(jnp.bfloat16)
        v = src[:, 2 * LANE:3 * LANE].astype(jnp.bfloat16)
        for kvh in range(NSA_KV_HEADS):
            sel = sel_ref[kvh]
            kt_ref[0, kvh] = lax.dot_general(sel, k, dn, preferred_element_type=jnp.float32).astype(jnp.bfloat16)
            v_ref[0, kvh] = lax.dot_general(v, sel, dn, preferred_element_type=jnp.float32).astype(jnp.bfloat16)


def _head_repeat_sel():
    sel = np.zeros((NSA_KV_HEADS, GW, LANE), np.float32)
    for kvh in range(NSA_KV_HEADS):
        for r in range(GW):
            sel[kvh, r, kvh * HEAD_DIM + r % HEAD_DIM] = 1.0
    return jnp.asarray(sel, dtype=jnp.bfloat16)


def nsa_prep(z, cs, B, S):
    n = B * S
    tp = ROW_TILE
    nb = S // tp
    zb = lambda col, w: pl.BlockSpec((tp, w), lambda b, i: (b * nb + i, col // w))
    kt = lambda: pl.BlockSpec((1, NSA_KV_HEADS, GW, tp), lambda b, i: (b, 0, 0, i))
    vv = lambda: pl.BlockSpec((1, NSA_KV_HEADS, tp, GW), lambda b, i: (b, 0, i, 0))
    kt_shape = jax.ShapeDtypeStruct((B, NSA_KV_HEADS, GW, S), jnp.bfloat16)
    v_shape = jax.ShapeDtypeStruct((B, NSA_KV_HEADS, S, GW), jnp.bfloat16)
    return pl.pallas_call(
        _nsa_prep_kernel,
        grid=(B, nb),
        in_specs=[zb(Z_Q, NSA_Q), zb(Z_QSW, NSA_Q), zb(Z_SLC, 384), zb(Z_WIN, 384),
                  pl.BlockSpec((tp, 2 * LANE), lambda b, i: (b * nb + i, 0)),
                  pl.BlockSpec((NSA_KV_HEADS, GW, LANE), lambda b, i: (0, 0, 0))],
        out_specs=[pl.BlockSpec((tp, NSA_Q), lambda b, i: (b * nb + i, 0)), kt(), vv(), kt(), vv()],
        out_shape=[jax.ShapeDtypeStruct((n, NSA_Q), jnp.bfloat16), kt_shape, v_shape, kt_shape, v_shape],
        compiler_params=pltpu.CompilerParams(dimension_semantics=("parallel", "parallel")),
        name="nsa_prep",
    )(z, z, z, z, cs, _head_repeat_sel())


def _gelu(x):
    return 0.5 * x * (1.0 + jnp.tanh(0.7978845608028654 * (x + 0.044715 * x * x * x)))


def _compress_kernel(gk_ref, gv_ref, pe_ref, w1_ref, w2_ref, cs_ref, kct_ref, vc_ref):
    nrow = gk_ref.shape[2]
    outs = []
    for kv, g_ref in enumerate((gk_ref, gv_ref)):
        g = g_ref[0, 0]
        lo = jnp.dot((g + pe_ref[kv, 0]).astype(jnp.bfloat16), w1_ref[kv, 0], preferred_element_type=jnp.float32)
        hi = jnp.dot((g + pe_ref[kv, 1]).astype(jnp.bfloat16), w1_ref[kv, 1], preferred_element_type=jnp.float32)
        hid = _gelu(lo + pltpu.roll(hi, nrow - 1, 0)).astype(jnp.bfloat16)
        outs.append(jnp.dot(hid, w2_ref[kv], preferred_element_type=jnp.float32))
    k = outs[0][:, :GW] * cs_ref[0, :, :GW] + outs[0][:, GW:] * cs_ref[0, :, GW:]
    kct_ref[0, 0] = k.T.astype(jnp.bfloat16)
    vc_ref[0, 0] = outs[1][:, :GW].astype(jnp.bfloat16)


def nsa_compress(z, positions, cmp_pos, cmp_w1, cmp_w2, B, S):
    ng = S // CMP_STRIDE
    grp = CMP_STRIDE * HEAD_DIM
    c = z[:, Z_CMP:Z_CMP + 256].reshape(B, ng, CMP_STRIDE, 2, NSA_KV_HEADS, HEAD_DIM)
    g = c.transpose(3, 0, 4, 1, 2, 5).reshape(2, B, NSA_KV_HEADS, ng, grp)
    pe = cmp_pos.reshape(2, 2, 1, grp)
    w1 = cmp_w1.reshape(2, 2, grp, CMP_HIDDEN).astype(jnp.bfloat16)
    rep = jnp.tile(cmp_w2, (1, 1, NSA_GROUP))
    sw = np.tile(_half_swap(1), NSA_GROUP) + np.repeat(np.arange(NSA_GROUP) * HEAD_DIM, HEAD_DIM)
    w2 = jnp.concatenate([rep, rep[:, :, sw]], axis=-1).astype(jnp.bfloat16)
    end = jnp.minimum(jnp.arange(ng) * CMP_STRIDE + CMP_BLOCK - 1, S - 1)
    half = HEAD_DIM // 2
    freq = ROPE_THETA ** (-jnp.arange(half, dtype=jnp.float32) / half)
    ang = positions[:, end].astype(jnp.float32)[..., None] * freq
    cos, sin = jnp.cos(ang), jnp.sin(ang)
    cs = jnp.concatenate([jnp.tile(jnp.concatenate([cos, cos], -1), (1, 1, NSA_GROUP)),
                          jnp.tile(jnp.concatenate([-sin, sin], -1), (1, 1, NSA_GROUP))], -1)
    gspec = lambda: pl.BlockSpec((1, 1, ng, grp), lambda b, k: (b, k, 0, 0))
    return pl.pallas_call(
        _compress_kernel,
        grid=(B, NSA_KV_HEADS),
        in_specs=[gspec(), gspec(),
                  pl.BlockSpec((2, 2, 1, grp), lambda b, k: (0, 0, 0, 0)),
                  pl.BlockSpec((2, 2, grp, CMP_HIDDEN), lambda b, k: (0, 0, 0, 0)),
                  pl.BlockSpec((2, CMP_HIDDEN, 2 * GW), lambda b, k: (0, 0, 0)),
                  pl.BlockSpec((1, ng, 2 * GW), lambda b, k: (b, 0, 0))],
        out_specs=[pl.BlockSpec((1, 1, GW, ng), lambda b, k: (b, k, 0, 0)),
                   pl.BlockSpec((1, 1, ng, GW), lambda b, k: (b, k, 0, 0))],
        out_shape=[jax.ShapeDtypeStruct((B, NSA_KV_HEADS, GW, ng), jnp.bfloat16),
                   jax.ShapeDtypeStruct((B, NSA_KV_HEADS, ng, GW), jnp.bfloat16)],
        compiler_params=pltpu.CompilerParams(dimension_semantics=("parallel", "parallel")),
        name="nsa_compress",
    )(g[0], g[1], pe, w1, w2, cs)


def _mix_kernel(hc_ref, on_ref, og_ref, mg_ref, x_ref, gate_ref, wc_ref, wn_ref, wg_ref, wo_ref,
                g2_ref, sc_ref, sh_ref, wq_ref, x1_ref, ut_ref, qh_ref):
    d = D_MODEL
    f32 = jnp.float32
    y = (jax.nn.sigmoid(mg_ref[:, 0:d]) * jnp.dot(hc_ref[...], wc_ref[...], preferred_element_type=f32)
         + jax.nn.sigmoid(mg_ref[:, d:2 * d]) * jnp.dot(on_ref[...], wn_ref[...], preferred_element_type=f32)
         + jax.nn.sigmoid(mg_ref[:, 2 * d:3 * d]) * jnp.dot(og_ref[...], wg_ref[...], preferred_element_type=f32))
    x1 = x_ref[...] + gate_ref[0] * jnp.dot(y.astype(jnp.bfloat16), wo_ref[...], preferred_element_type=f32)
    x1_ref[...] = x1
    u = x1 * lax.rsqrt(jnp.mean(x1 * x1, axis=1, keepdims=True) + EPS) * g2_ref[...] * sc_ref[0] + sh_ref[0]
    ut_ref[...] = u.T.astype(jnp.bfloat16)
    qh_ref[...] = jnp.dot(u.astype(jnp.bfloat16), wq_ref[...], preferred_element_type=f32).astype(qh_ref.dtype)


def mix_out(hc, on, og, z, x2, gate1, wc, wn, wg, wo, g2, scale2p, shift2, wq, S):
    n, d = x2.shape
    tm = MIX_TILE
    per_b = S // tm
    row = lambda w: pl.BlockSpec((tm, w), lambda i: (i, 0))
    mod = lambda: pl.BlockSpec((1, 1, d), lambda i: (i // per_b, 0, 0))
    full = lambda a: pl.BlockSpec(a.shape, lambda i: (0,) * a.ndim, pipeline_mode=pl.Buffered(1))
    nq = wq.shape[1]
    return pl.pallas_call(
        _mix_kernel,
        grid=(n // tm,),
        in_specs=[row(CONV_DIM), row(NSA_Q), row(GLA_V),
                  pl.BlockSpec((tm, N_BRANCH * d), lambda i: (i, Z_MG // (N_BRANCH * d))),
                  row(d), mod(), full(wc), full(wn), full(wg), full(wo),
                  pl.BlockSpec((1, d), lambda i: (0, 0)), mod(), mod(), full(wq)],
        out_specs=[row(d), pl.BlockSpec((d, tm), lambda i: (0, i)), row(nq)],
        out_shape=[jax.ShapeDtypeStruct((n, d), jnp.float32), jax.ShapeDtypeStruct((d, n), jnp.bfloat16),
                   jax.ShapeDtypeStruct((n, nq), jnp.bfloat16)],
        compiler_params=pltpu.CompilerParams(dimension_semantics=("parallel",), vmem_limit_bytes=VMEM_LIMIT),
        name="mix_out",
    )(hc, on, og, z, x2, gate1, wc, wn, wg, wo, g2.reshape(1, d), scale2p, shift2, wq)


def _conv_kernel(a_ref, ah_ref, w_ref, b_ref, g_ref, beta_ref, o_ref, hbuf):
    i = pl.program_id(1)
    ts = a_ref.shape[0]
    a = a_ref[...]
    hbuf[pl.ds(CONV_HALO, ts), :] = a[:, :CONV_DIM] * jax.nn.sigmoid(a[:, CONV_DIM:])
    ah = ah_ref[...]
    halo = ah[:, :CONV_DIM] * jax.nn.sigmoid(ah[:, CONV_DIM:])
    hbuf[pl.ds(0, CONV_HALO), :] = jnp.where(i > 0, halo, 0.0)
    acc = jnp.zeros((ts, CONV_DIM), jnp.float32) + b_ref[...]
    off = CONV_HALO - (CONV_WIDTH - 1)
    for k in range(CONV_WIDTH):
        acc = acc + w_ref[k:k + 1, :] * hbuf[pl.ds(off + k, ts), :]
    mu = jnp.mean(acc, axis=1, keepdims=True)
    d = acc - mu
    var = jnp.mean(d * d, axis=1, keepdims=True)
    y = d * lax.rsqrt(var + EPS) * g_ref[...] + beta_ref[...]
    o_ref[...] = (y * jax.nn.sigmoid(y)).astype(o_ref.dtype)


def conv_pallas(z, dw_w, dw_b, ln_g, ln_b, B, S):
    n = B * S
    ts = CONV_TS
    ns = S // ts
    hb = ts // CONV_HALO
    cb = Z_CONV // (2 * CONV_DIM)
    wpad = jnp.pad(dw_w, ((0, 32 - CONV_WIDTH), (0, 0)))
    vec = lambda: pl.BlockSpec((1, CONV_DIM), lambda b, i: (0, 0))
    return pl.pallas_call(
        _conv_kernel,
        grid=(B, ns),
        in_specs=[pl.BlockSpec((ts, 2 * CONV_DIM), lambda b, i: (b * ns + i, cb)),
                  pl.BlockSpec((CONV_HALO, 2 * CONV_DIM), lambda b, i: (jnp.maximum((b * ns + i) * hb - 1, 0), cb)),
                  pl.BlockSpec((32, CONV_DIM), lambda b, i: (0, 0)), vec(), vec(), vec()],
        out_specs=pl.BlockSpec((ts, CONV_DIM), lambda b, i: (b * ns + i, 0)),
        out_shape=jax.ShapeDtypeStruct((n, CONV_DIM), jnp.bfloat16),
        scratch_shapes=[pltpu.VMEM((ts + CONV_HALO, CONV_DIM), jnp.float32)],
        compiler_params=pltpu.CompilerParams(dimension_semantics=("parallel", "parallel")),
        name="conformer_conv",
    )(z, z, wpad, dw_b.reshape(1, -1), ln_g.reshape(1, -1), ln_b.reshape(1, -1))


def _msoftmax(s, mask):
    s = jnp.where(mask, s, NEG)
    m = jnp.max(s, axis=-1, keepdims=True)
    e = jnp.where(mask, jnp.exp(s - m), 0.0)
    l = jnp.sum(e, axis=-1, keepdims=True)
    return e / jnp.where(l > 0.0, l, 1.0)


def _nsa_kernel(q_ref, g_ref, kct_ref, vc_ref, kst_ref, vs_ref, kwt_ref, vw_ref, c2st_ref, o_ref, *, n_sel):
    G = NSA_GROUP
    qb = pl.program_id(2)
    t0 = pl.multiple_of(qb * TQ, TQ)
    q = q_ref[...]
    lane_g = lax.broadcasted_iota(jnp.int32, (TQ, GW), 1) // HEAD_DIM
    q4 = jnp.concatenate([jnp.where(lane_g == h, q, jnp.zeros_like(q)) for h in range(G)], axis=0)
    tpos = t0 + lax.broadcasted_iota(jnp.int32, (TQ, 1), 0)

    def heads(x):
        return x.reshape(G, TQ, x.shape[-1])

    def attend(p, v):
        return heads(jnp.dot(p.reshape(G * TQ, p.shape[-1]).astype(jnp.bfloat16), v,
                             preferred_element_type=jnp.float32))

    starts = [t0 - WINDOW + TQ * c for c in range(WINDOW // TQ + 1)]
    reads = [pl.multiple_of(jnp.maximum(s, 0), TQ) for s in starts]
    kwt = jnp.concatenate([kwt_ref[0, 0, :, pl.ds(r, TQ)] for r in reads], axis=1)
    vw = jnp.concatenate([vw_ref[0, 0, pl.ds(r, TQ), :] for r in reads], axis=0)
    wk = WINDOW + TQ
    kposw = t0 - WINDOW + lax.broadcasted_iota(jnp.int32, (TQ, wk), 1)
    diff = tpos - kposw
    maskw = (kposw >= 0) & (diff >= 0) & (diff < WINDOW)

    kct = kct_ref[0, 0]
    ncp = kct.shape[1]
    ncol = lax.broadcasted_iota(jnp.int32, (TQ, ncp), 1)
    maskc = (ncol * CMP_STRIDE + (CMP_BLOCK - 1)) <= tpos
    sc = heads(jnp.dot(q4, kct, preferred_element_type=jnp.float32))
    sw = heads(jnp.dot(q4, kwt, preferred_element_type=jnp.float32))
    pc = _msoftmax(sc, maskc[None])
    oc = attend(pc, vc_ref[0, 0])
    pw = _msoftmax(sw, maskw[None])
    ow = attend(pw, vw)
    pc_sum = jnp.sum(pc, axis=0)

    c2st = c2st_ref[...]
    jp = c2st.shape[0]
    hi = pc_sum.astype(jnp.bfloat16)
    lo = (pc_sum - hi.astype(jnp.float32)).astype(jnp.bfloat16)
    dn = (((1,), (1,)), ((), ()))
    imp_t = (lax.dot_general(c2st, hi, dn, preferred_element_type=jnp.float32)
             + lax.dot_general(c2st, lo, dn, preferred_element_type=jnp.float32))
    jrow = lax.broadcasted_iota(jnp.int32, (jp, TQ), 0)
    cur = (t0 + lax.broadcasted_iota(jnp.int32, (jp, TQ), 1)) // SLC_BLOCK
    forced = (jrow == 0) | (jrow == cur) | (jrow == cur - 1)
    score = jnp.where(jrow <= cur, imp_t + jnp.where(forced, FORCE_BONUS, 0.0), NEG)
    nblk = kst_ref.shape[3] // SLC_BLOCK
    nrow = -(-nblk // 8) * 8
    sc, jr = score[:nrow], jrow[:nrow]
    rank = jnp.zeros((nrow, TQ), jnp.int32)
    for i in range(nblk):
        si = sc[i:i + 1, :]
        rank = rank + ((si > sc) | ((si == sc) & (jr > i))).astype(jnp.int32)
    sel = ((rank < n_sel) & (sc > 0.5 * NEG)).astype(jnp.float32)
    if nrow < jp:
        sel = jnp.concatenate([sel, jnp.zeros((jp - nrow, TQ), jnp.float32)], axis=0)
    selb = sel.T.astype(jnp.bfloat16)

    nchunks = (t0 + TQ + KC - 1) // KC

    def chunk(c, carry):
        m_old, l_old, acc = carry
        k0 = pl.multiple_of(c * KC, KC)
        kt = kst_ref[0, 0, :, pl.ds(k0, KC)]
        v = vs_ref[0, 0, pl.ds(k0, KC), :]
        kblk = (k0 + lax.broadcasted_iota(jnp.int32, (jp, KC), 1)) // SLC_BLOCK
        expand = (kblk == lax.broadcasted_iota(jnp.int32, (jp, KC), 0)).astype(jnp.bfloat16)
        member = jnp.dot(selb, expand, preferred_element_type=jnp.float32) > 0.5
        kpos = k0 + lax.broadcasted_iota(jnp.int32, (TQ, KC), 1)
        msk = (member & (kpos <= tpos))[None]
        s = jnp.where(msk, heads(jnp.dot(q4, kt, preferred_element_type=jnp.float32)), NEG)
        m_new = jnp.maximum(m_old, jnp.max(s, axis=-1, keepdims=True))
        alpha = jnp.exp(m_old - m_new)
        p = jnp.exp(s - m_new)
        l_new = alpha * l_old + jnp.sum(p, axis=-1, keepdims=True)
        return m_new, l_new, alpha * acc + attend(p, v)

    def chunk_pair(i, carry):
        return chunk(2 * i + 1, chunk(2 * i, carry))

    init = (jnp.full((G, TQ, 1), NEG, jnp.float32), jnp.zeros((G, TQ, 1), jnp.float32),
            jnp.zeros((G, TQ, GW), jnp.float32))
    _, l_s, acc_s = lax.fori_loop(0, (nchunks + 1) // 2, chunk_pair, init)
    os_ = acc_s / jnp.where(l_s > 0.0, l_s, 1.0)

    sig = jax.nn.sigmoid(g_ref[...])
    out = jnp.zeros((TQ, GW), jnp.float32)
    for h in range(G):
        o_h = (sig[:, 3 * h:3 * h + 1] * oc[h] + sig[:, 3 * h + 1:3 * h + 2] * os_[h]
               + sig[:, 3 * h + 2:3 * h + 3] * ow[h])
        out = out + jnp.where(lane_g == h, o_h, 0.0)
    o_ref[...] = out.astype(o_ref.dtype)


def nsa_core(q, z, kct, vc, kst, vs, kwt, vw, B, S):
    n = B * S
    ncp = kct.shape[3]
    ncmp = (S - CMP_BLOCK) // CMP_STRIDE + 1
    nslc = S // SLC_BLOCK
    n_sel = min(SLC_TOPK, nslc)
    jp = LANE
    assert nslc <= jp and S % (2 * KC) == 0 and S % TQ == 0 and WINDOW % TQ == 0
    cs = np.arange(ncmp) * CMP_STRIDE
    ss = np.arange(nslc) * SLC_BLOCK
    ov = np.minimum(cs[:, None] + CMP_BLOCK, ss[None, :] + SLC_BLOCK) - np.maximum(cs[:, None], ss[None, :])
    c2s = np.zeros((ncp, jp), np.float32)
    c2s[:ncmp, :nslc] = np.clip(ov, 0, None) / CMP_BLOCK
    c2st = jnp.asarray(c2s.T, dtype=jnp.bfloat16)
    nqb = S // TQ

    def kv_spec(shp):
        return pl.BlockSpec((1, 1) + shp, lambda b, k, i: (b, k, 0, 0))

    return pl.pallas_call(
        functools.partial(_nsa_kernel, n_sel=n_sel),
        grid=(B, NSA_KV_HEADS, nqb),
        in_specs=[pl.BlockSpec((TQ, GW), lambda b, k, i: (b * nqb + i, k)),
                  pl.BlockSpec((TQ, LANE), lambda b, k, i: (b * nqb + i, Z_NG // LANE + k)),
                  kv_spec((GW, ncp)), kv_spec((ncp, GW)),
                  kv_spec((GW, S)), kv_spec((S, GW)),
                  kv_spec((GW, S)), kv_spec((S, GW)),
                  pl.BlockSpec((jp, ncp), lambda b, k, i: (0, 0))],
        out_specs=pl.BlockSpec((TQ, GW), lambda b, k, i: (b * nqb + i, k)),
        out_shape=jax.ShapeDtypeStruct((n, NSA_KV_HEADS * GW), jnp.bfloat16),
        compiler_params=pltpu.CompilerParams(dimension_semantics=("parallel", "parallel", "arbitrary"),
                                             vmem_limit_bytes=VMEM_LIMIT),
        name="nsa_core",
    )(q, z, kct, vc, kst, vs, kwt, vw, c2st)


def _gla_consts():
    C = GLA_CHUNK
    i = np.arange(C)
    mats = [np.tril(np.ones((C, C), np.float32))]
    for s in GLA_LEVELS:
        r = (i // (2 * s)) * 2 * s + s - 1
        upper = (i // s) % 2 == 1
        m = np.arange(C)[None, :]
        mq = ((m > r[:, None]) & (m <= i[:, None]) & upper[:, None]).astype(np.float32)
        mk = ((m > i[:, None]) & (m <= r[:, None]) & (~upper)[:, None]).astype(np.float32)
        mats += [mq, mk]
    return np.concatenate(mats, axis=0)


def _split3(x):
    h = x.astype(jnp.bfloat16)
    r = x - h.astype(jnp.float32)
    m = r.astype(jnp.bfloat16)
    l = (r - m.astype(jnp.float32)).astype(jnp.bfloat16)
    return h, m, l


def _gla_kernel(q_ref, k_ref, v_ref, r_ref, glr_ref, gw_ref, gb_ref, ng_ref, msel_ref, o_ref, state_ref):
    C = GLA_CHUNK

    @pl.when(pl.program_id(1) == 0)
    def _():
        state_ref[...] = jnp.zeros_like(state_ref)

    ii = lax.broadcasted_iota(jnp.int32, (C, C), 0)
    jj = lax.broadcasted_iota(jnp.int32, (C, C), 1)
    lane_h = lax.broadcasted_iota(jnp.int32, (C, GLA_QK), 1) // GLA_DK
    bd = (lax.broadcasted_iota(jnp.int32, (GLA_QK, GLA_V), 0) // GLA_DK
          == lax.broadcasted_iota(jnp.int32, (GLA_QK, GLA_V), 1) // GLA_DV)
    msel = msel_ref[...]
    gw = gw_ref[...].astype(jnp.bfloat16)
    dn = (((1,), (1,)), ((), ()))

    def chunk(c, carry):
        r0 = pl.multiple_of(c * C, C)
        q = q_ref[pl.ds(r0, C), :] * (GLA_DK ** -0.5)
        k = k_ref[pl.ds(r0, C), :]
        v = v_ref[pl.ds(r0, C), :].astype(jnp.bfloat16)
        z = jnp.dot(glr_ref[pl.ds(r0, C), :].astype(jnp.bfloat16), gw, preferred_element_type=jnp.float32) + gb_ref[...]
        a = jax.nn.log_sigmoid(z) / GLA_GATE_TEMP
        ah, am, al = _split3(a)
        ex = (jnp.dot(msel, ah, preferred_element_type=jnp.float32)
              + jnp.dot(msel, am, preferred_element_type=jnp.float32)
              + jnp.dot(msel, al, preferred_element_type=jnp.float32))
        bcum = ex[0:C]
        state = state_ref[...]
        o = jnp.dot((q * jnp.exp(bcum)).astype(jnp.bfloat16), state.astype(jnp.bfloat16),
                    preferred_element_type=jnp.float32)
        kb = k.astype(jnp.bfloat16)
        for h in range(GLA_HEADS):
            hm = lane_h == h
            att = jnp.where(ii == jj, lax.dot_general(jnp.where(hm, q, 0.0).astype(jnp.bfloat16), kb, dn,
                                                      preferred_element_type=jnp.float32), 0.0)
            for li, s in enumerate(GLA_LEVELS):
                eq = jnp.exp(ex[(1 + 2 * li) * C:(2 + 2 * li) * C])
                ek = jnp.exp(ex[(2 + 2 * li) * C:(3 + 2 * li) * C])
                qs = jnp.where(hm, q * eq, 0.0).astype(jnp.bfloat16)
                ks = (k * ek).astype(jnp.bfloat16)
                blk = (ii // (2 * s) == jj // (2 * s)) & ((ii // s) % 2 == 1) & ((jj // s) % 2 == 0)
                att = att + jnp.where(blk, lax.dot_general(qs, ks, dn, preferred_element_type=jnp.float32), 0.0)
            oh = jnp.dot(att.astype(jnp.bfloat16), v, preferred_element_type=jnp.float32)
            o = o + jnp.where(lax.broadcasted_iota(jnp.int32, (C, GLA_V), 1) // GLA_DV == h, oh, 0.0)
        blast = bcum[C - 1:C, :]
        kd_t = (k * jnp.exp(blast - bcum)).T.astype(jnp.bfloat16)
        decay_col = jnp.exp(jnp.sum(a.T, axis=1, keepdims=True))
        upd = jnp.dot(kd_t, v, preferred_element_type=jnp.float32)
        state_ref[...] = state * decay_col + jnp.where(bd, upd, 0.0)
        outs = []
        for h in range(GLA_HEADS):
            oh = o[:, h * GLA_DV:(h + 1) * GLA_DV]
            outs.append(oh * lax.rsqrt(jnp.mean(oh * oh, axis=1, keepdims=True) + EPS) * ng_ref[...])
        on = jnp.concatenate(outs, axis=1)
        r = r_ref[pl.ds(r0, C), :]
        o_ref[pl.ds(r0, C), :] = (on * (r * jax.nn.sigmoid(r))).astype(o_ref.dtype)
        return carry

    lax.fori_loop(0, q_ref.shape[0] // C, chunk, 0)


def gla_pallas(z, gate_w, gate_b, norm_g, B, S):
    n = B * S
    tb = GLA_TB
    nb = S // tb
    gw = jnp.pad(gate_w, ((0, LANE - gate_w.shape[0]), (0, 0)))
    msel = jnp.asarray(_gla_consts(), dtype=jnp.bfloat16)
    zb = lambda col, w: pl.BlockSpec((tb, w), lambda b, i: (b * nb + i, col // w))
    full = lambda shp: pl.BlockSpec(shp, lambda b, i: (0,) * len(shp))
    return pl.pallas_call(
        _gla_kernel,
        grid=(B, nb),
        in_specs=[zb(Z_GQ, GLA_QK), zb(Z_GK, GLA_QK), zb(Z_GV, GLA_V), zb(Z_GR, GLA_V), zb(Z_GLR, LANE),
                  full((LANE, GLA_QK)), full((1, GLA_QK)), full((1, GLA_DV)), full(msel.shape)],
        out_specs=pl.BlockSpec((tb, GLA_V), lambda b, i: (b * nb + i, 0)),
        out_shape=jax.ShapeDtypeStruct((n, GLA_V), jnp.bfloat16),
        scratch_shapes=[pltpu.VMEM((GLA_QK, GLA_V), jnp.float32)],
        compiler_params=pltpu.CompilerParams(dimension_semantics=("parallel", "arbitrary")),
        name="gla",
    )(z, z, z, z, z, gw, gate_b.reshape(1, -1), norm_g.reshape(1, -1), msel)


_FULL_BLOCKS = 8


def _extract_top(s, k, tie_break):
    n, t = s.shape
    row = lax.broadcasted_iota(jnp.int32, (n, t), 0)
    rank = jnp.full((n, t), k, jnp.int32)
    vals = []
    for r in range(k):
        m = jnp.max(s, axis=0, keepdims=True)
        hit = s == m
        if tie_break:
            hit = row == jnp.min(jnp.where(hit, row, n), axis=0, keepdims=True)
        rank = jnp.where(hit, r, rank)
        s = jnp.where(hit, float('-inf'), s)
        vals.append(m)
    return jnp.concatenate(vals, axis=0), rank


def _peer_gate_math(s1, s2, tie_break):
    K = PEER_TOPK
    t = s1.shape[1]
    v1, rank1 = _extract_top(s1, K, tie_break)
    v2, rank2 = _extract_top(s2, K, tie_break)
    blocks, poss = [], []
    row16 = lax.broadcasted_iota(jnp.int32, (K, t), 0)
    for r1 in range(_FULL_BLOCKS):
        c = v1[r1:r1 + 1, :] + v2
        blocks.append(jnp.where(row16 < K // (r1 + 1), c, float('-inf')))
        poss.append(row16 + r1 * K)
    row8 = lax.broadcasted_iota(jnp.int32, (K - _FULL_BLOCKS, t), 0)
    blocks.append(v1[_FULL_BLOCKS:, :] + v2[0:1, :])
    poss.append((row8 + _FULL_BLOCKS) * K)
    sels = [jnp.zeros(b.shape, jnp.bool_) for b in blocks]
    big = K * K
    for _ in range(K):
        m = blocks[0].max(axis=0, keepdims=True)
        for b in blocks[1:]:
            m = jnp.maximum(m, b.max(axis=0, keepdims=True))
        if tie_break:
            first = None
            for b, p in zip(blocks, poss):
                f = jnp.min(jnp.where(b == m, p, big), axis=0, keepdims=True)
                first = f if first is None else jnp.minimum(first, f)
            hits = [p == first for p in poss]
        else:
            hits = [b == m for b in blocks]
        for i, hit in enumerate(hits):
            sels[i] = sels[i] | hit
            blocks[i] = jnp.where(hit, float('-inf'), blocks[i])
    top = v1[0:1, :] + v2[0:1, :]
    cnt_rows = []
    z = jnp.zeros((1, t), jnp.float32)
    for r1 in range(_FULL_BLOCKS):
        sel = sels[r1]
        cnt_rows.append(jnp.sum(sel.astype(jnp.float32), axis=0, keepdims=True))
        c = v1[r1:r1 + 1, :] + v2
        z = z + jnp.sum(jnp.where(sel, jnp.exp(c - top), 0.0), axis=0, keepdims=True)
    sel = sels[_FULL_BLOCKS]
    cnt_tail = sel.astype(jnp.float32)
    c = v1[_FULL_BLOCKS:, :] + v2[0:1, :]
    z = z + jnp.sum(jnp.where(sel, jnp.exp(c - top), 0.0), axis=0, keepdims=True)
    cnt = jnp.concatenate(cnt_rows + [cnt_tail], axis=0)
    a = jnp.zeros(s1.shape, jnp.float32)
    for r in range(K):
        a = jnp.where(rank1 == r, cnt[r:r + 1, :], a)
    taken = (jnp.sum((rank1 < K).astype(jnp.float32), axis=0, keepdims=True),
             jnp.sum((rank2 < K).astype(jnp.float32), axis=0, keepdims=True),
             jnp.sum(cnt, axis=0, keepdims=True))
    return jnp.exp(s1 - v1[0:1, :]), a, rank2.astype(jnp.float32), jnp.exp(s2 - v2[0:1, :]) / z, taken


def _peer_gate_kernel(q_ref, keys_ref, e1_ref, a_ref, r2_ref, e2_ref):
    q = q_ref[...].astype(jnp.bfloat16)
    k1 = keys_ref[0, 0].astype(jnp.bfloat16)
    k2 = keys_ref[0, 1].astype(jnp.bfloat16)
    dn = (((1,), (1,)), ((), ()))
    s1 = lax.dot_general(k1, q[:, :PEER_DKEY // 2], dn, preferred_element_type=jnp.float32)
    s2 = lax.dot_general(k2, q[:, PEER_DKEY // 2:], dn, preferred_element_type=jnp.float32)

    def write(e1, a, r2, e2):
        e1_ref[0] = e1
        a_ref[0] = a
        r2_ref[0] = r2.astype(jnp.bfloat16)
        e2_ref[0] = e2.astype(jnp.bfloat16)

    e1, a, r2, e2, taken = _peer_gate_math(s1, s2, tie_break=False)
    write(e1, a, r2, e2)
    excess = sum(jnp.max(jnp.abs(c - PEER_TOPK)) for c in taken)

    @pl.when(excess > 0.0)
    def _():
        write(*_peer_gate_math(s1, s2, tie_break=True)[:4])


def peer_gates(qh, keys, tt=512):
    n = qh.shape[0]
    H = PEER_HEADS
    shp32 = jax.ShapeDtypeStruct((H, PEER_NKEYS, n), jnp.float32)
    shp16 = jax.ShapeDtypeStruct((H, PEER_NKEYS, n), jnp.bfloat16)
    ospec = pl.BlockSpec((1, PEER_NKEYS, tt), lambda i, h: (h, 0, i))
    return pl.pallas_call(
        _peer_gate_kernel,
        grid=(n // tt, H),
        in_specs=[pl.BlockSpec((tt, PEER_DKEY), lambda i, h: (i, h)),
                  pl.BlockSpec((1, 2, PEER_NKEYS, PEER_DKEY // 2), lambda i, h: (h, 0, 0, 0))],
        out_specs=[ospec, ospec, ospec, ospec],
        out_shape=[shp32, shp32, shp16, shp16],
        compiler_params=pltpu.CompilerParams(dimension_semantics=("parallel", "parallel")),
        name="peer_gates",
    )(qh, keys)


def _row_bf16(row, n):
    t = row.shape[1]
    tile = jnp.broadcast_to(row, (16, t)).astype(jnp.bfloat16)
    return jnp.concatenate([tile] * (n // 16), axis=0)


def _peer_dense_kernel(xt_ref, u_ref, vt_ref, e1_ref, a_ref, r2_ref, e2_ref, x1_ref, gate_ref, fg_ref,
                       o_ref, acc_ref, *, ec, final):
    j = pl.program_id(1)
    nsub = ec // PEER_NKEYS

    @pl.when(j == 0)
    def _():
        acc_ref[...] = jnp.zeros_like(acc_ref)

    assert nsub % 8 == 0
    r8 = pl.multiple_of(j * nsub, 8)
    a_tiles = [[a_ref[h, pl.ds(r8 + 8 * t, 8), :] for t in range(nsub // 8)] for h in range(PEER_HEADS)]
    e1_tiles = [[e1_ref[h, pl.ds(r8 + 8 * t, 8), :] for t in range(nsub // 8)] for h in range(PEER_HEADS)]

    def gate_mat(i):
        g = None
        for h in range(PEER_HEADS):
            cnt = _row_bf16(a_tiles[h][i // 8][i % 8:i % 8 + 1, :], PEER_NKEYS)
            e1 = _row_bf16(e1_tiles[h][i // 8][i % 8:i % 8 + 1, :], PEER_NKEYS)
            term = jnp.where(r2_ref[h] < cnt, e2_ref[h], jnp.bfloat16(0)) * e1
            g = term if g is None else g + term
        return g

    gs = PEER_GROUPS_PER_SCORE

    def scores(p):
        r0 = p * gs * PEER_NKEYS
        return jnp.dot(u_ref[r0:r0 + gs * PEER_NKEYS, :], xt_ref[...], preferred_element_type=jnp.float32)

    st = scores(0)
    ws = []
    for p in range(nsub // gs):
        st_next = scores(p + 1) if (p + 1) * gs < nsub else None
        for k in range(gs):
            i = p * gs + k
            ws.append(_gelu(st[k * PEER_NKEYS:(k + 1) * PEER_NKEYS].astype(jnp.bfloat16)) * gate_mat(i))
            if i % PEER_GROUPS_PER_DOT == PEER_GROUPS_PER_DOT - 1:
                w = jnp.concatenate(ws, axis=0)
                ws = []
                acc_ref[...] += jnp.dot(
                    vt_ref[:, (i + 1 - PEER_GROUPS_PER_DOT) * PEER_NKEYS:(i + 1) * PEER_NKEYS], w,
                    preferred_element_type=jnp.float32)
        st = st_next

    @pl.when(j == pl.num_programs(1) - 1)
    def _():
        x2 = x1_ref[...] + gate_ref[0] * acc_ref[...].T
        if final:
            x2 = x2 * lax.rsqrt(jnp.mean(x2 * x2, axis=1, keepdims=True) + EPS) * fg_ref[...]
        o_ref[...] = x2


def peer_dense(xt, u16, vt16, e1, a, r2, e2, x1, gate2, final_g, S, final, tt=512, ec=2048):
    d, n = xt.shape
    ne = u16.shape[0]
    H = PEER_HEADS
    per_b = S // tt
    gspec = lambda: pl.BlockSpec((H, PEER_NKEYS, tt), lambda i, j: (0, 0, i))
    return pl.pallas_call(
        functools.partial(_peer_dense_kernel, ec=ec, final=final),
        grid=(n // tt, ne // ec),
        in_specs=[pl.BlockSpec((d, tt), lambda i, j: (0, i)),
                  pl.BlockSpec((ec, d), lambda i, j: (j, 0)),
                  pl.BlockSpec((d, ec), lambda i, j: (0, j)),
                  gspec(), gspec(), gspec(), gspec(),
                  pl.BlockSpec((tt, d), lambda i, j: (i, 0)),
                  pl.BlockSpec((1, 1, d), lambda i, j: (i // per_b, 0, 0)),
                  pl.BlockSpec((1, d), lambda i, j: (0, 0))],
        out_specs=pl.BlockSpec((tt, d), lambda i, j: (i, 0)),
        out_shape=jax.ShapeDtypeStruct((n, d), jnp.float32),
        scratch_shapes=[pltpu.VMEM((d, tt), jnp.float32)],
        compiler_params=pltpu.CompilerParams(dimension_semantics=("parallel", "arbitrary"),
                                             vmem_limit_bytes=VMEM_LIMIT),
        name="peer_dense",
    )(xt, u16, vt16, e1, a, r2, e2, x1, gate2, final_g.reshape(1, d))


def adaln(c, w, b):
    mod = pmm(jax.nn.silu(c), w) + b
    shift, scale, gate = jnp.split(mod[:, None, :], 3, axis=-1)
    return shift, 1.0 + scale, gate


def kernel(x, c, positions, ada_w, ada_b, norm_g, w_in, conv_dw_w, conv_dw_b, conv_ln_g, conv_ln_b, w_conv_up,
           cmp_pos, cmp_w1, cmp_w2, w_nsa_up, gla_gate_w, gla_gate_b, gla_norm_g, w_gla_up, w_out,
           peer_wq, peer_keys, peer_u, peer_v, final_g):
    B, S, D = x.shape
    bf = jnp.bfloat16
    x2 = x.reshape(B * S, D)
    cs = rope_tables(positions)
    for l in range(DEPTH):
        shift, scale1p, gate = adaln(c, ada_w[l, 0], ada_b[l, 0])
        z = in_proj(x2, norm_g[l, 0], scale1p, shift, pack_w_in(w_in[l]), S)
        hc = conv_pallas(z, conv_dw_w[l], conv_dw_b[l], conv_ln_g[l], conv_ln_b[l], B, S)
        q, kst, vs, kwt, vw = nsa_prep(z, cs, B, S)
        kct, vc = nsa_compress(z, positions, cmp_pos[l], cmp_w1[l], cmp_w2[l], B, S)
        on = nsa_core(q, z, kct, vc, kst, vs, kwt, vw, B, S)
        og = gla_pallas(z, gla_gate_w[l], gla_gate_b[l], gla_norm_g[l], B, S)
        shift2, scale2p, gate2 = adaln(c, ada_w[l, 1], ada_b[l, 1])
        x1, ut, qh = mix_out(hc, on, og, z, x2, gate, w_conv_up[l].astype(bf), w_nsa_up[l].astype(bf),
                             w_gla_up[l].astype(bf), w_out[l].astype(bf), norm_g[l, 1], scale2p, shift2,
                             peer_wq[l].astype(bf), S)
        e1, a, r2, e2 = peer_gates(qh, peer_keys[l])
        x2 = peer_dense(ut, peer_u[l].astype(bf), peer_v[l].astype(bf).T, e1, a, r2, e2, x1, gate2, final_g,
                        S, final=(l == DEPTH - 1))
    return x2.reshape(B, S, D)
```

```python
import functools

import jax
import jax.numpy as jnp
import numpy as np
from jax import lax
from jax.experimental import pallas as pl
from jax.experimental.pallas import tpu as pltpu

D_MODEL = 1024
DEPTH = 2
EPS = 1e-6
NEG = -1e30
N_BRANCH = 3
CONV_DIM = 512
CONV_WIDTH = 31
NSA_HEADS = 8
NSA_KV_HEADS = 2
NSA_GROUP = NSA_HEADS // NSA_KV_HEADS
HEAD_DIM = 64
CMP_BLOCK = 32
CMP_STRIDE = 16
CMP_HIDDEN = 256
SLC_BLOCK = 64
SLC_TOPK = 16
WINDOW = 512
FORCE_BONUS = 1e4
ROPE_THETA = 10000.0
GLA_HEADS = 4
GLA_DK = 64
GLA_DV = 128
GLA_GATE_RANK = 16
GLA_GATE_TEMP = 16.0
GLA_CHUNK = 256
PEER_HEADS = 8
PEER_NKEYS = 128
PEER_DKEY = 256
PEER_TOPK = 16
NSA_Q = NSA_HEADS * HEAD_DIM
NSA_KV = 3 * 2 * NSA_KV_HEADS * HEAD_DIM
NSA_G = 3 * NSA_HEADS
GLA_QK = GLA_HEADS * GLA_DK
GLA_V = GLA_HEADS * GLA_DV
SPLITS = [2 * CONV_DIM, NSA_Q, NSA_KV, NSA_G, GLA_QK, GLA_QK, GLA_V, GLA_V, GLA_GATE_RANK, N_BRANCH * D_MODEL]

LANE = 128
TQ = 128
KC = 512
GW = NSA_GROUP * HEAD_DIM
VMEM_LIMIT = 48 * 1024 * 1024
GLA_LEVELS = tuple(GLA_CHUNK >> s for s in range(1, GLA_CHUNK.bit_length()))
GLA_TB = 512
CONV_TS = 512
CONV_HALO = 32
ROW_TILE = 1024
MIX_TILE = 512
PEER_GROUPS_PER_DOT = 2
PEER_GROUPS_PER_SCORE = 8

Z_MG = 0
Z_CONV = Z_MG + N_BRANCH * D_MODEL
Z_Q = Z_CONV + 2 * CONV_DIM
Z_QSW = Z_Q + NSA_Q
Z_GV = Z_QSW + NSA_Q
Z_GR = Z_GV + GLA_V
Z_GQ = Z_GR + GLA_V
Z_GK = Z_GQ + GLA_QK
Z_CMP = Z_GK + GLA_QK
Z_SLC = Z_CMP + 256
Z_WIN = Z_SLC + 384
Z_NG = Z_WIN + 384
Z_GLR = Z_NG + NSA_KV_HEADS * LANE
NZ = Z_GLR + LANE
assert Z_SLC % 384 == 0 and Z_WIN % 384 == 0 and Z_CONV % (2 * CONV_DIM) == 0 and Z_GQ % GLA_QK == 0


def _mm_kernel(a_ref, b_ref, o_ref):
    o_ref[...] = jnp.dot(a_ref[...].astype(jnp.bfloat16), b_ref[...].astype(jnp.bfloat16),
                         preferred_element_type=jnp.float32)


def _pick(n, cands):
    for c in cands:
        if n % c == 0:
            return c
    return n


def pmm(a, b):
    m, k = a.shape
    n = b.shape[1]
    n_pad = -n % LANE
    if n_pad:
        b = jnp.pad(b, ((0, 0), (0, n_pad)))
    m_pad = -m % 8
    if m_pad:
        a = jnp.pad(a, ((0, m_pad), (0, 0)))
    mp, np_ = m + m_pad, n + n_pad
    tm = _pick(mp, (512, 256, 128, 64, 32, 16, 8))
    tn = _pick(np_, (512, 640, 384, 256, 128))
    out = pl.pallas_call(
        _mm_kernel,
        grid=(mp // tm, np_ // tn),
        in_specs=[pl.BlockSpec((tm, k), lambda i, j: (i, 0)),
                  pl.BlockSpec((k, tn), lambda i, j: (0, j))],
        out_specs=pl.BlockSpec((tm, tn), lambda i, j: (i, j)),
        out_shape=jax.ShapeDtypeStruct((mp, np_), jnp.float32),
        compiler_params=pltpu.CompilerParams(dimension_semantics=("parallel", "parallel")),
        name="pmm",
    )(a, b)
    return out[:m, :n]


def _in_proj_kernel(x_ref, g_ref, sc_ref, sh_ref, w_ref, z_ref, u_ref):
    @pl.when(pl.program_id(1) == 0)
    def _():
        x = x_ref[...]
        y = x * lax.rsqrt(jnp.mean(x * x, axis=1, keepdims=True) + EPS) * g_ref[...]
        u_ref[...] = (y * sc_ref[0] + sh_ref[0]).astype(jnp.bfloat16)

    z_ref[...] = jnp.dot(u_ref[...], w_ref[...], preferred_element_type=jnp.float32).astype(z_ref.dtype)


def in_proj(x2, g, scale1p, shift, w16, S):
    n, d = x2.shape
    nz = w16.shape[1]
    tm = ROW_TILE
    tn = _pick(nz, (1152, 1024, 896, 512, 384, 256, 128))
    per_b = S // tm
    mod = lambda: pl.BlockSpec((1, 1, d), lambda i, j: (i // per_b, 0, 0))
    return pl.pallas_call(
        _in_proj_kernel,
        grid=(n // tm, nz // tn),
        in_specs=[pl.BlockSpec((tm, d), lambda i, j: (i, 0)),
                  pl.BlockSpec((1, d), lambda i, j: (0, 0)), mod(), mod(),
                  pl.BlockSpec((d, tn), lambda i, j: (0, j))],
        out_specs=pl.BlockSpec((tm, tn), lambda i, j: (i, j)),
        out_shape=jax.ShapeDtypeStruct((n, nz), jnp.bfloat16),
        scratch_shapes=[pltpu.VMEM((tm, d), jnp.bfloat16)],
        compiler_params=pltpu.CompilerParams(dimension_semantics=("parallel", "arbitrary"),
                                             vmem_limit_bytes=VMEM_LIMIT),
        name="in_proj",
    )(x2, g.reshape(1, d), scale1p, shift, w16)


def _half_swap(n_heads):
    idx = np.arange(n_heads * HEAD_DIM).reshape(n_heads, 2, HEAD_DIM // 2)
    return idx[:, ::-1, :].reshape(-1)


def pack_w_in(w_in):
    o = np.cumsum([0] + SPLITS)
    a_conv, nq, nkv, ng, gq, gk, gv, gr, glr, mg = [w_in[:, o[i]:o[i + 1]] for i in range(len(SPLITS))]
    kvw = 2 * NSA_KV_HEADS * HEAD_DIM
    kw = NSA_KV_HEADS * HEAD_DIM
    d = w_in.shape[0]
    sw = _half_swap(NSA_KV_HEADS)
    cols = [mg, a_conv, nq, nq[:, _half_swap(NSA_HEADS)], gv, gr, gq, gk, nkv[:, 0:kvw]]
    for br in (1, 2):
        k = nkv[:, br * kvw:br * kvw + kw]
        cols += [k, k[:, sw], nkv[:, br * kvw + kw:(br + 1) * kvw]]
    gpk = NSA_GROUP * 3
    for kvh in range(NSA_KV_HEADS):
        cols += [ng[:, kvh * gpk:(kvh + 1) * gpk], jnp.zeros((d, LANE - gpk), w_in.dtype)]
    cols += [glr, jnp.zeros((d, LANE - GLA_GATE_RANK), w_in.dtype)]
    w = jnp.concatenate(cols, axis=1)
    assert w.shape[1] == NZ
    return w.astype(jnp.bfloat16)


def rope_tables(positions):
    half = HEAD_DIM // 2
    freq = ROPE_THETA ** (-jnp.arange(half, dtype=jnp.float32) / half)
    ang = positions.astype(jnp.float32)[..., None] * freq
    cos, sin = jnp.cos(ang), jnp.sin(ang)
    c = jnp.tile(jnp.concatenate([cos, cos], -1), (1, 1, LANE // HEAD_DIM))
    s = jnp.tile(jnp.concatenate([-sin, sin], -1), (1, 1, LANE // HEAD_DIM))
    return jnp.concatenate([c, s], -1).reshape(-1, 2 * LANE)


def _nsa_prep_kernel(q_ref, qs_ref, slc_ref, win_ref, cs_ref, sel_ref, qo_ref, kst_ref, vs_ref, kwt_ref, vw_ref):
    cos = cs_ref[:, :LANE]
    sin = cs_ref[:, LANE:]
    scale = HEAD_DIM ** -0.5
    for t in range(NSA_Q // LANE):
        sl = slice(t * LANE, (t + 1) * LANE)
        qo_ref[:, sl] = ((q_ref[:, sl] * cos + qs_ref[:, sl] * sin) * scale).astype(jnp.bfloat16)
    dn = (((1,), (1,)), ((), ()))
    for src, kt_ref, v_ref in ((slc_ref, kst_ref, vs_ref), (win_ref, kwt_ref, vw_ref)):
        k = (src[:, 0:LANE] * cos + src[:, LANE:2 * LANE] * sin).astype(jnp.bfloat16)
        v = src[:, 2 * LANE:3 * LANE].astype(jnp.bfloat16)
        for kvh in range(NSA_KV_HEADS):
            sel = sel_ref[kvh]
            kt_ref[0, kvh] = lax.dot_general(sel, k, dn, preferred_element_type=jnp.float32).astype(jnp.bfloat16)
            v_ref[0, kvh] = lax.dot_general(v, sel, dn, preferred_element_type=jnp.float32).astype(jnp.bfloat16)


def _head_repeat_sel():
    sel = np.zeros((NSA_KV_HEADS, GW, LANE), np.float32)
    for kvh in range(NSA_KV_HEADS):
        for r in range(GW):
            sel[kvh, r, kvh * HEAD_DIM + r % HEAD_DIM] = 1.0
    return jnp.asarray(sel, dtype=jnp.bfloat16)


def nsa_prep(z, cs, B, S):
    n = B * S
    tp = ROW_TILE
    nb = S // tp
    zb = lambda col, w: pl.BlockSpec((tp, w), lambda b, i: (b * nb + i, col // w))
    kt = lambda: pl.BlockSpec((1, NSA_KV_HEADS, GW, tp), lambda b, i: (b, 0, 0, i))
    vv = lambda: pl.BlockSpec((1, NSA_KV_HEADS, tp, GW), lambda b, i: (b, 0, i, 0))
    kt_shape = jax.ShapeDtypeStruct((B, NSA_KV_HEADS, GW, S), jnp.bfloat16)
    v_shape = jax.ShapeDtypeStruct((B, NSA_KV_HEADS, S, GW), jnp.bfloat16)
    return pl.pallas_call(
        _nsa_prep_kernel,
        grid=(B, nb),
        in_specs=[zb(Z_Q, NSA_Q), zb(Z_QSW, NSA_Q), zb(Z_SLC, 384), zb(Z_WIN, 384),
                  pl.BlockSpec((tp, 2 * LANE), lambda b, i: (b * nb + i, 0)),
                  pl.BlockSpec((NSA_KV_HEADS, GW, LANE), lambda b, i: (0, 0, 0))],
        out_specs=[pl.BlockSpec((tp, NSA_Q), lambda b, i: (b * nb + i, 0)), kt(), vv(), kt(), vv()],
        out_shape=[jax.ShapeDtypeStruct((n, NSA_Q), jnp.bfloat16), kt_shape, v_shape, kt_shape, v_shape],
        compiler_params=pltpu.CompilerParams(dimension_semantics=("parallel", "parallel")),
        name="nsa_prep",
    )(z, z, z, z, cs, _head_repeat_sel())


def _gelu(x):
    return 0.5 * x * (1.0 + jnp.tanh(0.7978845608028654 * (x + 0.044715 * x * x * x)))


def _compress_kernel(gk_ref, gv_ref, pe_ref, w1_ref, w2_ref, cs_ref, kct_ref, vc_ref):
    nrow = gk_ref.shape[2]
    outs = []
    for kv, g_ref in enumerate((gk_ref, gv_ref)):
        g = g_ref[0, 0]
        lo = jnp.dot((g + pe_ref[kv, 0]).astype(jnp.bfloat16), w1_ref[kv, 0], preferred_element_type=jnp.float32)
        hi = jnp.dot((g + pe_ref[kv, 1]).astype(jnp.bfloat16), w1_ref[kv, 1], preferred_element_type=jnp.float32)
        hid = _gelu(lo + pltpu.roll(hi, nrow - 1, 0)).astype(jnp.bfloat16)
        outs.append(jnp.dot(hid, w2_ref[kv], preferred_element_type=jnp.float32))
    k = outs[0][:, :GW] * cs_ref[0, :, :GW] + outs[0][:, GW:] * cs_ref[0, :, GW:]
    kct_ref[0, 0] = k.T.astype(jnp.bfloat16)
    vc_ref[0, 0] = outs[1][:, :GW].astype(jnp.bfloat16)


def nsa_compress(z, positions, cmp_pos, cmp_w1, cmp_w2, B, S):
    ng = S // CMP_STRIDE
    grp = CMP_STRIDE * HEAD_DIM
    c = z[:, Z_CMP:Z_CMP + 256].astype(jnp.float32).reshape(B, ng, CMP_STRIDE, 2, NSA_KV_HEADS, HEAD_DIM)
    g = c.transpose(3, 0, 4, 1, 2, 5).reshape(2, B, NSA_KV_HEADS, ng, grp)
    pe = cmp_pos.reshape(2, 2, 1, grp)
    w1 = cmp_w1.reshape(2, 2, grp, CMP_HIDDEN).astype(jnp.bfloat16)
    rep = jnp.tile(cmp_w2, (1, 1, NSA_GROUP))
    sw = np.tile(_half_swap(1), NSA_GROUP) + np.repeat(np.arange(NSA_GROUP) * HEAD_DIM, HEAD_DIM)
    w2 = jnp.concatenate([rep, rep[:, :, sw]], axis=-1).astype(jnp.bfloat16)
    end = jnp.minimum(jnp.arange(ng) * CMP_STRIDE + CMP_BLOCK - 1, S - 1)
    half = HEAD_DIM // 2
    freq = ROPE_THETA ** (-jnp.arange(half, dtype=jnp.float32) / half)
    ang = positions[:, end].astype(jnp.float32)[..., None] * freq
    cos, sin = jnp.cos(ang), jnp.sin(ang)
    cs = jnp.concatenate([jnp.tile(jnp.concatenate([cos, cos], -1), (1, 1, NSA_GROUP)),
                          jnp.tile(jnp.concatenate([-sin, sin], -1), (1, 1, NSA_GROUP))], -1)
    gspec = lambda: pl.BlockSpec((1, 1, ng, grp), lambda b, k: (b, k, 0, 0))
    return pl.pallas_call(
        _compress_kernel,
        grid=(B, NSA_KV_HEADS),
        in_specs=[gspec(), gspec(),
                  pl.BlockSpec((2, 2, 1, grp), lambda b, k: (0, 0, 0, 0)),
                  pl.BlockSpec((2, 2, grp, CMP_HIDDEN), lambda b, k: (0, 0, 0, 0)),
                  pl.BlockSpec((2, CMP_HIDDEN, 2 * GW), lambda b, k: (0, 0, 0)),
                  pl.BlockSpec((1, ng, 2 * GW), lambda b, k: (b, 0, 0))],
        out_specs=[pl.BlockSpec((1, 1, GW, ng), lambda b, k: (b, k, 0, 0)),
                   pl.BlockSpec((1, 1, ng, GW), lambda b, k: (b, k, 0, 0))],
        out_shape=[jax.ShapeDtypeStruct((B, NSA_KV_HEADS, GW, ng), jnp.bfloat16),
                   jax.ShapeDtypeStruct((B, NSA_KV_HEADS, ng, GW), jnp.bfloat16)],
        compiler_params=pltpu.CompilerParams(dimension_semantics=("parallel", "parallel")),
        name="nsa_compress",
    )(g[0], g[1], pe, w1, w2, cs)


def _mix_kernel(hc_ref, on_ref, og_ref, mg_ref, x_ref, gate_ref, wc_ref, wn_ref, wg_ref, wo_ref,
                g2_ref, sc_ref, sh_ref, wq_ref, x1_ref, ut_ref, qh_ref):
    d = D_MODEL
    f32 = jnp.float32
    y = (jax.nn.sigmoid(mg_ref[:, 0:d].astype(f32)) * jnp.dot(hc_ref[...], wc_ref[...], preferred_element_type=f32)
         + jax.nn.sigmoid(mg_ref[:, d:2 * d].astype(f32)) * jnp.dot(on_ref[...], wn_ref[...], preferred_element_type=f32)
         + jax.nn.sigmoid(mg_ref[:, 2 * d:3 * d].astype(f32))
         * jnp.dot(og_ref[...], wg_ref[...], preferred_element_type=f32))
    x1 = x_ref[...] + gate_ref[0] * jnp.dot(y.astype(jnp.bfloat16), wo_ref[...], preferred_element_type=f32)
    x1_ref[...] = x1
    u = x1 * lax.rsqrt(jnp.mean(x1 * x1, axis=1, keepdims=True) + EPS) * g2_ref[...] * sc_ref[0] + sh_ref[0]
    ut_ref[...] = u.T.astype(jnp.bfloat16)
    qh_ref[...] = jnp.dot(u.astype(jnp.bfloat16), wq_ref[...], preferred_element_type=f32).astype(qh_ref.dtype)


def mix_out(hc, on, og, z, x2, gate1, wc, wn, wg, wo, g2, scale2p, shift2, wq, S):
    n, d = x2.shape
    tm = MIX_TILE
    per_b = S // tm
    row = lambda w: pl.BlockSpec((tm, w), lambda i: (i, 0))
    mod = lambda: pl.BlockSpec((1, 1, d), lambda i: (i // per_b, 0, 0))
    full = lambda a: pl.BlockSpec(a.shape, lambda i: (0,) * a.ndim, pipeline_mode=pl.Buffered(1))
    nq = wq.shape[1]
    return pl.pallas_call(
        _mix_kernel,
        grid=(n // tm,),
        in_specs=[row(CONV_DIM), row(NSA_Q), row(GLA_V),
                  pl.BlockSpec((tm, N_BRANCH * d), lambda i: (i, Z_MG // (N_BRANCH * d))),
                  row(d), mod(), full(wc), full(wn), full(wg), full(wo),
                  pl.BlockSpec((1, d), lambda i: (0, 0)), mod(), mod(), full(wq)],
        out_specs=[row(d), pl.BlockSpec((d, tm), lambda i: (0, i)), row(nq)],
        out_shape=[jax.ShapeDtypeStruct((n, d), jnp.float32), jax.ShapeDtypeStruct((d, n), jnp.bfloat16),
                   jax.ShapeDtypeStruct((n, nq), jnp.bfloat16)],
        compiler_params=pltpu.CompilerParams(dimension_semantics=("parallel",), vmem_limit_bytes=VMEM_LIMIT),
        name="mix_out",
    )(hc, on, og, z, x2, gate1, wc, wn, wg, wo, g2.reshape(1, d), scale2p, shift2, wq)


def _conv_kernel(a_ref, ah_ref, w_ref, b_ref, g_ref, beta_ref, o_ref, hbuf):
    i = pl.program_id(1)
    ts = a_ref.shape[0]
    a = a_ref[...].astype(jnp.float32)
    hbuf[pl.ds(CONV_HALO, ts), :] = a[:, :CONV_DIM] * jax.nn.sigmoid(a[:, CONV_DIM:])
    ah = ah_ref[...].astype(jnp.float32)
    halo = ah[:, :CONV_DIM] * jax.nn.sigmoid(ah[:, CONV_DIM:])
    hbuf[pl.ds(0, CONV_HALO), :] = jnp.where(i > 0, halo, 0.0)
    acc = jnp.zeros((ts, CONV_DIM), jnp.float32) + b_ref[...]
    off = CONV_HALO - (CONV_WIDTH - 1)
    for k in range(CONV_WIDTH):
        acc = acc + w_ref[k:k + 1, :] * hbuf[pl.ds(off + k, ts), :]
    mu = jnp.mean(acc, axis=1, keepdims=True)
    d = acc - mu
    var = jnp.mean(d * d, axis=1, keepdims=True)
    y = d * lax.rsqrt(var + EPS) * g_ref[...] + beta_ref[...]
    o_ref[...] = (y * jax.nn.sigmoid(y)).astype(o_ref.dtype)


def conv_pallas(z, dw_w, dw_b, ln_g, ln_b, B, S):
    n = B * S
    ts = CONV_TS
    ns = S // ts
    hb = ts // CONV_HALO
    cb = Z_CONV // (2 * CONV_DIM)
    wpad = jnp.pad(dw_w, ((0, 32 - CONV_WIDTH), (0, 0)))
    vec = lambda: pl.BlockSpec((1, CONV_DIM), lambda b, i: (0, 0))
    return pl.pallas_call(
        _conv_kernel,
        grid=(B, ns),
        in_specs=[pl.BlockSpec((ts, 2 * CONV_DIM), lambda b, i: (b * ns + i, cb)),
                  pl.BlockSpec((CONV_HALO, 2 * CONV_DIM), lambda b, i: (jnp.maximum((b * ns + i) * hb - 1, 0), cb)),
                  pl.BlockSpec((32, CONV_DIM), lambda b, i: (0, 0)), vec(), vec(), vec()],
        out_specs=pl.BlockSpec((ts, CONV_DIM), lambda b, i: (b * ns + i, 0)),
        out_shape=jax.ShapeDtypeStruct((n, CONV_DIM), jnp.bfloat16),
        scratch_shapes=[pltpu.VMEM((ts + CONV_HALO, CONV_DIM), jnp.float32)],
        compiler_params=pltpu.CompilerParams(dimension_semantics=("parallel", "parallel")),
        name="conformer_conv",
    )(z, z, wpad, dw_b.reshape(1, -1), ln_g.reshape(1, -1), ln_b.reshape(1, -1))


def _msoftmax(s, mask):
    s = jnp.where(mask, s, NEG)
    m = jnp.max(s, axis=-1, keepdims=True)
    e = jnp.where(mask, jnp.exp(s - m), 0.0)
    l = jnp.sum(e, axis=-1, keepdims=True)
    return e / jnp.where(l > 0.0, l, 1.0)


def _nsa_kernel(q_ref, g_ref, kct_ref, vc_ref, kst_ref, vs_ref, kwt_ref, vw_ref, c2st_ref, o_ref, *, n_sel):
    G = NSA_GROUP
    qb = pl.program_id(2)
    t0 = pl.multiple_of(qb * TQ, TQ)
    q = q_ref[...]
    lane_g = lax.broadcasted_iota(jnp.int32, (TQ, GW), 1) // HEAD_DIM
    q4 = jnp.concatenate([jnp.where(lane_g == h, q, jnp.zeros_like(q)) for h in range(G)], axis=0)
    tpos = t0 + lax.broadcasted_iota(jnp.int32, (TQ, 1), 0)

    def heads(x):
        return x.reshape(G, TQ, x.shape[-1])

    def attend(p, v):
        return heads(jnp.dot(p.reshape(G * TQ, p.shape[-1]).astype(jnp.bfloat16), v,
                             preferred_element_type=jnp.float32))

    starts = [t0 - WINDOW + TQ * c for c in range(WINDOW // TQ + 1)]
    reads = [pl.multiple_of(jnp.maximum(s, 0), TQ) for s in starts]
    kwt = jnp.concatenate([kwt_ref[0, 0, :, pl.ds(r, TQ)] for r in reads], axis=1)
    vw = jnp.concatenate([vw_ref[0, 0, pl.ds(r, TQ), :] for r in reads], axis=0)
    wk = WINDOW + TQ
    kposw = t0 - WINDOW + lax.broadcasted_iota(jnp.int32, (TQ, wk), 1)
    diff = tpos - kposw
    maskw = (kposw >= 0) & (diff >= 0) & (diff < WINDOW)

    kct = kct_ref[0, 0]
    ncp = kct.shape[1]
    ncol = lax.broadcasted_iota(jnp.int32, (TQ, ncp), 1)
    maskc = (ncol * CMP_STRIDE + (CMP_BLOCK - 1)) <= tpos
    sc = heads(jnp.dot(q4, kct, preferred_element_type=jnp.float32))
    sw = heads(jnp.dot(q4, kwt, preferred_element_type=jnp.float32))
    pc = _msoftmax(sc, maskc[None])
    oc = attend(pc, vc_ref[0, 0])
    pw = _msoftmax(sw, maskw[None])
    ow = attend(pw, vw)
    pc_sum = jnp.sum(pc, axis=0)

    c2st = c2st_ref[...]
    jp = c2st.shape[0]
    hi = pc_sum.astype(jnp.bfloat16)
    lo = (pc_sum - hi.astype(jnp.float32)).astype(jnp.bfloat16)
    dn = (((1,), (1,)), ((), ()))
    imp_t = (lax.dot_general(c2st, hi, dn, preferred_element_type=jnp.float32)
             + lax.dot_general(c2st, lo, dn, preferred_element_type=jnp.float32))
    jrow = lax.broadcasted_iota(jnp.int32, (jp, TQ), 0)
    cur = (t0 + lax.broadcasted_iota(jnp.int32, (jp, TQ), 1)) // SLC_BLOCK
    forced = (jrow == 0) | (jrow == cur) | (jrow == cur - 1)
    score = jnp.where(jrow <= cur, imp_t + jnp.where(forced, FORCE_BONUS, 0.0), NEG)
    nblk = kst_ref.shape[3] // SLC_BLOCK
    nrow = -(-nblk // 8) * 8
    sc, jr = score[:nrow], jrow[:nrow]
    rank = jnp.zeros((nrow, TQ), jnp.int32)
    for i in range(nblk):
        si = sc[i:i + 1, :]
        rank = rank + ((si > sc) | ((si == sc) & (jr > i))).astype(jnp.int32)
    sel = ((rank < n_sel) & (sc > 0.5 * NEG)).astype(jnp.float32)
    if nrow < jp:
        sel = jnp.concatenate([sel, jnp.zeros((jp - nrow, TQ), jnp.float32)], axis=0)
    selb = sel.T.astype(jnp.bfloat16)

    nchunks = (t0 + TQ + KC - 1) // KC

    def chunk(c, carry):
        m_old, l_old, acc = carry
        k0 = pl.multiple_of(c * KC, KC)
        kt = kst_ref[0, 0, :, pl.ds(k0, KC)]
        v = vs_ref[0, 0, pl.ds(k0, KC), :]
        kblk = (k0 + lax.broadcasted_iota(jnp.int32, (jp, KC), 1)) // SLC_BLOCK
        expand = (kblk == lax.broadcasted_iota(jnp.int32, (jp, KC), 0)).astype(jnp.bfloat16)
        member = jnp.dot(selb, expand, preferred_element_type=jnp.float32) > 0.5
        kpos = k0 + lax.broadcasted_iota(jnp.int32, (TQ, KC), 1)
        msk = (member & (kpos <= tpos))[None]
        s = jnp.where(msk, heads(jnp.dot(q4, kt, preferred_element_type=jnp.float32)), NEG)
        m_new = jnp.maximum(m_old, jnp.max(s, axis=-1, keepdims=True))
        alpha = jnp.exp(m_old - m_new)
        p = jnp.exp(s - m_new)
        l_new = alpha * l_old + jnp.sum(p, axis=-1, keepdims=True)
        return m_new, l_new, alpha * acc + attend(p, v)

    def chunk_pair(i, carry):
        return chunk(2 * i + 1, chunk(2 * i, carry))

    init = (jnp.full((G, TQ, 1), NEG, jnp.float32), jnp.zeros((G, TQ, 1), jnp.float32),
            jnp.zeros((G, TQ, GW), jnp.float32))
    _, l_s, acc_s = lax.fori_loop(0, (nchunks + 1) // 2, chunk_pair, init)
    os_ = acc_s / jnp.where(l_s > 0.0, l_s, 1.0)

    sig = jax.nn.sigmoid(g_ref[...].astype(jnp.float32))
    out = jnp.zeros((TQ, GW), jnp.float32)
    for h in range(G):
        o_h = (sig[:, 3 * h:3 * h + 1] * oc[h] + sig[:, 3 * h + 1:3 * h + 2] * os_[h]
               + sig[:, 3 * h + 2:3 * h + 3] * ow[h])
        out = out + jnp.where(lane_g == h, o_h, 0.0)
    o_ref[...] = out.astype(o_ref.dtype)


def nsa_core(q, z, kct, vc, kst, vs, kwt, vw, B, S):
    n = B * S
    ncp = kct.shape[3]
    ncmp = (S - CMP_BLOCK) // CMP_STRIDE + 1
    nslc = S // SLC_BLOCK
    n_sel = min(SLC_TOPK, nslc)
    jp = LANE
    assert nslc <= jp and S % (2 * KC) == 0 and S % TQ == 0 and WINDOW % TQ == 0
    cs = np.arange(ncmp) * CMP_STRIDE
    ss = np.arange(nslc) * SLC_BLOCK
    ov = np.minimum(cs[:, None] + CMP_BLOCK, ss[None, :] + SLC_BLOCK) - np.maximum(cs[:, None], ss[None, :])
    c2s = np.zeros((ncp, jp), np.float32)
    c2s[:ncmp, :nslc] = np.clip(ov, 0, None) / CMP_BLOCK
    c2st = jnp.asarray(c2s.T, dtype=jnp.bfloat16)
    nqb = S // TQ

    def kv_spec(shp):
        return pl.BlockSpec((1, 1) + shp, lambda b, k, i: (b, k, 0, 0))

    return pl.pallas_call(
        functools.partial(_nsa_kernel, n_sel=n_sel),
        grid=(B, NSA_KV_HEADS, nqb),
        in_specs=[pl.BlockSpec((TQ, GW), lambda b, k, i: (b * nqb + i, k)),
                  pl.BlockSpec((TQ, LANE), lambda b, k, i: (b * nqb + i, Z_NG // LANE + k)),
                  kv_spec((GW, ncp)), kv_spec((ncp, GW)),
                  kv_spec((GW, S)), kv_spec((S, GW)),
                  kv_spec((GW, S)), kv_spec((S, GW)),
                  pl.BlockSpec((jp, ncp), lambda b, k, i: (0, 0))],
        out_specs=pl.BlockSpec((TQ, GW), lambda b, k, i: (b * nqb + i, k)),
        out_shape=jax.ShapeDtypeStruct((n, NSA_KV_HEADS * GW), jnp.bfloat16),
        compiler_params=pltpu.CompilerParams(dimension_semantics=("parallel", "parallel", "arbitrary"),
                                             vmem_limit_bytes=VMEM_LIMIT),
        name="nsa_core",
    )(q, z, kct, vc, kst, vs, kwt, vw, c2st)


def _gla_consts():
    C = GLA_CHUNK
    i = np.arange(C)
    mats = [np.tril(np.ones((C, C), np.float32))]
    for s in GLA_LEVELS:
        r = (i // (2 * s)) * 2 * s + s - 1
        upper = (i // s) % 2 == 1
        m = np.arange(C)[None, :]
        mq = ((m > r[:, None]) & (m <= i[:, None]) & upper[:, None]).astype(np.float32)
        mk = ((m > i[:, None]) & (m <= r[:, None]) & (~upper)[:, None]).astype(np.float32)
        mats += [mq, mk]
    return np.concatenate(mats, axis=0)


def _split3(x):
    h = x.astype(jnp.bfloat16)
    r = x - h.astype(jnp.float32)
    m = r.astype(jnp.bfloat16)
    l = (r - m.astype(jnp.float32)).astype(jnp.bfloat16)
    return h, m, l


def _gla_kernel(q_ref, k_ref, v_ref, r_ref, glr_ref, gw_ref, gb_ref, ng_ref, msel_ref, o_ref, state_ref):
    C = GLA_CHUNK

    @pl.when(pl.program_id(1) == 0)
    def _():
        state_ref[...] = jnp.zeros_like(state_ref)

    ii = lax.broadcasted_iota(jnp.int32, (C, C), 0)
    jj = lax.broadcasted_iota(jnp.int32, (C, C), 1)
    lane_h = lax.broadcasted_iota(jnp.int32, (C, GLA_QK), 1) // GLA_DK
    bd = (lax.broadcasted_iota(jnp.int32, (GLA_QK, GLA_V), 0) // GLA_DK
          == lax.broadcasted_iota(jnp.int32, (GLA_QK, GLA_V), 1) // GLA_DV)
    msel = msel_ref[...]
    gw = gw_ref[...].astype(jnp.bfloat16)
    dn = (((1,), (1,)), ((), ()))

    def chunk(c, carry):
        r0 = pl.multiple_of(c * C, C)
        q = q_ref[pl.ds(r0, C), :].astype(jnp.float32) * (GLA_DK ** -0.5)
        k = k_ref[pl.ds(r0, C), :].astype(jnp.float32)
        v = v_ref[pl.ds(r0, C), :].astype(jnp.bfloat16)
        z = jnp.dot(glr_ref[pl.ds(r0, C), :].astype(jnp.bfloat16), gw, preferred_element_type=jnp.float32) + gb_ref[...]
        a = jax.nn.log_sigmoid(z) / GLA_GATE_TEMP
        ah, am, al = _split3(a)
        ex = (jnp.dot(msel, ah, preferred_element_type=jnp.float32)
              + jnp.dot(msel, am, preferred_element_type=jnp.float32)
              + jnp.dot(msel, al, preferred_element_type=jnp.float32))
        bcum = ex[0:C]
        state = state_ref[...]
        o = jnp.dot((q * jnp.exp(bcum)).astype(jnp.bfloat16), state.astype(jnp.bfloat16),
                    preferred_element_type=jnp.float32)
        kb = k.astype(jnp.bfloat16)
        for h in range(GLA_HEADS):
            hm = lane_h == h
            att = jnp.where(ii == jj, lax.dot_general(jnp.where(hm, q, 0.0).astype(jnp.bfloat16), kb, dn,
                                                      preferred_element_type=jnp.float32), 0.0)
            for li, s in enumerate(GLA_LEVELS):
                eq = jnp.exp(ex[(1 + 2 * li) * C:(2 + 2 * li) * C])
                ek = jnp.exp(ex[(2 + 2 * li) * C:(3 + 2 * li) * C])
                qs = jnp.where(hm, q * eq, 0.0).astype(jnp.bfloat16)
                ks = (k * ek).astype(jnp.bfloat16)
                blk = (ii // (2 * s) == jj // (2 * s)) & ((ii // s) % 2 == 1) & ((jj // s) % 2 == 0)
                att = att + jnp.where(blk, lax.dot_general(qs, ks, dn, preferred_element_type=jnp.float32), 0.0)
            oh = jnp.dot(att.astype(jnp.bfloat16), v, preferred_element_type=jnp.float32)
            o = o + jnp.where(lax.broadcasted_iota(jnp.int32, (C, GLA_V), 1) // GLA_DV == h, oh, 0.0)
        blast = bcum[C - 1:C, :]
        kd_t = (k * jnp.exp(blast - bcum)).T.astype(jnp.bfloat16)
        decay_col = jnp.exp(jnp.sum(a.T, axis=1, keepdims=True))
        upd = jnp.dot(kd_t, v, preferred_element_type=jnp.float32)
        state_ref[...] = state * decay_col + jnp.where(bd, upd, 0.0)
        outs = []
        for h in range(GLA_HEADS):
            oh = o[:, h * GLA_DV:(h + 1) * GLA_DV]
            outs.append(oh * lax.rsqrt(jnp.mean(oh * oh, axis=1, keepdims=True) + EPS) * ng_ref[...])
        on = jnp.concatenate(outs, axis=1)
        r = r_ref[pl.ds(r0, C), :].astype(jnp.float32)
        o_ref[pl.ds(r0, C), :] = (on * (r * jax.nn.sigmoid(r))).astype(o_ref.dtype)
        return carry

    lax.fori_loop(0, q_ref.shape[0] // C, chunk, 0)


def gla_pallas(z, gate_w, gate_b, norm_g, B, S):
    n = B * S
    tb = GLA_TB
    nb = S // tb
    gw = jnp.pad(gate_w, ((0, LANE - gate_w.shape[0]), (0, 0)))
    msel = jnp.asarray(_gla_consts(), dtype=jnp.bfloat16)
    zb = lambda col, w: pl.BlockSpec((tb, w), lambda b, i: (b * nb + i, col // w))
    full = lambda shp: pl.BlockSpec(shp, lambda b, i: (0,) * len(shp))
    return pl.pallas_call(
        _gla_kernel,
        grid=(B, nb),
        in_specs=[zb(Z_GQ, GLA_QK), zb(Z_GK, GLA_QK), zb(Z_GV, GLA_V), zb(Z_GR, GLA_V), zb(Z_GLR, LANE),
                  full((LANE, GLA_QK)), full((1, GLA_QK)), full((1, GLA_DV)), full(msel.shape)],
        out_specs=pl.BlockSpec((tb, GLA_V), lambda b, i: (b * nb + i, 0)),
        out_shape=jax.ShapeDtypeStruct((n, GLA_V), jnp.bfloat16),
        scratch_shapes=[pltpu.VMEM((GLA_QK, GLA_V), jnp.float32)],
        compiler_params=pltpu.CompilerParams(dimension_semantics=("parallel", "arbitrary")),
        name="gla",
    )(z, z, z, z, z, gw, gate_b.reshape(1, -1), norm_g.reshape(1, -1), msel)


_FULL_BLOCKS = 8


def _extract_top(s, k, tie_break):
    n, t = s.shape
    row = lax.broadcasted_iota(jnp.int32, (n, t), 0)
    rank = jnp.full((n, t), k, jnp.int32)
    vals = []
    for r in range(k):
        m = jnp.max(s, axis=0, keepdims=True)
        hit = s == m
        if tie_break:
            hit = row == jnp.min(jnp.where(hit, row, n), axis=0, keepdims=True)
        rank = jnp.where(hit, r, rank)
        s = jnp.where(hit, float('-inf'), s)
        vals.append(m)
    return jnp.concatenate(vals, axis=0), rank


def _peer_gate_math(s1, s2, tie_break):
    K = PEER_TOPK
    t = s1.shape[1]
    v1, rank1 = _extract_top(s1, K, tie_break)
    v2, rank2 = _extract_top(s2, K, tie_break)
    blocks, poss = [], []
    row16 = lax.broadcasted_iota(jnp.int32, (K, t), 0)
    for r1 in range(_FULL_BLOCKS):
        c = v1[r1:r1 + 1, :] + v2
        blocks.append(jnp.where(row16 < K // (r1 + 1), c, float('-inf')))
        poss.append(row16 + r1 * K)
    row8 = lax.broadcasted_iota(jnp.int32, (K - _FULL_BLOCKS, t), 0)
    blocks.append(v1[_FULL_BLOCKS:, :] + v2[0:1, :])
    poss.append((row8 + _FULL_BLOCKS) * K)
    sels = [jnp.zeros(b.shape, jnp.bool_) for b in blocks]
    big = K * K
    for _ in range(K):
        m = blocks[0].max(axis=0, keepdims=True)
        for b in blocks[1:]:
            m = jnp.maximum(m, b.max(axis=0, keepdims=True))
        if tie_break:
            first = None
            for b, p in zip(blocks, poss):
                f = jnp.min(jnp.where(b == m, p, big), axis=0, keepdims=True)
                first = f if first is None else jnp.minimum(first, f)
            hits = [p == first for p in poss]
        else:
            hits = [b == m for b in blocks]
        for i, hit in enumerate(hits):
            sels[i] = sels[i] | hit
            blocks[i] = jnp.where(hit, float('-inf'), blocks[i])
    top = v1[0:1, :] + v2[0:1, :]
    cnt_rows = []
    z = jnp.zeros((1, t), jnp.float32)
    for r1 in range(_FULL_BLOCKS):
        sel = sels[r1]
        cnt_rows.append(jnp.sum(sel.astype(jnp.float32), axis=0, keepdims=True))
        c = v1[r1:r1 + 1, :] + v2
        z = z + jnp.sum(jnp.where(sel, jnp.exp(c - top), 0.0), axis=0, keepdims=True)
    sel = sels[_FULL_BLOCKS]
    cnt_tail = sel.astype(jnp.float32)
    c = v1[_FULL_BLOCKS:, :] + v2[0:1, :]
    z = z + jnp.sum(jnp.where(sel, jnp.exp(c - top), 0.0), axis=0, keepdims=True)
    cnt = jnp.concatenate(cnt_rows + [cnt_tail], axis=0)
    a = jnp.zeros(s1.shape, jnp.float32)
    for r in range(K):
        a = jnp.where(rank1 == r, cnt[r:r + 1, :], a)
    taken = (jnp.sum((rank1 < K).astype(jnp.float32), axis=0, keepdims=True),
             jnp.sum((rank2 < K).astype(jnp.float32), axis=0, keepdims=True),
             jnp.sum(cnt, axis=0, keepdims=True))
    return jnp.exp(s1 - v1[0:1, :]), a, rank2.astype(jnp.float32), jnp.exp(s2 - v2[0:1, :]) / z, taken


def _peer_gate_kernel(q_ref, keys_ref, e1_ref, a_ref, r2_ref, e2_ref):
    q = q_ref[...].astype(jnp.bfloat16)
    k1 = keys_ref[0, 0].astype(jnp.bfloat16)
    k2 = keys_ref[0, 1].astype(jnp.bfloat16)
    dn = (((1,), (1,)), ((), ()))
    s1 = lax.dot_general(k1, q[:, :PEER_DKEY // 2], dn, preferred_element_type=jnp.float32)
    s2 = lax.dot_general(k2, q[:, PEER_DKEY // 2:], dn, preferred_element_type=jnp.float32)

    def write(e1, a, r2, e2):
        e1_ref[0] = e1
        a_ref[0] = a
        r2_ref[0] = r2.astype(jnp.bfloat16)
        e2_ref[0] = e2.astype(jnp.bfloat16)

    e1, a, r2, e2, taken = _peer_gate_math(s1, s2, tie_break=False)
    write(e1, a, r2, e2)
    excess = sum(jnp.max(jnp.abs(c - PEER_TOPK)) for c in taken)

    @pl.when(excess > 0.0)
    def _():
        write(*_peer_gate_math(s1, s2, tie_break=True)[:4])


def peer_gates(qh, keys, tt=512):
    n = qh.shape[0]
    H = PEER_HEADS
    shp32 = jax.ShapeDtypeStruct((H, PEER_NKEYS, n), jnp.float32)
    shp16 = jax.ShapeDtypeStruct((H, PEER_NKEYS, n), jnp.bfloat16)
    ospec = pl.BlockSpec((1, PEER_NKEYS, tt), lambda i, h: (h, 0, i))
    return pl.pallas_call(
        _peer_gate_kernel,
        grid=(n // tt, H),
        in_specs=[pl.BlockSpec((tt, PEER_DKEY), lambda i, h: (i, h)),
                  pl.BlockSpec((1, 2, PEER_NKEYS, PEER_DKEY // 2), lambda i, h: (h, 0, 0, 0))],
        out_specs=[ospec, ospec, ospec, ospec],
        out_shape=[shp32, shp32, shp16, shp16],
        compiler_params=pltpu.CompilerParams(dimension_semantics=("parallel", "parallel")),
        name="peer_gates",
    )(qh, keys)


def _row_bf16(row, n):
    t = row.shape[1]
    tile = jnp.broadcast_to(row, (16, t)).astype(jnp.bfloat16)
    return jnp.concatenate([tile] * (n // 16), axis=0)


def _peer_dense_kernel(xt_ref, u_ref, vt_ref, e1_ref, a_ref, r2_ref, e2_ref, x1_ref, gate_ref, fg_ref,
                       o_ref, acc_ref, *, ec, final):
    j = pl.program_id(1)
    nsub = ec // PEER_NKEYS

    @pl.when(j == 0)
    def _():
        acc_ref[...] = jnp.zeros_like(acc_ref)

    assert nsub % 8 == 0
    r8 = pl.multiple_of(j * nsub, 8)
    a_tiles = [[a_ref[h, pl.ds(r8 + 8 * t, 8), :] for t in range(nsub // 8)] for h in range(PEER_HEADS)]
    e1_tiles = [[e1_ref[h, pl.ds(r8 + 8 * t, 8), :] for t in range(nsub // 8)] for h in range(PEER_HEADS)]

    def gate_mat(i):
        g = None
        for h in range(PEER_HEADS):
            cnt = _row_bf16(a_tiles[h][i // 8][i % 8:i % 8 + 1, :], PEER_NKEYS)
            e1 = _row_bf16(e1_tiles[h][i // 8][i % 8:i % 8 + 1, :], PEER_NKEYS)
            term = jnp.where(r2_ref[h] < cnt, e2_ref[h], jnp.bfloat16(0)) * e1
            g = term if g is None else g + term
        return g

    gs = PEER_GROUPS_PER_SCORE

    def scores(p):
        r0 = p * gs * PEER_NKEYS
        return jnp.dot(u_ref[r0:r0 + gs * PEER_NKEYS, :], xt_ref[...], preferred_element_type=jnp.float32)

    st = scores(0)
    ws = []
    for p in range(nsub // gs):
        st_next = scores(p + 1) if (p + 1) * gs < nsub else None
        for k in range(gs):
            i = p * gs + k
            ws.append(_gelu(st[k * PEER_NKEYS:(k + 1) * PEER_NKEYS].astype(jnp.bfloat16)) * gate_mat(i))
            if i % PEER_GROUPS_PER_DOT == PEER_GROUPS_PER_DOT - 1:
                w = jnp.concatenate(ws, axis=0)
                ws = []
                acc_ref[...] += jnp.dot(
                    vt_ref[:, (i + 1 - PEER_GROUPS_PER_DOT) * PEER_NKEYS:(i + 1) * PEER_NKEYS], w,
                    preferred_element_type=jnp.float32)
        st = st_next

    @pl.when(j == pl.num_programs(1) - 1)
    def _():
        x2 = x1_ref[...] + gate_ref[0] * acc_ref[...].T
        if final:
            x2 = x2 * lax.rsqrt(jnp.mean(x2 * x2, axis=1, keepdims=True) + EPS) * fg_ref[...]
        o_ref[...] = x2


def peer_dense(xt, u16, vt16, e1, a, r2, e2, x1, gate2, final_g, S, final, tt=512, ec=2048):
    d, n = xt.shape
    ne = u16.shape[0]
    H = PEER_HEADS
    per_b = S // tt
    gspec = lambda: pl.BlockSpec((H, PEER_NKEYS, tt), lambda i, j: (0, 0, i))
    return pl.pallas_call(
        functools.partial(_peer_dense_kernel, ec=ec, final=final),
        grid=(n // tt, ne // ec),
        in_specs=[pl.BlockSpec((d, tt), lambda i, j: (0, i)),
                  pl.BlockSpec((ec, d), lambda i, j: (j, 0)),
                  pl.BlockSpec((d, ec), lambda i, j: (0, j)),
                  gspec(), gspec(), gspec(), gspec(),
                  pl.BlockSpec((tt, d), lambda i, j: (i, 0)),
                  pl.BlockSpec((1, 1, d), lambda i, j: (i // per_b, 0, 0)),
                  pl.BlockSpec((1, d), lambda i, j: (0, 0))],
        out_specs=pl.BlockSpec((tt, d), lambda i, j: (i, 0)),
        out_shape=jax.ShapeDtypeStruct((n, d), jnp.float32),
        scratch_shapes=[pltpu.VMEM((d, tt), jnp.float32)],
        compiler_params=pltpu.CompilerParams(dimension_semantics=("parallel", "arbitrary"),
                                             vmem_limit_bytes=VMEM_LIMIT),
        name="peer_dense",
    )(xt, u16, vt16, e1, a, r2, e2, x1, gate2, final_g.reshape(1, d))


def adaln(c, w, b):
    mod = pmm(jax.nn.silu(c), w) + b
    shift, scale, gate = jnp.split(mod[:, None, :], 3, axis=-1)
    return shift, 1.0 + scale, gate


def kernel(x, c, positions, ada_w, ada_b, norm_g, w_in, conv_dw_w, conv_dw_b, conv_ln_g, conv_ln_b, w_conv_up,
           cmp_pos, cmp_w1, cmp_w2, w_nsa_up, gla_gate_w, gla_gate_b, gla_norm_g, w_gla_up, w_out,
           peer_wq, peer_keys, peer_u, peer_v, final_g):
    B, S, D = x.shape
    bf = jnp.bfloat16
    x2 = x.reshape(B * S, D)
    cs = rope_tables(positions)
    for l in range(DEPTH):
        shift, scale1p, gate = adaln(c, ada_w[l, 0], ada_b[l, 0])
        z = in_proj(x2, norm_g[l, 0], scale1p, shift, pack_w_in(w_in[l]), S)
        hc = conv_pallas(z, conv_dw_w[l], conv_dw_b[l], conv_ln_g[l], conv_ln_b[l], B, S)
        q, kst, vs, kwt, vw = nsa_prep(z, cs, B, S)
        kct, vc = nsa_compress(z, positions, cmp_pos[l], cmp_w1[l], cmp_w2[l], B, S)
        on = nsa_core(q, z, kct, vc, kst, vs, kwt, vw, B, S)
        og = gla_pallas(z, gla_gate_w[l], gla_gate_b[l], gla_norm_g[l], B, S)
        shift2, scale2p, gate2 = adaln(c, ada_w[l, 1], ada_b[l, 1])
        x1, ut, qh = mix_out(hc, on, og, z, x2, gate, w_conv_up[l].astype(bf), w_nsa_up[l].astype(bf),
                             w_gla_up[l].astype(bf), w_out[l].astype(bf), norm_g[l, 1], scale2p, shift2,
                             peer_wq[l].astype(bf), S)
        e1, a, r2, e2 = peer_gates(qh, peer_keys[l])
        x2 = peer_dense(ut, peer_u[l].astype(bf), peer_v[l].astype(bf).T, e1, a, r2, e2, x1, gate2, final_g,
                        S, final=(l == DEPTH - 1))
    return x2.reshape(B, S, D)
```

```python
import functools

import jax
import jax.numpy as jnp
import numpy as np
from jax import lax
from jax.experimental import pallas as pl
from jax.experimental.pallas import tpu as pltpu

D_MODEL = 1024
DEPTH = 2
EPS = 1e-6
NEG = -1e30
N_BRANCH = 3
CONV_DIM = 512
CONV_WIDTH = 31
NSA_HEADS = 8
NSA_KV_HEADS = 2
NSA_GROUP = NSA_HEADS // NSA_KV_HEADS
HEAD_DIM = 64
CMP_BLOCK = 32
CMP_STRIDE = 16
CMP_HIDDEN = 256
SLC_BLOCK = 64
SLC_TOPK = 16
WINDOW = 512
FORCE_BONUS = 1e4
ROPE_THETA = 10000.0
GLA_HEADS = 4
GLA_DK = 64
GLA_DV = 128
GLA_GATE_RANK = 16
GLA_GATE_TEMP = 16.0
GLA_CHUNK = 256
PEER_HEADS = 8
PEER_NKEYS = 128
PEER_DKEY = 256
PEER_TOPK = 16
NSA_Q = NSA_HEADS * HEAD_DIM
NSA_KV = 3 * 2 * NSA_KV_HEADS * HEAD_DIM
NSA_G = 3 * NSA_HEADS
GLA_QK = GLA_HEADS * GLA_DK
GLA_V = GLA_HEADS * GLA_DV
SPLITS = [2 * CONV_DIM, NSA_Q, NSA_KV, NSA_G, GLA_QK, GLA_QK, GLA_V, GLA_V, GLA_GATE_RANK, N_BRANCH * D_MODEL]

LANE = 128
TQ = 128
KC = 512
GW = NSA_GROUP * HEAD_DIM
VMEM_LIMIT = 48 * 1024 * 1024
GLA_LEVELS = tuple(GLA_CHUNK >> s for s in range(1, GLA_CHUNK.bit_length()))
GLA_TB = 512
CONV_TS = 512
CONV_HALO = 32
ROW_TILE = 1024
MIX_TILE = 512
PEER_GROUPS_PER_DOT = 2
PEER_GROUPS_PER_SCORE = 8

Z_MG = 0
Z_CONV = Z_MG + N_BRANCH * D_MODEL
Z_Q = Z_CONV + 2 * CONV_DIM
Z_QSW = Z_Q + NSA_Q
Z_GV = Z_QSW + NSA_Q
Z_GR = Z_GV + GLA_V
Z_GQ = Z_GR + GLA_V
Z_GK = Z_GQ + GLA_QK
Z_CMP = Z_GK + GLA_QK
Z_SLC = Z_CMP + 256
Z_WIN = Z_SLC + 384
Z_NG = Z_WIN + 384
Z_GLR = Z_NG + NSA_KV_HEADS * LANE
NZ = Z_GLR + LANE
assert Z_SLC % 384 == 0 and Z_WIN % 384 == 0 and Z_CONV % (2 * CONV_DIM) == 0 and Z_GQ % GLA_QK == 0


def _mm_kernel(a_ref, b_ref, o_ref):
    o_ref[...] = jnp.dot(a_ref[...].astype(jnp.bfloat16), b_ref[...].astype(jnp.bfloat16),
                         preferred_element_type=jnp.float32)


def _pick(n, cands):
    for c in cands:
        if n % c == 0:
            return c
    return n


def pmm(a, b):
    m, k = a.shape
    n = b.shape[1]
    n_pad = -n % LANE
    if n_pad:
        b = jnp.pad(b, ((0, 0), (0, n_pad)))
    m_pad = -m % 8
    if m_pad:
        a = jnp.pad(a, ((0, m_pad), (0, 0)))
    mp, np_ = m + m_pad, n + n_pad
    tm = _pick(mp, (512, 256, 128, 64, 32, 16, 8))
    tn = _pick(np_, (512, 640, 384, 256, 128))
    out = pl.pallas_call(
        _mm_kernel,
        grid=(mp // tm, np_ // tn),
        in_specs=[pl.BlockSpec((tm, k), lambda i, j: (i, 0)),
                  pl.BlockSpec((k, tn), lambda i, j: (0, j))],
        out_specs=pl.BlockSpec((tm, tn), lambda i, j: (i, j)),
        out_shape=jax.ShapeDtypeStruct((mp, np_), jnp.float32),
        compiler_params=pltpu.CompilerParams(dimension_semantics=("parallel", "parallel")),
        name="pmm",
    )(a, b)
    return out[:m, :n]


def _in_proj_kernel(x_ref, g_ref, sc_ref, sh_ref, w_ref, z_ref, u_ref):
    @pl.when(pl.program_id(1) == 0)
    def _():
        x = x_ref[...]
        y = x * lax.rsqrt(jnp.mean(x * x, axis=1, keepdims=True) + EPS) * g_ref[...]
        u_ref[...] = (y * sc_ref[0] + sh_ref[0]).astype(jnp.bfloat16)

    z_ref[...] = jnp.dot(u_ref[...], w_ref[...], preferred_element_type=jnp.float32).astype(z_ref.dtype)


def in_proj(x2, g, scale1p, shift, w16, S):
    n, d = x2.shape
    nz = w16.shape[1]
    tm = ROW_TILE
    tn = _pick(nz, (1152, 1024, 896, 512, 384, 256, 128))
    per_b = S // tm
    mod = lambda: pl.BlockSpec((1, 1, d), lambda i, j: (i // per_b, 0, 0))
    return pl.pallas_call(
        _in_proj_kernel,
        grid=(n // tm, nz // tn),
        in_specs=[pl.BlockSpec((tm, d), lambda i, j: (i, 0)),
                  pl.BlockSpec((1, d), lambda i, j: (0, 0)), mod(), mod(),
                  pl.BlockSpec((d, tn), lambda i, j: (0, j))],
        out_specs=pl.BlockSpec((tm, tn), lambda i, j: (i, j)),
        out_shape=jax.ShapeDtypeStruct((n, nz), jnp.bfloat16),
        scratch_shapes=[pltpu.VMEM((tm, d), jnp.bfloat16)],
        compiler_params=pltpu.CompilerParams(dimension_semantics=("parallel", "arbitrary"),
                                             vmem_limit_bytes=VMEM_LIMIT),
        name="in_proj",
    )(x2, g.reshape(1, d), scale1p, shift, w16)


def _half_swap(n_heads):
    idx = np.arange(n_heads * HEAD_DIM).reshape(n_heads, 2, HEAD_DIM // 2)
    return idx[:, ::-1, :].reshape(-1)


def pack_w_in(w_in):
    o = np.cumsum([0] + SPLITS)
    a_conv, nq, nkv, ng, gq, gk, gv, gr, glr, mg = [w_in[:, o[i]:o[i + 1]] for i in range(len(SPLITS))]
    kvw = 2 * NSA_KV_HEADS * HEAD_DIM
    kw = NSA_KV_HEADS * HEAD_DIM
    d = w_in.shape[0]
    sw = _half_swap(NSA_KV_HEADS)
    cols = [mg, a_conv, nq, nq[:, _half_swap(NSA_HEADS)], gv, gr, gq, gk, nkv[:, 0:kvw]]
    for br in (1, 2):
        k = nkv[:, br * kvw:br * kvw + kw]
        cols += [k, k[:, sw], nkv[:, br * kvw + kw:(br + 1) * kvw]]
    gpk = NSA_GROUP * 3
    for kvh in range(NSA_KV_HEADS):
        cols += [ng[:, kvh * gpk:(kvh + 1) * gpk], jnp.zeros((d, LANE - gpk), w_in.dtype)]
    cols += [glr, jnp.zeros((d, LANE - GLA_GATE_RANK), w_in.dtype)]
    w = jnp.concatenate(cols, axis=1)
    assert w.shape[1] == NZ
    return w.astype(jnp.bfloat16)


def rope_tables(positions):
    half = HEAD_DIM // 2
    freq = ROPE_THETA ** (-jnp.arange(half, dtype=jnp.float32) / half)
    ang = positions.astype(jnp.float32)[..., None] * freq
    cos, sin = jnp.cos(ang), jnp.sin(ang)
    c = jnp.tile(jnp.concatenate([cos, cos], -1), (1, 1, LANE // HEAD_DIM))
    s = jnp.tile(jnp.concatenate([-sin, sin], -1), (1, 1, LANE // HEAD_DIM))
    return jnp.concatenate([c, s], -1).reshape(-1, 2 * LANE)


def _nsa_prep_kernel(q_ref, qs_ref, slc_ref, win_ref, cs_ref, sel_ref, qo_ref, kst_ref, vs_ref, kwt_ref, vw_ref):
    cos = cs_ref[:, :LANE]
    sin = cs_ref[:, LANE:]
    scale = HEAD_DIM ** -0.5
    for t in range(NSA_Q // LANE):
        sl = slice(t * LANE, (t + 1) * LANE)
        qo_ref[:, sl] = ((q_ref[:, sl] * cos + qs_ref[:, sl] * sin) * scale).astype(jnp.bfloat16)
    dn = (((1,), (1,)), ((), ()))
    for src, kt_ref, v_ref in ((slc_ref, kst_ref, vs_ref), (win_ref, kwt_ref, vw_ref)):
        k = (src[:, 0:LANE] * cos + src[:, LANE:2 * LANE] * sin).astype(jnp.bfloat16)
        v = src[:, 2 * LANE:3 * LANE].astype(jnp.bfloat16)
        for kvh in range(NSA_KV_HEADS):
            sel = sel_ref[kvh]
            kt_ref[0, kvh] = lax.dot_general(sel, k, dn, preferred_element_type=jnp.float32).astype(jnp.bfloat16)
            v_ref[0, kvh] = lax.dot_general(v, sel, dn, preferred_element_type=jnp.float32).astype(jnp.bfloat16)


def _head_repeat_sel():
    sel = np.zeros((NSA_KV_HEADS, GW, LANE), np.float32)
    for kvh in range(NSA_KV_HEADS):
        for r in range(GW):
            sel[kvh, r, kvh * HEAD_DIM + r % HEAD_DIM] = 1.0
    return jnp.asarray(sel, dtype=jnp.bfloat16)


def nsa_prep(z, cs, B, S):
    n = B * S
    tp = ROW_TILE
    nb = S // tp
    zb = lambda col, w: pl.BlockSpec((tp, w), lambda b, i: (b * nb + i, col // w))
    kt = lambda: pl.BlockSpec((1, NSA_KV_HEADS, GW, tp), lambda b, i: (b, 0, 0, i))
    vv = lambda: pl.BlockSpec((1, NSA_KV_HEADS, tp, GW), lambda b, i: (b, 0, i, 0))
    kt_shape = jax.ShapeDtypeStruct((B, NSA_KV_HEADS, GW, S), jnp.bfloat16)
    v_shape = jax.ShapeDtypeStruct((B, NSA_KV_HEADS, S, GW), jnp.bfloat16)
    return pl.pallas_call(
        _nsa_prep_kernel,
        grid=(B, nb),
        in_specs=[zb(Z_Q, NSA_Q), zb(Z_QSW, NSA_Q), zb(Z_SLC, 384), zb(Z_WIN, 384),
                  pl.BlockSpec((tp, 2 * LANE), lambda b, i: (b * nb + i, 0)),
                  pl.BlockSpec((NSA_KV_HEADS, GW, LANE), lambda b, i: (0, 0, 0))],
        out_specs=[pl.BlockSpec((tp, NSA_Q), lambda b, i: (b * nb + i, 0)), kt(), vv(), kt(), vv()],
        out_shape=[jax.ShapeDtypeStruct((n, NSA_Q), jnp.bfloat16), kt_shape, v_shape, kt_shape, v_shape],
        compiler_params=pltpu.CompilerParams(dimension_semantics=("parallel", "parallel")),
        name="nsa_prep",
    )(z, z, z, z, cs, _head_repeat_sel())


def _gelu(x):
    return 0.5 * x * (1.0 + jnp.tanh(0.7978845608028654 * (x + 0.044715 * x * x * x)))


def _compress_kernel(gk_ref, gv_ref, pe_ref, w1_ref, w2_ref, cs_ref, kct_ref, vc_ref):
    nrow = gk_ref.shape[2]
    outs = []
    for kv, g_ref in enumerate((gk_ref, gv_ref)):
        g = g_ref[0, 0]
        lo = jnp.dot((g + pe_ref[kv, 0]).astype(jnp.bfloat16), w1_ref[kv, 0], preferred_element_type=jnp.float32)
        hi = jnp.dot((g + pe_ref[kv, 1]).astype(jnp.bfloat16), w1_ref[kv, 1], preferred_element_type=jnp.float32)
        hid = _gelu(lo + pltpu.roll(hi, nrow - 1, 0)).astype(jnp.bfloat16)
        outs.append(jnp.dot(hid, w2_ref[kv], preferred_element_type=jnp.float32))
    k = outs[0][:, :GW] * cs_ref[0, :, :GW] + outs[0][:, GW:] * cs_ref[0, :, GW:]
    kct_ref[0, 0] = k.T.astype(jnp.bfloat16)
    vc_ref[0, 0] = outs[1][:, :GW].astype(jnp.bfloat16)


def nsa_compress(z, positions, cmp_pos, cmp_w1, cmp_w2, B, S):
    ng = S // CMP_STRIDE
    grp = CMP_STRIDE * HEAD_DIM
    c = z[:, Z_CMP:Z_CMP + 256].astype(jnp.float32).reshape(B, ng, CMP_STRIDE, 2, NSA_KV_HEADS, HEAD_DIM)
    g = c.transpose(3, 0, 4, 1, 2, 5).reshape(2, B, NSA_KV_HEADS, ng, grp)
    pe = cmp_pos.reshape(2, 2, 1, grp)
    w1 = cmp_w1.reshape(2, 2, grp, CMP_HIDDEN).astype(jnp.bfloat16)
    rep = jnp.tile(cmp_w2, (1, 1, NSA_GROUP))
    sw = np.tile(_half_swap(1), NSA_GROUP) + np.repeat(np.arange(NSA_GROUP) * HEAD_DIM, HEAD_DIM)
    w2 = jnp.concatenate([rep, rep[:, :, sw]], axis=-1).astype(jnp.bfloat16)
    end = jnp.minimum(jnp.arange(ng) * CMP_STRIDE + CMP_BLOCK - 1, S - 1)
    half = HEAD_DIM // 2
    freq = ROPE_THETA ** (-jnp.arange(half, dtype=jnp.float32) / half)
    ang = positions[:, end].astype(jnp.float32)[..., None] * freq
    cos, sin = jnp.cos(ang), jnp.sin(ang)
    cs = jnp.concatenate([jnp.tile(jnp.concatenate([cos, cos], -1), (1, 1, NSA_GROUP)),
                          jnp.tile(jnp.concatenate([-sin, sin], -1), (1, 1, NSA_GROUP))], -1)
    gspec = lambda: pl.BlockSpec((1, 1, ng, grp), lambda b, k: (b, k, 0, 0))
    return pl.pallas_call(
        _compress_kernel,
        grid=(B, NSA_KV_HEADS),
        in_specs=[gspec(), gspec(),
                  pl.BlockSpec((2, 2, 1, grp), lambda b, k: (0, 0, 0, 0)),
                  pl.BlockSpec((2, 2, grp, CMP_HIDDEN), lambda b, k: (0, 0, 0, 0)),
                  pl.BlockSpec((2, CMP_HIDDEN, 2 * GW), lambda b, k: (0, 0, 0)),
                  pl.BlockSpec((1, ng, 2 * GW), lambda b, k: (b, 0, 0))],
        out_specs=[pl.BlockSpec((1, 1, GW, ng), lambda b, k: (b, k, 0, 0)),
                   pl.BlockSpec((1, 1, ng, GW), lambda b, k: (b, k, 0, 0))],
        out_shape=[jax.ShapeDtypeStruct((B, NSA_KV_HEADS, GW, ng), jnp.bfloat16),
                   jax.ShapeDtypeStruct((B, NSA_KV_HEADS, ng, GW), jnp.bfloat16)],
        compiler_params=pltpu.CompilerParams(dimension_semantics=("parallel", "parallel")),
        name="nsa_compress",
    )(g[0], g[1], pe, w1, w2, cs)


def _mix_kernel(hc_ref, on_ref, og_ref, mg_ref, x_ref, gate_ref, wc_ref, wn_ref, wg_ref, wo_ref,
                g2_ref, sc_ref, sh_ref, wq_ref, x1_ref, ut_ref, qh_ref):
    d = D_MODEL
    f32 = jnp.float32
    y = (jax.nn.sigmoid(mg_ref[:, 0:d].astype(f32)) * jnp.dot(hc_ref[...], wc_ref[...], preferred_element_type=f32)
         + jax.nn.sigmoid(mg_ref[:, d:2 * d].astype(f32)) * jnp.dot(on_ref[...], wn_ref[...], preferred_element_type=f32)
         + jax.nn.sigmoid(mg_ref[:, 2 * d:3 * d].astype(f32))
         * jnp.dot(og_ref[...], wg_ref[...], preferred_element_type=f32))
    x1 = x_ref[...] + gate_ref[0] * jnp.dot(y.astype(jnp.bfloat16), wo_ref[...], preferred_element_type=f32)
    x1_ref[...] = x1
    u = x1 * lax.rsqrt(jnp.mean(x1 * x1, axis=1, keepdims=True) + EPS) * g2_ref[...] * sc_ref[0] + sh_ref[0]
    ut_ref[...] = u.T.astype(jnp.bfloat16)
    qh_ref[...] = jnp.dot(u.astype(jnp.bfloat16), wq_ref[...], preferred_element_type=f32).astype(qh_ref.dtype)


def mix_out(hc, on, og, z, x2, gate1, wc, wn, wg, wo, g2, scale2p, shift2, wq, S):
    n, d = x2.shape
    tm = MIX_TILE
    per_b = S // tm
    row = lambda w: pl.BlockSpec((tm, w), lambda i: (i, 0))
    mod = lambda: pl.BlockSpec((1, 1, d), lambda i: (i // per_b, 0, 0))
    full = lambda a: pl.BlockSpec(a.shape, lambda i: (0,) * a.ndim, pipeline_mode=pl.Buffered(1))
    nq = wq.shape[1]
    return pl.pallas_call(
        _mix_kernel,
        grid=(n // tm,),
        in_specs=[row(CONV_DIM), row(NSA_Q), row(GLA_V),
                  pl.BlockSpec((tm, N_BRANCH * d), lambda i: (i, Z_MG // (N_BRANCH * d))),
                  row(d), mod(), full(wc), full(wn), full(wg), full(wo),
                  pl.BlockSpec((1, d), lambda i: (0, 0)), mod(), mod(), full(wq)],
        out_specs=[row(d), pl.BlockSpec((d, tm), lambda i: (0, i)), row(nq)],
        out_shape=[jax.ShapeDtypeStruct((n, d), jnp.float32), jax.ShapeDtypeStruct((d, n), jnp.bfloat16),
                   jax.ShapeDtypeStruct((n, nq), jnp.bfloat16)],
        compiler_params=pltpu.CompilerParams(dimension_semantics=("parallel",), vmem_limit_bytes=VMEM_LIMIT),
        name="mix_out",
    )(hc, on, og, z, x2, gate1, wc, wn, wg, wo, g2.reshape(1, d), scale2p, shift2, wq)


def _conv_kernel(a_ref, ah_ref, w_ref, b_ref, g_ref, beta_ref, o_ref, hbuf):
    i = pl.program_id(1)
    ts = a_ref.shape[0]
    a = a_ref[...].astype(jnp.float32)
    hbuf[pl.ds(CONV_HALO, ts), :] = a[:, :CONV_DIM] * jax.nn.sigmoid(a[:, CONV_DIM:])
    ah = ah_ref[...].astype(jnp.float32)
    halo = ah[:, :CONV_DIM] * jax.nn.sigmoid(ah[:, CONV_DIM:])
    hbuf[pl.ds(0, CONV_HALO), :] = jnp.where(i > 0, halo, 0.0)
    acc = jnp.zeros((ts, CONV_DIM), jnp.float32) + b_ref[...]
    off = CONV_HALO - (CONV_WIDTH - 1)
    for k in range(CONV_WIDTH):
        acc = acc + w_ref[k:k + 1, :] * hbuf[pl.ds(off + k, ts), :]
    mu = jnp.mean(acc, axis=1, keepdims=True)
    d = acc - mu
    var = jnp.mean(d * d, axis=1, keepdims=True)
    y = d * lax.rsqrt(var + EPS) * g_ref[...] + beta_ref[...]
    o_ref[...] = (y * jax.nn.sigmoid(y)).astype(o_ref.dtype)


def conv_pallas(z, dw_w, dw_b, ln_g, ln_b, B, S):
    n = B * S
    ts = CONV_TS
    ns = S // ts
    hb = ts // CONV_HALO
    cb = Z_CONV // (2 * CONV_DIM)
    wpad = jnp.pad(dw_w, ((0, 32 - CONV_WIDTH), (0, 0)))
    vec = lambda: pl.BlockSpec((1, CONV_DIM), lambda b, i: (0, 0))
    return pl.pallas_call(
        _conv_kernel,
        grid=(B, ns),
        in_specs=[pl.BlockSpec((ts, 2 * CONV_DIM), lambda b, i: (b * ns + i, cb)),
                  pl.BlockSpec((CONV_HALO, 2 * CONV_DIM), lambda b, i: (jnp.maximum((b * ns + i) * hb - 1, 0), cb)),
                  pl.BlockSpec((32, CONV_DIM), lambda b, i: (0, 0)), vec(), vec(), vec()],
        out_specs=pl.BlockSpec((ts, CONV_DIM), lambda b, i: (b * ns + i, 0)),
        out_shape=jax.ShapeDtypeStruct((n, CONV_DIM), jnp.bfloat16),
        scratch_shapes=[pltpu.VMEM((ts + CONV_HALO, CONV_DIM), jnp.float32)],
        compiler_params=pltpu.CompilerParams(dimension_semantics=("parallel", "parallel")),
        name="conformer_conv",
    )(z, z, wpad, dw_b.reshape(1, -1), ln_g.reshape(1, -1), ln_b.reshape(1, -1))


def _msoftmax(s, mask):
    s = jnp.where(mask, s, NEG)
    m = jnp.max(s, axis=-1, keepdims=True)
    e = jnp.where(mask, jnp.exp(s - m), 0.0)
    l = jnp.sum(e, axis=-1, keepdims=True)
    return e / jnp.where(l > 0.0, l, 1.0)


def _nsa_kernel(q_ref, g_ref, kct_ref, vc_ref, kst_ref, vs_ref, kwt_ref, vw_ref, c2st_ref, o_ref, *, n_sel):
    G = NSA_GROUP
    qb = pl.program_id(2)
    t0 = pl.multiple_of(qb * TQ, TQ)
    q = q_ref[...]
    lane_g = lax.broadcasted_iota(jnp.int32, (TQ, GW), 1) // HEAD_DIM
    q4 = jnp.concatenate([jnp.where(lane_g == h, q, jnp.zeros_like(q)) for h in range(G)], axis=0)
    tpos = t0 + lax.broadcasted_iota(jnp.int32, (TQ, 1), 0)

    def heads(x):
        return x.reshape(G, TQ, x.shape[-1])

    def attend(p, v):
        return heads(jnp.dot(p.reshape(G * TQ, p.shape[-1]).astype(jnp.bfloat16), v,
                             preferred_element_type=jnp.float32))

    starts = [t0 - WINDOW + TQ * c for c in range(WINDOW // TQ + 1)]
    reads = [pl.multiple_of(jnp.maximum(s, 0), TQ) for s in starts]
    kwt = jnp.concatenate([kwt_ref[0, 0, :, pl.ds(r, TQ)] for r in reads], axis=1)
    vw = jnp.concatenate([vw_ref[0, 0, pl.ds(r, TQ), :] for r in reads], axis=0)
    wk = WINDOW + TQ
    kposw = t0 - WINDOW + lax.broadcasted_iota(jnp.int32, (TQ, wk), 1)
    diff = tpos - kposw
    maskw = (kposw >= 0) & (diff >= 0) & (diff < WINDOW)

    kct = kct_ref[0, 0]
    ncp = kct.shape[1]
    ncol = lax.broadcasted_iota(jnp.int32, (TQ, ncp), 1)
    maskc = (ncol * CMP_STRIDE + (CMP_BLOCK - 1)) <= tpos
    sc = heads(jnp.dot(q4, kct, preferred_element_type=jnp.float32))
    sw = heads(jnp.dot(q4, kwt, preferred_element_type=jnp.float32))
    pc = _msoftmax(sc, maskc[None])
    oc = attend(pc, vc_ref[0, 0])
    pw = _msoftmax(sw, maskw[None])
    ow = attend(pw, vw)
    pc_sum = jnp.sum(pc, axis=0)

    c2st = c2st_ref[...]
    jp = c2st.shape[0]
    hi = pc_sum.astype(jnp.bfloat16)
    lo = (pc_sum - hi.astype(jnp.float32)).astype(jnp.bfloat16)
    dn = (((1,), (1,)), ((), ()))
    imp_t = (lax.dot_general(c2st, hi, dn, preferred_element_type=jnp.float32)
             + lax.dot_general(c2st, lo, dn, preferred_element_type=jnp.float32))
    jrow = lax.broadcasted_iota(jnp.int32, (jp, TQ), 0)
    cur = (t0 + lax.broadcasted_iota(jnp.int32, (jp, TQ), 1)) // SLC_BLOCK
    forced = (jrow == 0) | (jrow == cur) | (jrow == cur - 1)
    score = jnp.where(jrow <= cur, imp_t + jnp.where(forced, FORCE_BONUS, 0.0), NEG)
    nblk = kst_ref.shape[3] // SLC_BLOCK
    nrow = -(-nblk // 8) * 8
    sc, jr = score[:nrow], jrow[:nrow]
    rank = jnp.zeros((nrow, TQ), jnp.int32)
    for i in range(nblk):
        si = sc[i:i + 1, :]
        rank = rank + ((si > sc) | ((si == sc) & (jr > i))).astype(jnp.int32)
    sel = ((rank < n_sel) & (sc > 0.5 * NEG)).astype(jnp.float32)
    if nrow < jp:
        sel = jnp.concatenate([sel, jnp.zeros((jp - nrow, TQ), jnp.float32)], axis=0)
    selb = sel.T.astype(jnp.bfloat16)

    nchunks = (t0 + TQ + KC - 1) // KC

    def chunk(c, carry):
        m_old, l_old, acc = carry
        k0 = pl.multiple_of(c * KC, KC)
        kt = kst_ref[0, 0, :, pl.ds(k0, KC)]
        v = vs_ref[0, 0, pl.ds(k0, KC), :]
        kblk = (k0 + lax.broadcasted_iota(jnp.int32, (jp, KC), 1)) // SLC_BLOCK
        expand = (kblk == lax.broadcasted_iota(jnp.int32, (jp, KC), 0)).astype(jnp.bfloat16)
        member = jnp.dot(selb, expand, preferred_element_type=jnp.float32) > 0.5
        kpos = k0 + lax.broadcasted_iota(jnp.int32, (TQ, KC), 1)
        msk = (member & (kpos <= tpos))[None]
        s = jnp.where(msk, heads(jnp.dot(q4, kt, preferred_element_type=jnp.float32)), NEG)
        m_new = jnp.maximum(m_old, jnp.max(s, axis=-1, keepdims=True))
        alpha = jnp.exp(m_old - m_new)
        p = jnp.exp(s - m_new)
        l_new = alpha * l_old + jnp.sum(p, axis=-1, keepdims=True)
        return m_new, l_new, alpha * acc + attend(p, v)

    def chunk_pair(i, carry):
        return chunk(2 * i + 1, chunk(2 * i, carry))

    init = (jnp.full((G, TQ, 1), NEG, jnp.float32), jnp.zeros((G, TQ, 1), jnp.float32),
            jnp.zeros((G, TQ, GW), jnp.float32))
    _, l_s, acc_s = lax.fori_loop(0, (nchunks + 1) // 2, chunk_pair, init)
    os_ = acc_s / jnp.where(l_s > 0.0, l_s, 1.0)

    sig = jax.nn.sigmoid(g_ref[...].astype(jnp.float32))
    out = jnp.zeros((TQ, GW), jnp.float32)
    for h in range(G):
        o_h = (sig[:, 3 * h:3 * h + 1] * oc[h] + sig[:, 3 * h + 1:3 * h + 2] * os_[h]
               + sig[:, 3 * h + 2:3 * h + 3] * ow[h])
        out = out + jnp.where(lane_g == h, o_h, 0.0)
    o_ref[...] = out.astype(o_ref.dtype)


def nsa_core(q, z, kct, vc, kst, vs, kwt, vw, B, S):
    n = B * S
    ncp = kct.shape[3]
    ncmp = (S - CMP_BLOCK) // CMP_STRIDE + 1
    nslc = S // SLC_BLOCK
    n_sel = min(SLC_TOPK, nslc)
    jp = LANE
    assert nslc <= jp and S % (2 * KC) == 0 and S % TQ == 0 and WINDOW % TQ == 0
    cs = np.arange(ncmp) * CMP_STRIDE
    ss = np.arange(nslc) * SLC_BLOCK
    ov = np.minimum(cs[:, None] + CMP_BLOCK, ss[None, :] + SLC_BLOCK) - np.maximum(cs[:, None], ss[None, :])
    c2s = np.zeros((ncp, jp), np.float32)
    c2s[:ncmp, :nslc] = np.clip(ov, 0, None) / CMP_BLOCK
    c2st = jnp.asarray(c2s.T, dtype=jnp.bfloat16)
    nqb = S // TQ

    def kv_spec(shp):
        return pl.BlockSpec((1, 1) + shp, lambda b, k, i: (b, k, 0, 0))

    return pl.pallas_call(
        functools.partial(_nsa_kernel, n_sel=n_sel),
        grid=(B, NSA_KV_HEADS, nqb),
        in_specs=[pl.BlockSpec((TQ, GW), lambda b, k, i: (b * nqb + i, k)),
                  pl.BlockSpec((TQ, LANE), lambda b, k, i: (b * nqb + i, Z_NG // LANE + k)),
                  kv_spec((GW, ncp)), kv_spec((ncp, GW)),
                  kv_spec((GW, S)), kv_spec((S, GW)),
                  kv_spec((GW, S)), kv_spec((S, GW)),
                  pl.BlockSpec((jp, ncp), lambda b, k, i: (0, 0))],
        out_specs=pl.BlockSpec((TQ, GW), lambda b, k, i: (b * nqb + i, k)),
        out_shape=jax.ShapeDtypeStruct((n, NSA_KV_HEADS * GW), jnp.bfloat16),
        compiler_params=pltpu.CompilerParams(dimension_semantics=("parallel", "parallel", "arbitrary"),
                                             vmem_limit_bytes=VMEM_LIMIT),
        name="nsa_core",
    )(q, z, kct, vc, kst, vs, kwt, vw, c2st)


def _gla_consts():
    C = GLA_CHUNK
    i = np.arange(C)
    mats = [np.tril(np.ones((C, C), np.float32))]
    for s in GLA_LEVELS:
        r = (i // (2 * s)) * 2 * s + s - 1
        upper = (i // s) % 2 == 1
        m = np.arange(C)[None, :]
        mq = ((m > r[:, None]) & (m <= i[:, None]) & upper[:, None]).astype(np.float32)
        mk = ((m > i[:, None]) & (m <= r[:, None]) & (~upper)[:, None]).astype(np.float32)
        mats += [mq, mk]
    return np.concatenate(mats, axis=0)


def _split3(x):
    h = x.astype(jnp.bfloat16)
    r = x - h.astype(jnp.float32)
    m = r.astype(jnp.bfloat16)
    l = (r - m.astype(jnp.float32)).astype(jnp.bfloat16)
    return h, m, l


def _gla_kernel(q_ref, k_ref, v_ref, r_ref, glr_ref, gw_ref, gb_ref, ng_ref, msel_ref, o_ref, state_ref):
    C = GLA_CHUNK

    @pl.when(pl.program_id(1) == 0)
    def _():
        state_ref[...] = jnp.zeros_like(state_ref)

    ii = lax.broadcasted_iota(jnp.int32, (C, C), 0)
    jj = lax.broadcasted_iota(jnp.int32, (C, C), 1)
    lane_h = lax.broadcasted_iota(jnp.int32, (C, GLA_QK), 1) // GLA_DK
    bd = (lax.broadcasted_iota(jnp.int32, (GLA_QK, GLA_V), 0) // GLA_DK
          == lax.broadcasted_iota(jnp.int32, (GLA_QK, GLA_V), 1) // GLA_DV)
    msel = msel_ref[...]
    gw = gw_ref[...].astype(jnp.bfloat16)
    dn = (((1,), (1,)), ((), ()))

    def chunk(c, carry):
        r0 = pl.multiple_of(c * C, C)
        q = q_ref[pl.ds(r0, C), :].astype(jnp.float32) * (GLA_DK ** -0.5)
        k = k_ref[pl.ds(r0, C), :].astype(jnp.float32)
        v = v_ref[pl.ds(r0, C), :].astype(jnp.bfloat16)
        z = jnp.dot(glr_ref[pl.ds(r0, C), :].astype(jnp.bfloat16), gw, preferred_element_type=jnp.float32) + gb_ref[...]
        a = jax.nn.log_sigmoid(z) / GLA_GATE_TEMP
        ah, am, al = _split3(a)
        ex = (jnp.dot(msel, ah, preferred_element_type=jnp.float32)
              + jnp.dot(msel, am, preferred_element_type=jnp.float32)
              + jnp.dot(msel, al, preferred_element_type=jnp.float32))
        bcum = ex[0:C]
        state = state_ref[...]
        o = jnp.dot((q * jnp.exp(bcum)).astype(jnp.bfloat16), state.astype(jnp.bfloat16),
                    preferred_element_type=jnp.float32)
        kb = k.astype(jnp.bfloat16)
        for h in range(GLA_HEADS):
            hm = lane_h == h
            att = jnp.where(ii == jj, lax.dot_general(jnp.where(hm, q, 0.0).astype(jnp.bfloat16), kb, dn,
                                                      preferred_element_type=jnp.float32), 0.0)
            for li, s in enumerate(GLA_LEVELS):
                eq = jnp.exp(ex[(1 + 2 * li) * C:(2 + 2 * li) * C])
                ek = jnp.exp(ex[(2 + 2 * li) * C:(3 + 2 * li) * C])
                qs = jnp.where(hm, q * eq, 0.0).astype(jnp.bfloat16)
                ks = (k * ek).astype(jnp.bfloat16)
                blk = (ii // (2 * s) == jj // (2 * s)) & ((ii // s) % 2 == 1) & ((jj // s) % 2 == 0)
                att = att + jnp.where(blk, lax.dot_general(qs, ks, dn, preferred_element_type=jnp.float32), 0.0)
            oh = jnp.dot(att.astype(jnp.bfloat16), v, preferred_element_type=jnp.float32)
            o = o + jnp.where(lax.broadcasted_iota(jnp.int32, (C, GLA_V), 1) // GLA_DV == h, oh, 0.0)
        blast = bcum[C - 1:C, :]
        kd_t = (k * jnp.exp(blast - bcum)).T.astype(jnp.bfloat16)
        decay_col = jnp.exp(jnp.sum(a.T, axis=1, keepdims=True))
        upd = jnp.dot(kd_t, v, preferred_element_type=jnp.float32)
        state_ref[...] = state * decay_col + jnp.where(bd, upd, 0.0)
        outs = []
        for h in range(GLA_HEADS):
            oh = o[:, h * GLA_DV:(h + 1) * GLA_DV]
            outs.append(oh * lax.rsqrt(jnp.mean(oh * oh, axis=1, keepdims=True) + EPS) * ng_ref[...])
        on = jnp.concatenate(outs, axis=1)
        r = r_ref[pl.ds(r0, C), :].astype(jnp.float32)
        o_ref[pl.ds(r0, C), :] = (on * (r * jax.nn.sigmoid(r))).astype(o_ref.dtype)
        return carry

    lax.fori_loop(0, q_ref.shape[0] // C, chunk, 0)


def gla_pallas(z, gate_w, gate_b, norm_g, B, S):
    n = B * S
    tb = GLA_TB
    nb = S // tb
    gw = jnp.pad(gate_w, ((0, LANE - gate_w.shape[0]), (0, 0)))
    msel = jnp.asarray(_gla_consts(), dtype=jnp.bfloat16)
    zb = lambda col, w: pl.BlockSpec((tb, w), lambda b, i: (b * nb + i, col // w))
    full = lambda shp: pl.BlockSpec(shp, lambda b, i: (0,) * len(shp))
    return pl.pallas_call(
        _gla_kernel,
        grid=(B, nb),
        in_specs=[zb(Z_GQ, GLA_QK), zb(Z_GK, GLA_QK), zb(Z_GV, GLA_V), zb(Z_GR, GLA_V), zb(Z_GLR, LANE),
                  full((LANE, GLA_QK)), full((1, GLA_QK)), full((1, GLA_DV)), full(msel.shape)],
        out_specs=pl.BlockSpec((tb, GLA_V), lambda b, i: (b * nb + i, 0)),
        out_shape=jax.ShapeDtypeStruct((n, GLA_V), jnp.bfloat16),
        scratch_shapes=[pltpu.VMEM((GLA_QK, GLA_V), jnp.float32)],
        compiler_params=pltpu.CompilerParams(dimension_semantics=("parallel", "arbitrary")),
        name="gla",
    )(z, z, z, z, z, gw, gate_b.reshape(1, -1), norm_g.reshape(1, -1), msel)


_FULL_BLOCKS = 8


def _extract_top(s, k, tie_break, want_rank=True):
    n, t = s.shape
    row = lax.broadcasted_iota(jnp.int32, (n, t), 0)
    rank = jnp.full((n, t), k, jnp.int32) if want_rank else None
    vals = []
    for r in range(k):
        m = jnp.max(s, axis=0, keepdims=True)
        hit = s == m
        if tie_break:
            hit = row == jnp.min(jnp.where(hit, row, n), axis=0, keepdims=True)
        if want_rank:
            rank = jnp.where(hit, r, rank)
        s = jnp.where(hit, float('-inf'), s)
        vals.append(m)
    taken = jnp.sum((s == float('-inf')).astype(jnp.float32), axis=0, keepdims=True)
    return jnp.concatenate(vals, axis=0), rank, taken


def _peer_gate_math(s1, s2, tie_break):
    K = PEER_TOPK
    t = s1.shape[1]
    v1, rank1, taken1 = _extract_top(s1, K, tie_break, want_rank=tie_break)
    v2, rank2, taken2 = _extract_top(s2, K, tie_break)
    blocks, poss = [], []
    row16 = lax.broadcasted_iota(jnp.int32, (K, t), 0)
    for r1 in range(_FULL_BLOCKS):
        c = v1[r1:r1 + 1, :] + v2
        blocks.append(jnp.where(row16 < K // (r1 + 1), c, float('-inf')))
        poss.append(row16 + r1 * K)
    row8 = lax.broadcasted_iota(jnp.int32, (K - _FULL_BLOCKS, t), 0)
    blocks.append(v1[_FULL_BLOCKS:, :] + v2[0:1, :])
    poss.append((row8 + _FULL_BLOCKS) * K)
    cand = jnp.concatenate(blocks, axis=0)
    pos = jnp.concatenate(poss, axis=0)
    sel_all = jnp.zeros(cand.shape, jnp.bool_)
    big = K * K
    for _ in range(K):
        m = jnp.max(cand, axis=0, keepdims=True)
        hit = cand == m
        if tie_break:
            hit = pos == jnp.min(jnp.where(hit, pos, big), axis=0, keepdims=True)
        sel_all = sel_all | hit
        cand = jnp.where(hit, float('-inf'), cand)
    sels = [sel_all[r1 * K:(r1 + 1) * K] for r1 in range(_FULL_BLOCKS)] + [sel_all[_FULL_BLOCKS * K:]]
    top = v1[0:1, :] + v2[0:1, :]
    cnt_rows = []
    z = jnp.zeros((1, t), jnp.float32)
    for r1 in range(_FULL_BLOCKS):
        sel = sels[r1]
        cnt_rows.append(jnp.sum(sel.astype(jnp.float32), axis=0, keepdims=True))
        c = v1[r1:r1 + 1, :] + v2
        z = z + jnp.sum(jnp.where(sel, jnp.exp(c - top), 0.0), axis=0, keepdims=True)
    sel = sels[_FULL_BLOCKS]
    cnt_tail = sel.astype(jnp.float32)
    c = v1[_FULL_BLOCKS:, :] + v2[0:1, :]
    z = z + jnp.sum(jnp.where(sel, jnp.exp(c - top), 0.0), axis=0, keepdims=True)
    cnt = jnp.concatenate(cnt_rows + [cnt_tail], axis=0)
    a = jnp.zeros(s1.shape, jnp.float32)
    for r in range(K):
        a = jnp.where((rank1 == r) if tie_break else (s1 == v1[r:r + 1, :]), cnt[r:r + 1, :], a)
    taken = (taken1, taken2, jnp.sum(cnt, axis=0, keepdims=True))
    return jnp.exp(s1 - v1[0:1, :]), a, rank2.astype(jnp.float32), jnp.exp(s2 - v2[0:1, :]) / z, taken


def _peer_gate_kernel(q_ref, keys_ref, e1_ref, a_ref, r2_ref, e2_ref):
    q = q_ref[...].astype(jnp.bfloat16)
    k1 = keys_ref[0, 0].astype(jnp.bfloat16)
    k2 = keys_ref[0, 1].astype(jnp.bfloat16)
    dn = (((1,), (1,)), ((), ()))
    s1 = lax.dot_general(k1, q[:, :PEER_DKEY // 2], dn, preferred_element_type=jnp.float32)
    s2 = lax.dot_general(k2, q[:, PEER_DKEY // 2:], dn, preferred_element_type=jnp.float32)

    def write(e1, a, r2, e2):
        e1_ref[0] = e1
        a_ref[0] = a
        r2_ref[0] = r2.astype(jnp.bfloat16)
        e2_ref[0] = e2.astype(jnp.bfloat16)

    e1, a, r2, e2, taken = _peer_gate_math(s1, s2, tie_break=False)
    write(e1, a, r2, e2)
    excess = sum(jnp.max(jnp.abs(c - PEER_TOPK)) for c in taken)

    @pl.when(excess > 0.0)
    def _():
        write(*_peer_gate_math(s1, s2, tie_break=True)[:4])


def peer_gates(qh, keys, tt=512):
    n = qh.shape[0]
    H = PEER_HEADS
    shp32 = jax.ShapeDtypeStruct((H, PEER_NKEYS, n), jnp.float32)
    shp16 = jax.ShapeDtypeStruct((H, PEER_NKEYS, n), jnp.bfloat16)
    ospec = pl.BlockSpec((1, PEER_NKEYS, tt), lambda i, h: (h, 0, i))
    return pl.pallas_call(
        _peer_gate_kernel,
        grid=(n // tt, H),
        in_specs=[pl.BlockSpec((tt, PEER_DKEY), lambda i, h: (i, h)),
                  pl.BlockSpec((1, 2, PEER_NKEYS, PEER_DKEY // 2), lambda i, h: (h, 0, 0, 0))],
        out_specs=[ospec, ospec, ospec, ospec],
        out_shape=[shp32, shp32, shp16, shp16],
        compiler_params=pltpu.CompilerParams(dimension_semantics=("parallel", "parallel")),
        name="peer_gates",
    )(qh, keys)


def _row_bf16(row, n):
    t = row.shape[1]
    tile = jnp.broadcast_to(row, (16, t)).astype(jnp.bfloat16)
    return jnp.concatenate([tile] * (n // 16), axis=0)


def _peer_dense_kernel(xt_ref, u_ref, vt_ref, e1_ref, a_ref, r2_ref, e2_ref, x1_ref, gate_ref, fg_ref,
                       o_ref, acc_ref, *, ec, final):
    j = pl.program_id(1)
    nsub = ec // PEER_NKEYS

    @pl.when(j == 0)
    def _():
        acc_ref[...] = jnp.zeros_like(acc_ref)

    assert nsub % 8 == 0
    r8 = pl.multiple_of(j * nsub, 8)
    a_tiles = [[a_ref[h, pl.ds(r8 + 8 * t, 8), :] for t in range(nsub // 8)] for h in range(PEER_HEADS)]
    e1_tiles = [[e1_ref[h, pl.ds(r8 + 8 * t, 8), :] for t in range(nsub // 8)] for h in range(PEER_HEADS)]

    def gate_mat(i):
        g = None
        for h in range(PEER_HEADS):
            cnt = _row_bf16(a_tiles[h][i // 8][i % 8:i % 8 + 1, :], PEER_NKEYS)
            e1 = _row_bf16(e1_tiles[h][i // 8][i % 8:i % 8 + 1, :], PEER_NKEYS)
            term = jnp.where(r2_ref[h] < cnt, e2_ref[h], jnp.bfloat16(0)) * e1
            g = term if g is None else g + term
        return g

    gs = PEER_GROUPS_PER_SCORE

    def scores(p):
        r0 = p * gs * PEER_NKEYS
        return jnp.dot(u_ref[r0:r0 + gs * PEER_NKEYS, :], xt_ref[...], preferred_element_type=jnp.float32)

    st = scores(0)
    ws = []
    for p in range(nsub // gs):
        st_next = scores(p + 1) if (p + 1) * gs < nsub else None
        for k in range(gs):
            i = p * gs + k
            ws.append(_gelu(st[k * PEER_NKEYS:(k + 1) * PEER_NKEYS].astype(jnp.bfloat16)) * gate_mat(i))
            if i % PEER_GROUPS_PER_DOT == PEER_GROUPS_PER_DOT - 1:
                w = jnp.concatenate(ws, axis=0)
                ws = []
                acc_ref[...] += jnp.dot(
                    vt_ref[:, (i + 1 - PEER_GROUPS_PER_DOT) * PEER_NKEYS:(i + 1) * PEER_NKEYS], w,
                    preferred_element_type=jnp.float32)
        st = st_next

    @pl.when(j == pl.num_programs(1) - 1)
    def _():
        x2 = x1_ref[...] + gate_ref[0] * acc_ref[...].T
        if final:
            x2 = x2 * lax.rsqrt(jnp.mean(x2 * x2, axis=1, keepdims=True) + EPS) * fg_ref[...]
        o_ref[...] = x2


def peer_dense(xt, u16, vt16, e1, a, r2, e2, x1, gate2, final_g, S, final, tt=512, ec=2048):
    d, n = xt.shape
    ne = u16.shape[0]
    H = PEER_HEADS
    per_b = S // tt
    gspec = lambda: pl.BlockSpec((H, PEER_NKEYS, tt), lambda i, j: (0, 0, i))
    return pl.pallas_call(
        functools.partial(_peer_dense_kernel, ec=ec, final=final),
        grid=(n // tt, ne // ec),
        in_specs=[pl.BlockSpec((d, tt), lambda i, j: (0, i)),
                  pl.BlockSpec((ec, d), lambda i, j: (j, 0)),
                  pl.BlockSpec((d, ec), lambda i, j: (0, j)),
                  gspec(), gspec(), gspec(), gspec(),
                  pl.BlockSpec((tt, d), lambda i, j: (i, 0)),
                  pl.BlockSpec((1, 1, d), lambda i, j: (i // per_b, 0, 0)),
                  pl.BlockSpec((1, d), lambda i, j: (0, 0))],
        out_specs=pl.BlockSpec((tt, d), lambda i, j: (i, 0)),
        out_shape=jax.ShapeDtypeStruct((n, d), jnp.float32),
        scratch_shapes=[pltpu.VMEM((d, tt), jnp.float32)],
        compiler_params=pltpu.CompilerParams(dimension_semantics=("parallel", "arbitrary"),
                                             vmem_limit_bytes=VMEM_LIMIT),
        name="peer_dense",
    )(xt, u16, vt16, e1, a, r2, e2, x1, gate2, final_g.reshape(1, d))


def adaln(c, w, b):
    mod = pmm(jax.nn.silu(c), w) + b
    shift, scale, gate = jnp.split(mod[:, None, :], 3, axis=-1)
    return shift, 1.0 + scale, gate


def kernel(x, c, positions, ada_w, ada_b, norm_g, w_in, conv_dw_w, conv_dw_b, conv_ln_g, conv_ln_b, w_conv_up,
           cmp_pos, cmp_w1, cmp_w2, w_nsa_up, gla_gate_w, gla_gate_b, gla_norm_g, w_gla_up, w_out,
           peer_wq, peer_keys, peer_u, peer_v, final_g):
    B, S, D = x.shape
    bf = jnp.bfloat16
    x2 = x.reshape(B * S, D)
    cs = rope_tables(positions)
    for l in range(DEPTH):
        shift, scale1p, gate = adaln(c, ada_w[l, 0], ada_b[l, 0])
        z = in_proj(x2, norm_g[l, 0], scale1p, shift, pack_w_in(w_in[l]), S)
        hc = conv_pallas(z, conv_dw_w[l], conv_dw_b[l], conv_ln_g[l], conv_ln_b[l], B, S)
        q, kst, vs, kwt, vw = nsa_prep(z, cs, B, S)
        kct, vc = nsa_compress(z, positions, cmp_pos[l], cmp_w1[l], cmp_w2[l], B, S)
        on = nsa_core(q, z, kct, vc, kst, vs, kwt, vw, B, S)
        og = gla_pallas(z, gla_gate_w[l], gla_gate_b[l], gla_norm_g[l], B, S)
        shift2, scale2p, gate2 = adaln(c, ada_w[l, 1], ada_b[l, 1])
        x1, ut, qh = mix_out(hc, on, og, z, x2, gate, w_conv_up[l].astype(bf), w_nsa_up[l].astype(bf),
                             w_gla_up[l].astype(bf), w_out[l].astype(bf), norm_g[l, 1], scale2p, shift2,
                             peer_wq[l].astype(bf), S)
        e1, a, r2, e2 = peer_gates(qh, peer_keys[l])
        x2 = peer_dense(ut, peer_u[l].astype(bf), peer_v[l].astype(bf).T, e1, a, r2, e2, x1, gate2, final_g,
                        S, final=(l == DEPTH - 1))
    return x2.reshape(B, S, D)
```

```python
import functools

import jax
import jax.numpy as jnp
import numpy as np
from jax import lax
from jax.experimental import pallas as pl
from jax.experimental.pallas import tpu as pltpu

D_MODEL = 1024
DEPTH = 2
EPS = 1e-6
NEG = -1e30
N_BRANCH = 3
CONV_DIM = 512
CONV_WIDTH = 31
NSA_HEADS = 8
NSA_KV_HEADS = 2
NSA_GROUP = NSA_HEADS // NSA_KV_HEADS
HEAD_DIM = 64
CMP_BLOCK = 32
CMP_STRIDE = 16
CMP_HIDDEN = 256
SLC_BLOCK = 64
SLC_TOPK = 16
WINDOW = 512
FORCE_BONUS = 1e4
ROPE_THETA = 10000.0
GLA_HEADS = 4
GLA_DK = 64
GLA_DV = 128
GLA_GATE_RANK = 16
GLA_GATE_TEMP = 16.0
GLA_CHUNK = 256
PEER_HEADS = 8
PEER_NKEYS = 128
PEER_DKEY = 256
PEER_TOPK = 16
NSA_Q = NSA_HEADS * HEAD_DIM
NSA_KV = 3 * 2 * NSA_KV_HEADS * HEAD_DIM
NSA_G = 3 * NSA_HEADS
GLA_QK = GLA_HEADS * GLA_DK
GLA_V = GLA_HEADS * GLA_DV
SPLITS = [2 * CONV_DIM, NSA_Q, NSA_KV, NSA_G, GLA_QK, GLA_QK, GLA_V, GLA_V, GLA_GATE_RANK, N_BRANCH * D_MODEL]

LANE = 128
TQ = 128
KC = 512
GW = NSA_GROUP * HEAD_DIM
VMEM_LIMIT = 48 * 1024 * 1024
GLA_LEVELS = tuple(GLA_CHUNK >> s for s in range(1, GLA_CHUNK.bit_length()))
GLA_TB = 512
CONV_TS = 512
CONV_HALO = 32
ROW_TILE = 1024
MIX_TILE = 512
PEER_GROUPS_PER_DOT = 2
PEER_GROUPS_PER_SCORE = 8

Z_MG = 0
Z_CONV = Z_MG + N_BRANCH * D_MODEL
Z_Q = Z_CONV + 2 * CONV_DIM
Z_QSW = Z_Q + NSA_Q
Z_GV = Z_QSW + NSA_Q
Z_GR = Z_GV + GLA_V
Z_GQ = Z_GR + GLA_V
Z_GK = Z_GQ + GLA_QK
Z_CMP = Z_GK + GLA_QK
Z_SLC = Z_CMP + 256
Z_WIN = Z_SLC + 384
Z_NG = Z_WIN + 384
Z_GLR = Z_NG + NSA_KV_HEADS * LANE
NZ = Z_GLR + LANE
assert Z_SLC % 384 == 0 and Z_WIN % 384 == 0 and Z_CONV % (2 * CONV_DIM) == 0 and Z_GQ % GLA_QK == 0


def _mm_kernel(a_ref, b_ref, o_ref):
    o_ref[...] = jnp.dot(a_ref[...].astype(jnp.bfloat16), b_ref[...].astype(jnp.bfloat16),
                         preferred_element_type=jnp.float32)


def _pick(n, cands):
    for c in cands:
        if n % c == 0:
            return c
    return n


def pmm(a, b):
    m, k = a.shape
    n = b.shape[1]
    n_pad = -n % LANE
    if n_pad:
        b = jnp.pad(b, ((0, 0), (0, n_pad)))
    m_pad = -m % 8
    if m_pad:
        a = jnp.pad(a, ((0, m_pad), (0, 0)))
    mp, np_ = m + m_pad, n + n_pad
    tm = _pick(mp, (512, 256, 128, 64, 32, 16, 8))
    tn = _pick(np_, (512, 640, 384, 256, 128))
    out = pl.pallas_call(
        _mm_kernel,
        grid=(mp // tm, np_ // tn),
        in_specs=[pl.BlockSpec((tm, k), lambda i, j: (i, 0)),
                  pl.BlockSpec((k, tn), lambda i, j: (0, j))],
        out_specs=pl.BlockSpec((tm, tn), lambda i, j: (i, j)),
        out_shape=jax.ShapeDtypeStruct((mp, np_), jnp.float32),
        compiler_params=pltpu.CompilerParams(dimension_semantics=("parallel", "parallel")),
        name="pmm",
    )(a, b)
    return out[:m, :n]


def _in_proj_kernel(x_ref, g_ref, sc_ref, sh_ref, w_ref, z_ref, u_ref):
    @pl.when(pl.program_id(1) == 0)
    def _():
        x = x_ref[...]
        y = x * lax.rsqrt(jnp.mean(x * x, axis=1, keepdims=True) + EPS) * g_ref[...]
        u_ref[...] = (y * sc_ref[0] + sh_ref[0]).astype(jnp.bfloat16)

    z_ref[...] = jnp.dot(u_ref[...], w_ref[...], preferred_element_type=jnp.float32).astype(z_ref.dtype)


def in_proj(x2, g, scale1p, shift, w16, S):
    n, d = x2.shape
    nz = w16.shape[1]
    tm = ROW_TILE
    tn = _pick(nz, (2688, 1152, 1024, 896, 512, 384, 256, 128))
    per_b = S // tm
    mod = lambda: pl.BlockSpec((1, 1, d), lambda i, j: (i // per_b, 0, 0))
    return pl.pallas_call(
        _in_proj_kernel,
        grid=(n // tm, nz // tn),
        in_specs=[pl.BlockSpec((tm, d), lambda i, j: (i, 0)),
                  pl.BlockSpec((1, d), lambda i, j: (0, 0)), mod(), mod(),
                  pl.BlockSpec((d, tn), lambda i, j: (0, j))],
        out_specs=pl.BlockSpec((tm, tn), lambda i, j: (i, j)),
        out_shape=jax.ShapeDtypeStruct((n, nz), jnp.bfloat16),
        scratch_shapes=[pltpu.VMEM((tm, d), jnp.bfloat16)],
        compiler_params=pltpu.CompilerParams(dimension_semantics=("parallel", "arbitrary"),
                                             vmem_limit_bytes=VMEM_LIMIT),
        name="in_proj",
    )(x2, g.reshape(1, d), scale1p, shift, w16)


def _half_swap(n_heads):
    idx = np.arange(n_heads * HEAD_DIM).reshape(n_heads, 2, HEAD_DIM // 2)
    return idx[:, ::-1, :].reshape(-1)


def pack_w_in(w_in):
    o = np.cumsum([0] + SPLITS)
    a_conv, nq, nkv, ng, gq, gk, gv, gr, glr, mg = [w_in[:, o[i]:o[i + 1]] for i in range(len(SPLITS))]
    kvw = 2 * NSA_KV_HEADS * HEAD_DIM
    kw = NSA_KV_HEADS * HEAD_DIM
    d = w_in.shape[0]
    sw = _half_swap(NSA_KV_HEADS)
    cols = [mg, a_conv, nq, nq[:, _half_swap(NSA_HEADS)], gv, gr, gq, gk, nkv[:, 0:kvw]]
    for br in (1, 2):
        k = nkv[:, br * kvw:br * kvw + kw]
        cols += [k, k[:, sw], nkv[:, br * kvw + kw:(br + 1) * kvw]]
    gpk = NSA_GROUP * 3
    for kvh in range(NSA_KV_HEADS):
        cols += [ng[:, kvh * gpk:(kvh + 1) * gpk], jnp.zeros((d, LANE - gpk), w_in.dtype)]
    cols += [glr, jnp.zeros((d, LANE - GLA_GATE_RANK), w_in.dtype)]
    w = jnp.concatenate(cols, axis=1)
    assert w.shape[1] == NZ
    return w.astype(jnp.bfloat16)


def rope_tables(positions):
    half = HEAD_DIM // 2
    freq = ROPE_THETA ** (-jnp.arange(half, dtype=jnp.float32) / half)
    ang = positions.astype(jnp.float32)[..., None] * freq
    cos, sin = jnp.cos(ang), jnp.sin(ang)
    c = jnp.tile(jnp.concatenate([cos, cos], -1), (1, 1, LANE // HEAD_DIM))
    s = jnp.tile(jnp.concatenate([-sin, sin], -1), (1, 1, LANE // HEAD_DIM))
    return jnp.concatenate([c, s], -1).reshape(-1, 2 * LANE)


def _nsa_prep_kernel(q_ref, qs_ref, slc_ref, win_ref, cs_ref, sel_ref, qo_ref, kst_ref, vs_ref, kwt_ref, vw_ref):
    cos = cs_ref[:, :LANE]
    sin = cs_ref[:, LANE:]
    scale = HEAD_DIM ** -0.5
    for t in range(NSA_Q // LANE):
        sl = slice(t * LANE, (t + 1) * LANE)
        qo_ref[:, sl] = ((q_ref[:, sl] * cos + qs_ref[:, sl] * sin) * scale).astype(jnp.bfloat16)
    dn = (((1,), (1,)), ((), ()))
    for src, kt_ref, v_ref in ((slc_ref, kst_ref, vs_ref), (win_ref, kwt_ref, vw_ref)):
        k = (src[:, 0:LANE] * cos + src[:, LANE:2 * LANE] * sin).astype(jnp.bfloat16)
        v = src[:, 2 * LANE:3 * LANE].astype(jnp.bfloat16)
        for kvh in range(NSA_KV_HEADS):
            sel = sel_ref[kvh]
            kt_ref[0, kvh] = lax.dot_general(sel, k, dn, preferred_element_type=jnp.float32).astype(jnp.bfloat16)
            v_ref[0, kvh] = lax.dot_general(v, sel, dn, preferred_element_type=jnp.float32).astype(jnp.bfloat16)


def _head_repeat_sel():
    sel = np.zeros((NSA_KV_HEADS, GW, LANE), np.float32)
    for kvh in range(NSA_KV_HEADS):
        for r in range(GW):
            sel[kvh, r, kvh * HEAD_DIM + r % HEAD_DIM] = 1.0
    return jnp.asarray(sel, dtype=jnp.bfloat16)


def nsa_prep(z, cs, B, S):
    n = B * S
    tp = ROW_TILE
    nb = S // tp
    zb = lambda col, w: pl.BlockSpec((tp, w), lambda b, i: (b * nb + i, col // w))
    kt = lambda: pl.BlockSpec((1, NSA_KV_HEADS, GW, tp), lambda b, i: (b, 0, 0, i))
    vv = lambda: pl.BlockSpec((1, NSA_KV_HEADS, tp, GW), lambda b, i: (b, 0, i, 0))
    kt_shape = jax.ShapeDtypeStruct((B, NSA_KV_HEADS, GW, S), jnp.bfloat16)
    v_shape = jax.ShapeDtypeStruct((B, NSA_KV_HEADS, S, GW), jnp.bfloat16)
    return pl.pallas_call(
        _nsa_prep_kernel,
        grid=(B, nb),
        in_specs=[zb(Z_Q, NSA_Q), zb(Z_QSW, NSA_Q), zb(Z_SLC, 384), zb(Z_WIN, 384),
                  pl.BlockSpec((tp, 2 * LANE), lambda b, i: (b * nb + i, 0)),
                  pl.BlockSpec((NSA_KV_HEADS, GW, LANE), lambda b, i: (0, 0, 0))],
        out_specs=[pl.BlockSpec((tp, NSA_Q), lambda b, i: (b * nb + i, 0)), kt(), vv(), kt(), vv()],
        out_shape=[jax.ShapeDtypeStruct((n, NSA_Q), jnp.bfloat16), kt_shape, v_shape, kt_shape, v_shape],
        compiler_params=pltpu.CompilerParams(dimension_semantics=("parallel", "parallel")),
        name="nsa_prep",
    )(z, z, z, z, cs, _head_repeat_sel())


def _gelu(x):
    return 0.5 * x * (1.0 + jnp.tanh(0.7978845608028654 * (x + 0.044715 * x * x * x)))


def _compress_kernel(gk_ref, gv_ref, pe_ref, w1_ref, w2_ref, cs_ref, kct_ref, vc_ref):
    nrow = gk_ref.shape[2]
    outs = []
    for kv, g_ref in enumerate((gk_ref, gv_ref)):
        g = g_ref[0, 0]
        lo = jnp.dot((g + pe_ref[kv, 0]).astype(jnp.bfloat16), w1_ref[kv, 0], preferred_element_type=jnp.float32)
        hi = jnp.dot((g + pe_ref[kv, 1]).astype(jnp.bfloat16), w1_ref[kv, 1], preferred_element_type=jnp.float32)
        hid = _gelu(lo + pltpu.roll(hi, nrow - 1, 0)).astype(jnp.bfloat16)
        outs.append(jnp.dot(hid, w2_ref[kv], preferred_element_type=jnp.float32))
    k = outs[0][:, :GW] * cs_ref[0, :, :GW] + outs[0][:, GW:] * cs_ref[0, :, GW:]
    kct_ref[0, 0] = k.T.astype(jnp.bfloat16)
    vc_ref[0, 0] = outs[1][:, :GW].astype(jnp.bfloat16)


def nsa_compress(z, positions, cmp_pos, cmp_w1, cmp_w2, B, S):
    ng = S // CMP_STRIDE
    grp = CMP_STRIDE * HEAD_DIM
    c = z[:, Z_CMP:Z_CMP + 256].astype(jnp.float32).reshape(B, ng, CMP_STRIDE, 2, NSA_KV_HEADS, HEAD_DIM)
    g = c.transpose(3, 0, 4, 1, 2, 5).reshape(2, B, NSA_KV_HEADS, ng, grp)
    pe = cmp_pos.reshape(2, 2, 1, grp)
    w1 = cmp_w1.reshape(2, 2, grp, CMP_HIDDEN).astype(jnp.bfloat16)
    rep = jnp.tile(cmp_w2, (1, 1, NSA_GROUP))
    sw = np.tile(_half_swap(1), NSA_GROUP) + np.repeat(np.arange(NSA_GROUP) * HEAD_DIM, HEAD_DIM)
    w2 = jnp.concatenate([rep, rep[:, :, sw]], axis=-1).astype(jnp.bfloat16)
    end = jnp.minimum(jnp.arange(ng) * CMP_STRIDE + CMP_BLOCK - 1, S - 1)
    half = HEAD_DIM // 2
    freq = ROPE_THETA ** (-jnp.arange(half, dtype=jnp.float32) / half)
    ang = positions[:, end].astype(jnp.float32)[..., None] * freq
    cos, sin = jnp.cos(ang), jnp.sin(ang)
    cs = jnp.concatenate([jnp.tile(jnp.concatenate([cos, cos], -1), (1, 1, NSA_GROUP)),
                          jnp.tile(jnp.concatenate([-sin, sin], -1), (1, 1, NSA_GROUP))], -1)
    gspec = lambda: pl.BlockSpec((1, 1, ng, grp), lambda b, k: (b, k, 0, 0))
    return pl.pallas_call(
        _compress_kernel,
        grid=(B, NSA_KV_HEADS),
        in_specs=[gspec(), gspec(),
                  pl.BlockSpec((2, 2, 1, grp), lambda b, k: (0, 0, 0, 0)),
                  pl.BlockSpec((2, 2, grp, CMP_HIDDEN), lambda b, k: (0, 0, 0, 0)),
                  pl.BlockSpec((2, CMP_HIDDEN, 2 * GW), lambda b, k: (0, 0, 0)),
                  pl.BlockSpec((1, ng, 2 * GW), lambda b, k: (b, 0, 0))],
        out_specs=[pl.BlockSpec((1, 1, GW, ng), lambda b, k: (b, k, 0, 0)),
                   pl.BlockSpec((1, 1, ng, GW), lambda b, k: (b, k, 0, 0))],
        out_shape=[jax.ShapeDtypeStruct((B, NSA_KV_HEADS, GW, ng), jnp.bfloat16),
                   jax.ShapeDtypeStruct((B, NSA_KV_HEADS, ng, GW), jnp.bfloat16)],
        compiler_params=pltpu.CompilerParams(dimension_semantics=("parallel", "parallel")),
        name="nsa_compress",
    )(g[0], g[1], pe, w1, w2, cs)


def _mix_kernel(hc_ref, on_ref, og_ref, mg_ref, x_ref, gate_ref, wc_ref, wn_ref, wg_ref, wo_ref,
                g2_ref, sc_ref, sh_ref, wq_ref, x1_ref, ut_ref, qh_ref):
    d = D_MODEL
    f32 = jnp.float32
    y = (jax.nn.sigmoid(mg_ref[:, 0:d].astype(f32)) * jnp.dot(hc_ref[...], wc_ref[...], preferred_element_type=f32)
         + jax.nn.sigmoid(mg_ref[:, d:2 * d].astype(f32)) * jnp.dot(on_ref[...], wn_ref[...], preferred_element_type=f32)
         + jax.nn.sigmoid(mg_ref[:, 2 * d:3 * d].astype(f32))
         * jnp.dot(og_ref[...], wg_ref[...], preferred_element_type=f32))
    x1 = x_ref[...] + gate_ref[0] * jnp.dot(y.astype(jnp.bfloat16), wo_ref[...], preferred_element_type=f32)
    x1_ref[...] = x1
    u = x1 * lax.rsqrt(jnp.mean(x1 * x1, axis=1, keepdims=True) + EPS) * g2_ref[...] * sc_ref[0] + sh_ref[0]
    ut_ref[...] = u.T.astype(jnp.bfloat16)
    qh_ref[...] = jnp.dot(u.astype(jnp.bfloat16), wq_ref[...], preferred_element_type=f32).astype(qh_ref.dtype)


def mix_out(hc, on, og, z, x2, gate1, wc, wn, wg, wo, g2, scale2p, shift2, wq, S):
    n, d = x2.shape
    tm = MIX_TILE
    per_b = S // tm
    row = lambda w: pl.BlockSpec((tm, w), lambda i: (i, 0))
    mod = lambda: pl.BlockSpec((1, 1, d), lambda i: (i // per_b, 0, 0))
    full = lambda a: pl.BlockSpec(a.shape, lambda i: (0,) * a.ndim, pipeline_mode=pl.Buffered(1))
    nq = wq.shape[1]
    return pl.pallas_call(
        _mix_kernel,
        grid=(n // tm,),
        in_specs=[row(CONV_DIM), row(NSA_Q), row(GLA_V),
                  pl.BlockSpec((tm, N_BRANCH * d), lambda i: (i, Z_MG // (N_BRANCH * d))),
                  row(d), mod(), full(wc), full(wn), full(wg), full(wo),
                  pl.BlockSpec((1, d), lambda i: (0, 0)), mod(), mod(), full(wq)],
        out_specs=[row(d), pl.BlockSpec((d, tm), lambda i: (0, i)), row(nq)],
        out_shape=[jax.ShapeDtypeStruct((n, d), jnp.float32), jax.ShapeDtypeStruct((d, n), jnp.bfloat16),
                   jax.ShapeDtypeStruct((n, nq), jnp.bfloat16)],
        compiler_params=pltpu.CompilerParams(dimension_semantics=("parallel",), vmem_limit_bytes=VMEM_LIMIT),
        name="mix_out",
    )(hc, on, og, z, x2, gate1, wc, wn, wg, wo, g2.reshape(1, d), scale2p, shift2, wq)


def _conv_kernel(a_ref, ah_ref, w_ref, b_ref, g_ref, beta_ref, o_ref, hbuf):
    i = pl.program_id(1)
    ts = a_ref.shape[0]
    a = a_ref[...].astype(jnp.float32)
    hbuf[pl.ds(CONV_HALO, ts), :] = a[:, :CONV_DIM] * jax.nn.sigmoid(a[:, CONV_DIM:])
    ah = ah_ref[...].astype(jnp.float32)
    halo = ah[:, :CONV_DIM] * jax.nn.sigmoid(ah[:, CONV_DIM:])
    hbuf[pl.ds(0, CONV_HALO), :] = jnp.where(i > 0, halo, 0.0)
    acc = jnp.zeros((ts, CONV_DIM), jnp.float32) + b_ref[...]
    off = CONV_HALO - (CONV_WIDTH - 1)
    for k in range(CONV_WIDTH):
        acc = acc + w_ref[k:k + 1, :] * hbuf[pl.ds(off + k, ts), :]
    mu = jnp.mean(acc, axis=1, keepdims=True)
    d = acc - mu
    var = jnp.mean(d * d, axis=1, keepdims=True)
    y = d * lax.rsqrt(var + EPS) * g_ref[...] + beta_ref[...]
    o_ref[...] = (y * jax.nn.sigmoid(y)).astype(o_ref.dtype)


def conv_pallas(z, dw_w, dw_b, ln_g, ln_b, B, S):
    n = B * S
    ts = CONV_TS
    ns = S // ts
    hb = ts // CONV_HALO
    cb = Z_CONV // (2 * CONV_DIM)
    wpad = jnp.pad(dw_w, ((0, 32 - CONV_WIDTH), (0, 0)))
    vec = lambda: pl.BlockSpec((1, CONV_DIM), lambda b, i: (0, 0))
    return pl.pallas_call(
        _conv_kernel,
        grid=(B, ns),
        in_specs=[pl.BlockSpec((ts, 2 * CONV_DIM), lambda b, i: (b * ns + i, cb)),
                  pl.BlockSpec((CONV_HALO, 2 * CONV_DIM), lambda b, i: (jnp.maximum((b * ns + i) * hb - 1, 0), cb)),
                  pl.BlockSpec((32, CONV_DIM), lambda b, i: (0, 0)), vec(), vec(), vec()],
        out_specs=pl.BlockSpec((ts, CONV_DIM), lambda b, i: (b * ns + i, 0)),
        out_shape=jax.ShapeDtypeStruct((n, CONV_DIM), jnp.bfloat16),
        scratch_shapes=[pltpu.VMEM((ts + CONV_HALO, CONV_DIM), jnp.float32)],
        compiler_params=pltpu.CompilerParams(dimension_semantics=("parallel", "parallel")),
        name="conformer_conv",
    )(z, z, wpad, dw_b.reshape(1, -1), ln_g.reshape(1, -1), ln_b.reshape(1, -1))


def _msoftmax(s, mask):
    s = jnp.where(mask, s, NEG)
    m = jnp.max(s, axis=-1, keepdims=True)
    e = jnp.where(mask, jnp.exp(s - m), 0.0)
    l = jnp.sum(e, axis=-1, keepdims=True)
    return e / jnp.where(l > 0.0, l, 1.0)


def _nsa_kernel(q_ref, g_ref, kct_ref, vc_ref, kst_ref, vs_ref, kwt_ref, vw_ref, c2st_ref, o_ref, *, n_sel):
    G = NSA_GROUP
    qb = pl.program_id(2)
    t0 = pl.multiple_of(qb * TQ, TQ)
    q = q_ref[...]
    lane_g = lax.broadcasted_iota(jnp.int32, (TQ, GW), 1) // HEAD_DIM
    q4 = jnp.concatenate([jnp.where(lane_g == h, q, jnp.zeros_like(q)) for h in range(G)], axis=0)
    tpos = t0 + lax.broadcasted_iota(jnp.int32, (TQ, 1), 0)

    def heads(x):
        return x.reshape(G, TQ, x.shape[-1])

    def attend(p, v):
        return heads(jnp.dot(p.reshape(G * TQ, p.shape[-1]).astype(jnp.bfloat16), v,
                             preferred_element_type=jnp.float32))

    starts = [t0 - WINDOW + TQ * c for c in range(WINDOW // TQ + 1)]
    reads = [pl.multiple_of(jnp.maximum(s, 0), TQ) for s in starts]
    kwt = jnp.concatenate([kwt_ref[0, 0, :, pl.ds(r, TQ)] for r in reads], axis=1)
    vw = jnp.concatenate([vw_ref[0, 0, pl.ds(r, TQ), :] for r in reads], axis=0)
    wk = WINDOW + TQ
    kposw = t0 - WINDOW + lax.broadcasted_iota(jnp.int32, (TQ, wk), 1)
    diff = tpos - kposw
    maskw = (kposw >= 0) & (diff >= 0) & (diff < WINDOW)

    kct = kct_ref[0, 0]
    ncp = kct.shape[1]
    ncol = lax.broadcasted_iota(jnp.int32, (TQ, ncp), 1)
    maskc = (ncol * CMP_STRIDE + (CMP_BLOCK - 1)) <= tpos
    sc = heads(jnp.dot(q4, kct, preferred_element_type=jnp.float32))
    sw = heads(jnp.dot(q4, kwt, preferred_element_type=jnp.float32))
    pc = _msoftmax(sc, maskc[None])
    oc = attend(pc, vc_ref[0, 0])
    pw = _msoftmax(sw, maskw[None])
    ow = attend(pw, vw)
    pc_sum = jnp.sum(pc, axis=0)

    c2st = c2st_ref[...]
    jp = c2st.shape[0]
    hi = pc_sum.astype(jnp.bfloat16)
    lo = (pc_sum - hi.astype(jnp.float32)).astype(jnp.bfloat16)
    dn = (((1,), (1,)), ((), ()))
    imp_t = (lax.dot_general(c2st, hi, dn, preferred_element_type=jnp.float32)
             + lax.dot_general(c2st, lo, dn, preferred_element_type=jnp.float32))
    jrow = lax.broadcasted_iota(jnp.int32, (jp, TQ), 0)
    cur = (t0 + lax.broadcasted_iota(jnp.int32, (jp, TQ), 1)) // SLC_BLOCK
    forced = (jrow == 0) | (jrow == cur) | (jrow == cur - 1)
    score = jnp.where(jrow <= cur, imp_t + jnp.where(forced, FORCE_BONUS, 0.0), NEG)
    nblk = kst_ref.shape[3] // SLC_BLOCK
    nrow = -(-nblk // 8) * 8
    sc, jr = score[:nrow], jrow[:nrow]
    rank = jnp.zeros((nrow, TQ), jnp.int32)
    for i in range(nblk):
        si = sc[i:i + 1, :]
        rank = rank + ((si > sc) | ((si == sc) & (jr > i))).astype(jnp.int32)
    sel = ((rank < n_sel) & (sc > 0.5 * NEG)).astype(jnp.float32)
    if nrow < jp:
        sel = jnp.concatenate([sel, jnp.zeros((jp - nrow, TQ), jnp.float32)], axis=0)
    selb = sel.T.astype(jnp.bfloat16)

    nchunks = (t0 + TQ + KC - 1) // KC

    def chunk(c, carry):
        m_old, l_old, acc = carry
        k0 = pl.multiple_of(c * KC, KC)
        kt = kst_ref[0, 0, :, pl.ds(k0, KC)]
        v = vs_ref[0, 0, pl.ds(k0, KC), :]
        kblk = (k0 + lax.broadcasted_iota(jnp.int32, (jp, KC), 1)) // SLC_BLOCK
        expand = (kblk == lax.broadcasted_iota(jnp.int32, (jp, KC), 0)).astype(jnp.bfloat16)
        member = jnp.dot(selb, expand, preferred_element_type=jnp.float32) > 0.5
        kpos = k0 + lax.broadcasted_iota(jnp.int32, (TQ, KC), 1)
        msk = (member & (kpos <= tpos))[None]
        s = jnp.where(msk, heads(jnp.dot(q4, kt, preferred_element_type=jnp.float32)), NEG)
        m_new = jnp.maximum(m_old, jnp.max(s, axis=-1, keepdims=True))
        alpha = jnp.exp(m_old - m_new)
        p = jnp.exp(s - m_new)
        l_new = alpha * l_old + jnp.sum(p, axis=-1, keepdims=True)
        return m_new, l_new, alpha * acc + attend(p, v)

    def chunk_pair(i, carry):
        return chunk(2 * i + 1, chunk(2 * i, carry))

    init = (jnp.full((G, TQ, 1), NEG, jnp.float32), jnp.zeros((G, TQ, 1), jnp.float32),
            jnp.zeros((G, TQ, GW), jnp.float32))
    _, l_s, acc_s = lax.fori_loop(0, (nchunks + 1) // 2, chunk_pair, init)
    os_ = acc_s / jnp.where(l_s > 0.0, l_s, 1.0)

    sig = jax.nn.sigmoid(g_ref[...].astype(jnp.float32))
    out = jnp.zeros((TQ, GW), jnp.float32)
    for h in range(G):
        o_h = (sig[:, 3 * h:3 * h + 1] * oc[h] + sig[:, 3 * h + 1:3 * h + 2] * os_[h]
               + sig[:, 3 * h + 2:3 * h + 3] * ow[h])
        out = out + jnp.where(lane_g == h, o_h, 0.0)
    o_ref[...] = out.astype(o_ref.dtype)


def nsa_core(q, z, kct, vc, kst, vs, kwt, vw, B, S):
    n = B * S
    ncp = kct.shape[3]
    ncmp = (S - CMP_BLOCK) // CMP_STRIDE + 1
    nslc = S // SLC_BLOCK
    n_sel = min(SLC_TOPK, nslc)
    jp = LANE
    assert nslc <= jp and S % (2 * KC) == 0 and S % TQ == 0 and WINDOW % TQ == 0
    cs = np.arange(ncmp) * CMP_STRIDE
    ss = np.arange(nslc) * SLC_BLOCK
    ov = np.minimum(cs[:, None] + CMP_BLOCK, ss[None, :] + SLC_BLOCK) - np.maximum(cs[:, None], ss[None, :])
    c2s = np.zeros((ncp, jp), np.float32)
    c2s[:ncmp, :nslc] = np.clip(ov, 0, None) / CMP_BLOCK
    c2st = jnp.asarray(c2s.T, dtype=jnp.bfloat16)
    nqb = S // TQ

    def kv_spec(shp):
        return pl.BlockSpec((1, 1) + shp, lambda b, k, i: (b, k, 0, 0))

    return pl.pallas_call(
        functools.partial(_nsa_kernel, n_sel=n_sel),
        grid=(B, NSA_KV_HEADS, nqb),
        in_specs=[pl.BlockSpec((TQ, GW), lambda b, k, i: (b * nqb + i, k)),
                  pl.BlockSpec((TQ, LANE), lambda b, k, i: (b * nqb + i, Z_NG // LANE + k)),
                  kv_spec((GW, ncp)), kv_spec((ncp, GW)),
                  kv_spec((GW, S)), kv_spec((S, GW)),
                  kv_spec((GW, S)), kv_spec((S, GW)),
                  pl.BlockSpec((jp, ncp), lambda b, k, i: (0, 0))],
        out_specs=pl.BlockSpec((TQ, GW), lambda b, k, i: (b * nqb + i, k)),
        out_shape=jax.ShapeDtypeStruct((n, NSA_KV_HEADS * GW), jnp.bfloat16),
        compiler_params=pltpu.CompilerParams(dimension_semantics=("parallel", "parallel", "arbitrary"),
                                             vmem_limit_bytes=VMEM_LIMIT),
        name="nsa_core",
    )(q, z, kct, vc, kst, vs, kwt, vw, c2st)


def _gla_consts():
    C = GLA_CHUNK
    i = np.arange(C)
    mats = [np.tril(np.ones((C, C), np.float32))]
    for s in GLA_LEVELS:
        r = (i // (2 * s)) * 2 * s + s - 1
        upper = (i // s) % 2 == 1
        m = np.arange(C)[None, :]
        mq = ((m > r[:, None]) & (m <= i[:, None]) & upper[:, None]).astype(np.float32)
        mk = ((m > i[:, None]) & (m <= r[:, None]) & (~upper)[:, None]).astype(np.float32)
        mats += [mq, mk]
    return np.concatenate(mats, axis=0)


def _split3(x):
    h = x.astype(jnp.bfloat16)
    r = x - h.astype(jnp.float32)
    m = r.astype(jnp.bfloat16)
    l = (r - m.astype(jnp.float32)).astype(jnp.bfloat16)
    return h, m, l


def _gla_kernel(q_ref, k_ref, v_ref, r_ref, glr_ref, gw_ref, gb_ref, ng_ref, msel_ref, o_ref, state_ref):
    C = GLA_CHUNK

    @pl.when(pl.program_id(1) == 0)
    def _():
        state_ref[...] = jnp.zeros_like(state_ref)

    ii = lax.broadcasted_iota(jnp.int32, (C, C), 0)
    jj = lax.broadcasted_iota(jnp.int32, (C, C), 1)
    lane_h = lax.broadcasted_iota(jnp.int32, (C, GLA_QK), 1) // GLA_DK
    bd = (lax.broadcasted_iota(jnp.int32, (GLA_QK, GLA_V), 0) // GLA_DK
          == lax.broadcasted_iota(jnp.int32, (GLA_QK, GLA_V), 1) // GLA_DV)
    msel = msel_ref[...]
    gw = gw_ref[...].astype(jnp.bfloat16)
    dn = (((1,), (1,)), ((), ()))

    def chunk(c, carry):
        r0 = pl.multiple_of(c * C, C)
        q = q_ref[pl.ds(r0, C), :].astype(jnp.float32) * (GLA_DK ** -0.5)
        k = k_ref[pl.ds(r0, C), :].astype(jnp.float32)
        v = v_ref[pl.ds(r0, C), :].astype(jnp.bfloat16)
        z = jnp.dot(glr_ref[pl.ds(r0, C), :].astype(jnp.bfloat16), gw, preferred_element_type=jnp.float32) + gb_ref[...]
        a = jax.nn.log_sigmoid(z) / GLA_GATE_TEMP
        ah, am, al = _split3(a)
        ex = (jnp.dot(msel, ah, preferred_element_type=jnp.float32)
              + jnp.dot(msel, am, preferred_element_type=jnp.float32)
              + jnp.dot(msel, al, preferred_element_type=jnp.float32))
        bcum = ex[0:C]
        state = state_ref[...]
        o = jnp.dot((q * jnp.exp(bcum)).astype(jnp.bfloat16), state.astype(jnp.bfloat16),
                    preferred_element_type=jnp.float32)
        kb = k.astype(jnp.bfloat16)
        for h in range(GLA_HEADS):
            hm = lane_h == h
            att = jnp.where(ii == jj, lax.dot_general(jnp.where(hm, q, 0.0).astype(jnp.bfloat16), kb, dn,
                                                      preferred_element_type=jnp.float32), 0.0)
            for li, s in enumerate(GLA_LEVELS):
                eq = jnp.exp(ex[(1 + 2 * li) * C:(2 + 2 * li) * C])
                ek = jnp.exp(ex[(2 + 2 * li) * C:(3 + 2 * li) * C])
                qs = jnp.where(hm, q * eq, 0.0).astype(jnp.bfloat16)
                ks = (k * ek).astype(jnp.bfloat16)
                blk = (ii // (2 * s) == jj // (2 * s)) & ((ii // s) % 2 == 1) & ((jj // s) % 2 == 0)
                att = att + jnp.where(blk, lax.dot_general(qs, ks, dn, preferred_element_type=jnp.float32), 0.0)
            oh = jnp.dot(att.astype(jnp.bfloat16), v, preferred_element_type=jnp.float32)
            o = o + jnp.where(lax.broadcasted_iota(jnp.int32, (C, GLA_V), 1) // GLA_DV == h, oh, 0.0)
        blast = bcum[C - 1:C, :]
        kd_t = (k * jnp.exp(blast - bcum)).T.astype(jnp.bfloat16)
        decay_col = jnp.exp(jnp.sum(a.T, axis=1, keepdims=True))
        upd = jnp.dot(kd_t, v, preferred_element_type=jnp.float32)
        state_ref[...] = state * decay_col + jnp.where(bd, upd, 0.0)
        outs = []
        for h in range(GLA_HEADS):
            oh = o[:, h * GLA_DV:(h + 1) * GLA_DV]
            outs.append(oh * lax.rsqrt(jnp.mean(oh * oh, axis=1, keepdims=True) + EPS) * ng_ref[...])
        on = jnp.concatenate(outs, axis=1)
        r = r_ref[pl.ds(r0, C), :].astype(jnp.float32)
        o_ref[pl.ds(r0, C), :] = (on * (r * jax.nn.sigmoid(r))).astype(o_ref.dtype)
        return carry

    lax.fori_loop(0, q_ref.shape[0] // C, chunk, 0)


def gla_pallas(z, gate_w, gate_b, norm_g, B, S):
    n = B * S
    tb = GLA_TB
    nb = S // tb
    gw = jnp.pad(gate_w, ((0, LANE - gate_w.shape[0]), (0, 0)))
    msel = jnp.asarray(_gla_consts(), dtype=jnp.bfloat16)
    zb = lambda col, w: pl.BlockSpec((tb, w), lambda b, i: (b * nb + i, col // w))
    full = lambda shp: pl.BlockSpec(shp, lambda b, i: (0,) * len(shp))
    return pl.pallas_call(
        _gla_kernel,
        grid=(B, nb),
        in_specs=[zb(Z_GQ, GLA_QK), zb(Z_GK, GLA_QK), zb(Z_GV, GLA_V), zb(Z_GR, GLA_V), zb(Z_GLR, LANE),
                  full((LANE, GLA_QK)), full((1, GLA_QK)), full((1, GLA_DV)), full(msel.shape)],
        out_specs=pl.BlockSpec((tb, GLA_V), lambda b, i: (b * nb + i, 0)),
        out_shape=jax.ShapeDtypeStruct((n, GLA_V), jnp.bfloat16),
        scratch_shapes=[pltpu.VMEM((GLA_QK, GLA_V), jnp.float32)],
        compiler_params=pltpu.CompilerParams(dimension_semantics=("parallel", "arbitrary")),
        name="gla",
    )(z, z, z, z, z, gw, gate_b.reshape(1, -1), norm_g.reshape(1, -1), msel)


_FULL_BLOCKS = 8


def _extract_top(s, k, tie_break, want_rank=True):
    n, t = s.shape
    row = lax.broadcasted_iota(jnp.int32, (n, t), 0)
    rank = jnp.full((n, t), k, jnp.int32) if want_rank else None
    vals = []
    for r in range(k):
        m = jnp.max(s, axis=0, keepdims=True)
        hit = s == m
        if tie_break:
            hit = row == jnp.min(jnp.where(hit, row, n), axis=0, keepdims=True)
        if want_rank:
            rank = jnp.where(hit, r, rank)
        s = jnp.where(hit, float('-inf'), s)
        vals.append(m)
    taken = jnp.sum((s == float('-inf')).astype(jnp.float32), axis=0, keepdims=True)
    return jnp.concatenate(vals, axis=0), rank, taken


def _peer_gate_math(s1, s2, tie_break):
    K = PEER_TOPK
    t = s1.shape[1]
    v1, rank1, taken1 = _extract_top(s1, K, tie_break, want_rank=tie_break)
    v2, rank2, taken2 = _extract_top(s2, K, tie_break)
    blocks, poss, nrows = [], [], []
    for r1 in range(_FULL_BLOCKS):
        n = K if r1 == 0 else K // 2
        row = lax.broadcasted_iota(jnp.int32, (n, t), 0)
        c = v1[r1:r1 + 1, :] + v2[0:n]
        blocks.append(jnp.where(row < K // (r1 + 1), c, float('-inf')))
        poss.append(row + r1 * K)
        nrows.append(n)
    row8 = lax.broadcasted_iota(jnp.int32, (K - _FULL_BLOCKS, t), 0)
    blocks.append(v1[_FULL_BLOCKS:, :] + v2[0:1, :])
    poss.append((row8 + _FULL_BLOCKS) * K)
    nrows.append(K - _FULL_BLOCKS)
    cand = jnp.concatenate(blocks, axis=0)
    pos = jnp.concatenate(poss, axis=0)
    sel_all = jnp.zeros(cand.shape, jnp.bool_)
    big = K * K
    for _ in range(K):
        m = jnp.max(cand, axis=0, keepdims=True)
        hit = cand == m
        if tie_break:
            hit = pos == jnp.min(jnp.where(hit, pos, big), axis=0, keepdims=True)
        sel_all = sel_all | hit
        cand = jnp.where(hit, float('-inf'), cand)
    starts = np.cumsum([0] + nrows)
    sels = [sel_all[starts[i]:starts[i + 1]] for i in range(len(nrows))]
    top = v1[0:1, :] + v2[0:1, :]
    cnt_rows = []
    z = jnp.zeros((1, t), jnp.float32)
    for r1 in range(_FULL_BLOCKS):
        sel = sels[r1]
        cnt_rows.append(jnp.sum(sel.astype(jnp.float32), axis=0, keepdims=True))
        c = v1[r1:r1 + 1, :] + v2[0:nrows[r1]]
        z = z + jnp.sum(jnp.where(sel, jnp.exp(c - top), 0.0), axis=0, keepdims=True)
    sel = sels[_FULL_BLOCKS]
    cnt_tail = sel.astype(jnp.float32)
    c = v1[_FULL_BLOCKS:, :] + v2[0:1, :]
    z = z + jnp.sum(jnp.where(sel, jnp.exp(c - top), 0.0), axis=0, keepdims=True)
    cnt = jnp.concatenate(cnt_rows + [cnt_tail], axis=0)
    a = jnp.zeros(s1.shape, jnp.float32)
    for r in range(K):
        a = jnp.where((rank1 == r) if tie_break else (s1 == v1[r:r + 1, :]), cnt[r:r + 1, :], a)
    taken = (taken1, taken2, jnp.sum(cnt, axis=0, keepdims=True))
    return jnp.exp(s1 - v1[0:1, :]), a, rank2.astype(jnp.float32), jnp.exp(s2 - v2[0:1, :]) / z, taken


def _peer_gate_kernel(q_ref, keys_ref, e1_ref, a_ref, r2_ref, e2_ref):
    q = q_ref[...].astype(jnp.bfloat16)
    k1 = keys_ref[0, 0].astype(jnp.bfloat16)
    k2 = keys_ref[0, 1].astype(jnp.bfloat16)
    dn = (((1,), (1,)), ((), ()))
    s1 = lax.dot_general(k1, q[:, :PEER_DKEY // 2], dn, preferred_element_type=jnp.float32)
    s2 = lax.dot_general(k2, q[:, PEER_DKEY // 2:], dn, preferred_element_type=jnp.float32)

    def write(e1, a, r2, e2):
        e1_ref[0] = e1
        a_ref[0] = a
        r2_ref[0] = r2.astype(jnp.bfloat16)
        e2_ref[0] = e2.astype(jnp.bfloat16)

    e1, a, r2, e2, taken = _peer_gate_math(s1, s2, tie_break=False)
    write(e1, a, r2, e2)
    excess = sum(jnp.max(jnp.abs(c - PEER_TOPK)) for c in taken)

    @pl.when(excess > 0.0)
    def _():
        write(*_peer_gate_math(s1, s2, tie_break=True)[:4])


def peer_gates(qh, keys, tt=512):
    n = qh.shape[0]
    H = PEER_HEADS
    shp32 = jax.ShapeDtypeStruct((H, PEER_NKEYS, n), jnp.float32)
    shp16 = jax.ShapeDtypeStruct((H, PEER_NKEYS, n), jnp.bfloat16)
    ospec = pl.BlockSpec((1, PEER_NKEYS, tt), lambda i, h: (h, 0, i))
    return pl.pallas_call(
        _peer_gate_kernel,
        grid=(n // tt, H),
        in_specs=[pl.BlockSpec((tt, PEER_DKEY), lambda i, h: (i, h)),
                  pl.BlockSpec((1, 2, PEER_NKEYS, PEER_DKEY // 2), lambda i, h: (h, 0, 0, 0))],
        out_specs=[ospec, ospec, ospec, ospec],
        out_shape=[shp32, shp32, shp16, shp16],
        compiler_params=pltpu.CompilerParams(dimension_semantics=("parallel", "parallel")),
        name="peer_gates",
    )(qh, keys)


def _row_bf16(row, n):
    t = row.shape[1]
    tile = jnp.broadcast_to(row, (16, t)).astype(jnp.bfloat16)
    return jnp.concatenate([tile] * (n // 16), axis=0)


def _peer_dense_kernel(xt_ref, u_ref, vt_ref, e1_ref, a_ref, r2_ref, e2_ref, x1_ref, gate_ref, fg_ref,
                       o_ref, acc_ref, *, ec, final):
    j = pl.program_id(1)
    nsub = ec // PEER_NKEYS

    @pl.when(j == 0)
    def _():
        acc_ref[...] = jnp.zeros_like(acc_ref)

    assert nsub % 8 == 0
    r8 = pl.multiple_of(j * nsub, 8)
    a_tiles = [[a_ref[h, pl.ds(r8 + 8 * t, 8), :] for t in range(nsub // 8)] for h in range(PEER_HEADS)]
    e1_tiles = [[e1_ref[h, pl.ds(r8 + 8 * t, 8), :] for t in range(nsub // 8)] for h in range(PEER_HEADS)]

    def gate_mat(i):
        g = None
        for h in range(PEER_HEADS):
            cnt = _row_bf16(a_tiles[h][i // 8][i % 8:i % 8 + 1, :], PEER_NKEYS)
            e1 = _row_bf16(e1_tiles[h][i // 8][i % 8:i % 8 + 1, :], PEER_NKEYS)
            term = jnp.where(r2_ref[h] < cnt, e2_ref[h], jnp.bfloat16(0)) * e1
            g = term if g is None else g + term
        return g

    gs = PEER_GROUPS_PER_SCORE

    def scores(p):
        r0 = p * gs * PEER_NKEYS
        return jnp.dot(u_ref[r0:r0 + gs * PEER_NKEYS, :], xt_ref[...], preferred_element_type=jnp.float32)

    st = scores(0)
    ws = []
    for p in range(nsub // gs):
        st_next = scores(p + 1) if (p + 1) * gs < nsub else None
        for k in range(gs):
            i = p * gs + k
            ws.append(_gelu(st[k * PEER_NKEYS:(k + 1) * PEER_NKEYS].astype(jnp.bfloat16)) * gate_mat(i))
            if i % PEER_GROUPS_PER_DOT == PEER_GROUPS_PER_DOT - 1:
                w = jnp.concatenate(ws, axis=0)
                ws = []
                acc_ref[...] += jnp.dot(
                    vt_ref[:, (i + 1 - PEER_GROUPS_PER_DOT) * PEER_NKEYS:(i + 1) * PEER_NKEYS], w,
                    preferred_element_type=jnp.float32)
        st = st_next

    @pl.when(j == pl.num_programs(1) - 1)
    def _():
        x2 = x1_ref[...] + gate_ref[0] * acc_ref[...].T
        if final:
            x2 = x2 * lax.rsqrt(jnp.mean(x2 * x2, axis=1, keepdims=True) + EPS) * fg_ref[...]
        o_ref[...] = x2


def peer_dense(xt, u16, vt16, e1, a, r2, e2, x1, gate2, final_g, S, final, tt=512, ec=2048):
    d, n = xt.shape
    ne = u16.shape[0]
    H = PEER_HEADS
    per_b = S // tt
    gspec = lambda: pl.BlockSpec((H, PEER_NKEYS, tt), lambda i, j: (0, 0, i))
    return pl.pallas_call(
        functools.partial(_peer_dense_kernel, ec=ec, final=final),
        grid=(n // tt, ne // ec),
        in_specs=[pl.BlockSpec((d, tt), lambda i, j: (0, i)),
                  pl.BlockSpec((ec, d), lambda i, j: (j, 0)),
                  pl.BlockSpec((d, ec), lambda i, j: (0, j)),
                  gspec(), gspec(), gspec(), gspec(),
                  pl.BlockSpec((tt, d), lambda i, j: (i, 0)),
                  pl.BlockSpec((1, 1, d), lambda i, j: (i // per_b, 0, 0)),
                  pl.BlockSpec((1, d), lambda i, j: (0, 0))],
        out_specs=pl.BlockSpec((tt, d), lambda i, j: (i, 0)),
        out_shape=jax.ShapeDtypeStruct((n, d), jnp.float32),
        scratch_shapes=[pltpu.VMEM((d, tt), jnp.float32)],
        compiler_params=pltpu.CompilerParams(dimension_semantics=("parallel", "arbitrary"),
                                             vmem_limit_bytes=VMEM_LIMIT),
        name="peer_dense",
    )(xt, u16, vt16, e1, a, r2, e2, x1, gate2, final_g.reshape(1, d))


def adaln(c, w, b):
    mod = pmm(jax.nn.silu(c), w) + b
    shift, scale, gate = jnp.split(mod[:, None, :], 3, axis=-1)
    return shift, 1.0 + scale, gate


def kernel(x, c, positions, ada_w, ada_b, norm_g, w_in, conv_dw_w, conv_dw_b, conv_ln_g, conv_ln_b, w_conv_up,
           cmp_pos, cmp_w1, cmp_w2, w_nsa_up, gla_gate_w, gla_gate_b, gla_norm_g, w_gla_up, w_out,
           peer_wq, peer_keys, peer_u, peer_v, final_g):
    B, S, D = x.shape
    bf = jnp.bfloat16
    x2 = x.reshape(B * S, D)
    cs = rope_tables(positions)
    for l in range(DEPTH):
        shift, scale1p, gate = adaln(c, ada_w[l, 0], ada_b[l, 0])
        z = in_proj(x2, norm_g[l, 0], scale1p, shift, pack_w_in(w_in[l]), S)
        hc = conv_pallas(z, conv_dw_w[l], conv_dw_b[l], conv_ln_g[l], conv_ln_b[l], B, S)
        q, kst, vs, kwt, vw = nsa_prep(z, cs, B, S)
        kct, vc = nsa_compress(z, positions, cmp_pos[l], cmp_w1[l], cmp_w2[l], B, S)
        on = nsa_core(q, z, kct, vc, kst, vs, kwt, vw, B, S)
        og = gla_pallas(z, gla_gate_w[l], gla_gate_b[l], gla_norm_g[l], B, S)
        shift2, scale2p, gate2 = adaln(c, ada_w[l, 1], ada_b[l, 1])
        x1, ut, qh = mix_out(hc, on, og, z, x2, gate, w_conv_up[l].astype(bf), w_nsa_up[l].astype(bf),
                             w_gla_up[l].astype(bf), w_out[l].astype(bf), norm_g[l, 1], scale2p, shift2,
                             peer_wq[l].astype(bf), S)
        e1, a, r2, e2 = peer_gates(qh, peer_keys[l])
        x2 = peer_dense(ut, peer_u[l].astype(bf), peer_v[l].astype(bf).T, e1, a, r2, e2, x1, gate2, final_g,
                        S, final=(l == DEPTH - 1))
    return x2.reshape(B, S, D)
```

```python
import functools

import jax
import jax.numpy as jnp
import numpy as np
from jax import lax
from jax.experimental import pallas as pl
from jax.experimental.pallas import tpu as pltpu

D_MODEL = 1024
DEPTH = 2
EPS = 1e-6
NEG = -1e30
N_BRANCH = 3
CONV_DIM = 512
CONV_WIDTH = 31
NSA_HEADS = 8
NSA_KV_HEADS = 2
NSA_GROUP = NSA_HEADS // NSA_KV_HEADS
HEAD_DIM = 64
CMP_BLOCK = 32
CMP_STRIDE = 16
CMP_HIDDEN = 256
SLC_BLOCK = 64
SLC_TOPK = 16
WINDOW = 512
FORCE_BONUS = 1e4
ROPE_THETA = 10000.0
GLA_HEADS = 4
GLA_DK = 64
GLA_DV = 128
GLA_GATE_RANK = 16
GLA_GATE_TEMP = 16.0
GLA_CHUNK = 256
PEER_HEADS = 8
PEER_NKEYS = 128
PEER_DKEY = 256
PEER_TOPK = 16
NSA_Q = NSA_HEADS * HEAD_DIM
NSA_KV = 3 * 2 * NSA_KV_HEADS * HEAD_DIM
NSA_G = 3 * NSA_HEADS
GLA_QK = GLA_HEADS * GLA_DK
GLA_V = GLA_HEADS * GLA_DV
SPLITS = [2 * CONV_DIM, NSA_Q, NSA_KV, NSA_G, GLA_QK, GLA_QK, GLA_V, GLA_V, GLA_GATE_RANK, N_BRANCH * D_MODEL]

LANE = 128
TQ = 128
KC = 512
GW = NSA_GROUP * HEAD_DIM
VMEM_LIMIT = 48 * 1024 * 1024
GLA_LEVELS = tuple(GLA_CHUNK >> s for s in range(1, GLA_CHUNK.bit_length()))
GLA_TB = 512
CONV_TS = 512
CONV_HALO = 32
ROW_TILE = 1024
MIX_TILE = 512
PEER_GROUPS_PER_DOT = 2
PEER_GROUPS_PER_SCORE = 8

Z_MG = 0
Z_CONV = Z_MG + N_BRANCH * D_MODEL
Z_Q = Z_CONV + 2 * CONV_DIM
Z_QSW = Z_Q + NSA_Q
Z_GV = Z_QSW + NSA_Q
Z_GR = Z_GV + GLA_V
Z_GQ = Z_GR + GLA_V
Z_GK = Z_GQ + GLA_QK
Z_CMP = Z_GK + GLA_QK
Z_SLC = Z_CMP + 256
Z_WIN = Z_SLC + 384
Z_NG = Z_WIN + 384
Z_GLR = Z_NG + NSA_KV_HEADS * LANE
NZ = Z_GLR + LANE
assert Z_SLC % 384 == 0 and Z_WIN % 384 == 0 and Z_CONV % (2 * CONV_DIM) == 0 and Z_GQ % GLA_QK == 0


def _mm_kernel(a_ref, b_ref, o_ref):
    o_ref[...] = jnp.dot(a_ref[...].astype(jnp.bfloat16), b_ref[...].astype(jnp.bfloat16),
                         preferred_element_type=jnp.float32)


def _pick(n, cands):
    for c in cands:
        if n % c == 0:
            return c
    return n


def pmm(a, b):
    m, k = a.shape
    n = b.shape[1]
    n_pad = -n % LANE
    if n_pad:
        b = jnp.pad(b, ((0, 0), (0, n_pad)))
    m_pad = -m % 8
    if m_pad:
        a = jnp.pad(a, ((0, m_pad), (0, 0)))
    mp, np_ = m + m_pad, n + n_pad
    tm = _pick(mp, (512, 256, 128, 64, 32, 16, 8))
    tn = _pick(np_, (512, 640, 384, 256, 128))
    out = pl.pallas_call(
        _mm_kernel,
        grid=(mp // tm, np_ // tn),
        in_specs=[pl.BlockSpec((tm, k), lambda i, j: (i, 0)),
                  pl.BlockSpec((k, tn), lambda i, j: (0, j))],
        out_specs=pl.BlockSpec((tm, tn), lambda i, j: (i, j)),
        out_shape=jax.ShapeDtypeStruct((mp, np_), jnp.float32),
        compiler_params=pltpu.CompilerParams(dimension_semantics=("parallel", "parallel")),
        name="pmm",
    )(a, b)
    return out[:m, :n]


def _in_proj_kernel(x_ref, g_ref, sc_ref, sh_ref, w_ref, z_ref, u_ref):
    @pl.when(pl.program_id(1) == 0)
    def _():
        x = x_ref[...]
        y = x * lax.rsqrt(jnp.mean(x * x, axis=1, keepdims=True) + EPS) * g_ref[...]
        u_ref[...] = (y * sc_ref[0] + sh_ref[0]).astype(jnp.bfloat16)

    z_ref[...] = jnp.dot(u_ref[...], w_ref[...], preferred_element_type=jnp.float32).astype(z_ref.dtype)


def in_proj(x2, g, scale1p, shift, w16, S):
    n, d = x2.shape
    nz = w16.shape[1]
    tm = ROW_TILE
    tn = _pick(nz, (2688, 1152, 1024, 896, 512, 384, 256, 128))
    per_b = S // tm
    mod = lambda: pl.BlockSpec((1, 1, d), lambda i, j: (i // per_b, 0, 0))
    return pl.pallas_call(
        _in_proj_kernel,
        grid=(n // tm, nz // tn),
        in_specs=[pl.BlockSpec((tm, d), lambda i, j: (i, 0)),
                  pl.BlockSpec((1, d), lambda i, j: (0, 0)), mod(), mod(),
                  pl.BlockSpec((d, tn), lambda i, j: (0, j))],
        out_specs=pl.BlockSpec((tm, tn), lambda i, j: (i, j)),
        out_shape=jax.ShapeDtypeStruct((n, nz), jnp.bfloat16),
        scratch_shapes=[pltpu.VMEM((tm, d), jnp.bfloat16)],
        compiler_params=pltpu.CompilerParams(dimension_semantics=("parallel", "arbitrary"),
                                             vmem_limit_bytes=VMEM_LIMIT,
                                             allow_input_fusion=[False, False, False, False, True]),
        name="in_proj",
    )(x2, g.reshape(1, d), scale1p, shift, w16)


def _half_swap(n_heads):
    idx = np.arange(n_heads * HEAD_DIM).reshape(n_heads, 2, HEAD_DIM // 2)
    return idx[:, ::-1, :].reshape(-1)


def pack_w_in(w_in):
    o = np.cumsum([0] + SPLITS)
    a_conv, nq, nkv, ng, gq, gk, gv, gr, glr, mg = [w_in[:, o[i]:o[i + 1]] for i in range(len(SPLITS))]
    kvw = 2 * NSA_KV_HEADS * HEAD_DIM
    kw = NSA_KV_HEADS * HEAD_DIM
    d = w_in.shape[0]
    sw = _half_swap(NSA_KV_HEADS)
    cols = [mg, a_conv, nq, nq[:, _half_swap(NSA_HEADS)], gv, gr, gq, gk, nkv[:, 0:kvw]]
    for br in (1, 2):
        k = nkv[:, br * kvw:br * kvw + kw]
        cols += [k, k[:, sw], nkv[:, br * kvw + kw:(br + 1) * kvw]]
    gpk = NSA_GROUP * 3
    for kvh in range(NSA_KV_HEADS):
        cols += [ng[:, kvh * gpk:(kvh + 1) * gpk], jnp.zeros((d, LANE - gpk), w_in.dtype)]
    cols += [glr, jnp.zeros((d, LANE - GLA_GATE_RANK), w_in.dtype)]
    w = jnp.concatenate(cols, axis=1)
    assert w.shape[1] == NZ
    return w.astype(jnp.bfloat16)


def rope_tables(positions):
    half = HEAD_DIM // 2
    freq = ROPE_THETA ** (-jnp.arange(half, dtype=jnp.float32) / half)
    ang = positions.astype(jnp.float32)[..., None] * freq
    cos, sin = jnp.cos(ang), jnp.sin(ang)
    c = jnp.tile(jnp.concatenate([cos, cos], -1), (1, 1, LANE // HEAD_DIM))
    s = jnp.tile(jnp.concatenate([-sin, sin], -1), (1, 1, LANE // HEAD_DIM))
    return jnp.concatenate([c, s], -1).reshape(-1, 2 * LANE)


def _nsa_prep_kernel(q_ref, qs_ref, slc_ref, win_ref, cs_ref, sel_ref, qo_ref, kst_ref, vs_ref, kwt_ref, vw_ref):
    cos = cs_ref[:, :LANE]
    sin = cs_ref[:, LANE:]
    scale = HEAD_DIM ** -0.5
    for t in range(NSA_Q // LANE):
        sl = slice(t * LANE, (t + 1) * LANE)
        qo_ref[:, sl] = ((q_ref[:, sl] * cos + qs_ref[:, sl] * sin) * scale).astype(jnp.bfloat16)
    dn = (((1,), (1,)), ((), ()))
    for src, kt_ref, v_ref in ((slc_ref, kst_ref, vs_ref), (win_ref, kwt_ref, vw_ref)):
        k = (src[:, 0:LANE] * cos + src[:, LANE:2 * LANE] * sin).astype(jnp.bfloat16)
        v = src[:, 2 * LANE:3 * LANE].astype(jnp.bfloat16)
        for kvh in range(NSA_KV_HEADS):
            sel = sel_ref[kvh]
            kt_ref[0, kvh] = lax.dot_general(sel, k, dn, preferred_element_type=jnp.float32).astype(jnp.bfloat16)
            v_ref[0, kvh] = lax.dot_general(v, sel, dn, preferred_element_type=jnp.float32).astype(jnp.bfloat16)


def _head_repeat_sel():
    sel = np.zeros((NSA_KV_HEADS, GW, LANE), np.float32)
    for kvh in range(NSA_KV_HEADS):
        for r in range(GW):
            sel[kvh, r, kvh * HEAD_DIM + r % HEAD_DIM] = 1.0
    return jnp.asarray(sel, dtype=jnp.bfloat16)


def nsa_prep(z, cs, B, S):
    n = B * S
    tp = ROW_TILE
    nb = S // tp
    zb = lambda col, w: pl.BlockSpec((tp, w), lambda b, i: (b * nb + i, col // w))
    kt = lambda: pl.BlockSpec((1, NSA_KV_HEADS, GW, tp), lambda b, i: (b, 0, 0, i))
    vv = lambda: pl.BlockSpec((1, NSA_KV_HEADS, tp, GW), lambda b, i: (b, 0, i, 0))
    kt_shape = jax.ShapeDtypeStruct((B, NSA_KV_HEADS, GW, S), jnp.bfloat16)
    v_shape = jax.ShapeDtypeStruct((B, NSA_KV_HEADS, S, GW), jnp.bfloat16)
    return pl.pallas_call(
        _nsa_prep_kernel,
        grid=(B, nb),
        in_specs=[zb(Z_Q, NSA_Q), zb(Z_QSW, NSA_Q), zb(Z_SLC, 384), zb(Z_WIN, 384),
                  pl.BlockSpec((tp, 2 * LANE), lambda b, i: (b * nb + i, 0)),
                  pl.BlockSpec((NSA_KV_HEADS, GW, LANE), lambda b, i: (0, 0, 0))],
        out_specs=[pl.BlockSpec((tp, NSA_Q), lambda b, i: (b * nb + i, 0)), kt(), vv(), kt(), vv()],
        out_shape=[jax.ShapeDtypeStruct((n, NSA_Q), jnp.bfloat16), kt_shape, v_shape, kt_shape, v_shape],
        compiler_params=pltpu.CompilerParams(dimension_semantics=("parallel", "parallel")),
        name="nsa_prep",
    )(z, z, z, z, cs, _head_repeat_sel())


def _gelu(x):
    return 0.5 * x * (1.0 + jnp.tanh(0.7978845608028654 * (x + 0.044715 * x * x * x)))


def _compress_kernel(gk_ref, gv_ref, pe_ref, w1_ref, w2_ref, cs_ref, kct_ref, vc_ref):
    nrow = gk_ref.shape[2]
    outs = []
    for kv, g_ref in enumerate((gk_ref, gv_ref)):
        g = g_ref[0, 0]
        lo = jnp.dot((g + pe_ref[kv, 0]).astype(jnp.bfloat16), w1_ref[kv, 0], preferred_element_type=jnp.float32)
        hi = jnp.dot((g + pe_ref[kv, 1]).astype(jnp.bfloat16), w1_ref[kv, 1], preferred_element_type=jnp.float32)
        hid = _gelu(lo + pltpu.roll(hi, nrow - 1, 0)).astype(jnp.bfloat16)
        outs.append(jnp.dot(hid, w2_ref[kv], preferred_element_type=jnp.float32))
    k = outs[0][:, :GW] * cs_ref[0, :, :GW] + outs[0][:, GW:] * cs_ref[0, :, GW:]
    kct_ref[0, 0] = k.T.astype(jnp.bfloat16)
    vc_ref[0, 0] = outs[1][:, :GW].astype(jnp.bfloat16)


def nsa_compress(z, positions, cmp_pos, cmp_w1, cmp_w2, B, S):
    ng = S // CMP_STRIDE
    grp = CMP_STRIDE * HEAD_DIM
    c = z[:, Z_CMP:Z_CMP + 256].astype(jnp.float32).reshape(B, ng, CMP_STRIDE, 2, NSA_KV_HEADS, HEAD_DIM)
    g = c.transpose(3, 0, 4, 1, 2, 5).reshape(2, B, NSA_KV_HEADS, ng, grp)
    pe = cmp_pos.reshape(2, 2, 1, grp)
    w1 = cmp_w1.reshape(2, 2, grp, CMP_HIDDEN).astype(jnp.bfloat16)
    rep = jnp.tile(cmp_w2, (1, 1, NSA_GROUP))
    sw = np.tile(_half_swap(1), NSA_GROUP) + np.repeat(np.arange(NSA_GROUP) * HEAD_DIM, HEAD_DIM)
    w2 = jnp.concatenate([rep, rep[:, :, sw]], axis=-1).astype(jnp.bfloat16)
    end = jnp.minimum(jnp.arange(ng) * CMP_STRIDE + CMP_BLOCK - 1, S - 1)
    half = HEAD_DIM // 2
    freq = ROPE_THETA ** (-jnp.arange(half, dtype=jnp.float32) / half)
    ang = positions[:, end].astype(jnp.float32)[..., None] * freq
    cos, sin = jnp.cos(ang), jnp.sin(ang)
    cs = jnp.concatenate([jnp.tile(jnp.concatenate([cos, cos], -1), (1, 1, NSA_GROUP)),
                          jnp.tile(jnp.concatenate([-sin, sin], -1), (1, 1, NSA_GROUP))], -1)
    gspec = lambda: pl.BlockSpec((1, 1, ng, grp), lambda b, k: (b, k, 0, 0))
    return pl.pallas_call(
        _compress_kernel,
        grid=(B, NSA_KV_HEADS),
        in_specs=[gspec(), gspec(),
                  pl.BlockSpec((2, 2, 1, grp), lambda b, k: (0, 0, 0, 0)),
                  pl.BlockSpec((2, 2, grp, CMP_HIDDEN), lambda b, k: (0, 0, 0, 0)),
                  pl.BlockSpec((2, CMP_HIDDEN, 2 * GW), lambda b, k: (0, 0, 0)),
                  pl.BlockSpec((1, ng, 2 * GW), lambda b, k: (b, 0, 0))],
        out_specs=[pl.BlockSpec((1, 1, GW, ng), lambda b, k: (b, k, 0, 0)),
                   pl.BlockSpec((1, 1, ng, GW), lambda b, k: (b, k, 0, 0))],
        out_shape=[jax.ShapeDtypeStruct((B, NSA_KV_HEADS, GW, ng), jnp.bfloat16),
                   jax.ShapeDtypeStruct((B, NSA_KV_HEADS, ng, GW), jnp.bfloat16)],
        compiler_params=pltpu.CompilerParams(dimension_semantics=("parallel", "parallel")),
        name="nsa_compress",
    )(g[0], g[1], pe, w1, w2, cs)


def _mix_kernel(hc_ref, on_ref, og_ref, mg_ref, x_ref, gate_ref, wc_ref, wn_ref, wg_ref, wo_ref,
                g2_ref, sc_ref, sh_ref, wq_ref, x1_ref, ut_ref, qh_ref):
    d = D_MODEL
    f32 = jnp.float32
    y = (jax.nn.sigmoid(mg_ref[:, 0:d].astype(f32)) * jnp.dot(hc_ref[...], wc_ref[...], preferred_element_type=f32)
         + jax.nn.sigmoid(mg_ref[:, d:2 * d].astype(f32)) * jnp.dot(on_ref[...], wn_ref[...], preferred_element_type=f32)
         + jax.nn.sigmoid(mg_ref[:, 2 * d:3 * d].astype(f32))
         * jnp.dot(og_ref[...], wg_ref[...], preferred_element_type=f32))
    x1 = x_ref[...] + gate_ref[0] * jnp.dot(y.astype(jnp.bfloat16), wo_ref[...], preferred_element_type=f32)
    x1_ref[...] = x1
    u = x1 * lax.rsqrt(jnp.mean(x1 * x1, axis=1, keepdims=True) + EPS) * g2_ref[...] * sc_ref[0] + sh_ref[0]
    ut_ref[...] = u.T.astype(jnp.bfloat16)
    qh_ref[...] = jnp.dot(u.astype(jnp.bfloat16), wq_ref[...], preferred_element_type=f32).astype(qh_ref.dtype)


def mix_out(hc, on, og, z, x2, gate1, wc, wn, wg, wo, g2, scale2p, shift2, wq, S):
    n, d = x2.shape
    tm = MIX_TILE
    per_b = S // tm
    row = lambda w: pl.BlockSpec((tm, w), lambda i: (i, 0))
    mod = lambda: pl.BlockSpec((1, 1, d), lambda i: (i // per_b, 0, 0))
    full = lambda a: pl.BlockSpec(a.shape, lambda i: (0,) * a.ndim, pipeline_mode=pl.Buffered(1))
    nq = wq.shape[1]
    return pl.pallas_call(
        _mix_kernel,
        grid=(n // tm,),
        in_specs=[row(CONV_DIM), row(NSA_Q), row(GLA_V),
                  pl.BlockSpec((tm, N_BRANCH * d), lambda i: (i, Z_MG // (N_BRANCH * d))),
                  row(d), mod(), full(wc), full(wn), full(wg), full(wo),
                  pl.BlockSpec((1, d), lambda i: (0, 0)), mod(), mod(), full(wq)],
        out_specs=[row(d), pl.BlockSpec((d, tm), lambda i: (0, i)), row(nq)],
        out_shape=[jax.ShapeDtypeStruct((n, d), jnp.float32), jax.ShapeDtypeStruct((d, n), jnp.bfloat16),
                   jax.ShapeDtypeStruct((n, nq), jnp.bfloat16)],
        compiler_params=pltpu.CompilerParams(dimension_semantics=("parallel",), vmem_limit_bytes=VMEM_LIMIT),
        name="mix_out",
    )(hc, on, og, z, x2, gate1, wc, wn, wg, wo, g2.reshape(1, d), scale2p, shift2, wq)


def _conv_kernel(a_ref, ah_ref, w_ref, b_ref, g_ref, beta_ref, o_ref, hbuf):
    i = pl.program_id(1)
    ts = a_ref.shape[0]
    a = a_ref[...].astype(jnp.float32)
    hbuf[pl.ds(CONV_HALO, ts), :] = a[:, :CONV_DIM] * jax.nn.sigmoid(a[:, CONV_DIM:])
    ah = ah_ref[...].astype(jnp.float32)
    halo = ah[:, :CONV_DIM] * jax.nn.sigmoid(ah[:, CONV_DIM:])
    hbuf[pl.ds(0, CONV_HALO), :] = jnp.where(i > 0, halo, 0.0)
    acc = jnp.zeros((ts, CONV_DIM), jnp.float32) + b_ref[...]
    off = CONV_HALO - (CONV_WIDTH - 1)
    for k in range(CONV_WIDTH):
        acc = acc + w_ref[k:k + 1, :] * hbuf[pl.ds(off + k, ts), :]
    mu = jnp.mean(acc, axis=1, keepdims=True)
    d = acc - mu
    var = jnp.mean(d * d, axis=1, keepdims=True)
    y = d * lax.rsqrt(var + EPS) * g_ref[...] + beta_ref[...]
    o_ref[...] = (y * jax.nn.sigmoid(y)).astype(o_ref.dtype)


def conv_pallas(z, dw_w, dw_b, ln_g, ln_b, B, S):
    n = B * S
    ts = CONV_TS
    ns = S // ts
    hb = ts // CONV_HALO
    cb = Z_CONV // (2 * CONV_DIM)
    wpad = jnp.pad(dw_w, ((0, 32 - CONV_WIDTH), (0, 0)))
    vec = lambda: pl.BlockSpec((1, CONV_DIM), lambda b, i: (0, 0))
    return pl.pallas_call(
        _conv_kernel,
        grid=(B, ns),
        in_specs=[pl.BlockSpec((ts, 2 * CONV_DIM), lambda b, i: (b * ns + i, cb)),
                  pl.BlockSpec((CONV_HALO, 2 * CONV_DIM), lambda b, i: (jnp.maximum((b * ns + i) * hb - 1, 0), cb)),
                  pl.BlockSpec((32, CONV_DIM), lambda b, i: (0, 0)), vec(), vec(), vec()],
        out_specs=pl.BlockSpec((ts, CONV_DIM), lambda b, i: (b * ns + i, 0)),
        out_shape=jax.ShapeDtypeStruct((n, CONV_DIM), jnp.bfloat16),
        scratch_shapes=[pltpu.VMEM((ts + CONV_HALO, CONV_DIM), jnp.float32)],
        compiler_params=pltpu.CompilerParams(dimension_semantics=("parallel", "parallel")),
        name="conformer_conv",
    )(z, z, wpad, dw_b.reshape(1, -1), ln_g.reshape(1, -1), ln_b.reshape(1, -1))


def _msoftmax(s, mask):
    s = jnp.where(mask, s, NEG)
    m = jnp.max(s, axis=-1, keepdims=True)
    e = jnp.where(mask, jnp.exp(s - m), 0.0)
    l = jnp.sum(e, axis=-1, keepdims=True)
    return e / jnp.where(l > 0.0, l, 1.0)


def _nsa_kernel(q_ref, g_ref, kct_ref, vc_ref, kst_ref, vs_ref, kwt_ref, vw_ref, c2st_ref, o_ref, *, n_sel):
    G = NSA_GROUP
    qb = pl.program_id(2)
    t0 = pl.multiple_of(qb * TQ, TQ)
    q = q_ref[...]
    lane_g = lax.broadcasted_iota(jnp.int32, (TQ, GW), 1) // HEAD_DIM
    q4 = jnp.concatenate([jnp.where(lane_g == h, q, jnp.zeros_like(q)) for h in range(G)], axis=0)
    tpos = t0 + lax.broadcasted_iota(jnp.int32, (TQ, 1), 0)

    def heads(x):
        return x.reshape(G, TQ, x.shape[-1])

    def attend(p, v):
        return heads(jnp.dot(p.reshape(G * TQ, p.shape[-1]).astype(jnp.bfloat16), v,
                             preferred_element_type=jnp.float32))

    starts = [t0 - WINDOW + TQ * c for c in range(WINDOW // TQ + 1)]
    reads = [pl.multiple_of(jnp.maximum(s, 0), TQ) for s in starts]
    kwt = jnp.concatenate([kwt_ref[0, 0, :, pl.ds(r, TQ)] for r in reads], axis=1)
    vw = jnp.concatenate([vw_ref[0, 0, pl.ds(r, TQ), :] for r in reads], axis=0)
    wk = WINDOW + TQ
    kposw = t0 - WINDOW + lax.broadcasted_iota(jnp.int32, (TQ, wk), 1)
    diff = tpos - kposw
    maskw = (kposw >= 0) & (diff >= 0) & (diff < WINDOW)

    kct = kct_ref[0, 0]
    ncp = kct.shape[1]
    ncol = lax.broadcasted_iota(jnp.int32, (TQ, ncp), 1)
    maskc = (ncol * CMP_STRIDE + (CMP_BLOCK - 1)) <= tpos
    sc = heads(jnp.dot(q4, kct, preferred_element_type=jnp.float32))
    sw = heads(jnp.dot(q4, kwt, preferred_element_type=jnp.float32))
    pc = _msoftmax(sc, maskc[None])
    oc = attend(pc, vc_ref[0, 0])
    pw = _msoftmax(sw, maskw[None])
    ow = attend(pw, vw)
    pc_sum = jnp.sum(pc, axis=0)

    c2st = c2st_ref[...]
    jp = c2st.shape[0]
    hi = pc_sum.astype(jnp.bfloat16)
    lo = (pc_sum - hi.astype(jnp.float32)).astype(jnp.bfloat16)
    dn = (((1,), (1,)), ((), ()))
    imp_t = (lax.dot_general(c2st, hi, dn, preferred_element_type=jnp.float32)
             + lax.dot_general(c2st, lo, dn, preferred_element_type=jnp.float32))
    jrow = lax.broadcasted_iota(jnp.int32, (jp, TQ), 0)
    cur = (t0 + lax.broadcasted_iota(jnp.int32, (jp, TQ), 1)) // SLC_BLOCK
    forced = (jrow == 0) | (jrow == cur) | (jrow == cur - 1)
    score = jnp.where(jrow <= cur, imp_t + jnp.where(forced, FORCE_BONUS, 0.0), NEG)
    nblk = kst_ref.shape[3] // SLC_BLOCK
    nrow = -(-nblk // 8) * 8
    sc, jr = score[:nrow], jrow[:nrow]
    rank = jnp.zeros((nrow, TQ), jnp.int32)
    for i in range(nblk):
        si = sc[i:i + 1, :]
        rank = rank + ((si > sc) | ((si == sc) & (jr > i))).astype(jnp.int32)
    sel = ((rank < n_sel) & (sc > 0.5 * NEG)).astype(jnp.float32)
    if nrow < jp:
        sel = jnp.concatenate([sel, jnp.zeros((jp - nrow, TQ), jnp.float32)], axis=0)
    selb = sel.T.astype(jnp.bfloat16)

    nchunks = (t0 + TQ + KC - 1) // KC

    def chunk(c, carry):
        m_old, l_old, acc = carry
        k0 = pl.multiple_of(c * KC, KC)
        kt = kst_ref[0, 0, :, pl.ds(k0, KC)]
        v = vs_ref[0, 0, pl.ds(k0, KC), :]
        kblk = (k0 + lax.broadcasted_iota(jnp.int32, (jp, KC), 1)) // SLC_BLOCK
        expand = (kblk == lax.broadcasted_iota(jnp.int32, (jp, KC), 0)).astype(jnp.bfloat16)
        member = jnp.dot(selb, expand, preferred_element_type=jnp.float32) > 0.5
        kpos = k0 + lax.broadcasted_iota(jnp.int32, (TQ, KC), 1)
        msk = (member & (kpos <= tpos))[None]
        s = jnp.where(msk, heads(jnp.dot(q4, kt, preferred_element_type=jnp.float32)), NEG)
        m_new = jnp.maximum(m_old, jnp.max(s, axis=-1, keepdims=True))
        alpha = jnp.exp(m_old - m_new)
        p = jnp.exp(s - m_new)
        l_new = alpha * l_old + jnp.sum(p, axis=-1, keepdims=True)
        return m_new, l_new, alpha * acc + attend(p, v)

    def chunk_pair(i, carry):
        return chunk(2 * i + 1, chunk(2 * i, carry))

    init = (jnp.full((G, TQ, 1), NEG, jnp.float32), jnp.zeros((G, TQ, 1), jnp.float32),
            jnp.zeros((G, TQ, GW), jnp.float32))
    _, l_s, acc_s = lax.fori_loop(0, (nchunks + 1) // 2, chunk_pair, init)
    os_ = acc_s / jnp.where(l_s > 0.0, l_s, 1.0)

    sig = jax.nn.sigmoid(g_ref[...].astype(jnp.float32))
    out = jnp.zeros((TQ, GW), jnp.float32)
    for h in range(G):
        o_h = (sig[:, 3 * h:3 * h + 1] * oc[h] + sig[:, 3 * h + 1:3 * h + 2] * os_[h]
               + sig[:, 3 * h + 2:3 * h + 3] * ow[h])
        out = out + jnp.where(lane_g == h, o_h, 0.0)
    o_ref[...] = out.astype(o_ref.dtype)


def nsa_core(q, z, kct, vc, kst, vs, kwt, vw, B, S):
    n = B * S
    ncp = kct.shape[3]
    ncmp = (S - CMP_BLOCK) // CMP_STRIDE + 1
    nslc = S // SLC_BLOCK
    n_sel = min(SLC_TOPK, nslc)
    jp = LANE
    assert nslc <= jp and S % (2 * KC) == 0 and S % TQ == 0 and WINDOW % TQ == 0
    cs = np.arange(ncmp) * CMP_STRIDE
    ss = np.arange(nslc) * SLC_BLOCK
    ov = np.minimum(cs[:, None] + CMP_BLOCK, ss[None, :] + SLC_BLOCK) - np.maximum(cs[:, None], ss[None, :])
    c2s = np.zeros((ncp, jp), np.float32)
    c2s[:ncmp, :nslc] = np.clip(ov, 0, None) / CMP_BLOCK
    c2st = jnp.asarray(c2s.T, dtype=jnp.bfloat16)
    nqb = S // TQ

    def kv_spec(shp):
        return pl.BlockSpec((1, 1) + shp, lambda b, k, i: (b, k, 0, 0))

    return pl.pallas_call(
        functools.partial(_nsa_kernel, n_sel=n_sel),
        grid=(B, NSA_KV_HEADS, nqb),
        in_specs=[pl.BlockSpec((TQ, GW), lambda b, k, i: (b * nqb + i, k)),
                  pl.BlockSpec((TQ, LANE), lambda b, k, i: (b * nqb + i, Z_NG // LANE + k)),
                  kv_spec((GW, ncp)), kv_spec((ncp, GW)),
                  kv_spec((GW, S)), kv_spec((S, GW)),
                  kv_spec((GW, S)), kv_spec((S, GW)),
                  pl.BlockSpec((jp, ncp), lambda b, k, i: (0, 0))],
        out_specs=pl.BlockSpec((TQ, GW), lambda b, k, i: (b * nqb + i, k)),
        out_shape=jax.ShapeDtypeStruct((n, NSA_KV_HEADS * GW), jnp.bfloat16),
        compiler_params=pltpu.CompilerParams(dimension_semantics=("parallel", "parallel", "arbitrary"),
                                             vmem_limit_bytes=VMEM_LIMIT),
        name="nsa_core",
    )(q, z, kct, vc, kst, vs, kwt, vw, c2st)


def _gla_consts():
    C = GLA_CHUNK
    i = np.arange(C)
    mats = [np.tril(np.ones((C, C), np.float32))]
    for s in GLA_LEVELS:
        r = (i // (2 * s)) * 2 * s + s - 1
        upper = (i // s) % 2 == 1
        m = np.arange(C)[None, :]
        mq = ((m > r[:, None]) & (m <= i[:, None]) & upper[:, None]).astype(np.float32)
        mk = ((m > i[:, None]) & (m <= r[:, None]) & (~upper)[:, None]).astype(np.float32)
        mats += [mq, mk]
    return np.concatenate(mats, axis=0)


def _split3(x):
    h = x.astype(jnp.bfloat16)
    r = x - h.astype(jnp.float32)
    m = r.astype(jnp.bfloat16)
    l = (r - m.astype(jnp.float32)).astype(jnp.bfloat16)
    return h, m, l


def _gla_kernel(q_ref, k_ref, v_ref, r_ref, glr_ref, gw_ref, gb_ref, ng_ref, msel_ref, o_ref, state_ref):
    C = GLA_CHUNK

    @pl.when(pl.program_id(1) == 0)
    def _():
        state_ref[...] = jnp.zeros_like(state_ref)

    ii = lax.broadcasted_iota(jnp.int32, (C, C), 0)
    jj = lax.broadcasted_iota(jnp.int32, (C, C), 1)
    lane_h = lax.broadcasted_iota(jnp.int32, (C, GLA_QK), 1) // GLA_DK
    bd = (lax.broadcasted_iota(jnp.int32, (GLA_QK, GLA_V), 0) // GLA_DK
          == lax.broadcasted_iota(jnp.int32, (GLA_QK, GLA_V), 1) // GLA_DV)
    msel = msel_ref[...]
    gw = gw_ref[...].astype(jnp.bfloat16)
    dn = (((1,), (1,)), ((), ()))

    def chunk(c, carry):
        r0 = pl.multiple_of(c * C, C)
        q = q_ref[pl.ds(r0, C), :].astype(jnp.float32) * (GLA_DK ** -0.5)
        k = k_ref[pl.ds(r0, C), :].astype(jnp.float32)
        v = v_ref[pl.ds(r0, C), :].astype(jnp.bfloat16)
        z = jnp.dot(glr_ref[pl.ds(r0, C), :].astype(jnp.bfloat16), gw, preferred_element_type=jnp.float32) + gb_ref[...]
        a = jax.nn.log_sigmoid(z) / GLA_GATE_TEMP
        ah, am, al = _split3(a)
        ex = (jnp.dot(msel, ah, preferred_element_type=jnp.float32)
              + jnp.dot(msel, am, preferred_element_type=jnp.float32)
              + jnp.dot(msel, al, preferred_element_type=jnp.float32))
        bcum = ex[0:C]
        state = state_ref[...]
        o = jnp.dot((q * jnp.exp(bcum)).astype(jnp.bfloat16), state.astype(jnp.bfloat16),
                    preferred_element_type=jnp.float32)
        kb = k.astype(jnp.bfloat16)
        for h in range(GLA_HEADS):
            hm = lane_h == h
            att = jnp.where(ii == jj, lax.dot_general(jnp.where(hm, q, 0.0).astype(jnp.bfloat16), kb, dn,
                                                      preferred_element_type=jnp.float32), 0.0)
            for li, s in enumerate(GLA_LEVELS):
                eq = jnp.exp(ex[(1 + 2 * li) * C:(2 + 2 * li) * C])
                ek = jnp.exp(ex[(2 + 2 * li) * C:(3 + 2 * li) * C])
                qs = jnp.where(hm, q * eq, 0.0).astype(jnp.bfloat16)
                ks = (k * ek).astype(jnp.bfloat16)
                blk = (ii // (2 * s) == jj // (2 * s)) & ((ii // s) % 2 == 1) & ((jj // s) % 2 == 0)
                att = att + jnp.where(blk, lax.dot_general(qs, ks, dn, preferred_element_type=jnp.float32), 0.0)
            oh = jnp.dot(att.astype(jnp.bfloat16), v, preferred_element_type=jnp.float32)
            o = o + jnp.where(lax.broadcasted_iota(jnp.int32, (C, GLA_V), 1) // GLA_DV == h, oh, 0.0)
        blast = bcum[C - 1:C, :]
        kd_t = (k * jnp.exp(blast - bcum)).T.astype(jnp.bfloat16)
        decay_col = jnp.exp(jnp.sum(a.T, axis=1, keepdims=True))
        upd = jnp.dot(kd_t, v, preferred_element_type=jnp.float32)
        state_ref[...] = state * decay_col + jnp.where(bd, upd, 0.0)
        outs = []
        for h in range(GLA_HEADS):
            oh = o[:, h * GLA_DV:(h + 1) * GLA_DV]
            outs.append(oh * lax.rsqrt(jnp.mean(oh * oh, axis=1, keepdims=True) + EPS) * ng_ref[...])
        on = jnp.concatenate(outs, axis=1)
        r = r_ref[pl.ds(r0, C), :].astype(jnp.float32)
        o_ref[pl.ds(r0, C), :] = (on * (r * jax.nn.sigmoid(r))).astype(o_ref.dtype)
        return carry

    lax.fori_loop(0, q_ref.shape[0] // C, chunk, 0)


def gla_pallas(z, gate_w, gate_b, norm_g, B, S):
    n = B * S
    tb = GLA_TB
    nb = S // tb
    gw = jnp.pad(gate_w, ((0, LANE - gate_w.shape[0]), (0, 0)))
    msel = jnp.asarray(_gla_consts(), dtype=jnp.bfloat16)
    zb = lambda col, w: pl.BlockSpec((tb, w), lambda b, i: (b * nb + i, col // w))
    full = lambda shp: pl.BlockSpec(shp, lambda b, i: (0,) * len(shp))
    return pl.pallas_call(
        _gla_kernel,
        grid=(B, nb),
        in_specs=[zb(Z_GQ, GLA_QK), zb(Z_GK, GLA_QK), zb(Z_GV, GLA_V), zb(Z_GR, GLA_V), zb(Z_GLR, LANE),
                  full((LANE, GLA_QK)), full((1, GLA_QK)), full((1, GLA_DV)), full(msel.shape)],
        out_specs=pl.BlockSpec((tb, GLA_V), lambda b, i: (b * nb + i, 0)),
        out_shape=jax.ShapeDtypeStruct((n, GLA_V), jnp.bfloat16),
        scratch_shapes=[pltpu.VMEM((GLA_QK, GLA_V), jnp.float32)],
        compiler_params=pltpu.CompilerParams(dimension_semantics=("parallel", "arbitrary")),
        name="gla",
    )(z, z, z, z, z, gw, gate_b.reshape(1, -1), norm_g.reshape(1, -1), msel)


_FULL_BLOCKS = 8


def _extract_top(s, k, tie_break, want_rank=True):
    n, t = s.shape
    row = lax.broadcasted_iota(jnp.int32, (n, t), 0)
    rank = jnp.full((n, t), k, jnp.int32) if want_rank else None
    vals = []
    for r in range(k):
        m = jnp.max(s, axis=0, keepdims=True)
        hit = s == m
        if tie_break:
            hit = row == jnp.min(jnp.where(hit, row, n), axis=0, keepdims=True)
        if want_rank:
            rank = jnp.where(hit, r, rank)
        s = jnp.where(hit, float('-inf'), s)
        vals.append(m)
    taken = jnp.sum((s == float('-inf')).astype(jnp.float32), axis=0, keepdims=True)
    return jnp.concatenate(vals, axis=0), rank, taken


def _peer_gate_math(s1, s2, tie_break):
    K = PEER_TOPK
    t = s1.shape[1]
    v1, rank1, taken1 = _extract_top(s1, K, tie_break, want_rank=tie_break)
    v2, rank2, taken2 = _extract_top(s2, K, tie_break)
    blocks, poss, nrows = [], [], []
    for r1 in range(_FULL_BLOCKS):
        n = K if r1 == 0 else K // 2
        row = lax.broadcasted_iota(jnp.int32, (n, t), 0)
        c = v1[r1:r1 + 1, :] + v2[0:n]
        blocks.append(jnp.where(row < K // (r1 + 1), c, float('-inf')))
        poss.append(row + r1 * K)
        nrows.append(n)
    row8 = lax.broadcasted_iota(jnp.int32, (K - _FULL_BLOCKS, t), 0)
    blocks.append(v1[_FULL_BLOCKS:, :] + v2[0:1, :])
    poss.append((row8 + _FULL_BLOCKS) * K)
    nrows.append(K - _FULL_BLOCKS)
    cand = jnp.concatenate(blocks, axis=0)
    pos = jnp.concatenate(poss, axis=0)
    sel_all = jnp.zeros(cand.shape, jnp.bool_)
    big = K * K
    for _ in range(K):
        m = jnp.max(cand, axis=0, keepdims=True)
        hit = cand == m
        if tie_break:
            hit = pos == jnp.min(jnp.where(hit, pos, big), axis=0, keepdims=True)
        sel_all = sel_all | hit
        cand = jnp.where(hit, float('-inf'), cand)
    starts = np.cumsum([0] + nrows)
    sels = [sel_all[starts[i]:starts[i + 1]] for i in range(len(nrows))]
    top = v1[0:1, :] + v2[0:1, :]
    cnt_rows = []
    z = jnp.zeros((1, t), jnp.float32)
    for r1 in range(_FULL_BLOCKS):
        sel = sels[r1]
        cnt_rows.append(jnp.sum(sel.astype(jnp.float32), axis=0, keepdims=True))
        c = v1[r1:r1 + 1, :] + v2[0:nrows[r1]]
        z = z + jnp.sum(jnp.where(sel, jnp.exp(c - top), 0.0), axis=0, keepdims=True)
    sel = sels[_FULL_BLOCKS]
    cnt_tail = sel.astype(jnp.float32)
    c = v1[_FULL_BLOCKS:, :] + v2[0:1, :]
    z = z + jnp.sum(jnp.where(sel, jnp.exp(c - top), 0.0), axis=0, keepdims=True)
    cnt = jnp.concatenate(cnt_rows + [cnt_tail], axis=0)
    a = jnp.zeros(s1.shape, jnp.float32)
    for r in range(K):
        a = jnp.where((rank1 == r) if tie_break else (s1 == v1[r:r + 1, :]), cnt[r:r + 1, :], a)
    taken = (taken1, taken2, jnp.sum(cnt, axis=0, keepdims=True))
    return jnp.exp(s1 - v1[0:1, :]), a, rank2.astype(jnp.float32), jnp.exp(s2 - v2[0:1, :]) / z, taken


def _peer_gate_kernel(q_ref, keys_ref, e1_ref, a_ref, r2_ref, e2_ref):
    q = q_ref[...].astype(jnp.bfloat16)
    k1 = keys_ref[0, 0].astype(jnp.bfloat16)
    k2 = keys_ref[0, 1].astype(jnp.bfloat16)
    dn = (((1,), (1,)), ((), ()))
    s1 = lax.dot_general(k1, q[:, :PEER_DKEY // 2], dn, preferred_element_type=jnp.float32)
    s2 = lax.dot_general(k2, q[:, PEER_DKEY // 2:], dn, preferred_element_type=jnp.float32)

    def write(e1, a, r2, e2):
        e1_ref[0] = e1
        a_ref[0] = a
        r2_ref[0] = r2.astype(jnp.bfloat16)
        e2_ref[0] = e2.astype(jnp.bfloat16)

    e1, a, r2, e2, taken = _peer_gate_math(s1, s2, tie_break=False)
    write(e1, a, r2, e2)
    excess = sum(jnp.max(jnp.abs(c - PEER_TOPK)) for c in taken)

    @pl.when(excess > 0.0)
    def _():
        write(*_peer_gate_math(s1, s2, tie_break=True)[:4])


def peer_gates(qh, keys, tt=512):
    n = qh.shape[0]
    H = PEER_HEADS
    shp32 = jax.ShapeDtypeStruct((H, PEER_NKEYS, n), jnp.float32)
    shp16 = jax.ShapeDtypeStruct((H, PEER_NKEYS, n), jnp.bfloat16)
    ospec = pl.BlockSpec((1, PEER_NKEYS, tt), lambda i, h: (h, 0, i))
    return pl.pallas_call(
        _peer_gate_kernel,
        grid=(n // tt, H),
        in_specs=[pl.BlockSpec((tt, PEER_DKEY), lambda i, h: (i, h)),
                  pl.BlockSpec((1, 2, PEER_NKEYS, PEER_DKEY // 2), lambda i, h: (h, 0, 0, 0))],
        out_specs=[ospec, ospec, ospec, ospec],
        out_shape=[shp32, shp32, shp16, shp16],
        compiler_params=pltpu.CompilerParams(dimension_semantics=("parallel", "parallel")),
        name="peer_gates",
    )(qh, keys)


def _row_bf16(row, n):
    t = row.shape[1]
    tile = jnp.broadcast_to(row, (16, t)).astype(jnp.bfloat16)
    return jnp.concatenate([tile] * (n // 16), axis=0)


def _peer_dense_kernel(xt_ref, u_ref, vt_ref, e1_ref, a_ref, r2_ref, e2_ref, x1_ref, gate_ref, fg_ref,
                       o_ref, acc_ref, *, ec, final):
    j = pl.program_id(1)
    nsub = ec // PEER_NKEYS

    @pl.when(j == 0)
    def _():
        acc_ref[...] = jnp.zeros_like(acc_ref)

    assert nsub % 8 == 0
    r8 = pl.multiple_of(j * nsub, 8)
    a_tiles = [[a_ref[h, pl.ds(r8 + 8 * t, 8), :] for t in range(nsub // 8)] for h in range(PEER_HEADS)]
    e1_tiles = [[e1_ref[h, pl.ds(r8 + 8 * t, 8), :] for t in range(nsub // 8)] for h in range(PEER_HEADS)]

    def gate_mat(i):
        g = None
        for h in range(PEER_HEADS):
            cnt = _row_bf16(a_tiles[h][i // 8][i % 8:i % 8 + 1, :], PEER_NKEYS)
            e1 = _row_bf16(e1_tiles[h][i // 8][i % 8:i % 8 + 1, :], PEER_NKEYS)
            term = jnp.where(r2_ref[h] < cnt, e2_ref[h], jnp.bfloat16(0)) * e1
            g = term if g is None else g + term
        return g

    gs = PEER_GROUPS_PER_SCORE

    def scores(p):
        r0 = p * gs * PEER_NKEYS
        return jnp.dot(u_ref[r0:r0 + gs * PEER_NKEYS, :], xt_ref[...], preferred_element_type=jnp.float32)

    st = scores(0)
    ws = []
    for p in range(nsub // gs):
        st_next = scores(p + 1) if (p + 1) * gs < nsub else None
        for k in range(gs):
            i = p * gs + k
            ws.append(_gelu(st[k * PEER_NKEYS:(k + 1) * PEER_NKEYS].astype(jnp.bfloat16)) * gate_mat(i))
            if i % PEER_GROUPS_PER_DOT == PEER_GROUPS_PER_DOT - 1:
                w = jnp.concatenate(ws, axis=0)
                ws = []
                acc_ref[...] += jnp.dot(
                    vt_ref[:, (i + 1 - PEER_GROUPS_PER_DOT) * PEER_NKEYS:(i + 1) * PEER_NKEYS], w,
                    preferred_element_type=jnp.float32)
        st = st_next

    @pl.when(j == pl.num_programs(1) - 1)
    def _():
        x2 = x1_ref[...] + gate_ref[0] * acc_ref[...].T
        if final:
            x2 = x2 * lax.rsqrt(jnp.mean(x2 * x2, axis=1, keepdims=True) + EPS) * fg_ref[...]
        o_ref[...] = x2


def peer_dense(xt, u16, vt16, e1, a, r2, e2, x1, gate2, final_g, S, final, tt=512, ec=2048):
    d, n = xt.shape
    ne = u16.shape[0]
    H = PEER_HEADS
    per_b = S // tt
    gspec = lambda: pl.BlockSpec((H, PEER_NKEYS, tt), lambda i, j: (0, 0, i))
    return pl.pallas_call(
        functools.partial(_peer_dense_kernel, ec=ec, final=final),
        grid=(n // tt, ne // ec),
        in_specs=[pl.BlockSpec((d, tt), lambda i, j: (0, i)),
                  pl.BlockSpec((ec, d), lambda i, j: (j, 0)),
                  pl.BlockSpec((d, ec), lambda i, j: (0, j)),
                  gspec(), gspec(), gspec(), gspec(),
                  pl.BlockSpec((tt, d), lambda i, j: (i, 0)),
                  pl.BlockSpec((1, 1, d), lambda i, j: (i // per_b, 0, 0)),
                  pl.BlockSpec((1, d), lambda i, j: (0, 0))],
        out_specs=pl.BlockSpec((tt, d), lambda i, j: (i, 0)),
        out_shape=jax.ShapeDtypeStruct((n, d), jnp.float32),
        scratch_shapes=[pltpu.VMEM((d, tt), jnp.float32)],
        compiler_params=pltpu.CompilerParams(dimension_semantics=("parallel", "arbitrary"),
                                             vmem_limit_bytes=VMEM_LIMIT),
        name="peer_dense",
    )(xt, u16, vt16, e1, a, r2, e2, x1, gate2, final_g.reshape(1, d))


def adaln(c, w, b):
    mod = pmm(jax.nn.silu(c), w) + b
    shift, scale, gate = jnp.split(mod[:, None, :], 3, axis=-1)
    return shift, 1.0 + scale, gate


def kernel(x, c, positions, ada_w, ada_b, norm_g, w_in, conv_dw_w, conv_dw_b, conv_ln_g, conv_ln_b, w_conv_up,
           cmp_pos, cmp_w1, cmp_w2, w_nsa_up, gla_gate_w, gla_gate_b, gla_norm_g, w_gla_up, w_out,
           peer_wq, peer_keys, peer_u, peer_v, final_g):
    B, S, D = x.shape
    bf = jnp.bfloat16
    x2 = x.reshape(B * S, D)
    cs = rope_tables(positions)
    for l in range(DEPTH):
        shift, scale1p, gate = adaln(c, ada_w[l, 0], ada_b[l, 0])
        z = in_proj(x2, norm_g[l, 0], scale1p, shift, pack_w_in(w_in[l]), S)
        hc = conv_pallas(z, conv_dw_w[l], conv_dw_b[l], conv_ln_g[l], conv_ln_b[l], B, S)
        q, kst, vs, kwt, vw = nsa_prep(z, cs, B, S)
        kct, vc = nsa_compress(z, positions, cmp_pos[l], cmp_w1[l], cmp_w2[l], B, S)
        on = nsa_core(q, z, kct, vc, kst, vs, kwt, vw, B, S)
        og = gla_pallas(z, gla_gate_w[l], gla_gate_b[l], gla_norm_g[l], B, S)
        shift2, scale2p, gate2 = adaln(c, ada_w[l, 1], ada_b[l, 1])
        x1, ut, qh = mix_out(hc, on, og, z, x2, gate, w_conv_up[l].astype(bf), w_nsa_up[l].astype(bf),
                             w_gla_up[l].astype(bf), w_out[l].astype(bf), norm_g[l, 1], scale2p, shift2,
                             peer_wq[l].astype(bf), S)
        e1, a, r2, e2 = peer_gates(qh, peer_keys[l])
        x2 = peer_dense(ut, peer_u[l].astype(bf), peer_v[l].astype(bf).T, e1, a, r2, e2, x1, gate2, final_g,
                        S, final=(l == DEPTH - 1))
    return x2.reshape(B, S, D)
```
